```python
import functools
import jax
import jax.numpy as jnp
from jax import lax
import numpy as np

D_MODEL = 1024
BATCH = 8
SEQ = 2048
DEPTH = 4

GRID_W = 64
CTX_LEN = 256
ROPE_THETA = 10000.0
NORM_EPS = 1e-6
NEG_BIG = -1e30
LB_FLOOR = 1e-30

D_MIX = D_MODEL
GROUP_W = D_MIX // 4

MLA_HEADS = 4
MLA_NOPE = 64
MLA_ROPE = 32
MLA_V = GROUP_W // MLA_HEADS
MLA_Q_RANK = 3 * D_MODEL // 16
MLA_KV_RANK = D_MODEL // 8
ATTN_QBLOCK = 128

HG_HEADS = 4
HG_DK = 64
HG_DV = GROUP_W // HG_HEADS
HG_CHUNK = 16

SWA_HEADS = 4
SWA_KV_HEADS = 2
SWA_HD = GROUP_W // SWA_HEADS
SWA_WINDOW = 128
SWA_BLOCK = 128

RET_HEADS = 4
RET_DK = 32
RET_DV = GROUP_W // RET_HEADS
RET_CHUNK = 64

D_FF = 2816
N_EXPERTS = 8
TOP_K = 2
D_FF_EXPERT = 3584

IN_SIZES = (
    MLA_Q_RANK, MLA_KV_RANK, MLA_ROPE,
    HG_HEADS * HG_DK, HG_HEADS * HG_DK, HG_HEADS * HG_DK, HG_HEADS * HG_DV, HG_HEADS * HG_DV,
    SWA_HEADS * SWA_HD, SWA_KV_HEADS * SWA_HD, SWA_KV_HEADS * SWA_HD,
    RET_HEADS * RET_DK, RET_HEADS * RET_DK, RET_HEADS * RET_DV, RET_HEADS * RET_DV,
)
D_IN = sum(IN_SIZES)

kernel_name = 'hybrid_mla_hgrn2_swa_retnet_moe_dit'


def rms_norm(x, g):
    xf = x.astype(jnp.float32)
    y = xf * lax.rsqrt(jnp.mean(xf * xf, axis=-1, keepdims=True) + NORM_EPS)
    return (y * g.astype(jnp.float32)).astype(x.dtype)


def modulate(h, shift, scale):
    return h * (1 + scale) + shift


def to_heads(t, n_heads):
    B, T, _ = t.shape
    return t.reshape(B, T, n_heads, -1).transpose(0, 2, 1, 3)


def apply_rope(x, ang):
    cos = jnp.cos(ang).astype(x.dtype)
    sin = jnp.sin(ang).astype(x.dtype)
    x1, x2 = jnp.split(x, 2, axis=-1)
    return jnp.concatenate([x1 * cos - x2 * sin, x1 * sin + x2 * cos], axis=-1)


def axial_angles(n_tok, rot_dim):
    rows = n_tok // GRID_W
    row = jnp.broadcast_to(jnp.arange(rows)[:, None], (rows, GRID_W)).reshape(-1)
    col = jnp.broadcast_to(jnp.arange(GRID_W)[None, :], (rows, GRID_W)).reshape(-1)
    n_freq = rot_dim // 4
    inv = ROPE_THETA ** (-jnp.arange(n_freq, dtype=jnp.float32) / n_freq)
    return jnp.concatenate([row.astype(jnp.float32)[:, None] * inv,
                            col.astype(jnp.float32)[:, None] * inv], axis=-1)


def linear_angles(pos, rot_dim):
    n_freq = rot_dim // 2
    inv = ROPE_THETA ** (-jnp.arange(n_freq, dtype=jnp.float32) / n_freq)
    return pos.astype(jnp.float32)[:, None] * inv


def dense_block_attention(q, k, v, scale):
    B, T, H, Dk = q.shape
    nb = T // ATTN_QBLOCK
    qb = jnp.moveaxis(q.reshape(B, nb, ATTN_QBLOCK, H, Dk), 1, 0)

    def one_block(qi):
        s = jnp.einsum('bqhd,bkhd->bhqk', qi, k).astype(jnp.float32) * scale
        p = jax.nn.softmax(s, axis=-1).astype(v.dtype)
        return jnp.einsum('bhqk,bkhd->bqhd', p, v)

    o = lax.map(one_block, qb)
    return jnp.moveaxis(o, 0, 1).reshape(B, T, H * v.shape[-1])


def mla_qkv(q_lat, kv_lat, k_pe, q_norm_g, w_uq, kv_norm_g, w_ukv, ang):
    B, T, _ = q_lat.shape
    q = (rms_norm(q_lat, q_norm_g) @ w_uq).reshape(B, T, MLA_HEADS, MLA_NOPE + MLA_ROPE)
    kv = (rms_norm(kv_lat, kv_norm_g) @ w_ukv).reshape(B, T, MLA_HEADS, MLA_NOPE + MLA_V)
    q_nope, q_pe = q[..., :MLA_NOPE], q[..., MLA_NOPE:]
    k_nope, v = kv[..., :MLA_NOPE], kv[..., MLA_NOPE:]
    if ang is not None:
        q_pe = apply_rope(q_pe, ang[:, None, :])
        k_pe = apply_rope(k_pe, ang)
    k_pe = jnp.broadcast_to(k_pe[:, :, None, :], (B, T, MLA_HEADS, MLA_ROPE))
    return (jnp.concatenate([q_nope, q_pe], axis=-1),
            jnp.concatenate([k_nope, k_pe], axis=-1), v)


def mla_mixer(lat_in, ctx_in, ang, q_norm_g, w_uq, kv_norm_g, w_ukv, with_ctx):
    ql, kl, vl = mla_qkv(*lat_in, q_norm_g, w_uq, kv_norm_g, w_ukv, ang)
    qc, kc, vc = mla_qkv(*ctx_in, q_norm_g, w_uq, kv_norm_g, w_ukv, None)
    scale = (MLA_NOPE + MLA_ROPE) ** -0.5
    o_lat = dense_block_attention(ql, jnp.concatenate([kc, kl], axis=1),
                                  jnp.concatenate([vc, vl], axis=1), scale)
    o_ctx = dense_block_attention(qc, kc, vc, scale) if with_ctx else None
    return o_lat, o_ctx


def hgrn_log_forget(z, lb):
    return jnp.logaddexp(jnp.log(jnp.maximum(lb, LB_FLOOR)), jnp.log1p(-lb) + jax.nn.log_sigmoid(z))


def hgrn_chunk_scan(q, k, v, log_f, s0):
    B, H, T, dk = q.shape
    dv = v.shape[-1]
    C = HG_CHUNK
    nc = T // C
    q, k, log_f = (t.reshape(B, H, nc, C, dk) for t in (q, k, log_f))
    v = v.reshape(B, H, nc, C, dv)
    b = jnp.cumsum(log_f, axis=3)
    incl = jnp.tril(jnp.ones((C, C), dtype=bool))[:, :, None]
    diff = b[:, :, :, :, None, :] - b[:, :, :, None, :, :]
    decay = jnp.where(incl, jnp.exp(jnp.where(incl, diff, 0.0)), 0.0)
    a = jnp.einsum('bhctk,bhctsk,bhcsk->bhcts', q, decay, k)
    o = jnp.einsum('bhcts,bhcsv->bhctv', a, v)
    b_last = b[:, :, :, -1:, :]
    u = jnp.einsum('bhcsk,bhcsv->bhckv', k * jnp.exp(b_last - b), v)
    d = jnp.exp(b_last[:, :, :, 0, :])

    def step(s, inp):
        d_c, u_c = inp
        return d_c[..., None] * s + u_c, s

    s_fin, s_start = lax.scan(step, s0, (jnp.moveaxis(d, 2, 0), jnp.moveaxis(u, 2, 0)))
    s_start = jnp.moveaxis(s_start, 0, 2)
    o = o + jnp.einsum('bhctk,bhckv->bhctv', q * jnp.exp(b), s_start)
    return o.reshape(B, H, T, dv), s_fin


def retention_chunk_scan(q, k, v, s0, log_g):
    B, H, T, dk = q.shape
    dv = v.shape[-1]
    C = RET_CHUNK
    nc = T // C
    q = q.reshape(B, H, nc, C, dk)
    k = k.reshape(B, H, nc, C, dk)
    v = v.reshape(B, H, nc, C, dv)
    idx = jnp.arange(C, dtype=jnp.float32)
    rel = idx[:, None] - idx[None, :]
    lg = log_g[:, None, None]
    d_intra = jnp.where(rel >= 0, jnp.exp(jnp.maximum(rel, 0.0) * lg), 0.0)
    a = jnp.einsum('bhctd,bhcsd->bhcts', q, k) * d_intra[None, :, None]
    o = jnp.einsum('bhcts,bhcsv->bhctv', a, v)
    xi = jnp.exp((idx + 1.0)[None, :] * log_g[:, None])
    zeta = jnp.exp((C - 1.0 - idx)[None, :] * log_g[:, None])
    u = jnp.einsum('bhcsd,hs,bhcsv->bhcdv', k, zeta, v)
    dc = jnp.exp(C * log_g)[None, :, None, None]

    def step(s, u_c):
        return dc * s + u_c, s

    s_fin, s_start = lax.scan(step, s0, jnp.moveaxis(u, 2, 0))
    s_start = jnp.moveaxis(s_start, 0, 2)
    o = o + jnp.einsum('bhctd,ht,bhcdv->bhctv', q, xi, s_start)
    return o.reshape(B, H, T, dv), s_fin


def bidirectional_scan(fns, ctx_seqs, lat_seqs, s0):
    o_lat, o_ctx = [], []
    for fn, cs, ls, rev in zip(fns, ctx_seqs, lat_seqs, (False, True)):
        if rev:
            cs = tuple(jnp.flip(t, axis=2) for t in cs)
            ls = tuple(jnp.flip(t, axis=2) for t in ls)
        oc, s_ctx = fn(*cs, s0)
        ol, _ = fn(*ls, s_ctx)
        if rev:
            oc, ol = jnp.flip(oc, axis=2), jnp.flip(ol, axis=2)
        o_lat.append(ol)
        o_ctx.append(oc)
    return o_lat[0] + o_lat[1], o_ctx[0] + o_ctx[1]


def hgrn2_mixer(lat_in, ctx_in, lb, norm_g, with_ctx):
    def prep(q, z_fwd, z_bwd, i):
        qh = to_heads(q, HG_HEADS).astype(jnp.float32) * HG_DK ** -0.5
        vh = to_heads(i, HG_HEADS).astype(jnp.float32)
        dirs = []
        for z, lb_d in ((z_fwd, lb[0]), (z_bwd, lb[1])):
            log_f = hgrn_log_forget(to_heads(z, HG_HEADS).astype(jnp.float32),
                                    lb_d.reshape(HG_HEADS, 1, HG_DK))
            dirs.append((qh, -jnp.expm1(log_f), vh, log_f))
        return dirs

    def finish(o, g):
        B, _, T, _ = o.shape
        y = rms_norm(jnp.swapaxes(o, 1, 2), norm_g.reshape(HG_HEADS, HG_DV)).reshape(B, T, GROUP_W)
        return (y * jax.nn.silu(g.astype(jnp.float32))).astype(g.dtype)

    B = lat_in[0].shape[0]
    s0 = jnp.zeros((B, HG_HEADS, HG_DK, HG_DV), jnp.float32)
    o_lat, o_ctx = bidirectional_scan((hgrn_chunk_scan, hgrn_chunk_scan),
                                      prep(*ctx_in[:4]), prep(*lat_in[:4]), s0)
    return finish(o_lat, lat_in[4]), (finish(o_ctx, ctx_in[4]) if with_ctx else None)


def swa_mixer(lat_in, ctx_in, ang, sink, with_ctx):
    ql, kl, vl = lat_in
    qc, kc, vc = ctx_in
    B, N, _ = ql.shape
    L = qc.shape[1]
    G = SWA_HEADS // SWA_KV_HEADS
    KVH, HD, BLK = SWA_KV_HEADS, SWA_HD, SWA_BLOCK
    ql = apply_rope(ql.reshape(B, N, KVH, G, HD), ang[:, None, None, :])
    kl = apply_rope(kl.reshape(B, N, KVH, HD), ang[:, None, :])
    vl = vl.reshape(B, N, KVH, HD)
    qc = qc.reshape(B, L, KVH, G, HD)
    kc = kc.reshape(B, L, KVH, HD)
    vc = vc.reshape(B, L, KVH, HD)
    scale = HD ** -0.5
    sink_hg = sink.astype(jnp.float32).reshape(KVH, G)

    nb = N // BLK
    qb = ql.reshape(B, nb, BLK, KVH, G, HD)
    pad = ((0, 0), (BLK, BLK), (0, 0), (0, 0))
    kp = jnp.pad(kl, pad).reshape(B, nb + 2, BLK, KVH, HD)
    vp = jnp.pad(vl, pad).reshape(B, nb + 2, BLK, KVH, HD)
    kw = jnp.concatenate([kp[:, :-2], kp[:, 1:-1], kp[:, 2:]], axis=2)
    vw = jnp.concatenate([vp[:, :-2], vp[:, 1:-1], vp[:, 2:]], axis=2)
    s_band = jnp.einsum('bnqhgd,bnkhd->bhgnqk', qb, kw).astype(jnp.float32) * scale
    qpos = jnp.arange(nb)[:, None] * BLK + jnp.arange(BLK)[None, :]
    kpos = (jnp.arange(nb)[:, None] - 1) * BLK + jnp.arange(3 * BLK)[None, :]
    valid = ((kpos >= 0) & (kpos < N))[:, None, :] & \
        (jnp.abs(qpos[:, :, None] - kpos[:, None, :]) <= SWA_WINDOW)
    s_band = jnp.where(valid, s_band, NEG_BIG)
    s_ctx = jnp.einsum('bnqhgd,bchd->bhgnqc', qb, kc).astype(jnp.float32) * scale
    s_sink = jnp.broadcast_to(sink_hg[None, :, :, None, None, None], s_ctx.shape[:-1] + (1,))
    p = jax.nn.softmax(jnp.concatenate([s_sink, s_ctx, s_band], axis=-1), axis=-1)
    p_ctx = p[..., 1:1 + L].astype(vl.dtype)
    p_band = p[..., 1 + L:].astype(vl.dtype)
    o = jnp.einsum('bhgnqc,bchd->bnqhgd', p_ctx, vc) + jnp.einsum('bhgnqk,bnkhd->bnqhgd', p_band, vw)
    o_lat = o.reshape(B, N, SWA_HEADS * HD)
    if not with_ctx:
        return o_lat, None
    s_cc = jnp.einsum('bqhgd,bkhd->bhgqk', qc, kc).astype(jnp.float32) * scale
    s_csink = jnp.broadcast_to(sink_hg[None, :, :, None, None], s_cc.shape[:-1] + (1,))
    pc = jax.nn.softmax(jnp.concatenate([s_csink, s_cc], axis=-1), axis=-1)[..., 1:].astype(vc.dtype)
    o_ctx = jnp.einsum('bhgqk,bkhd->bqhgd', pc, vc).reshape(B, L, SWA_HEADS * HD)
    return o_lat, o_ctx


def group_norm_heads(o, gain, bias):
    o = jnp.swapaxes(o, 1, 2)
    mu = jnp.mean(o, axis=-1, keepdims=True)
    var = jnp.mean(jnp.square(o - mu), axis=-1, keepdims=True)
    y = (o - mu) * lax.rsqrt(var + NORM_EPS)
    B, T, H, dv = y.shape
    return y.reshape(B, T, H * dv) * gain.astype(jnp.float32) + bias.astype(jnp.float32)


def retention_mixer(lat_in, ctx_in, log_gamma, gn_g, gn_b, with_ctx):
    L = ctx_in[0].shape[1]
    N = lat_in[0].shape[1]

    def prep(q, k, v, pos):
        ang = linear_angles(pos, RET_DK)
        qh = apply_rope(to_heads(q, RET_HEADS).astype(jnp.float32), ang) * RET_DK ** -0.5
        kh = apply_rope(to_heads(k, RET_HEADS).astype(jnp.float32), ang)
        vh = to_heads(v, RET_HEADS).astype(jnp.float32)
        return (qh, kh, vh)

    def finish(o, g):
        y = group_norm_heads(o, gn_g, gn_b)
        return (y * jax.nn.silu(g.astype(jnp.float32))).astype(g.dtype)

    cs = prep(*ctx_in[:3], jnp.arange(L))
    ls = prep(*lat_in[:3], L + jnp.arange(N))
    fns = tuple(functools.partial(retention_chunk_scan, log_g=log_gamma[d]) for d in range(2))
    B = lat_in[0].shape[0]
    s0 = jnp.zeros((B, RET_HEADS, RET_DK, RET_DV), jnp.float32)
    o_lat, o_ctx = bidirectional_scan(fns, (cs, cs), (ls, ls), s0)
    return finish(o_lat, lat_in[3]), (finish(o_ctx, ctx_in[3]) if with_ctx else None)


def token_mixers(p_lat, p_ctx, ang_mla, ang_swa, mla_q_norm_g, mla_w_uq, mla_kv_norm_g, mla_w_ukv,
                 hg_lb, hg_norm_g, swa_sink, ret_log_gamma, ret_gn_g, ret_gn_b, with_ctx):
    a_l, a_c = mla_mixer(p_lat[0:3], p_ctx[0:3], ang_mla, mla_q_norm_g, mla_w_uq,
                         mla_kv_norm_g, mla_w_ukv, with_ctx)
    b_l, b_c = hgrn2_mixer(p_lat[3:8], p_ctx[3:8], hg_lb, hg_norm_g, with_ctx)
    c_l, c_c = swa_mixer(p_lat[8:11], p_ctx[8:11], ang_swa, swa_sink, with_ctx)
    d_l, d_c = retention_mixer(p_lat[11:15], p_ctx[11:15], ret_log_gamma, ret_gn_g, ret_gn_b, with_ctx)
    o_lat = jnp.concatenate([a_l, b_l, c_l, d_l], axis=-1)
    o_ctx = jnp.concatenate([a_c, b_c, c_c, d_c], axis=-1) if with_ctx else None
    return o_lat, o_ctx


def swiglu(h, w_gate, w_up, w_down):
    return (jax.nn.silu(h @ w_gate) * (h @ w_up)) @ w_down


def moe_swiglu(h, w_router, w_gate, w_up, w_down):
    logits = (h @ w_router).astype(jnp.float32)
    top_val, top_idx = lax.top_k(logits, TOP_K)
    top_w = jax.nn.softmax(top_val, axis=-1)
    gates = jnp.sum(jax.nn.one_hot(top_idx, N_EXPERTS, dtype=jnp.float32) * top_w[..., None],
                    axis=-2).astype(h.dtype)
    out = jnp.zeros_like(h)
    for e in range(N_EXPERTS):
        out = out + gates[..., e:e + 1] * swiglu(h, w_gate[e], w_up[e], w_down[e])
    return out


def setup_inputs(seed: int = 0) -> dict:
    key = jax.random.key(seed)
    ks = iter(jax.random.split(key, 40))
    D = D_MODEL
    n_dense = (DEPTH + 1) // 2
    n_moe = DEPTH // 2

    def nrm(shape, scale):
        return jax.random.normal(next(ks), shape, jnp.float32) * scale

    ret_base = jnp.log(2.0 ** (5.0 + jnp.arange(RET_HEADS, dtype=jnp.float32)) - 1.0)
    return {
        'x': nrm((BATCH, SEQ, D), 1.0),
        'c': nrm((BATCH, D), 1.0),
        'ctx': nrm((BATCH, CTX_LEN, D), 1.0),
        'c_ctx': nrm((D,), 1.0),
        'w_ada': nrm((DEPTH, D, 6 * D), 0.5 * D ** -0.5),
        'b_ada': nrm((DEPTH, 6 * D), 0.02),
        'norm_mix_g': 1.0 + nrm((DEPTH, D), 0.02),
        'norm_ffn_g': 1.0 + nrm((DEPTH, D), 0.02),
        'w_in': nrm((DEPTH, D, D_IN), D ** -0.5),
        'w_out': nrm((DEPTH, D_MIX, D), D_MIX ** -0.5),
        'mla_q_norm_g': 1.0 + nrm((DEPTH, MLA_Q_RANK), 0.02),
        'mla_w_uq': nrm((DEPTH, MLA_Q_RANK, MLA_HEADS * (MLA_NOPE + MLA_ROPE)), MLA_Q_RANK ** -0.5),
        'mla_kv_norm_g': 1.0 + nrm((DEPTH, MLA_KV_RANK), 0.02),
        'mla_w_ukv': nrm((DEPTH, MLA_KV_RANK, MLA_HEADS * (MLA_NOPE + MLA_V)), MLA_KV_RANK ** -0.5),
        'hg_lb_logits': nrm((DEPTH, 2, HG_HEADS * HG_DK), 0.5),
        'hg_norm_g': 1.0 + nrm((DEPTH, HG_HEADS * HG_DV), 0.02),
        'swa_sink': nrm((DEPTH, SWA_HEADS), 0.5),
        'ret_decay_logit': ret_base + nrm((DEPTH, 2, RET_HEADS), 0.01),
        'ret_gn_g': 1.0 + nrm((DEPTH, RET_HEADS * RET_DV), 0.02),
        'ret_gn_b': nrm((DEPTH, RET_HEADS * RET_DV), 0.02),
        'ffn_w_gate': nrm((n_dense, D, D_FF), D ** -0.5),
        'ffn_w_up': nrm((n_dense, D, D_FF), D ** -0.5),
        'ffn_w_down': nrm((n_dense, D_FF, D), D_FF ** -0.5),
        'moe_w_router': nrm((n_moe, D, N_EXPERTS), D ** -0.5),
        'moe_w_gate': nrm((n_moe, N_EXPERTS, D, D_FF_EXPERT), D ** -0.5),
        'moe_w_up': nrm((n_moe, N_EXPERTS, D, D_FF_EXPERT), D ** -0.5),
        'moe_w_down': nrm((n_moe, N_EXPERTS, D_FF_EXPERT, D), D_FF_EXPERT ** -0.5),
        'final_norm_g': 1.0 + nrm((D,), 0.02),
    }


def reference(x, c, ctx, c_ctx, w_ada, b_ada, norm_mix_g, norm_ffn_g, w_in, w_out,
              mla_q_norm_g, mla_w_uq, mla_kv_norm_g, mla_w_ukv, hg_lb_logits, hg_norm_g,
              swa_sink, ret_decay_logit, ret_gn_g, ret_gn_b, ffn_w_gate, ffn_w_up, ffn_w_down,
              moe_w_router, moe_w_gate, moe_w_up, moe_w_down, final_norm_g):
    N = x.shape[1]
    ang_mla = axial_angles(N, MLA_ROPE)
    ang_swa = axial_angles(N, SWA_HD)
    lb_p = jax.nn.softmax(hg_lb_logits.astype(jnp.float32), axis=0)
    hg_lb = jnp.cumsum(lb_p, axis=0) - lb_p[0:1]
    ret_log_gamma = jax.nn.log_sigmoid(ret_decay_logit.astype(jnp.float32))
    split_points = [int(s) for s in np.cumsum(IN_SIZES)[:-1]]
    silu_c = jax.nn.silu(c)
    silu_cc = jax.nn.silu(c_ctx)

    h_lat, h_ctx = x, ctx
    for l in range(DEPTH):
        with_ctx = l < DEPTH - 1
        mod_l = silu_c @ w_ada[l] + b_ada[l]
        mod_c = silu_cc @ w_ada[l] + b_ada[l]
        sh_a, sc_a, g_a, sh_f, sc_f, g_f = jnp.split(mod_l[:, None, :], 6, axis=-1)
        csh_a, csc_a, cg_a, csh_f, csc_f, cg_f = jnp.split(mod_c, 6)

        a_lat = modulate(rms_norm(h_lat, norm_mix_g[l]), sh_a, sc_a)
        a_ctx = modulate(rms_norm(h_ctx, norm_mix_g[l]), csh_a, csc_a)
        p_lat = jnp.split(a_lat @ w_in[l], split_points, axis=-1)
        p_ctx = jnp.split(a_ctx @ w_in[l], split_points, axis=-1)
        o_lat, o_ctx = token_mixers(p_lat, p_ctx, ang_mla, ang_swa, mla_q_norm_g[l], mla_w_uq[l],
                                    mla_kv_norm_g[l], mla_w_ukv[l], hg_lb[l], hg_norm_g[l],
                                    swa_sink[l], ret_log_gamma[l], ret_gn_g[l], ret_gn_b[l], with_ctx)
        h_lat = h_lat + g_a * (o_lat @ w_out[l])
        if with_ctx:
            h_ctx = h_ctx + cg_a * (o_ctx @ w_out[l])

        if l % 2 == 0:
            ffn = functools.partial(swiglu, w_gate=ffn_w_gate[l // 2], w_up=ffn_w_up[l // 2],
                                    w_down=ffn_w_down[l // 2])
        else:
            ffn = functools.partial(moe_swiglu, w_router=moe_w_router[l // 2], w_gate=moe_w_gate[l // 2],
                                    w_up=moe_w_up[l // 2], w_down=moe_w_down[l // 2])
        h_lat = h_lat + g_f * ffn(modulate(rms_norm(h_lat, norm_ffn_g[l]), sh_f, sc_f))
        if with_ctx:
            h_ctx = h_ctx + cg_f * ffn(modulate(rms_norm(h_ctx, norm_ffn_g[l]), csh_f, csc_f))

    return rms_norm(h_lat, final_norm_g)
```

```python
import functools

import numpy as np
import jax
import jax.numpy as jnp
from jax import lax
from jax.experimental import pallas as pl
from jax.experimental.pallas import tpu as pltpu

F32 = jnp.float32
BF16 = jnp.bfloat16

D_MODEL = 1024
DEPTH = 4
GRID_W = 64
ROPE_THETA = 10000.0
NORM_EPS = 1e-6
LB_FLOOR = 1e-30
GROUP_W = 256

MLA_HEADS, MLA_NOPE, MLA_ROPE, MLA_V = 4, 64, 32, 64
MLA_Q_RANK, MLA_KV_RANK = 192, 128
HG_HEADS, HG_DK, HG_DV = 4, 64, 64
SWA_HEADS, SWA_KV_HEADS, SWA_HD, SWA_WINDOW = 4, 2, 64, 128
RET_HEADS, RET_DK, RET_DV = 4, 32, 64
N_EXPERTS, TOP_K = 8, 2

_IN_SIZES = (192, 128, 32, 256, 256, 256, 256, 256, 256, 128, 128, 128, 128, 256, 256)
_OFF = np.concatenate([[0], np.cumsum(_IN_SIZES)]).astype(np.int64)
D_IN = int(_OFF[-1])

W_MLA, W_HG, W_SWA, W_RET = 512, 1280, 896, 1024
W_EXT = W_MLA + W_HG + W_SWA + W_RET

LANE = 128
VMEM_LIMIT = 56 * 1024 * 1024


def _cparams(sem):
    return pltpu.CompilerParams(dimension_semantics=sem, vmem_limit_bytes=VMEM_LIMIT)


def _swap_halves(idx, width):
    idx = np.asarray(idx)
    base = (idx // width) * width
    j = idx % width
    return base + (j + width // 2) % width


def _ext_columns():
    z = lambda n: -np.ones((n,), np.int64)
    rng = lambda a, n: np.arange(a, a + n)
    o = _OFF
    k_pe = rng(o[2], 32)
    mla = np.concatenate([rng(o[0], 192), z(64), rng(o[1], 128), k_pe,
                          o[2] + _swap_halves(np.arange(32), 32), z(64)])
    hg = rng(o[3], 1280)
    q_heads = [rng(o[8] + 64 * h, 64) for h in range(4)]
    q_perm = np.concatenate([q_heads[0], q_heads[2], q_heads[1], q_heads[3]])
    k = rng(o[9], 128)
    v = rng(o[10], 128)
    sw = lambda cols, base: base + _swap_halves(cols - base, 64)
    swa = np.concatenate([q_perm, k, v, sw(q_perm, o[8]), sw(k, o[9])])
    rq, rk = rng(o[11], 128), rng(o[12], 128)
    ret = np.concatenate([rq, rk, o[11] + _swap_halves(rq - o[11], 32),
                          o[12] + _swap_halves(rk - o[12], 32), rng(o[13], 256), rng(o[14], 256)])
    cols = np.concatenate([mla, hg, swa, ret])
    assert cols.shape[0] == W_EXT
    return cols


_EXT_COLS = _ext_columns()


def _take_cols(w, cols):
    cols = np.asarray(cols)
    g = jnp.take(w, jnp.asarray(np.maximum(cols, 0)), axis=-1)
    return jnp.where(jnp.asarray(cols >= 0), g, jnp.zeros((), w.dtype))


def _axial_angles(n_tok, rot_dim):
    rows = n_tok // GRID_W
    row = jnp.broadcast_to(jnp.arange(rows)[:, None], (rows, GRID_W)).reshape(-1)
    col = jnp.broadcast_to(jnp.arange(GRID_W)[None, :], (rows, GRID_W)).reshape(-1)
    n_freq = rot_dim // 4
    inv = ROPE_THETA ** (-jnp.arange(n_freq, dtype=F32) / n_freq)
    return jnp.concatenate([row.astype(F32)[:, None] * inv, col.astype(F32)[:, None] * inv], axis=-1)


def _rope_tables(ang, n_ctx):
    cos, sin = jnp.cos(ang), jnp.sin(ang)
    c = jnp.concatenate([cos, cos], axis=-1)
    s = jnp.concatenate([-sin, sin], axis=-1)
    r = c.shape[-1]
    c = jnp.concatenate([jnp.ones((n_ctx, r), F32), c], axis=0)
    s = jnp.concatenate([jnp.zeros((n_ctx, r), F32), s], axis=0)
    return c, s


def _split2(x):
    hi = x.astype(BF16)
    lo = (x - hi.astype(F32)).astype(BF16)
    return hi, lo


def _seg_mean(x, ones_bd):
    hi, lo = _split2(x)
    return (jnp.dot(hi, ones_bd, preferred_element_type=F32)
            + jnp.dot(lo, ones_bd, preferred_element_type=F32))


def _silu(x):
    return x / (1.0 + jnp.exp(-x))


def _row_mod(m_ref, idx, row0, tm, n_ctx):
    rows = row0 + lax.broadcasted_iota(jnp.int32, (tm, 1), 0)
    mc = m_ref[0, 0, idx:idx + 1, :]
    ml = m_ref[0, 1, idx:idx + 1, :]
    return jnp.where(rows < n_ctx, mc, ml)


def _norm_mod(x, g, shift, scale):
    ms = jnp.mean(x * x, axis=-1, keepdims=True)
    y = x * lax.rsqrt(ms + NORM_EPS) * g
    return y * (1.0 + scale) + shift


def _mod_kernel(c_ref, w_ref, b_ref, o_ref):
    s = _silu(c_ref[...])
    o_ref[0] = jnp.dot(s.astype(BF16), w_ref[0].astype(BF16), preferred_element_type=F32) + b_ref[0]


def _modulation(cvec, w_ada, b_ada):
    R, D = cvec.shape
    depth, _, n6 = w_ada.shape
    tn = 1536
    return pl.pallas_call(
        _mod_kernel,
        grid=(depth, n6 // tn),
        in_specs=[pl.BlockSpec((R, D), lambda l, j: (0, 0)),
                  pl.BlockSpec((1, D, tn), lambda l, j: (l, 0, j)),
                  pl.BlockSpec((1, 1, tn), lambda l, j: (l, 0, j))],
        out_specs=pl.BlockSpec((1, R, tn), lambda l, j: (l, 0, j)),
        out_shape=jax.ShapeDtypeStruct((depth, R, n6), F32),
        compiler_params=_cparams(("parallel", "parallel")),
        name="adaln_mod",
    )(cvec, w_ada, b_ada.reshape(depth, 1, n6))


def _win_kernel(h_ref, m_ref, g_ref, w_ref, o_mla, o_hg, o_swa, o_ret, *, tm, n_ctx):
    row0 = pl.program_id(1) * tm
    a = _norm_mod(h_ref[0], g_ref[...], _row_mod(m_ref, 0, row0, tm, n_ctx),
                  _row_mod(m_ref, 1, row0, tm, n_ctx)).astype(BF16)
    c0 = 0
    for o_ref, w in ((o_mla, W_MLA), (o_hg, W_HG), (o_swa, W_SWA), (o_ret, W_RET)):
        o_ref[0] = jnp.dot(a, w_ref[:, c0:c0 + w], preferred_element_type=F32)
        c0 += w


def _in_proj(h, mod, g, w_ext, n_ctx, tm=384):
    B, S, D = h.shape
    widths = (W_MLA, W_HG, W_SWA, W_RET)
    return pl.pallas_call(
        functools.partial(_win_kernel, tm=tm, n_ctx=n_ctx),
        grid=(B, S // tm),
        in_specs=[pl.BlockSpec((1, tm, D), lambda b, i: (b, i, 0)),
                  pl.BlockSpec((1, 2, 8, D), lambda b, i: (b, 0, 0, 0)),
                  pl.BlockSpec((1, D), lambda b, i: (0, 0)),
                  pl.BlockSpec((D, W_EXT), lambda b, i: (0, 0))],
        out_specs=[pl.BlockSpec((1, tm, w), lambda b, i: (b, i, 0)) for w in widths],
        out_shape=[jax.ShapeDtypeStruct((B, S, w), F32) for w in widths],
        compiler_params=_cparams(("parallel", "parallel")),
        name="in_proj",
    )(h, mod, g.reshape(1, D), w_ext)


def _mla_weights(q_norm_g, w_uq, kv_norm_g, w_ukv):
    qcols, qsw, kcols = [], [], []
    for h in range(MLA_HEADS):
        b = 96 * h
        nope = np.arange(b, b + 64)
        pe = np.arange(b + 64, b + 96)
        pad = -np.ones((32,), np.int64)
        qcols.append(np.concatenate([nope, pe, pad]))
        qsw.append(np.concatenate([-np.ones((64,), np.int64), (b + 64) + _swap_halves(np.arange(32), 32), pad]))
        kcols.append(np.concatenate([np.arange(128 * h, 128 * h + 64), -np.ones((64,), np.int64)]))
    vcols = np.concatenate([np.arange(128 * h + 64, 128 * h + 128) for h in range(MLA_HEADS)])
    pad_rows = ((0, 256 - MLA_Q_RANK), (0, 0))
    wq = jnp.pad(_take_cols(w_uq, np.concatenate(qcols)), pad_rows).astype(BF16)
    wq_sw = jnp.pad(_take_cols(w_uq, np.concatenate(qsw)), pad_rows).astype(BF16)
    wk = _take_cols(w_ukv, np.concatenate(kcols)).astype(BF16)
    wv = _take_cols(w_ukv, vcols).astype(BF16)
    e = np.zeros((128, 512), np.float32)
    for h in range(MLA_HEADS):
        for j in range(32):
            e[j, 128 * h + 64 + j] = 1.0
            e[32 + j, 128 * h + 64 + j] = 1.0
    gq = jnp.pad(q_norm_g, (0, 256 - MLA_Q_RANK)).reshape(1, 256)
    return gq, wq, wq_sw, kv_norm_g.reshape(1, 128), wk, wv, jnp.asarray(e, BF16)


def _mla_tables(n_lat, n_ctx):
    ang = _axial_angles(n_lat, MLA_ROPE)
    c32, s32 = _rope_tables(ang, n_ctx)
    S = n_lat + n_ctx
    one, zero = jnp.ones((S, 64), F32), jnp.zeros((S, 64), F32)
    cq = jnp.concatenate([one, c32, jnp.zeros((S, 32), F32)], axis=-1)
    sq = jnp.concatenate([zero, s32, jnp.zeros((S, 32), F32)], axis=-1)
    tk = jnp.concatenate([c32, s32, zero], axis=-1)
    return cq, sq, tk


def _mla_prep_kernel(p_ref, cq_ref, sq_ref, tk_ref, gq_ref, wq_ref, wqs_ref, gk_ref, wk_ref, wv_ref, e_ref,
                     q_out, k_out, v_out):
    scale = (MLA_NOPE + MLA_ROPE) ** -0.5
    ql = p_ref[0, :, 0:256]
    msq = jnp.sum(ql * ql, axis=-1, keepdims=True) * (1.0 / MLA_Q_RANK)
    qn = (ql * lax.rsqrt(msq + NORM_EPS) * gq_ref[...]).astype(BF16)
    kvl = p_ref[0, :, 256:384]
    msk = jnp.mean(kvl * kvl, axis=-1, keepdims=True)
    kvn = (kvl * lax.rsqrt(msk + NORM_EPS) * gk_ref[...]).astype(BF16)
    q = jnp.dot(qn, wq_ref[...], preferred_element_type=F32)
    qs = jnp.dot(qn, wqs_ref[...], preferred_element_type=F32)
    cq = jnp.concatenate([cq_ref[...]] * MLA_HEADS, axis=-1)
    sq = jnp.concatenate([sq_ref[...]] * MLA_HEADS, axis=-1)
    q_out[0] = ((q * cq + qs * sq) * scale).astype(BF16)
    pe = (p_ref[0, :, 384:512] * tk_ref[...]).astype(BF16)
    k = (jnp.dot(kvn, wk_ref[...], preferred_element_type=F32)
         + jnp.dot(pe, e_ref[...], preferred_element_type=F32))
    k_out[0] = k.astype(BF16)
    v_out[0] = jnp.dot(kvn, wv_ref[...], preferred_element_type=F32).astype(BF16)


def _mla_attn_kernel(q_ref, k_ref, v_ref, o_ref, *, n_ctx, tq):
    S = k_ref.shape[1]
    lane_head = lax.broadcasted_iota(jnp.int32, (tq, GROUP_W), 1) // MLA_V

    def attend(nk):
        res = jnp.zeros((tq, GROUP_W), F32)
        v = v_ref[0, 0:nk, :]
        for h in range(MLA_HEADS):
            q = q_ref[0, :, LANE * h:LANE * (h + 1)]
            k = k_ref[0, 0:nk, LANE * h:LANE * (h + 1)]
            s = lax.dot_general(q, k, (((1,), (1,)), ((), ())), preferred_element_type=F32)
            m = jnp.max(s, axis=-1, keepdims=True)
            p = jnp.exp(s - m)
            l = jnp.sum(p, axis=-1, keepdims=True)
            o = jnp.dot(p.astype(BF16), v, preferred_element_type=F32)
            res = jnp.where(lane_head == h, o / l, res)
        o_ref[0] = res

    is_ctx = (pl.program_id(1) + 1) * tq <= n_ctx
    pl.when(is_ctx)(lambda: attend(n_ctx))
    pl.when(jnp.logical_not(is_ctx))(lambda: attend(S))


def _mla_mixer(p_mla, weights, tables, n_ctx, tm=384, tq=256):
    B, S, _ = p_mla.shape
    gq, wq, wq_sw, gk, wk, wv, e = weights
    cq, sq, tk = tables
    full = lambda a: pl.BlockSpec(a.shape, lambda b, i: (0,) * a.ndim)
    rows = lambda w: pl.BlockSpec((tm, w), lambda b, i: (i, 0))
    q, k, v = pl.pallas_call(
        _mla_prep_kernel,
        grid=(B, S // tm),
        in_specs=[pl.BlockSpec((1, tm, W_MLA), lambda b, i: (b, i, 0)), rows(128), rows(128), rows(128),
                  full(gq), full(wq), full(wq_sw), full(gk), full(wk), full(wv), full(e)],
        out_specs=[pl.BlockSpec((1, tm, 512), lambda b, i: (b, i, 0)),
                   pl.BlockSpec((1, tm, 512), lambda b, i: (b, i, 0)),
                   pl.BlockSpec((1, tm, 256), lambda b, i: (b, i, 0))],
        out_shape=[jax.ShapeDtypeStruct((B, S, 512), BF16), jax.ShapeDtypeStruct((B, S, 512), BF16),
                   jax.ShapeDtypeStruct((B, S, 256), BF16)],
        compiler_params=_cparams(("parallel", "parallel")),
        name="mla_prep",
    )(p_mla, cq, sq, tk, gq, wq, wq_sw, gk, wk, wv, e)
    assert n_ctx % tq == 0 and S % tq == 0
    return pl.pallas_call(
        functools.partial(_mla_attn_kernel, n_ctx=n_ctx, tq=tq),
        grid=(B, S // tq),
        in_specs=[pl.BlockSpec((1, tq, 512), lambda b, i: (b, i, 0)),
                  pl.BlockSpec((1, S, 512), lambda b, i: (b, 0, 0)),
                  pl.BlockSpec((1, S, 256), lambda b, i: (b, 0, 0))],
        out_specs=pl.BlockSpec((1, tq, GROUP_W), lambda b, i: (b, i, 0)),
        out_shape=jax.ShapeDtypeStruct((B, S, GROUP_W), F32),
        compiler_params=_cparams(("parallel", "arbitrary")),
        name="mla_attn",
    )(q, k, v)


def _swa_tables(n_lat, n_ctx):
    ang = _axial_angles(n_lat, SWA_HD)
    c64, s64 = _rope_tables(ang, n_ctx)
    return jnp.concatenate([c64, c64], axis=-1), jnp.concatenate([s64, s64], axis=-1)


def _swa_prep_kernel(p_ref, c_ref, s_ref, q_out, k_out, v_out):
    scale = SWA_HD ** -0.5
    c, s = c_ref[...], s_ref[...]
    c2 = jnp.concatenate([c, c], axis=-1)
    s2 = jnp.concatenate([s, s], axis=-1)
    q_out[0] = ((p_ref[0, :, 0:256] * c2 + p_ref[0, :, 512:768] * s2) * scale).astype(BF16)
    k_out[0] = (p_ref[0, :, 256:384] * c + p_ref[0, :, 768:896] * s).astype(BF16)
    v_out[0] = p_ref[0, :, 384:512].astype(BF16)


def _swa_attn_kernel(sink_ref, q_ref, kc_ref, vc_ref, kp_ref, kn_ref, kx_ref, vp_ref, vn_ref, vx_ref, o_ref,
                     *, n_ctx, blk):
    j = pl.program_id(1)
    ncb = n_ctx // blk
    nlb = pl.num_programs(1) - ncb
    lane = lax.broadcasted_iota(jnp.int32, (blk, LANE), 1)
    lo = lane < SWA_HD
    qa, qb = q_ref[0, :, 0:LANE], q_ref[0, :, LANE:2 * LANE]
    zero = jnp.zeros((), BF16)
    qs = jnp.concatenate([jnp.where(lo, qa, zero), jnp.where(lo, qb, zero),
                          jnp.where(lo, zero, qa), jnp.where(lo, zero, qb)], axis=0)
    rb = lax.broadcasted_iota(jnp.int32, (4 * blk, 1), 0) // blk
    sink = jnp.where(rb == 0, sink_ref[0], jnp.where(rb == 1, sink_ref[1],
                                                     jnp.where(rb == 2, sink_ref[2], sink_ref[3])))

    def finish(s, valid, vcat):
        if valid is not None:
            s = jnp.where(valid, s, -1e30)
        m = jnp.maximum(jnp.max(s, axis=-1, keepdims=True), sink)
        p = jnp.exp(s - m)
        if valid is not None:
            p = jnp.where(valid, p, 0.0)
        l = jnp.sum(p, axis=-1, keepdims=True) + jnp.exp(sink - m)
        r = jnp.dot(p.astype(BF16), vcat, preferred_element_type=F32) / l
        o_ref[0, :, 0:LANE] = jnp.where(lo, r[0:blk], r[2 * blk:3 * blk])
        o_ref[0, :, LANE:2 * LANE] = jnp.where(lo, r[blk:2 * blk], r[3 * blk:4 * blk])

    def ctx_block():
        s = lax.dot_general(qs, kc_ref[0], (((1,), (1,)), ((), ())), preferred_element_type=F32)
        finish(s, None, vc_ref[0])

    def lat_block():
        n = j - ncb
        kcat = jnp.concatenate([kc_ref[0], kp_ref[0], kn_ref[0], kx_ref[0]], axis=0)
        vcat = jnp.concatenate([vc_ref[0], vp_ref[0], vn_ref[0], vx_ref[0]], axis=0)
        s = lax.dot_general(qs, kcat, (((1,), (1,)), ((), ())), preferred_element_type=F32)
        t = lax.broadcasted_iota(jnp.int32, (4 * blk, n_ctx + 3 * blk), 0) % blk
        c = lax.broadcasted_iota(jnp.int32, (4 * blk, n_ctx + 3 * blk), 1) - n_ctx
        in_prev = (c >= 0) & (c < blk) & (c >= t) & (n > 0)
        in_cur = (c >= blk) & (c < 2 * blk)
        in_next = (c >= 2 * blk) & (c - 2 * blk <= t) & (n < nlb - 1)
        finish(s, (c < 0) | in_prev | in_cur | in_next, vcat)

    pl.when(j < ncb)(ctx_block)
    pl.when(j >= ncb)(lat_block)


def _swa_mixer(p_swa, sink, tables, n_ctx, tm=384, blk=128):
    B, S, _ = p_swa.shape
    c, s = tables
    q, k, v = pl.pallas_call(
        _swa_prep_kernel,
        grid=(B, S // tm),
        in_specs=[pl.BlockSpec((1, tm, W_SWA), lambda b, i: (b, i, 0)),
                  pl.BlockSpec((tm, LANE), lambda b, i: (i, 0)), pl.BlockSpec((tm, LANE), lambda b, i: (i, 0))],
        out_specs=[pl.BlockSpec((1, tm, 256), lambda b, i: (b, i, 0)),
                   pl.BlockSpec((1, tm, LANE), lambda b, i: (b, i, 0)),
                   pl.BlockSpec((1, tm, LANE), lambda b, i: (b, i, 0))],
        out_shape=[jax.ShapeDtypeStruct((B, S, 256), BF16), jax.ShapeDtypeStruct((B, S, LANE), BF16),
                   jax.ShapeDtypeStruct((B, S, LANE), BF16)],
        compiler_params=_cparams(("parallel", "parallel")),
        name="swa_prep",
    )(p_swa, c, s)
    nb = S // blk
    ncb = n_ctx // blk
    cur = lambda b, j: (b, j, 0)
    prev = lambda b, j: (b, jnp.maximum(j - 1, ncb), 0)
    nxt = lambda b, j: (b, jnp.minimum(j + 1, nb - 1), 0)
    kv = lambda f: pl.BlockSpec((1, blk, LANE), f)
    ctx = pl.BlockSpec((1, n_ctx, LANE), lambda b, j: (b, 0, 0))
    return pl.pallas_call(
        functools.partial(_swa_attn_kernel, n_ctx=n_ctx, blk=blk),
        grid=(B, nb),
        in_specs=[pl.BlockSpec(memory_space=pltpu.SMEM),
                  pl.BlockSpec((1, blk, 256), cur), ctx, ctx,
                  kv(prev), kv(cur), kv(nxt), kv(prev), kv(cur), kv(nxt)],
        out_specs=pl.BlockSpec((1, blk, GROUP_W), cur),
        out_shape=jax.ShapeDtypeStruct((B, S, GROUP_W), F32),
        compiler_params=_cparams(("parallel", "arbitrary")),
        name="swa_attn",
    )(sink.astype(F32), q, k, v, k, k, k, v, v, v)


def _bwd_chunk(j, ncc, nc):
    return jnp.where(j < ncc, ncc - 1 - j, nc - 1 - (j - ncc))


def _head_stack(x, width, heads):
    lane_h = lax.broadcasted_iota(jnp.int32, x.shape, 1) // width
    zero = jnp.zeros((), x.dtype)
    return jnp.concatenate([jnp.where(lane_h == h, x, zero) for h in range(heads)], axis=0)


def _head_unstack(y, rows, width, heads):
    lane_h = lax.broadcasted_iota(jnp.int32, (rows, heads * width), 1) // width
    out = y[0:rows]
    for h in range(1, heads):
        out = jnp.where(lane_h == h, y[h * rows:(h + 1) * rows], out)
    return out


RET_CHUNK = 256


def _ret_tables(log_gamma, n_ctx, S):
    C = RET_CHUNK
    t = jnp.arange(C, dtype=F32)
    rel = t[:, None] - t[None, :]
    lg = log_gamma.astype(F32)
    lg_l = jnp.repeat(lg, RET_DK, axis=1)
    dm, xi, zeta, dcay = [], [], [], []
    for d in range(2):
        r = rel if d == 0 else -rel
        dh = jnp.where(r >= 0, jnp.exp(jnp.maximum(r, 0.0)[None] * lg[d][:, None, None]), 0.0)
        dm.append(dh.reshape(RET_HEADS * C, C))
        tq = (t + 1.0) if d == 0 else (C - t)
        tk = (C - 1.0 - t) if d == 0 else t
        xi.append(jnp.exp(tq[:, None] * lg_l[d][None, :]))
        zeta.append(jnp.exp(tk[:, None] * lg_l[d][None, :]))
        dcay.append(jnp.broadcast_to(jnp.exp(C * lg_l[d])[:, None], (RET_HEADS * RET_DK, GROUP_W)))
    mask = (np.arange(128)[:, None] // RET_DK == np.arange(256)[None, :] // RET_DV).astype(np.float32)
    pos = jnp.arange(S, dtype=F32)
    inv = ROPE_THETA ** (-jnp.arange(RET_DK // 2, dtype=F32) / (RET_DK // 2))
    ang = pos[:, None] * inv
    cos, sin = jnp.cos(ang), jnp.sin(ang)
    c = jnp.tile(jnp.concatenate([cos, cos], axis=-1), (1, RET_HEADS))
    s = jnp.tile(jnp.concatenate([-sin, sin], axis=-1), (1, RET_HEADS))
    return (jnp.stack(dm), jnp.stack(xi), jnp.stack(zeta), jnp.stack(dcay), jnp.asarray(mask), c, s)


def _ret_kernel(qkf_ref, vf_ref, cf_ref, sf_ref, qkb_ref, vb_ref, cb_ref, sb_ref,
                dm_ref, xi_ref, zeta_ref, dcay_ref, mask_ref, of_ref, ob_ref, state):
    C = RET_CHUNK

    @pl.when(pl.program_id(1) == 0)
    def _():
        state[...] = jnp.zeros_like(state)

    dirs = ((qkf_ref, vf_ref, cf_ref, sf_ref, of_ref), (qkb_ref, vb_ref, cb_ref, sb_ref, ob_ref))
    for d, (qk_ref, v_ref, c_ref, s_ref, o_ref) in enumerate(dirs):
        c, s = c_ref[...], s_ref[...]
        qr = (qk_ref[0, :, 0:128] * c + qk_ref[0, :, 256:384] * s) * (RET_DK ** -0.5)
        kr = qk_ref[0, :, 128:256] * c + qk_ref[0, :, 384:512] * s
        vb = v_ref[0].astype(BF16)
        qs = _head_stack(qr.astype(BF16), RET_DK, RET_HEADS)
        sc = lax.dot_general(qs, kr.astype(BF16), (((1,), (1,)), ((), ())), preferred_element_type=F32)
        a = (sc * dm_ref[d]).astype(BF16)
        o = _head_unstack(jnp.dot(a, vb, preferred_element_type=F32), C, RET_DV, RET_HEADS)
        st = state[d]
        o = o + jnp.dot((qr * xi_ref[d]).astype(BF16), st.astype(BF16), preferred_element_type=F32)
        o_ref[0] = o
        kz = (kr * zeta_ref[d]).astype(BF16)
        u = lax.dot_general(kz, vb, (((0,), (0,)), ((), ())), preferred_element_type=F32)
        state[d] = st * dcay_ref[d] + u * mask_ref[...]


def _ret_mixer(p_ret, tables, n_ctx):
    B, S, _ = p_ret.shape
    C = RET_CHUNK
    dm, xi, zeta, dcay, mask, c, s = tables
    nc, ncc = S // C, n_ctx // C
    fwd = lambda b, j: (b, j, 0)
    bwd = lambda b, j: (b, _bwd_chunk(j, ncc, nc), 0)
    fwd_v = lambda b, j: (b, j, 2)
    bwd_v = lambda b, j: (b, _bwd_chunk(j, ncc, nc), 2)
    fwd_t = lambda b, j: (j, 0)
    bwd_t = lambda b, j: (_bwd_chunk(j, ncc, nc), 0)
    full = lambda a: pl.BlockSpec(a.shape, lambda b, j: (0,) * a.ndim)
    return pl.pallas_call(
        _ret_kernel,
        grid=(B, nc),
        in_specs=[pl.BlockSpec((1, C, 512), fwd), pl.BlockSpec((1, C, 256), fwd_v),
                  pl.BlockSpec((C, LANE), fwd_t), pl.BlockSpec((C, LANE), fwd_t),
                  pl.BlockSpec((1, C, 512), bwd), pl.BlockSpec((1, C, 256), bwd_v),
                  pl.BlockSpec((C, LANE), bwd_t), pl.BlockSpec((C, LANE), bwd_t),
                  full(dm), full(xi), full(zeta), full(dcay), full(mask)],
        out_specs=[pl.BlockSpec((1, C, GROUP_W), fwd), pl.BlockSpec((1, C, GROUP_W), bwd)],
        out_shape=[jax.ShapeDtypeStruct((B, S, GROUP_W), F32)] * 2,
        scratch_shapes=[pltpu.VMEM((2, RET_HEADS * RET_DK, GROUP_W), F32)],
        compiler_params=_cparams(("parallel", "arbitrary")),
        name="retention_scan",
    )(p_ret, p_ret, c, s, p_ret, p_ret, c, s, dm, xi, zeta, dcay, mask)


HG_CHUNK = 64


def _hg_level_tables(C):
    levels = []
    n = 2
    while n <= C:
        levels.append(n)
        n *= 2
    t = np.arange(C)
    mats, qroles, kroles = [], [], []
    for d in range(2):
        rows = []
        qr_d, kr_d = [], []
        for n in levels:
            m = n // 2
            start = (t // n) * n
            mid = start + m
            u = t - start
            M = np.zeros((C, C), np.float32)
            for i in range(C):
                if d == 0:
                    if u[i] >= m:
                        M[i, mid[i]:i + 1] = 1.0
                    else:
                        M[i, i + 1:mid[i]] = 1.0
                else:
                    if u[i] < m:
                        M[i, i:mid[i]] = 1.0
                    else:
                        M[i, mid[i]:i] = 1.0
            rows.append(M)
            qr_d.append((u >= m) if d == 0 else (u < m))
            kr_d.append((u < m) if d == 0 else (u >= m))
        Mq = np.zeros((C, C), np.float32)
        Mk = np.zeros((C, C), np.float32)
        for i in range(C):
            if d == 0:
                Mq[i, 0:i + 1] = 1.0
                Mk[i, i + 1:C] = 1.0
            else:
                Mq[i, i:C] = 1.0
                Mk[i, 0:i] = 1.0
        rows += [Mq, Mk, np.ones((8, C), np.float32)]
        mats.append(np.concatenate(rows, axis=0))
        qroles.append(np.stack(qr_d))
        kroles.append(np.stack(kr_d))
    same_block = np.stack([(t[:, None] // n == t[None, :] // n) for n in levels]).astype(np.float32)
    return levels, np.stack(mats), np.stack(qroles), np.stack(kroles), same_block


def _hg_kernel(qf_ref, zf_ref, if_ref, qb_ref, zb_ref, ib_ref, lb_ref, m_ref, role_ref, blk_ref,
               ones_ref, mask_ref, of_ref, ob_ref, state, *, n_levels):
    C = HG_CHUNK

    @pl.when(pl.program_id(1) == 0)
    def _():
        state[...] = jnp.zeros_like(state)

    dirs = ((qf_ref, zf_ref, if_ref, of_ref), (qb_ref, zb_ref, ib_ref, ob_ref))
    for d, (q_ref, z_ref, i_ref, o_ref) in enumerate(dirs):
        q = q_ref[0] * (HG_DK ** -0.5)
        z = z_ref[0]
        v = i_ref[0]
        vb = v.astype(BF16)
        lb = lb_ref[d]
        log_lb = jnp.log(jnp.maximum(lb, LB_FLOOR))
        log_sig = jnp.minimum(z, 0.0) - jnp.log(1.0 + jnp.exp(-jnp.abs(z)))
        bq = jnp.log(1.0 - lb) + log_sig
        logf = jnp.maximum(log_lb, bq) + jnp.log(1.0 + jnp.exp(-jnp.abs(log_lb - bq)))
        kk = 1.0 - jnp.exp(logf)
        hi, lo = _split2(logf)
        r = (jnp.dot(m_ref[d], hi, preferred_element_type=F32)
             + jnp.dot(m_ref[d], lo, preferred_element_type=F32))
        acc = jnp.zeros((HG_HEADS * C, C), F32)
        for lv in range(n_levels):
            e = jnp.exp(r[lv * C:(lv + 1) * C])
            roles = role_ref[d, lv]
            qe = jnp.where(roles > 0.5, q * e, 0.0).astype(BF16)
            ke = jnp.where(roles > 0.5, 0.0, kk * e).astype(BF16)
            a = lax.dot_general(_head_stack(qe, HG_DK, HG_HEADS), ke, (((1,), (1,)), ((), ())),
                                preferred_element_type=F32)
            acc = acc + a * blk_ref[lv]
        o = _head_unstack(jnp.dot(acc.astype(BF16), vb, preferred_element_type=F32), C, HG_DV, HG_HEADS)
        o = o + jnp.dot((q * kk).astype(BF16), ones_ref[...], preferred_element_type=F32) * v
        st = state[d]
        e_q = jnp.exp(r[n_levels * C:(n_levels + 1) * C])
        o = o + lax.dot_general((q * e_q).astype(BF16), st.astype(BF16), (((1,), (1,)), ((), ())),
                                preferred_element_type=F32)
        o_ref[0] = o
        e_k = jnp.exp(r[(n_levels + 1) * C:(n_levels + 2) * C])
        ku = (kk * e_k).astype(BF16)
        u = lax.dot_general(vb, ku, (((0,), (0,)), ((), ())), preferred_element_type=F32)
        dtot = jnp.exp(r[(n_levels + 2) * C:(n_levels + 2) * C + 1])
        state[d] = st * dtot + u * mask_ref[...]


def _hg_mixer(p_hg, lb, n_ctx):
    B, S, _ = p_hg.shape
    C = HG_CHUNK
    levels, mats, qroles, kroles, same_block = _hg_level_tables(C)
    nl = len(levels)
    m_all = jnp.asarray(mats, BF16)
    roles = jnp.asarray(np.broadcast_to(qroles[..., None], (2, nl, C, GROUP_W)).astype(np.float32))
    blk = jnp.asarray(np.tile(same_block, (1, HG_HEADS, 1)))
    hd = np.arange(GROUP_W) // HG_DK
    bd = (hd[:, None] == hd[None, :]).astype(np.float32)
    ones_bd = jnp.asarray(bd, BF16)
    mask = jnp.asarray(bd)
    nc, ncc = S // C, n_ctx // C
    col = lambda k, rev: (lambda b, j: (b, _bwd_chunk(j, ncc, nc) if rev else j, k))
    blkspec = lambda k, rev: pl.BlockSpec((1, C, GROUP_W), col(k, rev))
    full = lambda a: pl.BlockSpec(a.shape, lambda b, j: (0,) * a.ndim)
    lb3 = lb.reshape(2, 1, GROUP_W).astype(F32)
    return pl.pallas_call(
        functools.partial(_hg_kernel, n_levels=nl),
        grid=(B, nc),
        in_specs=[blkspec(0, False), blkspec(1, False), blkspec(3, False),
                  blkspec(0, True), blkspec(2, True), blkspec(3, True),
                  full(lb3), full(m_all), full(roles), full(blk), full(ones_bd), full(mask)],
        out_specs=[blkspec(0, False), blkspec(0, True)],
        out_shape=[jax.ShapeDtypeStruct((B, S, GROUP_W), F32)] * 2,
        scratch_shapes=[pltpu.VMEM((2, GROUP_W, GROUP_W), F32)],
        compiler_params=_cparams(("parallel", "arbitrary")),
        name="hgrn2_scan",
    )(p_hg, p_hg, p_hg, p_hg, p_hg, p_hg, lb3, m_all, roles, blk, ones_bd, mask)


_SWA_OUT_PERM = np.concatenate([np.arange(64 * h, 64 * h + 64) for h in (0, 2, 1, 3)])


def _wout_kernel(h_ref, m_ref, a_ref, hf_ref, hb_ref, hgate_ref, c_ref, rf_ref, rb_ref, rgate_ref,
                 hgn_ref, rgg_ref, rgb_ref, ones_ref, w_ref, o_ref, *, tm, n_ctx):
    row0 = pl.program_id(1) * tm
    ones = ones_ref[...]
    o = hf_ref[0] + hb_ref[0]
    yb = o * lax.rsqrt(_seg_mean(o * o, ones) + NORM_EPS) * hgn_ref[...] * _silu(hgate_ref[0])
    o = rf_ref[0] + rb_ref[0]
    xc = o - _seg_mean(o, ones)
    yd = (xc * lax.rsqrt(_seg_mean(xc * xc, ones) + NORM_EPS) * rgg_ref[...] + rgb_ref[...]) * _silu(rgate_ref[0])
    acc = jnp.dot(a_ref[0].astype(BF16), w_ref[0:256, :], preferred_element_type=F32)
    acc += jnp.dot(yb.astype(BF16), w_ref[256:512, :], preferred_element_type=F32)
    acc += jnp.dot(c_ref[0].astype(BF16), w_ref[512:768, :], preferred_element_type=F32)
    acc += jnp.dot(yd.astype(BF16), w_ref[768:1024, :], preferred_element_type=F32)
    o_ref[0] = h_ref[0] + _row_mod(m_ref, 2, row0, tm, n_ctx) * acc


def _out_proj(h, mod, o_mla, hg_f, hg_b, p_hg, o_swa, ret_f, ret_b, p_ret, hg_norm_g, ret_gn_g, ret_gn_b,
              w_out_p, n_ctx, tm=384):
    B, S, D = h.shape
    hd = np.arange(GROUP_W) // HG_DV
    ones_bd = jnp.asarray((hd[:, None] == hd[None, :]).astype(np.float32) / HG_DV, BF16)
    row = lambda w, k=0: pl.BlockSpec((1, tm, w), lambda b, i: (b, i, k))
    vec = pl.BlockSpec((1, GROUP_W), lambda b, i: (0, 0))
    return pl.pallas_call(
        functools.partial(_wout_kernel, tm=tm, n_ctx=n_ctx),
        grid=(B, S // tm),
        in_specs=[row(D), pl.BlockSpec((1, 2, 8, D), lambda b, i: (b, 0, 0, 0)),
                  row(GROUP_W), row(GROUP_W), row(GROUP_W), row(GROUP_W, 4),
                  row(GROUP_W), row(GROUP_W), row(GROUP_W), row(GROUP_W, 3),
                  vec, vec, vec, pl.BlockSpec((GROUP_W, GROUP_W), lambda b, i: (0, 0)),
                  pl.BlockSpec((D, D), lambda b, i: (0, 0))],
        out_specs=row(D),
        out_shape=jax.ShapeDtypeStruct((B, S, D), F32),
        compiler_params=_cparams(("parallel", "parallel")),
        name="out_proj",
    )(h, mod, o_mla, hg_f, hg_b, p_hg, o_swa, ret_f, ret_b, p_ret,
      hg_norm_g.reshape(1, GROUP_W), ret_gn_g.reshape(1, GROUP_W), ret_gn_b.reshape(1, GROUP_W), ones_bd, w_out_p)


def _ffn_kernel(h_ref, m_ref, g_ref, wg_ref, wu_ref, wd_ref, o_ref, xn, acc, *, tm, n_ctx):
    j = pl.program_id(2)
    row0 = pl.program_id(1) * tm

    @pl.when(j == 0)
    def _():
        xn[...] = _norm_mod(h_ref[0], g_ref[...], _row_mod(m_ref, 3, row0, tm, n_ctx),
                            _row_mod(m_ref, 4, row0, tm, n_ctx)).astype(BF16)
        acc[...] = jnp.zeros_like(acc)

    x = xn[...]
    hg = jnp.dot(x, wg_ref[...].astype(BF16), preferred_element_type=F32)
    hu = jnp.dot(x, wu_ref[...].astype(BF16), preferred_element_type=F32)
    acc[...] += jnp.dot((_silu(hg) * hu).astype(BF16), wd_ref[...].astype(BF16), preferred_element_type=F32)

    @pl.when(j == pl.num_programs(2) - 1)
    def _():
        o_ref[0] = h_ref[0] + _row_mod(m_ref, 5, row0, tm, n_ctx) * acc[...]


def _dense_ffn(h, mod, g, w_gate, w_up, w_down, n_ctx, tm=1152, tf=256):
    B, S, D = h.shape
    dff = w_gate.shape[1]
    return pl.pallas_call(
        functools.partial(_ffn_kernel, tm=tm, n_ctx=n_ctx),
        grid=(B, S // tm, dff // tf),
        in_specs=[pl.BlockSpec((1, tm, D), lambda b, i, j: (b, i, 0)),
                  pl.BlockSpec((1, 2, 8, D), lambda b, i, j: (b, 0, 0, 0)),
                  pl.BlockSpec((1, D), lambda b, i, j: (0, 0)),
                  pl.BlockSpec((D, tf), lambda b, i, j: (0, j)),
                  pl.BlockSpec((D, tf), lambda b, i, j: (0, j)),
                  pl.BlockSpec((tf, D), lambda b, i, j: (j, 0))],
        out_specs=pl.BlockSpec((1, tm, D), lambda b, i, j: (b, i, 0)),
        out_shape=jax.ShapeDtypeStruct((B, S, D), F32),
        scratch_shapes=[pltpu.VMEM((tm, D), BF16), pltpu.VMEM((tm, D), F32)],
        compiler_params=_cparams(("parallel", "parallel", "arbitrary")),
        name="dense_ffn",
    )(h, mod, g.reshape(1, D), w_gate, w_up, w_down)


MOE_TM = 1024
MOE_TF = 512
GATHER_ROWS = 1024
COMBINE_ROWS = 384


def _router_kernel(h_ref, m_ref, g_ref, wr_ref, xn_ref, route_ref, *, tm, n_ctx):
    row0 = pl.program_id(1) * tm
    a = _norm_mod(h_ref[0], g_ref[...], _row_mod(m_ref, 3, row0, tm, n_ctx), _row_mod(m_ref, 4, row0, tm, n_ctx))
    xn_ref[0] = a
    logits = jnp.dot(a, wr_ref[...], preferred_element_type=F32, precision=lax.Precision.HIGHEST)
    lane = lax.broadcasted_iota(jnp.int32, logits.shape, 1)
    neg = jnp.float32(-jnp.inf)
    lg = jnp.where(lane < N_EXPERTS, logits, neg)
    v1 = jnp.max(lg, axis=-1, keepdims=True)
    i1 = jnp.min(jnp.where(lg == v1, lane, LANE), axis=-1, keepdims=True)
    lg2 = jnp.where(lane == i1, neg, lg)
    v2 = jnp.max(lg2, axis=-1, keepdims=True)
    i2 = jnp.min(jnp.where(lg2 == v2, lane, LANE), axis=-1, keepdims=True)
    e = jnp.exp(v2 - v1)
    w1 = 1.0 / (1.0 + e)
    w2 = e / (1.0 + e)
    route_ref[0] = jnp.where(lane == 0, i1.astype(F32),
                             jnp.where(lane == 1, i2.astype(F32),
                                       jnp.where(lane == 2, w1, jnp.where(lane == 3, w2, 0.0))))


def _gather_kernel(idx_ref, x_hbm, o_ref, sem):
    rows = o_ref.shape[0]

    def start(r, carry):
        pltpu.make_async_copy(x_hbm.at[pl.ds(idx_ref[0, 0, r], 1)], o_ref.at[pl.ds(r, 1)], sem).start()
        return carry

    def wait(r, carry):
        pltpu.make_async_copy(x_hbm.at[pl.ds(0, 1)], o_ref.at[pl.ds(r, 1)], sem).wait()
        return carry

    lax.fori_loop(0, rows, start, 0)
    lax.fori_loop(0, rows, wait, 0)


def _expert_kernel(te_ref, nu_ref, x_ref, wg_ref, wu_ref, wd_ref, o_ref, acc):
    i, j = pl.program_id(0), pl.program_id(1)
    used = i < nu_ref[0]

    @pl.when(j == 0)
    def _():
        acc[...] = jnp.zeros_like(acc)

    @pl.when(used)
    def _():
        x = x_ref[...].astype(BF16)
        hg = jnp.dot(x, wg_ref[0].astype(BF16), preferred_element_type=F32)
        hu = jnp.dot(x, wu_ref[0].astype(BF16), preferred_element_type=F32)
        acc[...] += jnp.dot((_silu(hg) * hu).astype(BF16), wd_ref[0].astype(BF16), preferred_element_type=F32)

    @pl.when(j == pl.num_programs(1) - 1)
    def _():
        o_ref[...] = acc[...]


def _combine_kernel(slot_ref, h_ref, m_ref, route_ref, y_hbm, o_ref, buf, sem, *, tm, n_ctx):
    row0 = pl.program_id(1) * tm

    def start(r, carry):
        for k in range(TOP_K):
            pltpu.make_async_copy(y_hbm.at[pl.ds(slot_ref[0, 0, 0, k * tm + r], 1)], buf.at[k, pl.ds(r, 1)], sem).start()
        return carry

    def wait(r, carry):
        for k in range(TOP_K):
            pltpu.make_async_copy(y_hbm.at[pl.ds(0, 1)], buf.at[k, pl.ds(r, 1)], sem).wait()
        return carry

    lax.fori_loop(0, tm, start, 0)
    lax.fori_loop(0, tm, wait, 0)
    w1 = route_ref[0, :, 2:3]
    w2 = route_ref[0, :, 3:4]
    o_ref[0] = h_ref[0] + _row_mod(m_ref, 5, row0, tm, n_ctx) * (w1 * buf[0] + w2 * buf[1])


def _moe_ffn(h, mod, g, w_router, w_gate, w_up, w_down, n_ctx, tm=384):
    B, S, D = h.shape
    T = B * S
    dffe = w_gate.shape[-1]
    wr = jnp.pad(w_router, ((0, 0), (0, LANE - N_EXPERTS)))
    xn, route = pl.pallas_call(
        functools.partial(_router_kernel, tm=tm, n_ctx=n_ctx),
        grid=(B, S // tm),
        in_specs=[pl.BlockSpec((1, tm, D), lambda b, i: (b, i, 0)),
                  pl.BlockSpec((1, 2, 8, D), lambda b, i: (b, 0, 0, 0)),
                  pl.BlockSpec((1, D), lambda b, i: (0, 0)),
                  pl.BlockSpec((D, LANE), lambda b, i: (0, 0))],
        out_specs=[pl.BlockSpec((1, tm, D), lambda b, i: (b, i, 0)),
                   pl.BlockSpec((1, tm, LANE), lambda b, i: (b, i, 0))],
        out_shape=[jax.ShapeDtypeStruct((B, S, D), F32), jax.ShapeDtypeStruct((B, S, LANE), F32)],
        compiler_params=_cparams(("parallel", "parallel")),
        name="moe_router",
    )(h, mod, g.reshape(1, D), wr)

    P = TOP_K * T + N_EXPERTS * MOE_TM
    n_tiles = P // MOE_TM
    e_all = jnp.concatenate([route[..., 0].reshape(T), route[..., 1].reshape(T)]).astype(jnp.int32)
    onehot = (e_all[:, None] == jnp.arange(N_EXPERTS, dtype=jnp.int32)[None, :]).astype(jnp.int32)
    csum = jnp.cumsum(onehot, axis=0)
    rank = jnp.sum((csum - onehot) * onehot, axis=1)
    cnt = csum[-1]
    ptiles = (cnt + MOE_TM - 1) // MOE_TM
    tile_end = jnp.cumsum(ptiles)
    gstart = (tile_end - ptiles) * MOE_TM
    slot = jnp.take(gstart, e_all) + rank
    tok = jnp.tile(jnp.arange(T, dtype=jnp.int32), TOP_K)
    src_tok = jnp.zeros((P,), jnp.int32).at[slot].set(tok)
    n_used = tile_end[-1:].astype(jnp.int32)
    tile_e = jnp.minimum(jnp.searchsorted(tile_end, jnp.arange(n_tiles, dtype=jnp.int32), side="right"),
                         N_EXPERTS - 1).astype(jnp.int32)

    xs = pl.pallas_call(
        _gather_kernel,
        grid=(P // GATHER_ROWS,),
        in_specs=[pl.BlockSpec((1, 1, GATHER_ROWS), lambda i: (i, 0, 0), memory_space=pltpu.SMEM),
                  pl.BlockSpec(memory_space=pl.ANY)],
        out_specs=pl.BlockSpec((GATHER_ROWS, D), lambda i: (i, 0)),
        out_shape=jax.ShapeDtypeStruct((P, D), F32),
        scratch_shapes=[pltpu.SemaphoreType.DMA(())],
        compiler_params=_cparams(("arbitrary",)),
        name="moe_gather",
    )(src_tok.reshape(P // GATHER_ROWS, 1, GATHER_ROWS), xn.reshape(T, D))

    nf = dffe // MOE_TF
    last = lambda i, nu: jnp.minimum(i, nu[0] - 1)
    jj = lambda i, j, nu: jnp.where(i < nu[0], j, nf - 1)
    ys = pl.pallas_call(
        _expert_kernel,
        grid_spec=pltpu.PrefetchScalarGridSpec(
            num_scalar_prefetch=2,
            grid=(n_tiles, nf),
            in_specs=[pl.BlockSpec((MOE_TM, D), lambda i, j, te, nu: (last(i, nu), 0)),
                      pl.BlockSpec((1, D, MOE_TF), lambda i, j, te, nu: (te[last(i, nu)], 0, jj(i, j, nu))),
                      pl.BlockSpec((1, D, MOE_TF), lambda i, j, te, nu: (te[last(i, nu)], 0, jj(i, j, nu))),
                      pl.BlockSpec((1, MOE_TF, D), lambda i, j, te, nu: (te[last(i, nu)], jj(i, j, nu), 0))],
            out_specs=pl.BlockSpec((MOE_TM, D), lambda i, j, te, nu: (i, 0)),
            scratch_shapes=[pltpu.VMEM((MOE_TM, D), F32)]),
        out_shape=jax.ShapeDtypeStruct((P, D), F32),
        compiler_params=_cparams(("arbitrary", "arbitrary")),
        name="moe_experts",
    )(tile_e, n_used, xs, w_gate, w_up, w_down)

    slots = slot.reshape(TOP_K, B, S // COMBINE_ROWS, COMBINE_ROWS).transpose(1, 2, 0, 3)
    slots = slots.reshape(B, S // COMBINE_ROWS, 1, TOP_K * COMBINE_ROWS)
    return pl.pallas_call(
        functools.partial(_combine_kernel, tm=COMBINE_ROWS, n_ctx=n_ctx),
        grid=(B, S // COMBINE_ROWS),
        in_specs=[pl.BlockSpec((1, 1, 1, TOP_K * COMBINE_ROWS), lambda b, i: (b, i, 0, 0), memory_space=pltpu.SMEM),
                  pl.BlockSpec((1, COMBINE_ROWS, D), lambda b, i: (b, i, 0)),
                  pl.BlockSpec((1, 2, 8, D), lambda b, i: (b, 0, 0, 0)),
                  pl.BlockSpec((1, COMBINE_ROWS, LANE), lambda b, i: (b, i, 0)),
                  pl.BlockSpec(memory_space=pl.ANY)],
        out_specs=pl.BlockSpec((1, COMBINE_ROWS, D), lambda b, i: (b, i, 0)),
        out_shape=jax.ShapeDtypeStruct((B, S, D), F32),
        scratch_shapes=[pltpu.VMEM((TOP_K, COMBINE_ROWS, D), F32), pltpu.SemaphoreType.DMA(())],
        compiler_params=_cparams(("arbitrary", "arbitrary")),
        name="moe_combine",
    )(slots, h, mod, route, ys)


def _final_norm_kernel(h_ref, g_ref, o_ref):
    x = h_ref[0]
    o_ref[0] = x * lax.rsqrt(jnp.mean(x * x, axis=-1, keepdims=True) + NORM_EPS) * g_ref[...]


def _final_norm(h, g, n_ctx, tm=256):
    B, S, D = h.shape
    N = S - n_ctx
    return pl.pallas_call(
        _final_norm_kernel,
        grid=(B, N // tm),
        in_specs=[pl.BlockSpec((1, tm, D), lambda b, i: (b, n_ctx // tm + i, 0)),
                  pl.BlockSpec((1, D), lambda b, i: (0, 0))],
        out_specs=pl.BlockSpec((1, tm, D), lambda b, i: (b, i, 0)),
        out_shape=jax.ShapeDtypeStruct((B, N, D), F32),
        compiler_params=_cparams(("parallel", "parallel")),
        name="final_norm",
    )(h, g.reshape(1, D))


MOD_ROWS = 16


def kernel(x, c, ctx, c_ctx, w_ada, b_ada, norm_mix_g, norm_ffn_g, w_in, w_out, mla_q_norm_g, mla_w_uq, mla_kv_norm_g, mla_w_ukv, hg_lb_logits, hg_norm_g, swa_sink, ret_decay_logit, ret_gn_g, ret_gn_b, ffn_w_gate, ffn_w_up, ffn_w_down, moe_w_router, moe_w_gate, moe_w_up, moe_w_down, final_norm_g):
    B, N, D = x.shape
    L = ctx.shape[1]
    S = L + N
    assert B + 1 <= MOD_ROWS
    h = jnp.concatenate([ctx, x], axis=1)
    cvec = jnp.zeros((MOD_ROWS, D), F32).at[0].set(c_ctx).at[1:1 + B].set(c)
    mod_all = _modulation(cvec, w_ada, b_ada).reshape(DEPTH, MOD_ROWS, 6, D)

    lb_p = jax.nn.softmax(hg_lb_logits.astype(F32), axis=0)
    hg_lb = jnp.cumsum(lb_p, axis=0) - lb_p[0:1]
    ret_log_gamma = jax.nn.log_sigmoid(ret_decay_logit.astype(F32))

    mla_tab = _mla_tables(N, L)
    swa_tab = _swa_tables(N, L)
    w_out_perm = np.concatenate([np.arange(512), 512 + _SWA_OUT_PERM, np.arange(768, 1024)])

    for l in range(DEPTH):
        m = mod_all[l]
        mod = jnp.stack([jnp.broadcast_to(m[0], (B, 6, D)), m[1:1 + B]], axis=1)
        mod = jnp.pad(mod, ((0, 0), (0, 0), (0, 2), (0, 0)))
        w_ext = _take_cols(w_in[l], _EXT_COLS).astype(BF16)
        p_mla, p_hg, p_swa, p_ret = _in_proj(h, mod, norm_mix_g[l], w_ext, L)
        o_mla = _mla_mixer(p_mla, _mla_weights(mla_q_norm_g[l], mla_w_uq[l], mla_kv_norm_g[l], mla_w_ukv[l]),
                           mla_tab, L)
        hg_f, hg_b = _hg_mixer(p_hg, hg_lb[l], L)
        o_swa = _swa_mixer(p_swa, swa_sink[l], swa_tab, L)
        ret_f, ret_b = _ret_mixer(p_ret, _ret_tables(ret_log_gamma[l], L, S), L)
        w_out_p = jnp.take(w_out[l], jnp.asarray(w_out_perm), axis=0).astype(BF16)
        h = _out_proj(h, mod, o_mla, hg_f, hg_b, p_hg, o_swa, ret_f, ret_b, p_ret,
                      hg_norm_g[l], ret_gn_g[l], ret_gn_b[l], w_out_p, L)
        if l % 2 == 0:
            h = _dense_ffn(h, mod, norm_ffn_g[l], ffn_w_gate[l // 2], ffn_w_up[l // 2], ffn_w_down[l // 2], L,
                           tm=1152 if S % 1152 == 0 else 384)
        else:
            h = _moe_ffn(h, mod, norm_ffn_g[l], moe_w_router[l // 2], moe_w_gate[l // 2], moe_w_up[l // 2],
                         moe_w_down[l // 2], L)
    return _final_norm(h, final_norm_g, L)
```

```python
import functools

import numpy as np
import jax
import jax.numpy as jnp
from jax import lax
from jax.experimental import pallas as pl
from jax.experimental.pallas import tpu as pltpu

F32 = jnp.float32
BF16 = jnp.bfloat16

D_MODEL = 1024
DEPTH = 4
GRID_W = 64
ROPE_THETA = 10000.0
NORM_EPS = 1e-6
LB_FLOOR = 1e-30
GROUP_W = 256

MLA_HEADS, MLA_NOPE, MLA_ROPE, MLA_V = 4, 64, 32, 64
MLA_Q_RANK, MLA_KV_RANK = 192, 128
HG_HEADS, HG_DK, HG_DV = 4, 64, 64
SWA_HEADS, SWA_KV_HEADS, SWA_HD, SWA_WINDOW = 4, 2, 64, 128
RET_HEADS, RET_DK, RET_DV = 4, 32, 64
N_EXPERTS, TOP_K = 8, 2

_IN_SIZES = (192, 128, 32, 256, 256, 256, 256, 256, 256, 128, 128, 128, 128, 256, 256)
_OFF = np.concatenate([[0], np.cumsum(_IN_SIZES)]).astype(np.int64)
D_IN = int(_OFF[-1])

W_MLA, W_HG, W_SWA, W_RET = 512, 1280, 896, 1024
W_EXT = W_MLA + W_HG + W_SWA + W_RET

LANE = 128
VMEM_LIMIT = 56 * 1024 * 1024


def _cparams(sem):
    return pltpu.CompilerParams(dimension_semantics=sem, vmem_limit_bytes=VMEM_LIMIT)


def _swap_halves(idx, width):
    idx = np.asarray(idx)
    base = (idx // width) * width
    j = idx % width
    return base + (j + width // 2) % width


def _ext_columns():
    z = lambda n: -np.ones((n,), np.int64)
    rng = lambda a, n: np.arange(a, a + n)
    o = _OFF
    k_pe = rng(o[2], 32)
    mla = np.concatenate([rng(o[0], 192), z(64), rng(o[1], 128), k_pe,
                          o[2] + _swap_halves(np.arange(32), 32), z(64)])
    hg = rng(o[3], 1280)
    q_heads = [rng(o[8] + 64 * h, 64) for h in range(4)]
    q_perm = np.concatenate([q_heads[0], q_heads[2], q_heads[1], q_heads[3]])
    k = rng(o[9], 128)
    v = rng(o[10], 128)
    sw = lambda cols, base: base + _swap_halves(cols - base, 64)
    swa = np.concatenate([q_perm, k, v, sw(q_perm, o[8]), sw(k, o[9])])
    rq, rk = rng(o[11], 128), rng(o[12], 128)
    ret = np.concatenate([rq, rk, o[11] + _swap_halves(rq - o[11], 32),
                          o[12] + _swap_halves(rk - o[12], 32), rng(o[13], 256), rng(o[14], 256)])
    cols = np.concatenate([mla, hg, swa, ret])
    assert cols.shape[0] == W_EXT
    return cols


_EXT_COLS = _ext_columns()


def _take_cols(w, cols):
    cols = np.asarray(cols)
    g = jnp.take(w, jnp.asarray(np.maximum(cols, 0)), axis=-1)
    return jnp.where(jnp.asarray(cols >= 0), g, jnp.zeros((), w.dtype))


def _axial_angles(n_tok, rot_dim):
    rows = n_tok // GRID_W
    row = jnp.broadcast_to(jnp.arange(rows)[:, None], (rows, GRID_W)).reshape(-1)
    col = jnp.broadcast_to(jnp.arange(GRID_W)[None, :], (rows, GRID_W)).reshape(-1)
    n_freq = rot_dim // 4
    inv = ROPE_THETA ** (-jnp.arange(n_freq, dtype=F32) / n_freq)
    return jnp.concatenate([row.astype(F32)[:, None] * inv, col.astype(F32)[:, None] * inv], axis=-1)


def _rope_tables(ang, n_ctx):
    cos, sin = jnp.cos(ang), jnp.sin(ang)
    c = jnp.concatenate([cos, cos], axis=-1)
    s = jnp.concatenate([-sin, sin], axis=-1)
    r = c.shape[-1]
    c = jnp.concatenate([jnp.ones((n_ctx, r), F32), c], axis=0)
    s = jnp.concatenate([jnp.zeros((n_ctx, r), F32), s], axis=0)
    return c, s


def _split2(x):
    hi = x.astype(BF16)
    lo = (x - hi.astype(F32)).astype(BF16)
    return hi, lo


def _seg_mean(x, ones_bd):
    hi, lo = _split2(x)
    return (jnp.dot(hi, ones_bd, preferred_element_type=F32)
            + jnp.dot(lo, ones_bd, preferred_element_type=F32))


def _silu(x):
    return x / (1.0 + jnp.exp(-x))


def _row_mod(m_ref, idx, row0, tm, n_ctx):
    rows = row0 + lax.broadcasted_iota(jnp.int32, (tm, 1), 0)
    mc = m_ref[0, 0, idx:idx + 1, :]
    ml = m_ref[0, 1, idx:idx + 1, :]
    return jnp.where(rows < n_ctx, mc, ml)


def _norm_mod(x, g, shift, scale):
    ms = jnp.mean(x * x, axis=-1, keepdims=True)
    y = x * lax.rsqrt(ms + NORM_EPS) * g
    return y * (1.0 + scale) + shift


def _mod_kernel(c_ref, w_ref, b_ref, o_ref):
    s = _silu(c_ref[...])
    o_ref[0] = jnp.dot(s.astype(BF16), w_ref[0].astype(BF16), preferred_element_type=F32) + b_ref[0]


def _modulation(cvec, w_ada, b_ada):
    R, D = cvec.shape
    depth, _, n6 = w_ada.shape
    tn = 1536
    return pl.pallas_call(
        _mod_kernel,
        grid=(depth, n6 // tn),
        in_specs=[pl.BlockSpec((R, D), lambda l, j: (0, 0)),
                  pl.BlockSpec((1, D, tn), lambda l, j: (l, 0, j)),
                  pl.BlockSpec((1, 1, tn), lambda l, j: (l, 0, j))],
        out_specs=pl.BlockSpec((1, R, tn), lambda l, j: (l, 0, j)),
        out_shape=jax.ShapeDtypeStruct((depth, R, n6), F32),
        compiler_params=_cparams(("parallel", "parallel")),
        name="adaln_mod",
    )(cvec, w_ada, b_ada.reshape(depth, 1, n6))


def _win_kernel(h_ref, m_ref, g_ref, w_ref, o_mla, o_hg, o_swa, o_ret, *, tm, n_ctx):
    row0 = pl.program_id(1) * tm
    a = _norm_mod(h_ref[0], g_ref[...], _row_mod(m_ref, 0, row0, tm, n_ctx),
                  _row_mod(m_ref, 1, row0, tm, n_ctx)).astype(BF16)
    c0 = 0
    for o_ref, w in ((o_mla, W_MLA), (o_hg, W_HG), (o_swa, W_SWA), (o_ret, W_RET)):
        o_ref[0] = jnp.dot(a, w_ref[:, c0:c0 + w], preferred_element_type=F32)
        c0 += w


def _in_proj(h, mod, g, w_ext, n_ctx, tm=384):
    B, S, D = h.shape
    widths = (W_MLA, W_HG, W_SWA, W_RET)
    return pl.pallas_call(
        functools.partial(_win_kernel, tm=tm, n_ctx=n_ctx),
        grid=(B, S // tm),
        in_specs=[pl.BlockSpec((1, tm, D), lambda b, i: (b, i, 0)),
                  pl.BlockSpec((1, 2, 8, D), lambda b, i: (b, 0, 0, 0)),
                  pl.BlockSpec((1, D), lambda b, i: (0, 0)),
                  pl.BlockSpec((D, W_EXT), lambda b, i: (0, 0))],
        out_specs=[pl.BlockSpec((1, tm, w), lambda b, i: (b, i, 0)) for w in widths],
        out_shape=[jax.ShapeDtypeStruct((B, S, w), F32) for w in widths],
        compiler_params=_cparams(("parallel", "parallel")),
        name="in_proj",
    )(h, mod, g.reshape(1, D), w_ext)


def _mla_weights(q_norm_g, w_uq, kv_norm_g, w_ukv):
    qcols, qsw, kcols = [], [], []
    for h in range(MLA_HEADS):
        b = 96 * h
        nope = np.arange(b, b + 64)
        pe = np.arange(b + 64, b + 96)
        pad = -np.ones((32,), np.int64)
        qcols.append(np.concatenate([nope, pe, pad]))
        qsw.append(np.concatenate([-np.ones((64,), np.int64), (b + 64) + _swap_halves(np.arange(32), 32), pad]))
        kcols.append(np.concatenate([np.arange(128 * h, 128 * h + 64), -np.ones((64,), np.int64)]))
    vcols = np.concatenate([np.arange(128 * h + 64, 128 * h + 128) for h in range(MLA_HEADS)])
    pad_rows = ((0, 256 - MLA_Q_RANK), (0, 0))
    wq = jnp.pad(_take_cols(w_uq, np.concatenate(qcols)), pad_rows).astype(BF16)
    wq_sw = jnp.pad(_take_cols(w_uq, np.concatenate(qsw)), pad_rows).astype(BF16)
    wk = _take_cols(w_ukv, np.concatenate(kcols)).astype(BF16)
    wv = _take_cols(w_ukv, vcols).astype(BF16)
    e = np.zeros((128, 512), np.float32)
    for h in range(MLA_HEADS):
        for j in range(32):
            e[j, 128 * h + 64 + j] = 1.0
            e[32 + j, 128 * h + 64 + j] = 1.0
    gq = jnp.pad(q_norm_g, (0, 256 - MLA_Q_RANK)).reshape(1, 256)
    return gq, wq, wq_sw, kv_norm_g.reshape(1, 128), wk, wv, jnp.asarray(e, BF16)


def _mla_tables(n_lat, n_ctx):
    ang = _axial_angles(n_lat, MLA_ROPE)
    c32, s32 = _rope_tables(ang, n_ctx)
    S = n_lat + n_ctx
    one, zero = jnp.ones((S, 64), F32), jnp.zeros((S, 64), F32)
    cq = jnp.concatenate([one, c32, jnp.zeros((S, 32), F32)], axis=-1)
    sq = jnp.concatenate([zero, s32, jnp.zeros((S, 32), F32)], axis=-1)
    tk = jnp.concatenate([c32, s32, zero], axis=-1)
    return cq, sq, tk


def _mla_prep_kernel(p_ref, cq_ref, sq_ref, tk_ref, gq_ref, wq_ref, wqs_ref, gk_ref, wk_ref, wv_ref, e_ref,
                     q_out, k_out, v_out):
    scale = (MLA_NOPE + MLA_ROPE) ** -0.5
    ql = p_ref[0, :, 0:256]
    msq = jnp.sum(ql * ql, axis=-1, keepdims=True) * (1.0 / MLA_Q_RANK)
    qn = (ql * lax.rsqrt(msq + NORM_EPS) * gq_ref[...]).astype(BF16)
    kvl = p_ref[0, :, 256:384]
    msk = jnp.mean(kvl * kvl, axis=-1, keepdims=True)
    kvn = (kvl * lax.rsqrt(msk + NORM_EPS) * gk_ref[...]).astype(BF16)
    q = jnp.dot(qn, wq_ref[...], preferred_element_type=F32)
    qs = jnp.dot(qn, wqs_ref[...], preferred_element_type=F32)
    cq = jnp.concatenate([cq_ref[...]] * MLA_HEADS, axis=-1)
    sq = jnp.concatenate([sq_ref[...]] * MLA_HEADS, axis=-1)
    q_out[0] = ((q * cq + qs * sq) * scale).astype(BF16)
    pe = (p_ref[0, :, 384:512] * tk_ref[...]).astype(BF16)
    k = (jnp.dot(kvn, wk_ref[...], preferred_element_type=F32)
         + jnp.dot(pe, e_ref[...], preferred_element_type=F32))
    k_out[0] = k.astype(BF16)
    v_out[0] = jnp.dot(kvn, wv_ref[...], preferred_element_type=F32).astype(BF16)


def _mla_attn_kernel(q_ref, k_ref, v_ref, o_ref, *, n_ctx, tq):
    S = k_ref.shape[1]
    lane_head = lax.broadcasted_iota(jnp.int32, (tq, GROUP_W), 1) // MLA_V

    def attend(nk):
        res = jnp.zeros((tq, GROUP_W), F32)
        v = v_ref[0, 0:nk, :]
        for h in range(MLA_HEADS):
            q = q_ref[0, :, LANE * h:LANE * (h + 1)]
            k = k_ref[0, 0:nk, LANE * h:LANE * (h + 1)]
            s = lax.dot_general(q, k, (((1,), (1,)), ((), ())), preferred_element_type=F32)
            m = jnp.max(s, axis=-1, keepdims=True)
            p = jnp.exp(s - m)
            l = jnp.sum(p, axis=-1, keepdims=True)
            o = jnp.dot(p.astype(BF16), v, preferred_element_type=F32)
            res = jnp.where(lane_head == h, o / l, res)
        o_ref[0] = res

    is_ctx = (pl.program_id(1) + 1) * tq <= n_ctx
    pl.when(is_ctx)(lambda: attend(n_ctx))
    pl.when(jnp.logical_not(is_ctx))(lambda: attend(S))


def _mla_mixer(p_mla, weights, tables, n_ctx, tm=384, tq=256):
    B, S, _ = p_mla.shape
    gq, wq, wq_sw, gk, wk, wv, e = weights
    cq, sq, tk = tables
    full = lambda a: pl.BlockSpec(a.shape, lambda b, i: (0,) * a.ndim)
    rows = lambda w: pl.BlockSpec((tm, w), lambda b, i: (i, 0))
    q, k, v = pl.pallas_call(
        _mla_prep_kernel,
        grid=(B, S // tm),
        in_specs=[pl.BlockSpec((1, tm, W_MLA), lambda b, i: (b, i, 0)), rows(128), rows(128), rows(128),
                  full(gq), full(wq), full(wq_sw), full(gk), full(wk), full(wv), full(e)],
        out_specs=[pl.BlockSpec((1, tm, 512), lambda b, i: (b, i, 0)),
                   pl.BlockSpec((1, tm, 512), lambda b, i: (b, i, 0)),
                   pl.BlockSpec((1, tm, 256), lambda b, i: (b, i, 0))],
        out_shape=[jax.ShapeDtypeStruct((B, S, 512), BF16), jax.ShapeDtypeStruct((B, S, 512), BF16),
                   jax.ShapeDtypeStruct((B, S, 256), BF16)],
        compiler_params=_cparams(("parallel", "parallel")),
        name="mla_prep",
    )(p_mla, cq, sq, tk, gq, wq, wq_sw, gk, wk, wv, e)
    assert n_ctx % tq == 0 and S % tq == 0
    return pl.pallas_call(
        functools.partial(_mla_attn_kernel, n_ctx=n_ctx, tq=tq),
        grid=(B, S // tq),
        in_specs=[pl.BlockSpec((1, tq, 512), lambda b, i: (b, i, 0)),
                  pl.BlockSpec((1, S, 512), lambda b, i: (b, 0, 0)),
                  pl.BlockSpec((1, S, 256), lambda b, i: (b, 0, 0))],
        out_specs=pl.BlockSpec((1, tq, GROUP_W), lambda b, i: (b, i, 0)),
        out_shape=jax.ShapeDtypeStruct((B, S, GROUP_W), F32),
        compiler_params=_cparams(("parallel", "arbitrary")),
        name="mla_attn",
    )(q, k, v)


def _swa_tables(n_lat, n_ctx):
    ang = _axial_angles(n_lat, SWA_HD)
    c64, s64 = _rope_tables(ang, n_ctx)
    return jnp.concatenate([c64, c64], axis=-1), jnp.concatenate([s64, s64], axis=-1)


def _swa_prep_kernel(p_ref, c_ref, s_ref, q_out, k_out, v_out):
    scale = SWA_HD ** -0.5
    c, s = c_ref[...], s_ref[...]
    c2 = jnp.concatenate([c, c], axis=-1)
    s2 = jnp.concatenate([s, s], axis=-1)
    q_out[0] = ((p_ref[0, :, 0:256] * c2 + p_ref[0, :, 512:768] * s2) * scale).astype(BF16)
    k_out[0] = (p_ref[0, :, 256:384] * c + p_ref[0, :, 768:896] * s).astype(BF16)
    v_out[0] = p_ref[0, :, 384:512].astype(BF16)


def _swa_attn_kernel(sink_ref, q_ref, kc_ref, vc_ref, kp_ref, kn_ref, kx_ref, vp_ref, vn_ref, vx_ref, o_ref,
                     *, n_ctx, blk):
    j = pl.program_id(1)
    ncb = n_ctx // blk
    nlb = pl.num_programs(1) - ncb
    lane = lax.broadcasted_iota(jnp.int32, (blk, LANE), 1)
    lo = lane < SWA_HD
    qa, qb = q_ref[0, :, 0:LANE], q_ref[0, :, LANE:2 * LANE]
    zero = jnp.zeros((), BF16)
    qs = jnp.concatenate([jnp.where(lo, qa, zero), jnp.where(lo, qb, zero),
                          jnp.where(lo, zero, qa), jnp.where(lo, zero, qb)], axis=0)
    rb = lax.broadcasted_iota(jnp.int32, (4 * blk, 1), 0) // blk
    sink = jnp.where(rb == 0, sink_ref[0], jnp.where(rb == 1, sink_ref[1],
                                                     jnp.where(rb == 2, sink_ref[2], sink_ref[3])))

    def finish(s, valid, vcat):
        if valid is not None:
            s = jnp.where(valid, s, -1e30)
        m = jnp.maximum(jnp.max(s, axis=-1, keepdims=True), sink)
        p = jnp.exp(s - m)
        if valid is not None:
            p = jnp.where(valid, p, 0.0)
        l = jnp.sum(p, axis=-1, keepdims=True) + jnp.exp(sink - m)
        r = jnp.dot(p.astype(BF16), vcat, preferred_element_type=F32) / l
        o_ref[0, :, 0:LANE] = jnp.where(lo, r[0:blk], r[2 * blk:3 * blk])
        o_ref[0, :, LANE:2 * LANE] = jnp.where(lo, r[blk:2 * blk], r[3 * blk:4 * blk])

    def ctx_block():
        s = lax.dot_general(qs, kc_ref[0], (((1,), (1,)), ((), ())), preferred_element_type=F32)
        finish(s, None, vc_ref[0])

    def lat_block():
        n = j - ncb
        kcat = jnp.concatenate([kc_ref[0], kp_ref[0], kn_ref[0], kx_ref[0]], axis=0)
        vcat = jnp.concatenate([vc_ref[0], vp_ref[0], vn_ref[0], vx_ref[0]], axis=0)
        s = lax.dot_general(qs, kcat, (((1,), (1,)), ((), ())), preferred_element_type=F32)
        t = lax.broadcasted_iota(jnp.int32, (4 * blk, n_ctx + 3 * blk), 0) % blk
        c = lax.broadcasted_iota(jnp.int32, (4 * blk, n_ctx + 3 * blk), 1) - n_ctx
        in_prev = (c >= 0) & (c < blk) & (c >= t) & (n > 0)
        in_cur = (c >= blk) & (c < 2 * blk)
        in_next = (c >= 2 * blk) & (c - 2 * blk <= t) & (n < nlb - 1)
        finish(s, (c < 0) | in_prev | in_cur | in_next, vcat)

    pl.when(j < ncb)(ctx_block)
    pl.when(j >= ncb)(lat_block)


def _swa_mixer(p_swa, sink, tables, n_ctx, tm=384, blk=128):
    B, S, _ = p_swa.shape
    c, s = tables
    q, k, v = pl.pallas_call(
        _swa_prep_kernel,
        grid=(B, S // tm),
        in_specs=[pl.BlockSpec((1, tm, W_SWA), lambda b, i: (b, i, 0)),
                  pl.BlockSpec((tm, LANE), lambda b, i: (i, 0)), pl.BlockSpec((tm, LANE), lambda b, i: (i, 0))],
        out_specs=[pl.BlockSpec((1, tm, 256), lambda b, i: (b, i, 0)),
                   pl.BlockSpec((1, tm, LANE), lambda b, i: (b, i, 0)),
                   pl.BlockSpec((1, tm, LANE), lambda b, i: (b, i, 0))],
        out_shape=[jax.ShapeDtypeStruct((B, S, 256), BF16), jax.ShapeDtypeStruct((B, S, LANE), BF16),
                   jax.ShapeDtypeStruct((B, S, LANE), BF16)],
        compiler_params=_cparams(("parallel", "parallel")),
        name="swa_prep",
    )(p_swa, c, s)
    nb = S // blk
    ncb = n_ctx // blk
    cur = lambda b, j: (b, j, 0)
    prev = lambda b, j: (b, jnp.maximum(j - 1, ncb), 0)
    nxt = lambda b, j: (b, jnp.minimum(j + 1, nb - 1), 0)
    kv = lambda f: pl.BlockSpec((1, blk, LANE), f)
    ctx = pl.BlockSpec((1, n_ctx, LANE), lambda b, j: (b, 0, 0))
    return pl.pallas_call(
        functools.partial(_swa_attn_kernel, n_ctx=n_ctx, blk=blk),
        grid=(B, nb),
        in_specs=[pl.BlockSpec(memory_space=pltpu.SMEM),
                  pl.BlockSpec((1, blk, 256), cur), ctx, ctx,
                  kv(prev), kv(cur), kv(nxt), kv(prev), kv(cur), kv(nxt)],
        out_specs=pl.BlockSpec((1, blk, GROUP_W), cur),
        out_shape=jax.ShapeDtypeStruct((B, S, GROUP_W), F32),
        compiler_params=_cparams(("parallel", "arbitrary")),
        name="swa_attn",
    )(sink.astype(F32), q, k, v, k, k, k, v, v, v)


def _bwd_chunk(j, ncc, nc):
    return jnp.where(j < ncc, ncc - 1 - j, nc - 1 - (j - ncc))


def _head_stack(x, width, heads):
    lane_h = lax.broadcasted_iota(jnp.int32, x.shape, 1) // width
    zero = jnp.zeros((), x.dtype)
    return jnp.concatenate([jnp.where(lane_h == h, x, zero) for h in range(heads)], axis=0)


def _head_unstack(y, rows, width, heads):
    lane_h = lax.broadcasted_iota(jnp.int32, (rows, heads * width), 1) // width
    out = y[0:rows]
    for h in range(1, heads):
        out = jnp.where(lane_h == h, y[h * rows:(h + 1) * rows], out)
    return out


RET_CHUNK = 256


def _ret_tables(log_gamma, n_ctx, S):
    C = RET_CHUNK
    t = jnp.arange(C, dtype=F32)
    rel = t[:, None] - t[None, :]
    lg = log_gamma.astype(F32)
    lg_l = jnp.repeat(lg, RET_DK, axis=1)
    dm, xi, zeta, dcay = [], [], [], []
    for d in range(2):
        r = rel if d == 0 else -rel
        dh = jnp.where(r >= 0, jnp.exp(jnp.maximum(r, 0.0)[None] * lg[d][:, None, None]), 0.0)
        dm.append(dh.reshape(RET_HEADS * C, C))
        tq = (t + 1.0) if d == 0 else (C - t)
        tk = (C - 1.0 - t) if d == 0 else t
        xi.append(jnp.exp(tq[:, None] * lg_l[d][None, :]))
        zeta.append(jnp.exp(tk[:, None] * lg_l[d][None, :]))
        dcay.append(jnp.broadcast_to(jnp.exp(C * lg_l[d])[:, None], (RET_HEADS * RET_DK, GROUP_W)))
    mask = (np.arange(128)[:, None] // RET_DK == np.arange(256)[None, :] // RET_DV).astype(np.float32)
    pos = jnp.arange(S, dtype=F32)
    inv = ROPE_THETA ** (-jnp.arange(RET_DK // 2, dtype=F32) / (RET_DK // 2))
    ang = pos[:, None] * inv
    cos, sin = jnp.cos(ang), jnp.sin(ang)
    c = jnp.tile(jnp.concatenate([cos, cos], axis=-1), (1, RET_HEADS))
    s = jnp.tile(jnp.concatenate([-sin, sin], axis=-1), (1, RET_HEADS))
    return (jnp.stack(dm), jnp.stack(xi), jnp.stack(zeta), jnp.stack(dcay), jnp.asarray(mask), c, s)


def _ret_kernel(qkf_ref, vf_ref, cf_ref, sf_ref, qkb_ref, vb_ref, cb_ref, sb_ref,
                dm_ref, xi_ref, zeta_ref, dcay_ref, mask_ref, of_ref, ob_ref, state):
    C = RET_CHUNK

    @pl.when(pl.program_id(1) == 0)
    def _():
        state[...] = jnp.zeros_like(state)

    dirs = ((qkf_ref, vf_ref, cf_ref, sf_ref, of_ref), (qkb_ref, vb_ref, cb_ref, sb_ref, ob_ref))
    for d, (qk_ref, v_ref, c_ref, s_ref, o_ref) in enumerate(dirs):
        c, s = c_ref[...], s_ref[...]
        qr = (qk_ref[0, :, 0:128] * c + qk_ref[0, :, 256:384] * s) * (RET_DK ** -0.5)
        kr = qk_ref[0, :, 128:256] * c + qk_ref[0, :, 384:512] * s
        vb = v_ref[0].astype(BF16)
        qs = _head_stack(qr.astype(BF16), RET_DK, RET_HEADS)
        sc = lax.dot_general(qs, kr.astype(BF16), (((1,), (1,)), ((), ())), preferred_element_type=F32)
        a = (sc * dm_ref[d]).astype(BF16)
        o = _head_unstack(jnp.dot(a, vb, preferred_element_type=F32), C, RET_DV, RET_HEADS)
        st = state[d]
        o = o + jnp.dot((qr * xi_ref[d]).astype(BF16), st.astype(BF16), preferred_element_type=F32)
        o_ref[0] = o
        kz = (kr * zeta_ref[d]).astype(BF16)
        u = lax.dot_general(kz, vb, (((0,), (0,)), ((), ())), preferred_element_type=F32)
        state[d] = st * dcay_ref[d] + u * mask_ref[...]


def _ret_mixer(p_ret, tables, n_ctx):
    B, S, _ = p_ret.shape
    C = RET_CHUNK
    dm, xi, zeta, dcay, mask, c, s = tables
    nc, ncc = S // C, n_ctx // C
    fwd = lambda b, j: (b, j, 0)
    bwd = lambda b, j: (b, _bwd_chunk(j, ncc, nc), 0)
    fwd_v = lambda b, j: (b, j, 2)
    bwd_v = lambda b, j: (b, _bwd_chunk(j, ncc, nc), 2)
    fwd_t = lambda b, j: (j, 0)
    bwd_t = lambda b, j: (_bwd_chunk(j, ncc, nc), 0)
    full = lambda a: pl.BlockSpec(a.shape, lambda b, j: (0,) * a.ndim)
    return pl.pallas_call(
        _ret_kernel,
        grid=(B, nc),
        in_specs=[pl.BlockSpec((1, C, 512), fwd), pl.BlockSpec((1, C, 256), fwd_v),
                  pl.BlockSpec((C, LANE), fwd_t), pl.BlockSpec((C, LANE), fwd_t),
                  pl.BlockSpec((1, C, 512), bwd), pl.BlockSpec((1, C, 256), bwd_v),
                  pl.BlockSpec((C, LANE), bwd_t), pl.BlockSpec((C, LANE), bwd_t),
                  full(dm), full(xi), full(zeta), full(dcay), full(mask)],
        out_specs=[pl.BlockSpec((1, C, GROUP_W), fwd), pl.BlockSpec((1, C, GROUP_W), bwd)],
        out_shape=[jax.ShapeDtypeStruct((B, S, GROUP_W), F32)] * 2,
        scratch_shapes=[pltpu.VMEM((2, RET_HEADS * RET_DK, GROUP_W), F32)],
        compiler_params=_cparams(("parallel", "arbitrary")),
        name="retention_scan",
    )(p_ret, p_ret, c, s, p_ret, p_ret, c, s, dm, xi, zeta, dcay, mask)


HG_CHUNK = 64


def _hg_level_tables(C):
    levels = []
    n = 2
    while n <= C:
        levels.append(n)
        n *= 2
    t = np.arange(C)
    mats, qroles, kroles = [], [], []
    for d in range(2):
        rows = []
        qr_d, kr_d = [], []
        for n in levels:
            m = n // 2
            start = (t // n) * n
            mid = start + m
            u = t - start
            M = np.zeros((C, C), np.float32)
            for i in range(C):
                if d == 0:
                    if u[i] >= m:
                        M[i, mid[i]:i + 1] = 1.0
                    else:
                        M[i, i + 1:mid[i]] = 1.0
                else:
                    if u[i] < m:
                        M[i, i:mid[i]] = 1.0
                    else:
                        M[i, mid[i]:i] = 1.0
            rows.append(M)
            qr_d.append((u >= m) if d == 0 else (u < m))
            kr_d.append((u < m) if d == 0 else (u >= m))
        Mq = np.zeros((C, C), np.float32)
        Mk = np.zeros((C, C), np.float32)
        for i in range(C):
            if d == 0:
                Mq[i, 0:i + 1] = 1.0
                Mk[i, i + 1:C] = 1.0
            else:
                Mq[i, i:C] = 1.0
                Mk[i, 0:i] = 1.0
        rows += [Mq, Mk, np.ones((8, C), np.float32)]
        mats.append(np.concatenate(rows, axis=0))
        qroles.append(np.stack(qr_d))
        kroles.append(np.stack(kr_d))
    same_block = np.stack([(t[:, None] // n == t[None, :] // n) for n in levels]).astype(np.float32)
    return levels, np.stack(mats), np.stack(qroles), np.stack(kroles), same_block


def _hg_kernel(qf_ref, zf_ref, if_ref, qb_ref, zb_ref, ib_ref, lb_ref, m_ref, role_ref, blk_ref,
               ones_ref, mask_ref, of_ref, ob_ref, state, *, n_levels):
    C = HG_CHUNK

    @pl.when(pl.program_id(1) == 0)
    def _():
        state[...] = jnp.zeros_like(state)

    dirs = ((qf_ref, zf_ref, if_ref, of_ref), (qb_ref, zb_ref, ib_ref, ob_ref))
    for d, (q_ref, z_ref, i_ref, o_ref) in enumerate(dirs):
        q = q_ref[0] * (HG_DK ** -0.5)
        z = z_ref[0]
        v = i_ref[0]
        vb = v.astype(BF16)
        lb = lb_ref[d]
        log_lb = jnp.log(jnp.maximum(lb, LB_FLOOR))
        log_sig = jnp.minimum(z, 0.0) - jnp.log(1.0 + jnp.exp(-jnp.abs(z)))
        bq = jnp.log(1.0 - lb) + log_sig
        logf = jnp.maximum(log_lb, bq) + jnp.log(1.0 + jnp.exp(-jnp.abs(log_lb - bq)))
        kk = 1.0 - jnp.exp(logf)
        hi, lo = _split2(logf)
        r = (jnp.dot(m_ref[d], hi, preferred_element_type=F32)
             + jnp.dot(m_ref[d], lo, preferred_element_type=F32))
        acc = jnp.zeros((HG_HEADS * C, C), F32)
        for lv in range(n_levels):
            e = jnp.exp(r[lv * C:(lv + 1) * C])
            roles = role_ref[d, lv]
            qe = jnp.where(roles > 0.5, q * e, 0.0).astype(BF16)
            ke = jnp.where(roles > 0.5, 0.0, kk * e).astype(BF16)
            a = lax.dot_general(_head_stack(qe, HG_DK, HG_HEADS), ke, (((1,), (1,)), ((), ())),
                                preferred_element_type=F32)
            acc = acc + a * blk_ref[lv]
        o = _head_unstack(jnp.dot(acc.astype(BF16), vb, preferred_element_type=F32), C, HG_DV, HG_HEADS)
        o = o + jnp.dot((q * kk).astype(BF16), ones_ref[...], preferred_element_type=F32) * v
        st = state[d]
        e_q = jnp.exp(r[n_levels * C:(n_levels + 1) * C])
        o = o + lax.dot_general((q * e_q).astype(BF16), st.astype(BF16), (((1,), (1,)), ((), ())),
                                preferred_element_type=F32)
        o_ref[0] = o
        e_k = jnp.exp(r[(n_levels + 1) * C:(n_levels + 2) * C])
        ku = (kk * e_k).astype(BF16)
        u = lax.dot_general(vb, ku, (((0,), (0,)), ((), ())), preferred_element_type=F32)
        dtot = jnp.exp(r[(n_levels + 2) * C:(n_levels + 2) * C + 1])
        state[d] = st * dtot + u * mask_ref[...]


def _hg_mixer(p_hg, lb, n_ctx):
    B, S, _ = p_hg.shape
    C = HG_CHUNK
    levels, mats, qroles, kroles, same_block = _hg_level_tables(C)
    nl = len(levels)
    m_all = jnp.asarray(mats, BF16)
    roles = jnp.asarray(np.broadcast_to(qroles[..., None], (2, nl, C, GROUP_W)).astype(np.float32))
    blk = jnp.asarray(np.tile(same_block, (1, HG_HEADS, 1)))
    hd = np.arange(GROUP_W) // HG_DK
    bd = (hd[:, None] == hd[None, :]).astype(np.float32)
    ones_bd = jnp.asarray(bd, BF16)
    mask = jnp.asarray(bd)
    nc, ncc = S // C, n_ctx // C
    col = lambda k, rev: (lambda b, j: (b, _bwd_chunk(j, ncc, nc) if rev else j, k))
    blkspec = lambda k, rev: pl.BlockSpec((1, C, GROUP_W), col(k, rev))
    full = lambda a: pl.BlockSpec(a.shape, lambda b, j: (0,) * a.ndim)
    lb3 = lb.reshape(2, 1, GROUP_W).astype(F32)
    return pl.pallas_call(
        functools.partial(_hg_kernel, n_levels=nl),
        grid=(B, nc),
        in_specs=[blkspec(0, False), blkspec(1, False), blkspec(3, False),
                  blkspec(0, True), blkspec(2, True), blkspec(3, True),
                  full(lb3), full(m_all), full(roles), full(blk), full(ones_bd), full(mask)],
        out_specs=[blkspec(0, False), blkspec(0, True)],
        out_shape=[jax.ShapeDtypeStruct((B, S, GROUP_W), F32)] * 2,
        scratch_shapes=[pltpu.VMEM((2, GROUP_W, GROUP_W), F32)],
        compiler_params=_cparams(("parallel", "arbitrary")),
        name="hgrn2_scan",
    )(p_hg, p_hg, p_hg, p_hg, p_hg, p_hg, lb3, m_all, roles, blk, ones_bd, mask)


_SWA_OUT_PERM = np.concatenate([np.arange(64 * h, 64 * h + 64) for h in (0, 2, 1, 3)])


def _wout_kernel(h_ref, m_ref, a_ref, hf_ref, hb_ref, hgate_ref, c_ref, rf_ref, rb_ref, rgate_ref,
                 hgn_ref, rgg_ref, rgb_ref, ones_ref, w_ref, o_ref, *, tm, n_ctx):
    row0 = pl.program_id(1) * tm
    ones = ones_ref[...]
    o = hf_ref[0] + hb_ref[0]
    yb = o * lax.rsqrt(_seg_mean(o * o, ones) + NORM_EPS) * hgn_ref[...] * _silu(hgate_ref[0])
    o = rf_ref[0] + rb_ref[0]
    xc = o - _seg_mean(o, ones)
    yd = (xc * lax.rsqrt(_seg_mean(xc * xc, ones) + NORM_EPS) * rgg_ref[...] + rgb_ref[...]) * _silu(rgate_ref[0])
    acc = jnp.dot(a_ref[0].astype(BF16), w_ref[0:256, :], preferred_element_type=F32)
    acc += jnp.dot(yb.astype(BF16), w_ref[256:512, :], preferred_element_type=F32)
    acc += jnp.dot(c_ref[0].astype(BF16), w_ref[512:768, :], preferred_element_type=F32)
    acc += jnp.dot(yd.astype(BF16), w_ref[768:1024, :], preferred_element_type=F32)
    o_ref[0] = h_ref[0] + _row_mod(m_ref, 2, row0, tm, n_ctx) * acc


def _out_proj(h, mod, o_mla, hg_f, hg_b, p_hg, o_swa, ret_f, ret_b, p_ret, hg_norm_g, ret_gn_g, ret_gn_b,
              w_out_p, n_ctx, tm=384):
    B, S, D = h.shape
    hd = np.arange(GROUP_W) // HG_DV
    ones_bd = jnp.asarray((hd[:, None] == hd[None, :]).astype(np.float32) / HG_DV, BF16)
    row = lambda w, k=0: pl.BlockSpec((1, tm, w), lambda b, i: (b, i, k))
    vec = pl.BlockSpec((1, GROUP_W), lambda b, i: (0, 0))
    return pl.pallas_call(
        functools.partial(_wout_kernel, tm=tm, n_ctx=n_ctx),
        grid=(B, S // tm),
        in_specs=[row(D), pl.BlockSpec((1, 2, 8, D), lambda b, i: (b, 0, 0, 0)),
                  row(GROUP_W), row(GROUP_W), row(GROUP_W), row(GROUP_W, 4),
                  row(GROUP_W), row(GROUP_W), row(GROUP_W), row(GROUP_W, 3),
                  vec, vec, vec, pl.BlockSpec((GROUP_W, GROUP_W), lambda b, i: (0, 0)),
                  pl.BlockSpec((D, D), lambda b, i: (0, 0))],
        out_specs=row(D),
        out_shape=jax.ShapeDtypeStruct((B, S, D), F32),
        compiler_params=_cparams(("parallel", "parallel")),
        name="out_proj",
    )(h, mod, o_mla, hg_f, hg_b, p_hg, o_swa, ret_f, ret_b, p_ret,
      hg_norm_g.reshape(1, GROUP_W), ret_gn_g.reshape(1, GROUP_W), ret_gn_b.reshape(1, GROUP_W), ones_bd, w_out_p)


def _ffn_kernel(h_ref, m_ref, g_ref, wg_ref, wu_ref, wd_ref, o_ref, xn, acc, *, tm, n_ctx):
    j = pl.program_id(2)
    row0 = pl.program_id(1) * tm

    @pl.when(j == 0)
    def _():
        xn[...] = _norm_mod(h_ref[0], g_ref[...], _row_mod(m_ref, 3, row0, tm, n_ctx),
                            _row_mod(m_ref, 4, row0, tm, n_ctx)).astype(BF16)
        acc[...] = jnp.zeros_like(acc)

    x = xn[...]
    hg = jnp.dot(x, wg_ref[0].astype(BF16), preferred_element_type=F32)
    hu = jnp.dot(x, wu_ref[0].astype(BF16), preferred_element_type=F32)
    acc[...] += jnp.dot((_silu(hg) * hu).astype(BF16), wd_ref[0].astype(BF16), preferred_element_type=F32)

    @pl.when(j == pl.num_programs(2) - 1)
    def _():
        o_ref[0] = h_ref[0] + _row_mod(m_ref, 5, row0, tm, n_ctx) * acc[...]


def _dense_ffn(h, mod, g, w_gate, w_up, w_down, li, n_ctx, tm=1152, tf=256):
    B, S, D = h.shape
    dff = w_gate.shape[-1]
    return pl.pallas_call(
        functools.partial(_ffn_kernel, tm=tm, n_ctx=n_ctx),
        grid=(B, S // tm, dff // tf),
        in_specs=[pl.BlockSpec((1, tm, D), lambda b, i, j: (b, i, 0)),
                  pl.BlockSpec((1, 2, 8, D), lambda b, i, j: (b, 0, 0, 0)),
                  pl.BlockSpec((1, D), lambda b, i, j: (0, 0)),
                  pl.BlockSpec((1, D, tf), lambda b, i, j: (li, 0, j)),
                  pl.BlockSpec((1, D, tf), lambda b, i, j: (li, 0, j)),
                  pl.BlockSpec((1, tf, D), lambda b, i, j: (li, j, 0))],
        out_specs=pl.BlockSpec((1, tm, D), lambda b, i, j: (b, i, 0)),
        out_shape=jax.ShapeDtypeStruct((B, S, D), F32),
        scratch_shapes=[pltpu.VMEM((tm, D), BF16), pltpu.VMEM((tm, D), F32)],
        compiler_params=_cparams(("parallel", "parallel", "arbitrary")),
        name="dense_ffn",
    )(h, mod, g.reshape(1, D), w_gate, w_up, w_down)


MOE_TM = 1024
MOE_TF = 512
GATHER_ROWS = 1024
COMBINE_ROWS = 384
DMA_UNROLL = 8


def _router_kernel(h_ref, m_ref, g_ref, wr_ref, xn_ref, route_ref, *, tm, n_ctx):
    row0 = pl.program_id(1) * tm
    a = _norm_mod(h_ref[0], g_ref[...], _row_mod(m_ref, 3, row0, tm, n_ctx), _row_mod(m_ref, 4, row0, tm, n_ctx))
    xn_ref[0] = a
    logits = jnp.dot(a, wr_ref[...], preferred_element_type=F32, precision=lax.Precision.HIGHEST)
    lane = lax.broadcasted_iota(jnp.int32, logits.shape, 1)
    neg = jnp.float32(-jnp.inf)
    lg = jnp.where(lane < N_EXPERTS, logits, neg)
    v1 = jnp.max(lg, axis=-1, keepdims=True)
    i1 = jnp.min(jnp.where(lg == v1, lane, LANE), axis=-1, keepdims=True)
    lg2 = jnp.where(lane == i1, neg, lg)
    v2 = jnp.max(lg2, axis=-1, keepdims=True)
    i2 = jnp.min(jnp.where(lg2 == v2, lane, LANE), axis=-1, keepdims=True)
    e = jnp.exp(v2 - v1)
    w1 = 1.0 / (1.0 + e)
    w2 = e / (1.0 + e)
    route_ref[0] = jnp.where(lane == 0, i1.astype(F32),
                             jnp.where(lane == 1, i2.astype(F32),
                                       jnp.where(lane == 2, w1, jnp.where(lane == 3, w2, 0.0))))


def _gather_kernel(idx_ref, x_hbm, o_ref, sem):
    rows = o_ref.shape[0]

    def start(r, carry):
        pltpu.make_async_copy(x_hbm.at[pl.ds(idx_ref[0, 0, r], 1)], o_ref.at[pl.ds(r, 1)], sem).start()
        return carry

    lax.fori_loop(0, rows, start, 0, unroll=DMA_UNROLL)
    pltpu.make_async_copy(x_hbm.at[pl.ds(0, rows)], o_ref, sem).wait()


def _expert_kernel(te_ref, nu_ref, x_ref, wg_ref, wu_ref, wd_ref, o_ref, acc):
    i, j = pl.program_id(0), pl.program_id(1)
    used = i < nu_ref[0]

    @pl.when(j == 0)
    def _():
        acc[...] = jnp.zeros_like(acc)

    @pl.when(used)
    def _():
        x = x_ref[...].astype(BF16)
        hg = jnp.dot(x, wg_ref[0, 0].astype(BF16), preferred_element_type=F32)
        hu = jnp.dot(x, wu_ref[0, 0].astype(BF16), preferred_element_type=F32)
        acc[...] += jnp.dot((_silu(hg) * hu).astype(BF16), wd_ref[0, 0].astype(BF16), preferred_element_type=F32)

    @pl.when(j == pl.num_programs(1) - 1)
    def _():
        o_ref[...] = acc[...]


def _combine_kernel(slot_ref, h_ref, m_ref, route_ref, y_hbm, o_ref, buf, sem, *, tm, n_ctx):
    row0 = pl.program_id(1) * tm

    def start(r, carry):
        for k in range(TOP_K):
            pltpu.make_async_copy(y_hbm.at[pl.ds(slot_ref[0, 0, 0, k * tm + r], 1)], buf.at[k, pl.ds(r, 1)], sem).start()
        return carry

    lax.fori_loop(0, tm, start, 0, unroll=DMA_UNROLL)
    for k in range(TOP_K):
        pltpu.make_async_copy(y_hbm.at[pl.ds(0, tm)], buf.at[k], sem).wait()
    w1 = route_ref[0, :, 2:3]
    w2 = route_ref[0, :, 3:4]
    o_ref[0] = h_ref[0] + _row_mod(m_ref, 5, row0, tm, n_ctx) * (w1 * buf[0] + w2 * buf[1])


def _moe_ffn(h, mod, g, w_router, w_gate, w_up, w_down, li, n_ctx, tm=384):
    B, S, D = h.shape
    T = B * S
    dffe = w_gate.shape[-1]
    wr = jnp.pad(w_router, ((0, 0), (0, LANE - N_EXPERTS)))
    xn, route = pl.pallas_call(
        functools.partial(_router_kernel, tm=tm, n_ctx=n_ctx),
        grid=(B, S // tm),
        in_specs=[pl.BlockSpec((1, tm, D), lambda b, i: (b, i, 0)),
                  pl.BlockSpec((1, 2, 8, D), lambda b, i: (b, 0, 0, 0)),
                  pl.BlockSpec((1, D), lambda b, i: (0, 0)),
                  pl.BlockSpec((D, LANE), lambda b, i: (0, 0))],
        out_specs=[pl.BlockSpec((1, tm, D), lambda b, i: (b, i, 0)),
                   pl.BlockSpec((1, tm, LANE), lambda b, i: (b, i, 0))],
        out_shape=[jax.ShapeDtypeStruct((B, S, D), F32), jax.ShapeDtypeStruct((B, S, LANE), F32)],
        compiler_params=_cparams(("parallel", "parallel")),
        name="moe_router",
    )(h, mod, g.reshape(1, D), wr)

    P = TOP_K * T + N_EXPERTS * MOE_TM
    n_tiles = P // MOE_TM
    e_all = jnp.concatenate([route[..., 0].reshape(T), route[..., 1].reshape(T)]).astype(jnp.int32)
    onehot = (e_all[:, None] == jnp.arange(N_EXPERTS, dtype=jnp.int32)[None, :]).astype(jnp.int32)
    csum = jnp.cumsum(onehot, axis=0)
    rank = jnp.sum((csum - onehot) * onehot, axis=1)
    cnt = csum[-1]
    ptiles = (cnt + MOE_TM - 1) // MOE_TM
    tile_end = jnp.cumsum(ptiles)
    gstart = (tile_end - ptiles) * MOE_TM
    slot = jnp.take(gstart, e_all) + rank
    tok = jnp.tile(jnp.arange(T, dtype=jnp.int32), TOP_K)
    src_tok = jnp.zeros((P,), jnp.int32).at[slot].set(tok)
    n_used = tile_end[-1:].astype(jnp.int32)
    tile_ids = jnp.arange(n_tiles, dtype=jnp.int32)
    tile_e = jnp.minimum(jnp.sum((tile_ids[:, None] >= tile_end[None, :]).astype(jnp.int32), axis=1),
                         N_EXPERTS - 1)

    xs = pl.pallas_call(
        _gather_kernel,
        grid=(P // GATHER_ROWS,),
        in_specs=[pl.BlockSpec((1, 1, GATHER_ROWS), lambda i: (i, 0, 0), memory_space=pltpu.SMEM),
                  pl.BlockSpec(memory_space=pl.ANY)],
        out_specs=pl.BlockSpec((GATHER_ROWS, D), lambda i: (i, 0)),
        out_shape=jax.ShapeDtypeStruct((P, D), F32),
        scratch_shapes=[pltpu.SemaphoreType.DMA(())],
        compiler_params=_cparams(("arbitrary",)),
        name="moe_gather",
    )(src_tok.reshape(P // GATHER_ROWS, 1, GATHER_ROWS), xn.reshape(T, D))

    nf = dffe // MOE_TF
    last = lambda i, nu: jnp.minimum(i, nu[0] - 1)
    jj = lambda i, j, nu: jnp.where(i < nu[0], j, nf - 1)
    ys = pl.pallas_call(
        _expert_kernel,
        grid_spec=pltpu.PrefetchScalarGridSpec(
            num_scalar_prefetch=2,
            grid=(n_tiles, nf),
            in_specs=[pl.BlockSpec((MOE_TM, D), lambda i, j, te, nu: (last(i, nu), 0)),
                      pl.BlockSpec((1, 1, D, MOE_TF), lambda i, j, te, nu: (li, te[last(i, nu)], 0, jj(i, j, nu))),
                      pl.BlockSpec((1, 1, D, MOE_TF), lambda i, j, te, nu: (li, te[last(i, nu)], 0, jj(i, j, nu))),
                      pl.BlockSpec((1, 1, MOE_TF, D), lambda i, j, te, nu: (li, te[last(i, nu)], jj(i, j, nu), 0))],
            out_specs=pl.BlockSpec((MOE_TM, D), lambda i, j, te, nu: (i, 0)),
            scratch_shapes=[pltpu.VMEM((MOE_TM, D), F32)]),
        out_shape=jax.ShapeDtypeStruct((P, D), F32),
        compiler_params=_cparams(("arbitrary", "arbitrary")),
        name="moe_experts",
    )(tile_e, n_used, xs, w_gate, w_up, w_down)

    slots = slot.reshape(TOP_K, B, S // COMBINE_ROWS, COMBINE_ROWS).transpose(1, 2, 0, 3)
    slots = slots.reshape(B, S // COMBINE_ROWS, 1, TOP_K * COMBINE_ROWS)
    return pl.pallas_call(
        functools.partial(_combine_kernel, tm=COMBINE_ROWS, n_ctx=n_ctx),
        grid=(B, S // COMBINE_ROWS),
        in_specs=[pl.BlockSpec((1, 1, 1, TOP_K * COMBINE_ROWS), lambda b, i: (b, i, 0, 0), memory_space=pltpu.SMEM),
                  pl.BlockSpec((1, COMBINE_ROWS, D), lambda b, i: (b, i, 0)),
                  pl.BlockSpec((1, 2, 8, D), lambda b, i: (b, 0, 0, 0)),
                  pl.BlockSpec((1, COMBINE_ROWS, LANE), lambda b, i: (b, i, 0)),
                  pl.BlockSpec(memory_space=pl.ANY)],
        out_specs=pl.BlockSpec((1, COMBINE_ROWS, D), lambda b, i: (b, i, 0)),
        out_shape=jax.ShapeDtypeStruct((B, S, D), F32),
        scratch_shapes=[pltpu.VMEM((TOP_K, COMBINE_ROWS, D), F32), pltpu.SemaphoreType.DMA(())],
        compiler_params=_cparams(("arbitrary", "arbitrary")),
        name="moe_combine",
    )(slots, h, mod, route, ys)


def _final_norm_kernel(h_ref, g_ref, o_ref):
    x = h_ref[0]
    o_ref[0] = x * lax.rsqrt(jnp.mean(x * x, axis=-1, keepdims=True) + NORM_EPS) * g_ref[...]


def _final_norm(h, g, n_ctx, tm=256):
    B, S, D = h.shape
    N = S - n_ctx
    return pl.pallas_call(
        _final_norm_kernel,
        grid=(B, N // tm),
        in_specs=[pl.BlockSpec((1, tm, D), lambda b, i: (b, n_ctx // tm + i, 0)),
                  pl.BlockSpec((1, D), lambda b, i: (0, 0))],
        out_specs=pl.BlockSpec((1, tm, D), lambda b, i: (b, i, 0)),
        out_shape=jax.ShapeDtypeStruct((B, N, D), F32),
        compiler_params=_cparams(("parallel", "parallel")),
        name="final_norm",
    )(h, g.reshape(1, D))


MOD_ROWS = 16


def kernel(x, c, ctx, c_ctx, w_ada, b_ada, norm_mix_g, norm_ffn_g, w_in, w_out, mla_q_norm_g, mla_w_uq, mla_kv_norm_g, mla_w_ukv, hg_lb_logits, hg_norm_g, swa_sink, ret_decay_logit, ret_gn_g, ret_gn_b, ffn_w_gate, ffn_w_up, ffn_w_down, moe_w_router, moe_w_gate, moe_w_up, moe_w_down, final_norm_g):
    B, N, D = x.shape
    L = ctx.shape[1]
    S = L + N
    assert B + 1 <= MOD_ROWS
    h = jnp.concatenate([ctx, x], axis=1)
    cvec = jnp.zeros((MOD_ROWS, D), F32).at[0].set(c_ctx).at[1:1 + B].set(c)
    mod_all = _modulation(cvec, w_ada, b_ada).reshape(DEPTH, MOD_ROWS, 6, D)

    lb_p = jax.nn.softmax(hg_lb_logits.astype(F32), axis=0)
    hg_lb = jnp.cumsum(lb_p, axis=0) - lb_p[0:1]
    ret_log_gamma = jax.nn.log_sigmoid(ret_decay_logit.astype(F32))

    mla_tab = _mla_tables(N, L)
    swa_tab = _swa_tables(N, L)
    w_out_perm = np.concatenate([np.arange(512), 512 + _SWA_OUT_PERM, np.arange(768, 1024)])

    for l in range(DEPTH):
        m = mod_all[l]
        mod = jnp.stack([jnp.broadcast_to(m[0], (B, 6, D)), m[1:1 + B]], axis=1)
        mod = jnp.pad(mod, ((0, 0), (0, 0), (0, 2), (0, 0)))
        w_ext = _take_cols(w_in[l], _EXT_COLS).astype(BF16)
        p_mla, p_hg, p_swa, p_ret = _in_proj(h, mod, norm_mix_g[l], w_ext, L)
        o_mla = _mla_mixer(p_mla, _mla_weights(mla_q_norm_g[l], mla_w_uq[l], mla_kv_norm_g[l], mla_w_ukv[l]),
                           mla_tab, L)
        hg_f, hg_b = _hg_mixer(p_hg, hg_lb[l], L)
        o_swa = _swa_mixer(p_swa, swa_sink[l], swa_tab, L)
        ret_f, ret_b = _ret_mixer(p_ret, _ret_tables(ret_log_gamma[l], L, S), L)
        w_out_p = jnp.take(w_out[l], jnp.asarray(w_out_perm), axis=0).astype(BF16)
        h = _out_proj(h, mod, o_mla, hg_f, hg_b, p_hg, o_swa, ret_f, ret_b, p_ret,
                      hg_norm_g[l], ret_gn_g[l], ret_gn_b[l], w_out_p, L)
        if l % 2 == 0:
            h = _dense_ffn(h, mod, norm_ffn_g[l], ffn_w_gate, ffn_w_up, ffn_w_down, l // 2, L,
                           tm=1152 if S % 1152 == 0 else 384)
        else:
            h = _moe_ffn(h, mod, norm_ffn_g[l], moe_w_router[l // 2], moe_w_gate, moe_w_up, moe_w_down, l // 2, L)
    return _final_norm(h, final_norm_g, L)
```

```python
import functools

import numpy as np
import jax
import jax.numpy as jnp
from jax import lax
from jax.experimental import pallas as pl
from jax.experimental.pallas import tpu as pltpu

F32 = jnp.float32
BF16 = jnp.bfloat16

D_MODEL = 1024
DEPTH = 4
GRID_W = 64
ROPE_THETA = 10000.0
NORM_EPS = 1e-6
LB_FLOOR = 1e-30
GROUP_W = 256

MLA_HEADS, MLA_NOPE, MLA_ROPE, MLA_V = 4, 64, 32, 64
MLA_Q_RANK, MLA_KV_RANK = 192, 128
HG_HEADS, HG_DK, HG_DV = 4, 64, 64
SWA_HEADS, SWA_KV_HEADS, SWA_HD, SWA_WINDOW = 4, 2, 64, 128
RET_HEADS, RET_DK, RET_DV = 4, 32, 64
N_EXPERTS, TOP_K = 8, 2

_IN_SIZES = (192, 128, 32, 256, 256, 256, 256, 256, 256, 128, 128, 128, 128, 256, 256)
_OFF = np.concatenate([[0], np.cumsum(_IN_SIZES)]).astype(np.int64)
D_IN = int(_OFF[-1])

W_MLA, W_HG, W_SWA, W_RET = 512, 1280, 896, 1024
W_EXT = W_MLA + W_HG + W_SWA + W_RET

LANE = 128
VMEM_LIMIT = 56 * 1024 * 1024


def _cparams(sem):
    return pltpu.CompilerParams(dimension_semantics=sem, vmem_limit_bytes=VMEM_LIMIT)


def _swap_halves(idx, width):
    idx = np.asarray(idx)
    base = (idx // width) * width
    j = idx % width
    return base + (j + width // 2) % width


def _ext_columns():
    z = lambda n: -np.ones((n,), np.int64)
    rng = lambda a, n: np.arange(a, a + n)
    o = _OFF
    k_pe = rng(o[2], 32)
    mla = np.concatenate([rng(o[0], 192), z(64), rng(o[1], 128), k_pe,
                          o[2] + _swap_halves(np.arange(32), 32), z(64)])
    hg = rng(o[3], 1280)
    q_heads = [rng(o[8] + 64 * h, 64) for h in range(4)]
    q_perm = np.concatenate([q_heads[0], q_heads[2], q_heads[1], q_heads[3]])
    k = rng(o[9], 128)
    v = rng(o[10], 128)
    sw = lambda cols, base: base + _swap_halves(cols - base, 64)
    swa = np.concatenate([q_perm, k, v, sw(q_perm, o[8]), sw(k, o[9])])
    rq, rk = rng(o[11], 128), rng(o[12], 128)
    ret = np.concatenate([rq, rk, o[11] + _swap_halves(rq - o[11], 32),
                          o[12] + _swap_halves(rk - o[12], 32), rng(o[13], 256), rng(o[14], 256)])
    cols = np.concatenate([mla, hg, swa, ret])
    assert cols.shape[0] == W_EXT
    return cols


_EXT_COLS = _ext_columns()


def _take_cols(w, cols):
    cols = np.asarray(cols)
    g = jnp.take(w, jnp.asarray(np.maximum(cols, 0)), axis=-1)
    return jnp.where(jnp.asarray(cols >= 0), g, jnp.zeros((), w.dtype))


def _axial_angles(n_tok, rot_dim):
    rows = n_tok // GRID_W
    row = jnp.broadcast_to(jnp.arange(rows)[:, None], (rows, GRID_W)).reshape(-1)
    col = jnp.broadcast_to(jnp.arange(GRID_W)[None, :], (rows, GRID_W)).reshape(-1)
    n_freq = rot_dim // 4
    inv = ROPE_THETA ** (-jnp.arange(n_freq, dtype=F32) / n_freq)
    return jnp.concatenate([row.astype(F32)[:, None] * inv, col.astype(F32)[:, None] * inv], axis=-1)


def _rope_tables(ang, n_ctx):
    cos, sin = jnp.cos(ang), jnp.sin(ang)
    c = jnp.concatenate([cos, cos], axis=-1)
    s = jnp.concatenate([-sin, sin], axis=-1)
    r = c.shape[-1]
    c = jnp.concatenate([jnp.ones((n_ctx, r), F32), c], axis=0)
    s = jnp.concatenate([jnp.zeros((n_ctx, r), F32), s], axis=0)
    return c, s


def _split2(x):
    hi = x.astype(BF16)
    lo = (x - hi.astype(F32)).astype(BF16)
    return hi, lo


def _seg_mean(x, ones_bd):
    hi, lo = _split2(x)
    return (jnp.dot(hi, ones_bd, preferred_element_type=F32)
            + jnp.dot(lo, ones_bd, preferred_element_type=F32))


def _silu(x):
    return x / (1.0 + jnp.exp(-x))


def _row_mod(m_ref, idx, row0, tm, n_ctx):
    rows = row0 + lax.broadcasted_iota(jnp.int32, (tm, 1), 0)
    mc = m_ref[0, 0, idx:idx + 1, :]
    ml = m_ref[0, 1, idx:idx + 1, :]
    return jnp.where(rows < n_ctx, mc, ml)


def _norm_mod(x, g, shift, scale):
    ms = jnp.mean(x * x, axis=-1, keepdims=True)
    y = x * lax.rsqrt(ms + NORM_EPS) * g
    return y * (1.0 + scale) + shift


def _mod_kernel(c_ref, w_ref, b_ref, o_ref):
    s = _silu(c_ref[...])
    o_ref[0] = jnp.dot(s.astype(BF16), w_ref[0].astype(BF16), preferred_element_type=F32) + b_ref[0]


def _modulation(cvec, w_ada, b_ada):
    R, D = cvec.shape
    depth, _, n6 = w_ada.shape
    tn = 1536
    return pl.pallas_call(
        _mod_kernel,
        grid=(depth, n6 // tn),
        in_specs=[pl.BlockSpec((R, D), lambda l, j: (0, 0)),
                  pl.BlockSpec((1, D, tn), lambda l, j: (l, 0, j)),
                  pl.BlockSpec((1, 1, tn), lambda l, j: (l, 0, j))],
        out_specs=pl.BlockSpec((1, R, tn), lambda l, j: (l, 0, j)),
        out_shape=jax.ShapeDtypeStruct((depth, R, n6), F32),
        compiler_params=_cparams(("parallel", "parallel")),
        name="adaln_mod",
    )(cvec, w_ada, b_ada.reshape(depth, 1, n6))


def _win_kernel(h_ref, m_ref, g_ref, w_ref, o_mla, o_hg, o_swa, o_ret, *, tm, n_ctx):
    row0 = pl.program_id(1) * tm
    a = _norm_mod(h_ref[0], g_ref[...], _row_mod(m_ref, 0, row0, tm, n_ctx),
                  _row_mod(m_ref, 1, row0, tm, n_ctx)).astype(BF16)
    c0 = 0
    for o_ref, w in ((o_mla, W_MLA), (o_hg, W_HG), (o_swa, W_SWA), (o_ret, W_RET)):
        o_ref[0] = jnp.dot(a, w_ref[:, c0:c0 + w], preferred_element_type=F32)
        c0 += w


def _in_proj(h, mod, g, w_ext, n_ctx, tm=384):
    B, S, D = h.shape
    widths = (W_MLA, W_HG, W_SWA, W_RET)
    return pl.pallas_call(
        functools.partial(_win_kernel, tm=tm, n_ctx=n_ctx),
        grid=(B, S // tm),
        in_specs=[pl.BlockSpec((1, tm, D), lambda b, i: (b, i, 0)),
                  pl.BlockSpec((1, 2, 8, D), lambda b, i: (b, 0, 0, 0)),
                  pl.BlockSpec((1, D), lambda b, i: (0, 0)),
                  pl.BlockSpec((D, W_EXT), lambda b, i: (0, 0))],
        out_specs=[pl.BlockSpec((1, tm, w), lambda b, i: (b, i, 0)) for w in widths],
        out_shape=[jax.ShapeDtypeStruct((B, S, w), F32) for w in widths],
        compiler_params=_cparams(("parallel", "parallel")),
        name="in_proj",
    )(h, mod, g.reshape(1, D), w_ext)


def _mla_weights(q_norm_g, w_uq, kv_norm_g, w_ukv):
    qcols, qsw, kcols = [], [], []
    for h in range(MLA_HEADS):
        b = 96 * h
        nope = np.arange(b, b + 64)
        pe = np.arange(b + 64, b + 96)
        pad = -np.ones((32,), np.int64)
        qcols.append(np.concatenate([nope, pe, pad]))
        qsw.append(np.concatenate([-np.ones((64,), np.int64), (b + 64) + _swap_halves(np.arange(32), 32), pad]))
        kcols.append(np.concatenate([np.arange(128 * h, 128 * h + 64), -np.ones((64,), np.int64)]))
    vcols = np.concatenate([np.concatenate([np.arange(128 * h + 64, 128 * h + 128), -np.ones((64,), np.int64)])
                            for h in range(MLA_HEADS)])
    pad_rows = ((0, 256 - MLA_Q_RANK), (0, 0))
    wq = jnp.pad(_take_cols(w_uq, np.concatenate(qcols)), pad_rows).astype(BF16)
    wq_sw = jnp.pad(_take_cols(w_uq, np.concatenate(qsw)), pad_rows).astype(BF16)
    wk = _take_cols(w_ukv, np.concatenate(kcols)).astype(BF16)
    wv = _take_cols(w_ukv, vcols).astype(BF16)
    e = np.zeros((128, 512), np.float32)
    for h in range(MLA_HEADS):
        for j in range(32):
            e[j, 128 * h + 64 + j] = 1.0
            e[32 + j, 128 * h + 64 + j] = 1.0
    gq = jnp.pad(q_norm_g, (0, 256 - MLA_Q_RANK)).reshape(1, 256)
    return gq, wq, wq_sw, kv_norm_g.reshape(1, 128), wk, wv, jnp.asarray(e, BF16)


def _mla_tables(n_lat, n_ctx):
    ang = _axial_angles(n_lat, MLA_ROPE)
    c32, s32 = _rope_tables(ang, n_ctx)
    S = n_lat + n_ctx
    one, zero = jnp.ones((S, 64), F32), jnp.zeros((S, 64), F32)
    cq = jnp.concatenate([one, c32, jnp.zeros((S, 32), F32)], axis=-1)
    sq = jnp.concatenate([zero, s32, jnp.zeros((S, 32), F32)], axis=-1)
    tk = jnp.concatenate([c32, s32, zero], axis=-1)
    return cq, sq, tk


def _mla_prep_kernel(p_ref, cq_ref, sq_ref, tk_ref, gq_ref, wq_ref, wqs_ref, gk_ref, wk_ref, wv_ref, e_ref,
                     q_out, k_out, v_out):
    scale = (MLA_NOPE + MLA_ROPE) ** -0.5 * float(np.log2(np.e))
    ql = p_ref[0, :, 0:256]
    msq = jnp.sum(ql * ql, axis=-1, keepdims=True) * (1.0 / MLA_Q_RANK)
    qn = (ql * lax.rsqrt(msq + NORM_EPS) * gq_ref[...]).astype(BF16)
    kvl = p_ref[0, :, 256:384]
    msk = jnp.mean(kvl * kvl, axis=-1, keepdims=True)
    kvn = (kvl * lax.rsqrt(msk + NORM_EPS) * gk_ref[...]).astype(BF16)
    q = jnp.dot(qn, wq_ref[...], preferred_element_type=F32)
    qs = jnp.dot(qn, wqs_ref[...], preferred_element_type=F32)
    cq = jnp.concatenate([cq_ref[...]] * MLA_HEADS, axis=-1)
    sq = jnp.concatenate([sq_ref[...]] * MLA_HEADS, axis=-1)
    q_out[0] = ((q * cq + qs * sq) * scale).astype(BF16)
    pe = (p_ref[0, :, 384:512] * tk_ref[...]).astype(BF16)
    k = (jnp.dot(kvn, wk_ref[...], preferred_element_type=F32)
         + jnp.dot(pe, e_ref[...], preferred_element_type=F32))
    k_out[0] = k.astype(BF16)
    lane = lax.broadcasted_iota(jnp.int32, (1, MLA_HEADS * LANE), 1) % LANE
    ones = jnp.where(lane >= MLA_V, 1.0, 0.0)
    v_out[0] = (jnp.dot(kvn, wv_ref[...], preferred_element_type=F32) + ones).astype(BF16)


def _mla_attn_kernel(q_ref, k_ref, v_ref, o_ref, *, n_ctx, tq):
    S = k_ref.shape[1]

    lo = lax.broadcasted_iota(jnp.int32, (tq, LANE), 1) < MLA_V

    def attend(nk):
        pv = []
        for h in range(MLA_HEADS):
            q = q_ref[0, :, LANE * h:LANE * (h + 1)]
            k = k_ref[0, 0:nk, LANE * h:LANE * (h + 1)]
            s = lax.dot_general(q, k, (((1,), (1,)), ((), ())), preferred_element_type=F32)
            p = jnp.exp2(s - jnp.max(s, axis=-1, keepdims=True))
            pv.append(jnp.dot(p.astype(BF16), v_ref[0, 0:nk, LANE * h:LANE * (h + 1)], preferred_element_type=F32))
        for pair in range(MLA_HEADS // 2):
            a, b = pv[2 * pair], pv[2 * pair + 1]
            o_ref[0, :, LANE * pair:LANE * (pair + 1)] = jnp.where(
                lo, a / pltpu.roll(a, MLA_V, axis=1), pltpu.roll(b, MLA_V, axis=1) / b)

    is_ctx = (pl.program_id(1) + 1) * tq <= n_ctx
    pl.when(is_ctx)(lambda: attend(n_ctx))
    pl.when(jnp.logical_not(is_ctx))(lambda: attend(S))


def _mla_mixer(p_mla, weights, tables, n_ctx, tm=384, tq=256):
    B, S, _ = p_mla.shape
    gq, wq, wq_sw, gk, wk, wv, e = weights
    cq, sq, tk = tables
    full = lambda a: pl.BlockSpec(a.shape, lambda b, i: (0,) * a.ndim)
    rows = lambda w: pl.BlockSpec((tm, w), lambda b, i: (i, 0))
    q, k, v = pl.pallas_call(
        _mla_prep_kernel,
        grid=(B, S // tm),
        in_specs=[pl.BlockSpec((1, tm, W_MLA), lambda b, i: (b, i, 0)), rows(128), rows(128), rows(128),
                  full(gq), full(wq), full(wq_sw), full(gk), full(wk), full(wv), full(e)],
        out_specs=[pl.BlockSpec((1, tm, 512), lambda b, i: (b, i, 0)),
                   pl.BlockSpec((1, tm, 512), lambda b, i: (b, i, 0)),
                   pl.BlockSpec((1, tm, 512), lambda b, i: (b, i, 0))],
        out_shape=[jax.ShapeDtypeStruct((B, S, 512), BF16)] * 3,
        compiler_params=_cparams(("parallel", "parallel")),
        name="mla_prep",
    )(p_mla, cq, sq, tk, gq, wq, wq_sw, gk, wk, wv, e)
    assert n_ctx % tq == 0 and S % tq == 0
    return pl.pallas_call(
        functools.partial(_mla_attn_kernel, n_ctx=n_ctx, tq=tq),
        grid=(B, S // tq),
        in_specs=[pl.BlockSpec((1, tq, 512), lambda b, i: (b, i, 0)),
                  pl.BlockSpec((1, S, 512), lambda b, i: (b, 0, 0)),
                  pl.BlockSpec((1, S, 512), lambda b, i: (b, 0, 0))],
        out_specs=pl.BlockSpec((1, tq, GROUP_W), lambda b, i: (b, i, 0)),
        out_shape=jax.ShapeDtypeStruct((B, S, GROUP_W), F32),
        compiler_params=_cparams(("parallel", "arbitrary")),
        name="mla_attn",
    )(q, k, v)


def _swa_tables(n_lat, n_ctx):
    ang = _axial_angles(n_lat, SWA_HD)
    c64, s64 = _rope_tables(ang, n_ctx)
    return jnp.concatenate([c64, c64], axis=-1), jnp.concatenate([s64, s64], axis=-1)


def _swa_prep_kernel(p_ref, c_ref, s_ref, q_out, k_out, v_out):
    scale = SWA_HD ** -0.5
    c, s = c_ref[...], s_ref[...]
    c2 = jnp.concatenate([c, c], axis=-1)
    s2 = jnp.concatenate([s, s], axis=-1)
    q_out[0] = ((p_ref[0, :, 0:256] * c2 + p_ref[0, :, 512:768] * s2) * scale).astype(BF16)
    k_out[0] = (p_ref[0, :, 256:384] * c + p_ref[0, :, 768:896] * s).astype(BF16)
    v_out[0] = p_ref[0, :, 384:512].astype(BF16)


def _swa_attn_kernel(sink_ref, q_ref, kc_ref, vc_ref, kp_ref, kn_ref, kx_ref, vp_ref, vn_ref, vx_ref, o_ref,
                     *, n_ctx, blk):
    j = pl.program_id(1)
    ncb = n_ctx // blk
    nlb = pl.num_programs(1) - ncb
    lane = lax.broadcasted_iota(jnp.int32, (blk, LANE), 1)
    lo = lane < SWA_HD
    qa, qb = q_ref[0, :, 0:LANE], q_ref[0, :, LANE:2 * LANE]
    zero = jnp.zeros((), BF16)
    qs = jnp.concatenate([jnp.where(lo, qa, zero), jnp.where(lo, qb, zero),
                          jnp.where(lo, zero, qa), jnp.where(lo, zero, qb)], axis=0)
    rb = lax.broadcasted_iota(jnp.int32, (4 * blk, 1), 0) // blk
    sink = jnp.where(rb == 0, sink_ref[0], jnp.where(rb == 1, sink_ref[1],
                                                     jnp.where(rb == 2, sink_ref[2], sink_ref[3])))

    def finish(s, valid, vcat):
        if valid is not None:
            s = jnp.where(valid, s, -1e30)
        m = jnp.maximum(jnp.max(s, axis=-1, keepdims=True), sink)
        p = jnp.exp(s - m)
        if valid is not None:
            p = jnp.where(valid, p, 0.0)
        l = jnp.sum(p, axis=-1, keepdims=True) + jnp.exp(sink - m)
        r = jnp.dot(p.astype(BF16), vcat, preferred_element_type=F32) / l
        o_ref[0, :, 0:LANE] = jnp.where(lo, r[0:blk], r[2 * blk:3 * blk])
        o_ref[0, :, LANE:2 * LANE] = jnp.where(lo, r[blk:2 * blk], r[3 * blk:4 * blk])

    def ctx_block():
        s = lax.dot_general(qs, kc_ref[0], (((1,), (1,)), ((), ())), preferred_element_type=F32)
        finish(s, None, vc_ref[0])

    def lat_block():
        n = j - ncb
        kcat = jnp.concatenate([kc_ref[0], kp_ref[0], kn_ref[0], kx_ref[0]], axis=0)
        vcat = jnp.concatenate([vc_ref[0], vp_ref[0], vn_ref[0], vx_ref[0]], axis=0)
        s = lax.dot_general(qs, kcat, (((1,), (1,)), ((), ())), preferred_element_type=F32)
        t = lax.broadcasted_iota(jnp.int32, (4 * blk, n_ctx + 3 * blk), 0) % blk
        c = lax.broadcasted_iota(jnp.int32, (4 * blk, n_ctx + 3 * blk), 1) - n_ctx
        in_prev = (c >= 0) & (c < blk) & (c >= t) & (n > 0)
        in_cur = (c >= blk) & (c < 2 * blk)
        in_next = (c >= 2 * blk) & (c - 2 * blk <= t) & (n < nlb - 1)
        finish(s, (c < 0) | in_prev | in_cur | in_next, vcat)

    pl.when(j < ncb)(ctx_block)
    pl.when(j >= ncb)(lat_block)


def _swa_mixer(p_swa, sink, tables, n_ctx, tm=384, blk=128):
    B, S, _ = p_swa.shape
    c, s = tables
    q, k, v = pl.pallas_call(
        _swa_prep_kernel,
        grid=(B, S // tm),
        in_specs=[pl.BlockSpec((1, tm, W_SWA), lambda b, i: (b, i, 0)),
                  pl.BlockSpec((tm, LANE), lambda b, i: (i, 0)), pl.BlockSpec((tm, LANE), lambda b, i: (i, 0))],
        out_specs=[pl.BlockSpec((1, tm, 256), lambda b, i: (b, i, 0)),
                   pl.BlockSpec((1, tm, LANE), lambda b, i: (b, i, 0)),
                   pl.BlockSpec((1, tm, LANE), lambda b, i: (b, i, 0))],
        out_shape=[jax.ShapeDtypeStruct((B, S, 256), BF16), jax.ShapeDtypeStruct((B, S, LANE), BF16),
                   jax.ShapeDtypeStruct((B, S, LANE), BF16)],
        compiler_params=_cparams(("parallel", "parallel")),
        name="swa_prep",
    )(p_swa, c, s)
    nb = S // blk
    ncb = n_ctx // blk
    cur = lambda b, j: (b, j, 0)
    prev = lambda b, j: (b, jnp.maximum(j - 1, ncb), 0)
    nxt = lambda b, j: (b, jnp.minimum(j + 1, nb - 1), 0)
    kv = lambda f: pl.BlockSpec((1, blk, LANE), f)
    ctx = pl.BlockSpec((1, n_ctx, LANE), lambda b, j: (b, 0, 0))
    return pl.pallas_call(
        functools.partial(_swa_attn_kernel, n_ctx=n_ctx, blk=blk),
        grid=(B, nb),
        in_specs=[pl.BlockSpec(memory_space=pltpu.SMEM),
                  pl.BlockSpec((1, blk, 256), cur), ctx, ctx,
                  kv(prev), kv(cur), kv(nxt), kv(prev), kv(cur), kv(nxt)],
        out_specs=pl.BlockSpec((1, blk, GROUP_W), cur),
        out_shape=jax.ShapeDtypeStruct((B, S, GROUP_W), F32),
        compiler_params=_cparams(("parallel", "arbitrary")),
        name="swa_attn",
    )(sink.astype(F32), q, k, v, k, k, k, v, v, v)


def _bwd_chunk(j, ncc, nc):
    return jnp.where(j < ncc, ncc - 1 - j, nc - 1 - (j - ncc))


def _head_stack(x, width, heads):
    lane_h = lax.broadcasted_iota(jnp.int32, x.shape, 1) // width
    zero = jnp.zeros((), x.dtype)
    return jnp.concatenate([jnp.where(lane_h == h, x, zero) for h in range(heads)], axis=0)


def _head_unstack(y, rows, width, heads):
    lane_h = lax.broadcasted_iota(jnp.int32, (rows, heads * width), 1) // width
    out = y[0:rows]
    for h in range(1, heads):
        out = jnp.where(lane_h == h, y[h * rows:(h + 1) * rows], out)
    return out


RET_CHUNK = 256


def _ret_tables(log_gamma, n_ctx, S):
    C = RET_CHUNK
    t = jnp.arange(C, dtype=F32)
    rel = t[:, None] - t[None, :]
    lg = log_gamma.astype(F32)
    lg_l = jnp.repeat(lg, RET_DK, axis=1)
    dm, xi, zeta, dcay = [], [], [], []
    for d in range(2):
        r = rel if d == 0 else -rel
        dh = jnp.where(r >= 0, jnp.exp(jnp.maximum(r, 0.0)[None] * lg[d][:, None, None]), 0.0)
        dm.append(dh.reshape(RET_HEADS * C, C))
        tq = (t + 1.0) if d == 0 else (C - t)
        tk = (C - 1.0 - t) if d == 0 else t
        xi.append(jnp.exp(tq[:, None] * lg_l[d][None, :]))
        zeta.append(jnp.exp(tk[:, None] * lg_l[d][None, :]))
        dcay.append(jnp.broadcast_to(jnp.exp(C * lg_l[d])[:, None], (RET_HEADS * RET_DK, GROUP_W)))
    mask = (np.arange(128)[:, None] // RET_DK == np.arange(256)[None, :] // RET_DV).astype(np.float32)
    pos = jnp.arange(S, dtype=F32)
    inv = ROPE_THETA ** (-jnp.arange(RET_DK // 2, dtype=F32) / (RET_DK // 2))
    ang = pos[:, None] * inv
    cos, sin = jnp.cos(ang), jnp.sin(ang)
    c = jnp.tile(jnp.concatenate([cos, cos], axis=-1), (1, RET_HEADS))
    s = jnp.tile(jnp.concatenate([-sin, sin], axis=-1), (1, RET_HEADS))
    return (jnp.stack(dm), jnp.stack(xi), jnp.stack(zeta), jnp.stack(dcay), jnp.asarray(mask), c, s)


def _ret_kernel(qkf_ref, vf_ref, cf_ref, sf_ref, qkb_ref, vb_ref, cb_ref, sb_ref,
                dm_ref, xi_ref, zeta_ref, dcay_ref, mask_ref, of_ref, ob_ref, state):
    C = RET_CHUNK

    @pl.when(pl.program_id(1) == 0)
    def _():
        state[...] = jnp.zeros_like(state)

    dirs = ((qkf_ref, vf_ref, cf_ref, sf_ref, of_ref), (qkb_ref, vb_ref, cb_ref, sb_ref, ob_ref))
    for d, (qk_ref, v_ref, c_ref, s_ref, o_ref) in enumerate(dirs):
        c, s = c_ref[...], s_ref[...]
        qr = (qk_ref[0, :, 0:128] * c + qk_ref[0, :, 256:384] * s) * (RET_DK ** -0.5)
        kr = qk_ref[0, :, 128:256] * c + qk_ref[0, :, 384:512] * s
        vb = v_ref[0].astype(BF16)
        qs = _head_stack(qr.astype(BF16), RET_DK, RET_HEADS)
        sc = lax.dot_general(qs, kr.astype(BF16), (((1,), (1,)), ((), ())), preferred_element_type=F32)
        a = (sc * dm_ref[d]).astype(BF16)
        o = _head_unstack(jnp.dot(a, vb, preferred_element_type=F32), C, RET_DV, RET_HEADS)
        st = state[d]
        o = o + jnp.dot((qr * xi_ref[d]).astype(BF16), st.astype(BF16), preferred_element_type=F32)
        o_ref[0] = o
        kz = (kr * zeta_ref[d]).astype(BF16)
        u = lax.dot_general(kz, vb, (((0,), (0,)), ((), ())), preferred_element_type=F32)
        state[d] = st * dcay_ref[d] + u * mask_ref[...]


def _ret_mixer(p_ret, tables, n_ctx):
    B, S, _ = p_ret.shape
    C = RET_CHUNK
    dm, xi, zeta, dcay, mask, c, s = tables
    nc, ncc = S // C, n_ctx // C
    fwd = lambda b, j: (b, j, 0)
    bwd = lambda b, j: (b, _bwd_chunk(j, ncc, nc), 0)
    fwd_v = lambda b, j: (b, j, 2)
    bwd_v = lambda b, j: (b, _bwd_chunk(j, ncc, nc), 2)
    fwd_t = lambda b, j: (j, 0)
    bwd_t = lambda b, j: (_bwd_chunk(j, ncc, nc), 0)
    full = lambda a: pl.BlockSpec(a.shape, lambda b, j: (0,) * a.ndim)
    return pl.pallas_call(
        _ret_kernel,
        grid=(B, nc),
        in_specs=[pl.BlockSpec((1, C, 512), fwd), pl.BlockSpec((1, C, 256), fwd_v),
                  pl.BlockSpec((C, LANE), fwd_t), pl.BlockSpec((C, LANE), fwd_t),
                  pl.BlockSpec((1, C, 512), bwd), pl.BlockSpec((1, C, 256), bwd_v),
                  pl.BlockSpec((C, LANE), bwd_t), pl.BlockSpec((C, LANE), bwd_t),
                  full(dm), full(xi), full(zeta), full(dcay), full(mask)],
        out_specs=[pl.BlockSpec((1, C, GROUP_W), fwd), pl.BlockSpec((1, C, GROUP_W), bwd)],
        out_shape=[jax.ShapeDtypeStruct((B, S, GROUP_W), F32)] * 2,
        scratch_shapes=[pltpu.VMEM((2, RET_HEADS * RET_DK, GROUP_W), F32)],
        compiler_params=_cparams(("parallel", "arbitrary")),
        name="retention_scan",
    )(p_ret, p_ret, c, s, p_ret, p_ret, c, s, dm, xi, zeta, dcay, mask)


HG_CHUNK = 128


def _hg_level_tables(C):
    levels = []
    n = 2
    while n <= C:
        levels.append(n)
        n *= 2
    t = np.arange(C)
    mats, qroles, kroles = [], [], []
    for d in range(2):
        rows = []
        qr_d, kr_d = [], []
        for n in levels:
            m = n // 2
            start = (t // n) * n
            mid = start + m
            u = t - start
            M = np.zeros((C, C), np.float32)
            for i in range(C):
                if d == 0:
                    if u[i] >= m:
                        M[i, mid[i]:i + 1] = 1.0
                    else:
                        M[i, i + 1:mid[i]] = 1.0
                else:
                    if u[i] < m:
                        M[i, i:mid[i]] = 1.0
                    else:
                        M[i, mid[i]:i] = 1.0
            rows.append(M)
            qr_d.append((u >= m) if d == 0 else (u < m))
            kr_d.append((u < m) if d == 0 else (u >= m))
        Mq = np.zeros((C, C), np.float32)
        Mk = np.zeros((C, C), np.float32)
        for i in range(C):
            if d == 0:
                Mq[i, 0:i + 1] = 1.0
                Mk[i, i + 1:C] = 1.0
            else:
                Mq[i, i:C] = 1.0
                Mk[i, 0:i] = 1.0
        rows += [Mq, Mk, np.ones((8, C), np.float32)]
        mats.append(np.concatenate(rows, axis=0))
        qroles.append(np.stack(qr_d))
        kroles.append(np.stack(kr_d))
    same_block = np.stack([(t[:, None] // n == t[None, :] // n) for n in levels]).astype(np.float32)
    return levels, np.stack(mats), np.stack(qroles), np.stack(kroles), same_block


def _hg_kernel(qf_ref, zf_ref, if_ref, qb_ref, zb_ref, ib_ref, lb_ref, m_ref, role_ref, blk_ref,
               ones_ref, mask_ref, of_ref, ob_ref, state, *, n_levels):
    C = HG_CHUNK

    @pl.when(pl.program_id(1) == 0)
    def _():
        state[...] = jnp.zeros_like(state)

    dirs = ((qf_ref, zf_ref, if_ref), (qb_ref, zb_ref, ib_ref))
    st = [state[0], state[1]]
    q, v, vb, kk, r = [], [], [], [], []
    for d, (q_ref, z_ref, i_ref) in enumerate(dirs):
        z = z_ref[0]
        lb = lb_ref[d]
        lbf = jnp.maximum(lb, LB_FLOOR)
        t = jnp.exp(-jnp.abs(z))
        rcp = 1.0 / (1.0 + t)
        pos = z >= 0.0
        sig = jnp.where(pos, rcp, t * rcp)
        nsig = jnp.where(pos, t * rcp, rcp)
        logf = jnp.log(lbf + (1.0 - lb) * sig)
        kk.append((1.0 - lb) * nsig - (lbf - lb))
        hi, lo = _split2(logf)
        r.append(jnp.dot(m_ref[d], hi, preferred_element_type=F32)
                 + jnp.dot(m_ref[d], lo, preferred_element_type=F32))
        q.append(q_ref[0] * (HG_DK ** -0.5))
        v.append(i_ref[0])
        vb.append(v[d].astype(BF16))
    acc = [jnp.zeros((HG_HEADS * C, C), F32), jnp.zeros((HG_HEADS * C, C), F32)]
    for lv in range(n_levels):
        for d in range(2):
            e = jnp.exp(r[d][lv * C:(lv + 1) * C])
            roles = role_ref[d, lv]
            qe = jnp.where(roles > 0.5, q[d] * e, 0.0).astype(BF16)
            ke = jnp.where(roles > 0.5, 0.0, kk[d] * e).astype(BF16)
            a = lax.dot_general(_head_stack(qe, HG_DK, HG_HEADS), ke, (((1,), (1,)), ((), ())),
                                preferred_element_type=F32)
            acc[d] = acc[d] + a * blk_ref[lv]
    outs = []
    for d in range(2):
        o = _head_unstack(jnp.dot(acc[d].astype(BF16), vb[d], preferred_element_type=F32), C, HG_DV, HG_HEADS)
        o = o + jnp.dot((q[d] * kk[d]).astype(BF16), ones_ref[...], preferred_element_type=F32) * v[d]
        e_q = jnp.exp(r[d][n_levels * C:(n_levels + 1) * C])
        o = o + lax.dot_general((q[d] * e_q).astype(BF16), st[d].astype(BF16), (((1,), (1,)), ((), ())),
                                preferred_element_type=F32)
        outs.append(o)
    new_state = []
    for d in range(2):
        e_k = jnp.exp(r[d][(n_levels + 1) * C:(n_levels + 2) * C])
        ku = (kk[d] * e_k).astype(BF16)
        u = lax.dot_general(vb[d], ku, (((0,), (0,)), ((), ())), preferred_element_type=F32)
        dtot = jnp.exp(r[d][(n_levels + 2) * C:(n_levels + 2) * C + 1])
        new_state.append(st[d] * dtot + u * mask_ref[...])
    of_ref[0] = outs[0]
    ob_ref[0] = outs[1]
    state[0] = new_state[0]
    state[1] = new_state[1]


def _hg_mixer(p_hg, lb, n_ctx):
    B, S, _ = p_hg.shape
    C = HG_CHUNK
    levels, mats, qroles, kroles, same_block = _hg_level_tables(C)
    nl = len(levels)
    m_all = jnp.asarray(mats, BF16)
    roles = jnp.asarray(np.broadcast_to(qroles[..., None], (2, nl, C, GROUP_W)).astype(np.float32))
    blk = jnp.asarray(np.tile(same_block, (1, HG_HEADS, 1)))
    hd = np.arange(GROUP_W) // HG_DK
    bd = (hd[:, None] == hd[None, :]).astype(np.float32)
    ones_bd = jnp.asarray(bd, BF16)
    mask = jnp.asarray(bd)
    nc, ncc = S // C, n_ctx // C
    col = lambda k, rev: (lambda b, j: (b, _bwd_chunk(j, ncc, nc) if rev else j, k))
    blkspec = lambda k, rev: pl.BlockSpec((1, C, GROUP_W), col(k, rev))
    full = lambda a: pl.BlockSpec(a.shape, lambda b, j: (0,) * a.ndim)
    lb3 = lb.reshape(2, 1, GROUP_W).astype(F32)
    return pl.pallas_call(
        functools.partial(_hg_kernel, n_levels=nl),
        grid=(B, nc),
        in_specs=[blkspec(0, False), blkspec(1, False), blkspec(3, False),
                  blkspec(0, True), blkspec(2, True), blkspec(3, True),
                  full(lb3), full(m_all), full(roles), full(blk), full(ones_bd), full(mask)],
        out_specs=[blkspec(0, False), blkspec(0, True)],
        out_shape=[jax.ShapeDtypeStruct((B, S, GROUP_W), F32)] * 2,
        scratch_shapes=[pltpu.VMEM((2, GROUP_W, GROUP_W), F32)],
        compiler_params=_cparams(("parallel", "arbitrary")),
        name="hgrn2_scan",
    )(p_hg, p_hg, p_hg, p_hg, p_hg, p_hg, lb3, m_all, roles, blk, ones_bd, mask)


_SWA_OUT_PERM = np.concatenate([np.arange(64 * h, 64 * h + 64) for h in (0, 2, 1, 3)])


def _wout_kernel(h_ref, m_ref, a_ref, hf_ref, hb_ref, hgate_ref, c_ref, rf_ref, rb_ref, rgate_ref,
                 hgn_ref, rgg_ref, rgb_ref, ones_ref, w_ref, o_ref, *, tm, n_ctx):
    row0 = pl.program_id(1) * tm
    ones = ones_ref[...]
    o = hf_ref[0] + hb_ref[0]
    yb = o * lax.rsqrt(_seg_mean(o * o, ones) + NORM_EPS) * hgn_ref[...] * _silu(hgate_ref[0])
    o = rf_ref[0] + rb_ref[0]
    xc = o - _seg_mean(o, ones)
    yd = (xc * lax.rsqrt(_seg_mean(xc * xc, ones) + NORM_EPS) * rgg_ref[...] + rgb_ref[...]) * _silu(rgate_ref[0])
    acc = jnp.dot(a_ref[0].astype(BF16), w_ref[0:256, :], preferred_element_type=F32)
    acc += jnp.dot(yb.astype(BF16), w_ref[256:512, :], preferred_element_type=F32)
    acc += jnp.dot(c_ref[0].astype(BF16), w_ref[512:768, :], preferred_element_type=F32)
    acc += jnp.dot(yd.astype(BF16), w_ref[768:1024, :], preferred_element_type=F32)
    o_ref[0] = h_ref[0] + _row_mod(m_ref, 2, row0, tm, n_ctx) * acc


def _out_proj(h, mod, o_mla, hg_f, hg_b, p_hg, o_swa, ret_f, ret_b, p_ret, hg_norm_g, ret_gn_g, ret_gn_b,
              w_out_p, n_ctx, tm=384):
    B, S, D = h.shape
    hd = np.arange(GROUP_W) // HG_DV
    ones_bd = jnp.asarray((hd[:, None] == hd[None, :]).astype(np.float32) / HG_DV, BF16)
    row = lambda w, k=0: pl.BlockSpec((1, tm, w), lambda b, i: (b, i, k))
    vec = pl.BlockSpec((1, GROUP_W), lambda b, i: (0, 0))
    return pl.pallas_call(
        functools.partial(_wout_kernel, tm=tm, n_ctx=n_ctx),
        grid=(B, S // tm),
        in_specs=[row(D), pl.BlockSpec((1, 2, 8, D), lambda b, i: (b, 0, 0, 0)),
                  row(GROUP_W), row(GROUP_W), row(GROUP_W), row(GROUP_W, 4),
                  row(GROUP_W), row(GROUP_W), row(GROUP_W), row(GROUP_W, 3),
                  vec, vec, vec, pl.BlockSpec((GROUP_W, GROUP_W), lambda b, i: (0, 0)),
                  pl.BlockSpec((D, D), lambda b, i: (0, 0))],
        out_specs=row(D),
        out_shape=jax.ShapeDtypeStruct((B, S, D), F32),
        compiler_params=_cparams(("parallel", "parallel")),
        name="out_proj",
    )(h, mod, o_mla, hg_f, hg_b, p_hg, o_swa, ret_f, ret_b, p_ret,
      hg_norm_g.reshape(1, GROUP_W), ret_gn_g.reshape(1, GROUP_W), ret_gn_b.reshape(1, GROUP_W), ones_bd, w_out_p)


def _ffn_kernel(h_ref, m_ref, g_ref, wg_ref, wu_ref, wd_ref, o_ref, xn, acc, *, tm, n_ctx):
    j = pl.program_id(2)
    row0 = pl.program_id(1) * tm

    @pl.when(j == 0)
    def _():
        xn[...] = _norm_mod(h_ref[0], g_ref[...], _row_mod(m_ref, 3, row0, tm, n_ctx),
                            _row_mod(m_ref, 4, row0, tm, n_ctx)).astype(BF16)
        acc[...] = jnp.zeros_like(acc)

    x = xn[...]
    hg = jnp.dot(x, wg_ref[0].astype(BF16), preferred_element_type=F32)
    hu = jnp.dot(x, wu_ref[0].astype(BF16), preferred_element_type=F32)
    acc[...] += jnp.dot((_silu(hg) * hu).astype(BF16), wd_ref[0].astype(BF16), preferred_element_type=F32)

    @pl.when(j == pl.num_programs(2) - 1)
    def _():
        o_ref[0] = h_ref[0] + _row_mod(m_ref, 5, row0, tm, n_ctx) * acc[...]


def _dense_ffn(h, mod, g, w_gate, w_up, w_down, li, n_ctx, tm=1152, tf=256):
    B, S, D = h.shape
    dff = w_gate.shape[-1]
    return pl.pallas_call(
        functools.partial(_ffn_kernel, tm=tm, n_ctx=n_ctx),
        grid=(B, S // tm, dff // tf),
        in_specs=[pl.BlockSpec((1, tm, D), lambda b, i, j: (b, i, 0)),
                  pl.BlockSpec((1, 2, 8, D), lambda b, i, j: (b, 0, 0, 0)),
                  pl.BlockSpec((1, D), lambda b, i, j: (0, 0)),
                  pl.BlockSpec((1, D, tf), lambda b, i, j: (li, 0, j)),
                  pl.BlockSpec((1, D, tf), lambda b, i, j: (li, 0, j)),
                  pl.BlockSpec((1, tf, D), lambda b, i, j: (li, j, 0))],
        out_specs=pl.BlockSpec((1, tm, D), lambda b, i, j: (b, i, 0)),
        out_shape=jax.ShapeDtypeStruct((B, S, D), F32),
        scratch_shapes=[pltpu.VMEM((tm, D), BF16), pltpu.VMEM((tm, D), F32)],
        compiler_params=_cparams(("parallel", "parallel", "arbitrary")),
        name="dense_ffn",
    )(h, mod, g.reshape(1, D), w_gate, w_up, w_down)


MOE_TM = 1024
MOE_TF = 512
COMBINE_ROWS = 384
DMA_UNROLL = 8
SUBLANES = 8
ZERO_ROWS = 256


def _router_kernel(h_ref, m_ref, g_ref, wr_ref, tri_ref, xn_ref, route_ref, cnt_ref, counts, *, tm, n_ctx):
    b, i = pl.program_id(0), pl.program_id(1)
    row0 = i * tm

    @pl.when((b == 0) & (i == 0))
    def _():
        counts[...] = jnp.zeros_like(counts)

    a = _norm_mod(h_ref[0], g_ref[...], _row_mod(m_ref, 3, row0, tm, n_ctx), _row_mod(m_ref, 4, row0, tm, n_ctx))
    xn_ref[0] = a
    logits = jnp.dot(a, wr_ref[...], preferred_element_type=F32, precision=lax.Precision.HIGHEST)
    lane = lax.broadcasted_iota(jnp.int32, logits.shape, 1)
    neg = jnp.float32(-jnp.inf)
    lg = jnp.where(lane < N_EXPERTS, logits, neg)
    v1 = jnp.max(lg, axis=-1, keepdims=True)
    i1 = jnp.min(jnp.where(lg == v1, lane, LANE), axis=-1, keepdims=True)
    lg2 = jnp.where(lane == i1, neg, lg)
    v2 = jnp.max(lg2, axis=-1, keepdims=True)
    i2 = jnp.min(jnp.where(lg2 == v2, lane, LANE), axis=-1, keepdims=True)
    e = jnp.exp(v2 - v1)
    w1 = 1.0 / (1.0 + e)
    w2 = e / (1.0 + e)
    oh1, oh2 = lane == i1, lane == i2
    onehot = jnp.where(oh1 | oh2, 1.0, 0.0)
    earlier = jnp.dot(tri_ref[...], onehot.astype(BF16), preferred_element_type=F32)
    pos = earlier + counts[...]
    r1 = jnp.sum(jnp.where(oh1, pos, 0.0), axis=-1, keepdims=True)
    r2 = jnp.sum(jnp.where(oh2, pos, 0.0), axis=-1, keepdims=True)
    counts[...] = counts[...] + jnp.sum(onehot, axis=0, keepdims=True)
    cnt_ref[...] = jnp.broadcast_to(counts[...], cnt_ref.shape)
    route_ref[0] = jnp.where(lane == 0, i1.astype(F32),
                             jnp.where(lane == 1, i2.astype(F32),
                                       jnp.where(lane == 2, w1,
                                                 jnp.where(lane == 3, w2,
                                                           jnp.where(lane == 4, r1, jnp.where(lane == 5, r2, 0.0))))))


def _scatter_kernel(meta_ref, slot_ref, x_ref, xs_hbm, zeros_v, sem, *, tm, tile_rows, total_rows):
    b, i = pl.program_id(0), pl.program_id(1)

    def start(r, carry):
        for k in range(TOP_K):
            pltpu.make_async_copy(x_ref.at[0, pl.ds(r, 1)], xs_hbm.at[pl.ds(slot_ref[0, 0, 0, k * tm + r], 1)],
                                  sem).start()
        return carry

    lax.fori_loop(0, tm, start, 0, unroll=DMA_UNROLL)
    for k in range(TOP_K):
        pltpu.make_async_copy(x_ref.at[0], xs_hbm.at[pl.ds(0, tm)], sem).wait()

    @pl.when((b == pl.num_programs(0) - 1) & (i == pl.num_programs(1) - 1))
    def _():
        zeros_v[...] = jnp.zeros_like(zeros_v)
        z1 = zeros_v.at[pl.ds(0, 1)]
        z8 = zeros_v.at[pl.ds(0, SUBLANES)]

        def fill(lo, end):
            mid = jnp.minimum(((lo + SUBLANES - 1) // SUBLANES) * SUBLANES, end)
            big = mid + ((end - mid) // ZERO_ROWS) * ZERO_ROWS
            n_row, n_big, n_small = mid - lo, (big - mid) // ZERO_ROWS, (end - big) // SUBLANES

            def start_row(r, c):
                pltpu.make_async_copy(z1, xs_hbm.at[pl.ds(lo + r, 1)], sem).start()
                return c

            def wait_row(r, c):
                pltpu.make_async_copy(z1, xs_hbm.at[pl.ds(0, 1)], sem).wait()
                return c

            def start_big(q, c):
                dst = xs_hbm.at[pl.ds(pl.multiple_of(mid + q * ZERO_ROWS, SUBLANES), ZERO_ROWS)]
                pltpu.make_async_copy(zeros_v, dst, sem).start()
                return c

            def wait_big(q, c):
                pltpu.make_async_copy(zeros_v, xs_hbm.at[pl.ds(0, ZERO_ROWS)], sem).wait()
                return c

            def start_small(q, c):
                dst = xs_hbm.at[pl.ds(pl.multiple_of(big + q * SUBLANES, SUBLANES), SUBLANES)]
                pltpu.make_async_copy(z8, dst, sem).start()
                return c

            def wait_small(q, c):
                pltpu.make_async_copy(z8, xs_hbm.at[pl.ds(0, SUBLANES)], sem).wait()
                return c

            lax.fori_loop(0, n_row, start_row, 0)
            lax.fori_loop(0, n_big, start_big, 0)
            lax.fori_loop(0, n_small, start_small, 0)
            lax.fori_loop(0, n_row, wait_row, 0)
            lax.fori_loop(0, n_big, wait_big, 0)
            lax.fori_loop(0, n_small, wait_small, 0)

        for e in range(N_EXPERTS):
            c = meta_ref[e]
            g0 = meta_ref[N_EXPERTS + e]
            fill(g0 + c, g0 + ((c + tile_rows - 1) // tile_rows) * tile_rows)
        fill(meta_ref[2 * N_EXPERTS] * tile_rows, jnp.int32(total_rows))


def _expert_kernel(te_ref, nu_ref, x_ref, wg_ref, wu_ref, wd_ref, o_ref, acc):
    i, j = pl.program_id(0), pl.program_id(1)
    used = i < nu_ref[0]

    @pl.when(j == 0)
    def _():
        acc[...] = jnp.zeros_like(acc)

    @pl.when(used)
    def _():
        x = x_ref[...].astype(BF16)
        hg = jnp.dot(x, wg_ref[0, 0].astype(BF16), preferred_element_type=F32)
        hu = jnp.dot(x, wu_ref[0, 0].astype(BF16), preferred_element_type=F32)
        acc[...] += jnp.dot((_silu(hg) * hu).astype(BF16), wd_ref[0, 0].astype(BF16), preferred_element_type=F32)

    @pl.when(j == pl.num_programs(1) - 1)
    def _():
        o_ref[...] = acc[...]


def _combine_kernel(slot_ref, h_ref, m_ref, route_ref, y_hbm, o_ref, buf, sem, *, tm, n_ctx):
    row0 = pl.program_id(1) * tm

    def start(r, carry):
        for k in range(TOP_K):
            pltpu.make_async_copy(y_hbm.at[pl.ds(slot_ref[0, 0, 0, k * tm + r], 1)], buf.at[k, pl.ds(r, 1)], sem).start()
        return carry

    lax.fori_loop(0, tm, start, 0, unroll=DMA_UNROLL)
    for k in range(TOP_K):
        pltpu.make_async_copy(y_hbm.at[pl.ds(0, tm)], buf.at[k], sem).wait()
    w1 = route_ref[0, :, 2:3]
    w2 = route_ref[0, :, 3:4]
    o_ref[0] = h_ref[0] + _row_mod(m_ref, 5, row0, tm, n_ctx) * (w1 * buf[0] + w2 * buf[1])


def _moe_ffn(h, mod, g, w_router, w_gate, w_up, w_down, li, n_ctx, tm=384):
    B, S, D = h.shape
    T = B * S
    dffe = w_gate.shape[-1]
    te_rows = MOE_TM if T % MOE_TM == 0 else 512
    n_tiles = TOP_K * T // te_rows + N_EXPERTS
    P = n_tiles * te_rows
    wr = jnp.pad(w_router, ((0, 0), (0, LANE - N_EXPERTS)))
    tri = jnp.asarray(np.tril(np.ones((tm, tm), np.float32), -1), BF16)
    xn, route, cnt = pl.pallas_call(
        functools.partial(_router_kernel, tm=tm, n_ctx=n_ctx),
        grid=(B, S // tm),
        in_specs=[pl.BlockSpec((1, tm, D), lambda b, i: (b, i, 0)),
                  pl.BlockSpec((1, 2, 8, D), lambda b, i: (b, 0, 0, 0)),
                  pl.BlockSpec((1, D), lambda b, i: (0, 0)),
                  pl.BlockSpec((D, LANE), lambda b, i: (0, 0)),
                  pl.BlockSpec((tm, tm), lambda b, i: (0, 0))],
        out_specs=[pl.BlockSpec((1, tm, D), lambda b, i: (b, i, 0)),
                   pl.BlockSpec((1, tm, LANE), lambda b, i: (b, i, 0)),
                   pl.BlockSpec((SUBLANES, LANE), lambda b, i: (0, 0))],
        out_shape=[jax.ShapeDtypeStruct((B, S, D), F32), jax.ShapeDtypeStruct((B, S, LANE), F32),
                   jax.ShapeDtypeStruct((SUBLANES, LANE), F32)],
        scratch_shapes=[pltpu.VMEM((1, LANE), F32)],
        compiler_params=_cparams(("arbitrary", "arbitrary")),
        name="moe_router",
    )(h, mod, g.reshape(1, D), wr, tri)

    counts = cnt[0, :N_EXPERTS].astype(jnp.int32)
    ptiles = (counts + te_rows - 1) // te_rows
    tile_end = jnp.cumsum(ptiles)
    gstart = (tile_end - ptiles) * te_rows
    n_used = tile_end[-1:].astype(jnp.int32)
    tile_ids = jnp.arange(n_tiles, dtype=jnp.int32)
    tile_e = jnp.minimum(jnp.sum((tile_ids[:, None] >= tile_end[None, :]).astype(jnp.int32), axis=1),
                         N_EXPERTS - 1)
    meta = jnp.concatenate([counts, gstart, n_used])
    slots = jnp.take(gstart, route[..., 0:TOP_K].astype(jnp.int32)) + route[..., 4:4 + TOP_K].astype(jnp.int32)
    slots = slots.reshape(B, S // COMBINE_ROWS, COMBINE_ROWS, TOP_K).transpose(0, 1, 3, 2)
    slots = slots.reshape(B, S // COMBINE_ROWS, 1, TOP_K * COMBINE_ROWS)

    xs = pl.pallas_call(
        functools.partial(_scatter_kernel, tm=COMBINE_ROWS, tile_rows=te_rows, total_rows=P),
        grid_spec=pltpu.PrefetchScalarGridSpec(
            num_scalar_prefetch=1,
            grid=(B, S // COMBINE_ROWS),
            in_specs=[pl.BlockSpec((1, 1, 1, TOP_K * COMBINE_ROWS), lambda b, i, mt: (b, i, 0, 0),
                                   memory_space=pltpu.SMEM),
                      pl.BlockSpec((1, COMBINE_ROWS, D), lambda b, i, mt: (b, i, 0))],
            out_specs=pl.BlockSpec(memory_space=pl.ANY),
            scratch_shapes=[pltpu.VMEM((ZERO_ROWS, D), F32), pltpu.SemaphoreType.DMA(())]),
        out_shape=jax.ShapeDtypeStruct((P, D), F32),
        compiler_params=_cparams(("arbitrary", "arbitrary")),
        name="moe_scatter",
    )(meta, slots, xn)

    nf = dffe // MOE_TF
    last = lambda i, nu: jnp.minimum(i, nu[0] - 1)
    jj = lambda i, j, nu: jnp.where(i < nu[0], j, nf - 1)
    ys = pl.pallas_call(
        _expert_kernel,
        grid_spec=pltpu.PrefetchScalarGridSpec(
            num_scalar_prefetch=2,
            grid=(n_tiles, nf),
            in_specs=[pl.BlockSpec((te_rows, D), lambda i, j, te, nu: (last(i, nu), 0)),
                      pl.BlockSpec((1, 1, D, MOE_TF), lambda i, j, te, nu: (li, te[last(i, nu)], 0, jj(i, j, nu))),
                      pl.BlockSpec((1, 1, D, MOE_TF), lambda i, j, te, nu: (li, te[last(i, nu)], 0, jj(i, j, nu))),
                      pl.BlockSpec((1, 1, MOE_TF, D), lambda i, j, te, nu: (li, te[last(i, nu)], jj(i, j, nu), 0))],
            out_specs=pl.BlockSpec((te_rows, D), lambda i, j, te, nu: (i, 0)),
            scratch_shapes=[pltpu.VMEM((te_rows, D), F32)]),
        out_shape=jax.ShapeDtypeStruct((P, D), F32),
        compiler_params=_cparams(("arbitrary", "arbitrary")),
        name="moe_experts",
    )(tile_e, n_used, xs, w_gate, w_up, w_down)
    return pl.pallas_call(
        functools.partial(_combine_kernel, tm=COMBINE_ROWS, n_ctx=n_ctx),
        grid=(B, S // COMBINE_ROWS),
        in_specs=[pl.BlockSpec((1, 1, 1, TOP_K * COMBINE_ROWS), lambda b, i: (b, i, 0, 0), memory_space=pltpu.SMEM),
                  pl.BlockSpec((1, COMBINE_ROWS, D), lambda b, i: (b, i, 0)),
                  pl.BlockSpec((1, 2, 8, D), lambda b, i: (b, 0, 0, 0)),
                  pl.BlockSpec((1, COMBINE_ROWS, LANE), lambda b, i: (b, i, 0)),
                  pl.BlockSpec(memory_space=pl.ANY)],
        out_specs=pl.BlockSpec((1, COMBINE_ROWS, D), lambda b, i: (b, i, 0)),
        out_shape=jax.ShapeDtypeStruct((B, S, D), F32),
        scratch_shapes=[pltpu.VMEM((TOP_K, COMBINE_ROWS, D), F32), pltpu.SemaphoreType.DMA(())],
        compiler_params=_cparams(("arbitrary", "arbitrary")),
        name="moe_combine",
    )(slots, h, mod, route, ys)


def _final_norm_kernel(h_ref, g_ref, o_ref):
    x = h_ref[0]
    o_ref[0] = x * lax.rsqrt(jnp.mean(x * x, axis=-1, keepdims=True) + NORM_EPS) * g_ref[...]


def _final_norm(h, g, n_ctx, tm=256):
    B, S, D = h.shape
    N = S - n_ctx
    return pl.pallas_call(
        _final_norm_kernel,
        grid=(B, N // tm),
        in_specs=[pl.BlockSpec((1, tm, D), lambda b, i: (b, n_ctx // tm + i, 0)),
                  pl.BlockSpec((1, D), lambda b, i: (0, 0))],
        out_specs=pl.BlockSpec((1, tm, D), lambda b, i: (b, i, 0)),
        out_shape=jax.ShapeDtypeStruct((B, N, D), F32),
        compiler_params=_cparams(("parallel", "parallel")),
        name="final_norm",
    )(h, g.reshape(1, D))


MOD_ROWS = 16


def kernel(x, c, ctx, c_ctx, w_ada, b_ada, norm_mix_g, norm_ffn_g, w_in, w_out, mla_q_norm_g, mla_w_uq, mla_kv_norm_g, mla_w_ukv, hg_lb_logits, hg_norm_g, swa_sink, ret_decay_logit, ret_gn_g, ret_gn_b, ffn_w_gate, ffn_w_up, ffn_w_down, moe_w_router, moe_w_gate, moe_w_up, moe_w_down, final_norm_g):
    B, N, D = x.shape
    L = ctx.shape[1]
    S = L + N
    assert B + 1 <= MOD_ROWS
    h = jnp.concatenate([ctx, x], axis=1)
    cvec = jnp.zeros((MOD_ROWS, D), F32).at[0].set(c_ctx).at[1:1 + B].set(c)
    mod_all = _modulation(cvec, w_ada, b_ada).reshape(DEPTH, MOD_ROWS, 6, D)

    lb_p = jax.nn.softmax(hg_lb_logits.astype(F32), axis=0)
    hg_lb = jnp.cumsum(lb_p, axis=0) - lb_p[0:1]
    ret_log_gamma = jax.nn.log_sigmoid(ret_decay_logit.astype(F32))

    mla_tab = _mla_tables(N, L)
    swa_tab = _swa_tables(N, L)
    w_out_perm = np.concatenate([np.arange(512), 512 + _SWA_OUT_PERM, np.arange(768, 1024)])

    for l in range(DEPTH):
        m = mod_all[l]
        mod = jnp.stack([jnp.broadcast_to(m[0], (B, 6, D)), m[1:1 + B]], axis=1)
        mod = jnp.pad(mod, ((0, 0), (0, 0), (0, 2), (0, 0)))
        w_ext = _take_cols(w_in[l], _EXT_COLS).astype(BF16)
        p_mla, p_hg, p_swa, p_ret = _in_proj(h, mod, norm_mix_g[l], w_ext, L)
        o_mla = _mla_mixer(p_mla, _mla_weights(mla_q_norm_g[l], mla_w_uq[l], mla_kv_norm_g[l], mla_w_ukv[l]),
                           mla_tab, L)
        hg_f, hg_b = _hg_mixer(p_hg, hg_lb[l], L)
        o_swa = _swa_mixer(p_swa, swa_sink[l], swa_tab, L)
        ret_f, ret_b = _ret_mixer(p_ret, _ret_tables(ret_log_gamma[l], L, S), L)
        w_out_p = jnp.take(w_out[l], jnp.asarray(w_out_perm), axis=0).astype(BF16)
        h = _out_proj(h, mod, o_mla, hg_f, hg_b, p_hg, o_swa, ret_f, ret_b, p_ret,
                      hg_norm_g[l], ret_gn_g[l], ret_gn_b[l], w_out_p, L)
        if l % 2 == 0:
            h = _dense_ffn(h, mod, norm_ffn_g[l], ffn_w_gate, ffn_w_up, ffn_w_down, l // 2, L,
                           tm=1152 if S % 1152 == 0 else 384)
        else:
            h = _moe_ffn(h, mod, norm_ffn_g[l], moe_w_router[l // 2], moe_w_gate, moe_w_up, moe_w_down, l // 2, L)
    return _final_norm(h, final_norm_g, L)
```

```python
import functools
import math

import numpy as np
import jax
import jax.numpy as jnp
from jax import lax
from jax.experimental import pallas as pl
from jax.experimental.pallas import tpu as pltpu

F32 = jnp.float32
BF16 = jnp.bfloat16

D_MODEL = 1024
DEPTH = 4
GRID_W = 64
ROPE_THETA = 10000.0
NORM_EPS = 1e-6
LB_FLOOR = 1e-30
GROUP_W = 256

MLA_HEADS, MLA_NOPE, MLA_ROPE, MLA_V = 4, 64, 32, 64
MLA_Q_RANK, MLA_KV_RANK = 192, 128
HG_HEADS, HG_DK, HG_DV = 4, 64, 64
SWA_HEADS, SWA_KV_HEADS, SWA_HD, SWA_WINDOW = 4, 2, 64, 128
RET_HEADS, RET_DK, RET_DV = 4, 32, 64
N_EXPERTS, TOP_K = 8, 2

_IN_SIZES = (192, 128, 32, 256, 256, 256, 256, 256, 256, 128, 128, 128, 128, 256, 256)
_OFF = np.concatenate([[0], np.cumsum(_IN_SIZES)]).astype(np.int64)
D_IN = int(_OFF[-1])

W_MLA, W_HG, W_SWA, W_RET = 512, 1280, 896, 1024
W_EXT = W_MLA + W_HG + W_SWA + W_RET

LANE = 128
VMEM_LIMIT = 56 * 1024 * 1024


def _cparams(sem):
    return pltpu.CompilerParams(dimension_semantics=sem, vmem_limit_bytes=VMEM_LIMIT)


def _swap_halves(idx, width):
    idx = np.asarray(idx)
    base = (idx // width) * width
    j = idx % width
    return base + (j + width // 2) % width


def _ext_columns():
    z = lambda n: -np.ones((n,), np.int64)
    rng = lambda a, n: np.arange(a, a + n)
    o = _OFF
    k_pe = rng(o[2], 32)
    mla = np.concatenate([rng(o[0], 192), z(64), rng(o[1], 128), k_pe,
                          o[2] + _swap_halves(np.arange(32), 32), z(64)])
    hg = rng(o[3], 1280)
    q_heads = [rng(o[8] + 64 * h, 64) for h in range(4)]
    q_perm = np.concatenate([q_heads[0], q_heads[2], q_heads[1], q_heads[3]])
    k = rng(o[9], 128)
    v = rng(o[10], 128)
    sw = lambda cols, base: base + _swap_halves(cols - base, 64)
    swa = np.concatenate([q_perm, k, v, sw(q_perm, o[8]), sw(k, o[9])])
    rq, rk = rng(o[11], 128), rng(o[12], 128)
    ret = np.concatenate([rq, rk, o[11] + _swap_halves(rq - o[11], 32),
                          o[12] + _swap_halves(rk - o[12], 32), rng(o[13], 256), rng(o[14], 256)])
    cols = np.concatenate([mla, hg, swa, ret])
    assert cols.shape[0] == W_EXT
    return cols


_EXT_COLS = _ext_columns()


def _take_cols(w, cols):
    cols = np.asarray(cols)
    g = jnp.take(w, jnp.asarray(np.maximum(cols, 0)), axis=-1)
    return jnp.where(jnp.asarray(cols >= 0), g, jnp.zeros((), w.dtype))


def _axial_angles(n_tok, rot_dim):
    rows = n_tok // GRID_W
    row = jnp.broadcast_to(jnp.arange(rows)[:, None], (rows, GRID_W)).reshape(-1)
    col = jnp.broadcast_to(jnp.arange(GRID_W)[None, :], (rows, GRID_W)).reshape(-1)
    n_freq = rot_dim // 4
    inv = ROPE_THETA ** (-jnp.arange(n_freq, dtype=F32) / n_freq)
    return jnp.concatenate([row.astype(F32)[:, None] * inv, col.astype(F32)[:, None] * inv], axis=-1)


def _rope_tables(ang, n_ctx):
    cos, sin = jnp.cos(ang), jnp.sin(ang)
    c = jnp.concatenate([cos, cos], axis=-1)
    s = jnp.concatenate([-sin, sin], axis=-1)
    r = c.shape[-1]
    c = jnp.concatenate([jnp.ones((n_ctx, r), F32), c], axis=0)
    s = jnp.concatenate([jnp.zeros((n_ctx, r), F32), s], axis=0)
    return c, s


def _split2(x):
    hi = x.astype(BF16)
    lo = (x - hi.astype(F32)).astype(BF16)
    return hi, lo


def _seg_mean(x, ones_bd):
    hi, lo = _split2(x)
    return (jnp.dot(hi, ones_bd, preferred_element_type=F32)
            + jnp.dot(lo, ones_bd, preferred_element_type=F32))


def _silu(x):
    return x / (1.0 + jnp.exp(-x))


def _row_mod(m_ref, idx, row0, tm, n_ctx):
    rows = row0 + lax.broadcasted_iota(jnp.int32, (tm, 1), 0)
    mc = m_ref[0, 0, idx:idx + 1, :]
    ml = m_ref[0, 1, idx:idx + 1, :]
    return jnp.where(rows < n_ctx, mc, ml)


def _norm_mod(x, g, shift, scale):
    ms = jnp.mean(x * x, axis=-1, keepdims=True)
    y = x * lax.rsqrt(ms + NORM_EPS) * g
    return y * (1.0 + scale) + shift


def _mod_kernel(c_ref, w_ref, b_ref, o_ref):
    s = _silu(c_ref[...])
    o_ref[0] = jnp.dot(s.astype(BF16), w_ref[0].astype(BF16), preferred_element_type=F32) + b_ref[0]


def _modulation(cvec, w_ada, b_ada):
    R, D = cvec.shape
    depth, _, n6 = w_ada.shape
    tn = 1536
    return pl.pallas_call(
        _mod_kernel,
        grid=(depth, n6 // tn),
        in_specs=[pl.BlockSpec((R, D), lambda l, j: (0, 0)),
                  pl.BlockSpec((1, D, tn), lambda l, j: (l, 0, j)),
                  pl.BlockSpec((1, 1, tn), lambda l, j: (l, 0, j))],
        out_specs=pl.BlockSpec((1, R, tn), lambda l, j: (l, 0, j)),
        out_shape=jax.ShapeDtypeStruct((depth, R, n6), F32),
        compiler_params=_cparams(("parallel", "parallel")),
        name="adaln_mod",
    )(cvec, w_ada, b_ada.reshape(depth, 1, n6))


def _win_kernel(h_ref, m_ref, g_ref, w_ref, o_mla, o_hg, o_swa, o_ret, *, tm, n_ctx):
    row0 = pl.program_id(1) * tm
    a = _norm_mod(h_ref[0], g_ref[...], _row_mod(m_ref, 0, row0, tm, n_ctx),
                  _row_mod(m_ref, 1, row0, tm, n_ctx)).astype(BF16)
    c0 = 0
    for o_ref, w in ((o_mla, W_MLA), (o_hg, W_HG), (o_swa, W_SWA), (o_ret, W_RET)):
        o_ref[0] = jnp.dot(a, w_ref[:, c0:c0 + w], preferred_element_type=F32)
        c0 += w


def _in_proj(h, mod, g, w_ext, n_ctx, tm=384):
    B, S, D = h.shape
    widths = (W_MLA, W_HG, W_SWA, W_RET)
    return pl.pallas_call(
        functools.partial(_win_kernel, tm=tm, n_ctx=n_ctx),
        grid=(B, S // tm),
        in_specs=[pl.BlockSpec((1, tm, D), lambda b, i: (b, i, 0)),
                  pl.BlockSpec((1, 2, 8, D), lambda b, i: (b, 0, 0, 0)),
                  pl.BlockSpec((1, D), lambda b, i: (0, 0)),
                  pl.BlockSpec((D, W_EXT), lambda b, i: (0, 0))],
        out_specs=[pl.BlockSpec((1, tm, w), lambda b, i: (b, i, 0)) for w in widths],
        out_shape=[jax.ShapeDtypeStruct((B, S, w), F32) for w in widths],
        compiler_params=_cparams(("parallel", "parallel")),
        name="in_proj",
    )(h, mod, g.reshape(1, D), w_ext)


def _mla_weights(q_norm_g, w_uq, kv_norm_g, w_ukv):
    qcols, qsw, kcols = [], [], []
    for h in range(MLA_HEADS):
        b = 96 * h
        nope = np.arange(b, b + 64)
        pe = np.arange(b + 64, b + 96)
        pad = -np.ones((32,), np.int64)
        qcols.append(np.concatenate([nope, pe, pad]))
        qsw.append(np.concatenate([-np.ones((64,), np.int64), (b + 64) + _swap_halves(np.arange(32), 32), pad]))
        kcols.append(np.concatenate([np.arange(128 * h, 128 * h + 64), -np.ones((64,), np.int64)]))
    vcols = np.concatenate([np.concatenate([np.arange(128 * h + 64, 128 * h + 128), -np.ones((64,), np.int64)])
                            for h in range(MLA_HEADS)])
    pad_rows = ((0, 256 - MLA_Q_RANK), (0, 0))
    wq = jnp.pad(_take_cols(w_uq, np.concatenate(qcols)), pad_rows).astype(BF16)
    wq_sw = jnp.pad(_take_cols(w_uq, np.concatenate(qsw)), pad_rows).astype(BF16)
    wk = _take_cols(w_ukv, np.concatenate(kcols)).astype(BF16)
    wv = _take_cols(w_ukv, vcols).astype(BF16)
    e = np.zeros((128, 512), np.float32)
    for h in range(MLA_HEADS):
        for j in range(32):
            e[j, 128 * h + 64 + j] = 1.0
            e[32 + j, 128 * h + 64 + j] = 1.0
    gq = jnp.pad(q_norm_g, (0, 256 - MLA_Q_RANK)).reshape(1, 256)
    return gq, wq, wq_sw, kv_norm_g.reshape(1, 128), wk, wv, jnp.asarray(e, BF16)


def _mla_tables(n_lat, n_ctx):
    ang = _axial_angles(n_lat, MLA_ROPE)
    c32, s32 = _rope_tables(ang, n_ctx)
    S = n_lat + n_ctx
    one, zero = jnp.ones((S, 64), F32), jnp.zeros((S, 64), F32)
    cq = jnp.concatenate([one, c32, jnp.zeros((S, 32), F32)], axis=-1)
    sq = jnp.concatenate([zero, s32, jnp.zeros((S, 32), F32)], axis=-1)
    tk = jnp.concatenate([c32, s32, zero], axis=-1)
    return cq, sq, tk


def _mla_prep_kernel(p_ref, cq_ref, sq_ref, tk_ref, gq_ref, wq_ref, wqs_ref, gk_ref, wk_ref, wv_ref, e_ref,
                     q_out, k_out, v_out):
    scale = (MLA_NOPE + MLA_ROPE) ** -0.5 * float(np.log2(np.e))
    ql = p_ref[0, :, 0:256]
    msq = jnp.sum(ql * ql, axis=-1, keepdims=True) * (1.0 / MLA_Q_RANK)
    qn = (ql * lax.rsqrt(msq + NORM_EPS) * gq_ref[...]).astype(BF16)
    kvl = p_ref[0, :, 256:384]
    msk = jnp.mean(kvl * kvl, axis=-1, keepdims=True)
    kvn = (kvl * lax.rsqrt(msk + NORM_EPS) * gk_ref[...]).astype(BF16)
    q = jnp.dot(qn, wq_ref[...], preferred_element_type=F32)
    qs = jnp.dot(qn, wqs_ref[...], preferred_element_type=F32)
    cq = jnp.concatenate([cq_ref[...]] * MLA_HEADS, axis=-1)
    sq = jnp.concatenate([sq_ref[...]] * MLA_HEADS, axis=-1)
    q_out[0] = ((q * cq + qs * sq) * scale).astype(BF16)
    pe = (p_ref[0, :, 384:512] * tk_ref[...]).astype(BF16)
    k = (jnp.dot(kvn, wk_ref[...], preferred_element_type=F32)
         + jnp.dot(pe, e_ref[...], preferred_element_type=F32))
    k_out[0] = k.astype(BF16)
    lane = lax.broadcasted_iota(jnp.int32, (1, MLA_HEADS * LANE), 1) % LANE
    ones = jnp.where(lane >= MLA_V, 1.0, 0.0)
    v_out[0] = (jnp.dot(kvn, wv_ref[...], preferred_element_type=F32) + ones).astype(BF16)


def _mla_attn_kernel(q_ref, k_ref, v_ref, o_ref, *, n_ctx, tq):
    S = k_ref.shape[1]

    lo = lax.broadcasted_iota(jnp.int32, (tq, LANE), 1) < MLA_V

    def attend(nk):
        pv = []
        for h in range(MLA_HEADS):
            q = q_ref[0, :, LANE * h:LANE * (h + 1)]
            k = k_ref[0, 0:nk, LANE * h:LANE * (h + 1)]
            s = lax.dot_general(q, k, (((1,), (1,)), ((), ())), preferred_element_type=F32)
            p = jnp.exp2(s - jnp.max(s, axis=-1, keepdims=True))
            pv.append(jnp.dot(p.astype(BF16), v_ref[0, 0:nk, LANE * h:LANE * (h + 1)], preferred_element_type=F32))
        for pair in range(MLA_HEADS // 2):
            a, b = pv[2 * pair], pv[2 * pair + 1]
            o_ref[0, :, LANE * pair:LANE * (pair + 1)] = jnp.where(
                lo, a / pltpu.roll(a, MLA_V, axis=1), pltpu.roll(b, MLA_V, axis=1) / b)

    is_ctx = (pl.program_id(1) + 1) * tq <= n_ctx
    pl.when(is_ctx)(lambda: attend(n_ctx))
    pl.when(jnp.logical_not(is_ctx))(lambda: attend(S))


def _mla_mixer(p_mla, weights, tables, n_ctx, tm=384, tq=256):
    B, S, _ = p_mla.shape
    gq, wq, wq_sw, gk, wk, wv, e = weights
    cq, sq, tk = tables
    full = lambda a: pl.BlockSpec(a.shape, lambda b, i: (0,) * a.ndim)
    rows = lambda w: pl.BlockSpec((tm, w), lambda b, i: (i, 0))
    q, k, v = pl.pallas_call(
        _mla_prep_kernel,
        grid=(B, S // tm),
        in_specs=[pl.BlockSpec((1, tm, W_MLA), lambda b, i: (b, i, 0)), rows(128), rows(128), rows(128),
                  full(gq), full(wq), full(wq_sw), full(gk), full(wk), full(wv), full(e)],
        out_specs=[pl.BlockSpec((1, tm, 512), lambda b, i: (b, i, 0)),
                   pl.BlockSpec((1, tm, 512), lambda b, i: (b, i, 0)),
                   pl.BlockSpec((1, tm, 512), lambda b, i: (b, i, 0))],
        out_shape=[jax.ShapeDtypeStruct((B, S, 512), BF16)] * 3,
        compiler_params=_cparams(("parallel", "parallel")),
        name="mla_prep",
    )(p_mla, cq, sq, tk, gq, wq, wq_sw, gk, wk, wv, e)
    assert n_ctx % tq == 0 and S % tq == 0
    return pl.pallas_call(
        functools.partial(_mla_attn_kernel, n_ctx=n_ctx, tq=tq),
        grid=(B, S // tq),
        in_specs=[pl.BlockSpec((1, tq, 512), lambda b, i: (b, i, 0)),
                  pl.BlockSpec((1, S, 512), lambda b, i: (b, 0, 0)),
                  pl.BlockSpec((1, S, 512), lambda b, i: (b, 0, 0))],
        out_specs=pl.BlockSpec((1, tq, GROUP_W), lambda b, i: (b, i, 0)),
        out_shape=jax.ShapeDtypeStruct((B, S, GROUP_W), F32),
        compiler_params=_cparams(("parallel", "arbitrary")),
        name="mla_attn",
    )(q, k, v)


def _swa_tables(n_lat, n_ctx):
    ang = _axial_angles(n_lat, SWA_HD)
    c64, s64 = _rope_tables(ang, n_ctx)
    return jnp.concatenate([c64, c64], axis=-1), jnp.concatenate([s64, s64], axis=-1)


def _swa_prep_kernel(p_ref, c_ref, s_ref, q_out, k_out, v_out):
    scale = SWA_HD ** -0.5
    c, s = c_ref[...], s_ref[...]
    c2 = jnp.concatenate([c, c], axis=-1)
    s2 = jnp.concatenate([s, s], axis=-1)
    q_out[0] = ((p_ref[0, :, 0:256] * c2 + p_ref[0, :, 512:768] * s2) * scale).astype(BF16)
    k_out[0] = (p_ref[0, :, 256:384] * c + p_ref[0, :, 768:896] * s).astype(BF16)
    v_out[0] = p_ref[0, :, 384:512].astype(BF16)


def _swa_attn_kernel(sink_ref, q_ref, kc_ref, vc_ref, kp_ref, kn_ref, kx_ref, vp_ref, vn_ref, vx_ref, bias_ref,
                     o_ref, *, n_ctx, blk):
    j = pl.program_id(1)
    ncb = n_ctx // blk
    lane = lax.broadcasted_iota(jnp.int32, (blk, LANE), 1)
    lo = lane < SWA_HD
    qa, qb = q_ref[0, :, 0:LANE], q_ref[0, :, LANE:2 * LANE]
    zero = jnp.zeros((), BF16)
    qs = jnp.concatenate([jnp.where(lo, qa, zero), jnp.where(lo, qb, zero),
                          jnp.where(lo, zero, qa), jnp.where(lo, zero, qb)], axis=0)
    rb = lax.broadcasted_iota(jnp.int32, (4 * blk, 1), 0) // blk
    sink = jnp.where(rb == 0, sink_ref[0], jnp.where(rb == 1, sink_ref[1],
                                                     jnp.where(rb == 2, sink_ref[2], sink_ref[3])))

    def finish(s, vcat):
        m = jnp.maximum(jnp.max(s, axis=-1, keepdims=True), sink)
        p = jnp.exp(s - m)
        l = jnp.sum(p, axis=-1, keepdims=True) + jnp.exp(sink - m)
        r = jnp.dot(p.astype(BF16), vcat, preferred_element_type=F32) / l
        o_ref[0, :, 0:LANE] = jnp.where(lo, r[0:blk], r[2 * blk:3 * blk])
        o_ref[0, :, LANE:2 * LANE] = jnp.where(lo, r[blk:2 * blk], r[3 * blk:4 * blk])

    def ctx_block():
        s = lax.dot_general(qs, kc_ref[0], (((1,), (1,)), ((), ())), preferred_element_type=F32)
        finish(s, vc_ref[0])

    def lat_block():
        kcat = jnp.concatenate([kc_ref[0], kp_ref[0], kn_ref[0], kx_ref[0]], axis=0)
        vcat = jnp.concatenate([vc_ref[0], vp_ref[0], vn_ref[0], vx_ref[0]], axis=0)
        s = lax.dot_general(qs, kcat, (((1,), (1,)), ((), ())), preferred_element_type=F32)
        finish(s + jnp.concatenate([bias_ref[0]] * SWA_HEADS, axis=0), vcat)

    pl.when(j < ncb)(ctx_block)
    pl.when(j >= ncb)(lat_block)


def _swa_mixer(p_swa, sink, tables, n_ctx, tm=384, blk=128):
    B, S, _ = p_swa.shape
    c, s = tables
    q, k, v = pl.pallas_call(
        _swa_prep_kernel,
        grid=(B, S // tm),
        in_specs=[pl.BlockSpec((1, tm, W_SWA), lambda b, i: (b, i, 0)),
                  pl.BlockSpec((tm, LANE), lambda b, i: (i, 0)), pl.BlockSpec((tm, LANE), lambda b, i: (i, 0))],
        out_specs=[pl.BlockSpec((1, tm, 256), lambda b, i: (b, i, 0)),
                   pl.BlockSpec((1, tm, LANE), lambda b, i: (b, i, 0)),
                   pl.BlockSpec((1, tm, LANE), lambda b, i: (b, i, 0))],
        out_shape=[jax.ShapeDtypeStruct((B, S, 256), BF16), jax.ShapeDtypeStruct((B, S, LANE), BF16),
                   jax.ShapeDtypeStruct((B, S, LANE), BF16)],
        compiler_params=_cparams(("parallel", "parallel")),
        name="swa_prep",
    )(p_swa, c, s)
    nb = S // blk
    ncb = n_ctx // blk
    cur = lambda b, j: (b, j, 0)
    prev = lambda b, j: (b, jnp.maximum(j - 1, ncb), 0)
    nxt = lambda b, j: (b, jnp.minimum(j + 1, nb - 1), 0)
    kv = lambda f: pl.BlockSpec((1, blk, LANE), f)
    ctx = pl.BlockSpec((1, n_ctx, LANE), lambda b, j: (b, 0, 0))
    t = np.arange(blk)[:, None]
    c = np.arange(n_ctx + 3 * blk)[None, :] - n_ctx
    bias = np.stack([np.where((c < 0) | ((c >= 0) & (c < blk) & (c >= t) & bool(hp)) | ((c >= blk) & (c < 2 * blk))
                              | ((c >= 2 * blk) & (c - 2 * blk <= t) & bool(hn)), 0.0, -1e30)
                     for hp in (0, 1) for hn in (0, 1)]).astype(np.float32)
    kind = lambda b, j: (2 * (j > ncb).astype(jnp.int32) + (j < nb - 1).astype(jnp.int32), 0, 0)
    return pl.pallas_call(
        functools.partial(_swa_attn_kernel, n_ctx=n_ctx, blk=blk),
        grid=(B, nb),
        in_specs=[pl.BlockSpec(memory_space=pltpu.SMEM),
                  pl.BlockSpec((1, blk, 256), cur), ctx, ctx,
                  kv(prev), kv(cur), kv(nxt), kv(prev), kv(cur), kv(nxt),
                  pl.BlockSpec((1, blk, n_ctx + 3 * blk), kind)],
        out_specs=pl.BlockSpec((1, blk, GROUP_W), cur),
        out_shape=jax.ShapeDtypeStruct((B, S, GROUP_W), F32),
        compiler_params=_cparams(("parallel", "arbitrary")),
        name="swa_attn",
    )(sink.astype(F32), q, k, v, k, k, k, v, v, v, jnp.asarray(bias))


def _bwd_chunk(j, ncc, nc):
    return jnp.where(j < ncc, ncc - 1 - j, nc - 1 - (j - ncc))


def _head_stack(x, width, heads):
    lane_h = lax.broadcasted_iota(jnp.int32, x.shape, 1) // width
    zero = jnp.zeros((), x.dtype)
    return jnp.concatenate([jnp.where(lane_h == h, x, zero) for h in range(heads)], axis=0)


def _head_unstack(y, rows, width, heads):
    lane_h = lax.broadcasted_iota(jnp.int32, (rows, heads * width), 1) // width
    out = y[0:rows]
    for h in range(1, heads):
        out = jnp.where(lane_h == h, y[h * rows:(h + 1) * rows], out)
    return out


RET_CHUNK = 256


def _ret_tables(log_gamma, n_ctx, S):
    C = RET_CHUNK
    t = jnp.arange(C, dtype=F32)
    rel = t[:, None] - t[None, :]
    lg = log_gamma.astype(F32)
    lg_l = jnp.repeat(lg, RET_DK, axis=1)
    dm, xi, zeta, dcay = [], [], [], []
    for d in range(2):
        r = rel if d == 0 else -rel
        dh = jnp.where(r >= 0, jnp.exp(jnp.maximum(r, 0.0)[None] * lg[d][:, None, None]), 0.0)
        dm.append(dh.reshape(RET_HEADS * C, C))
        tq = (t + 1.0) if d == 0 else (C - t)
        tk = (C - 1.0 - t) if d == 0 else t
        xi.append(jnp.exp(tq[:, None] * lg_l[d][None, :]))
        zeta.append(jnp.exp(tk[:, None] * lg_l[d][None, :]))
        dcay.append(jnp.broadcast_to(jnp.exp(C * lg_l[d])[:, None], (RET_HEADS * RET_DK, GROUP_W)))
    mask = (np.arange(128)[:, None] // RET_DK == np.arange(256)[None, :] // RET_DV).astype(np.float32)
    pos = jnp.arange(S, dtype=F32)
    inv = ROPE_THETA ** (-jnp.arange(RET_DK // 2, dtype=F32) / (RET_DK // 2))
    ang = pos[:, None] * inv
    cos, sin = jnp.cos(ang), jnp.sin(ang)
    c = jnp.tile(jnp.concatenate([cos, cos], axis=-1), (1, RET_HEADS))
    s = jnp.tile(jnp.concatenate([-sin, sin], axis=-1), (1, RET_HEADS))
    return (jnp.stack(dm), jnp.stack(xi), jnp.stack(zeta), jnp.stack(dcay), jnp.asarray(mask), c, s)


def _ret_kernel(qkf_ref, vf_ref, cf_ref, sf_ref, qkb_ref, vb_ref, cb_ref, sb_ref,
                dm_ref, xi_ref, zeta_ref, dcay_ref, mask_ref, of_ref, ob_ref, state):
    C = RET_CHUNK

    @pl.when(pl.program_id(1) == 0)
    def _():
        state[...] = jnp.zeros_like(state)

    dirs = ((qkf_ref, vf_ref, cf_ref, sf_ref, of_ref), (qkb_ref, vb_ref, cb_ref, sb_ref, ob_ref))
    for d, (qk_ref, v_ref, c_ref, s_ref, o_ref) in enumerate(dirs):
        c, s = c_ref[...], s_ref[...]
        qr = (qk_ref[0, :, 0:128] * c + qk_ref[0, :, 256:384] * s) * (RET_DK ** -0.5)
        kr = qk_ref[0, :, 128:256] * c + qk_ref[0, :, 384:512] * s
        vb = v_ref[0].astype(BF16)
        qs = _head_stack(qr.astype(BF16), RET_DK, RET_HEADS)
        sc = lax.dot_general(qs, kr.astype(BF16), (((1,), (1,)), ((), ())), preferred_element_type=F32)
        a = (sc * dm_ref[d]).astype(BF16)
        o = _head_unstack(jnp.dot(a, vb, preferred_element_type=F32), C, RET_DV, RET_HEADS)
        st = state[d]
        o = o + jnp.dot((qr * xi_ref[d]).astype(BF16), st.astype(BF16), preferred_element_type=F32)
        o_ref[0] = o
        kz = (kr * zeta_ref[d]).astype(BF16)
        u = lax.dot_general(kz, vb, (((0,), (0,)), ((), ())), preferred_element_type=F32)
        state[d] = st * dcay_ref[d] + u * mask_ref[...]


def _ret_mixer(p_ret, tables, n_ctx):
    B, S, _ = p_ret.shape
    C = RET_CHUNK
    dm, xi, zeta, dcay, mask, c, s = tables
    nc, ncc = S // C, n_ctx // C
    fwd = lambda b, j: (b, j, 0)
    bwd = lambda b, j: (b, _bwd_chunk(j, ncc, nc), 0)
    fwd_v = lambda b, j: (b, j, 2)
    bwd_v = lambda b, j: (b, _bwd_chunk(j, ncc, nc), 2)
    fwd_t = lambda b, j: (j, 0)
    bwd_t = lambda b, j: (_bwd_chunk(j, ncc, nc), 0)
    full = lambda a: pl.BlockSpec(a.shape, lambda b, j: (0,) * a.ndim)
    return pl.pallas_call(
        _ret_kernel,
        grid=(B, nc),
        in_specs=[pl.BlockSpec((1, C, 512), fwd), pl.BlockSpec((1, C, 256), fwd_v),
                  pl.BlockSpec((C, LANE), fwd_t), pl.BlockSpec((C, LANE), fwd_t),
                  pl.BlockSpec((1, C, 512), bwd), pl.BlockSpec((1, C, 256), bwd_v),
                  pl.BlockSpec((C, LANE), bwd_t), pl.BlockSpec((C, LANE), bwd_t),
                  full(dm), full(xi), full(zeta), full(dcay), full(mask)],
        out_specs=[pl.BlockSpec((1, C, GROUP_W), fwd), pl.BlockSpec((1, C, GROUP_W), bwd)],
        out_shape=[jax.ShapeDtypeStruct((B, S, GROUP_W), F32)] * 2,
        scratch_shapes=[pltpu.VMEM((2, RET_HEADS * RET_DK, GROUP_W), F32)],
        compiler_params=_cparams(("parallel", "arbitrary")),
        name="retention_scan",
    )(p_ret, p_ret, c, s, p_ret, p_ret, c, s, dm, xi, zeta, dcay, mask)


HG_CHUNK = 128


def _hg_level_tables(C):
    levels = []
    n = 2
    while n <= C:
        levels.append(n)
        n *= 2
    t = np.arange(C)
    mats, qroles, kroles = [], [], []
    for d in range(2):
        rows = []
        qr_d, kr_d = [], []
        for n in levels:
            m = n // 2
            start = (t // n) * n
            mid = start + m
            u = t - start
            M = np.zeros((C, C), np.float32)
            for i in range(C):
                if d == 0:
                    if u[i] >= m:
                        M[i, mid[i]:i + 1] = 1.0
                    else:
                        M[i, i + 1:mid[i]] = 1.0
                else:
                    if u[i] < m:
                        M[i, i:mid[i]] = 1.0
                    else:
                        M[i, mid[i]:i] = 1.0
            rows.append(M)
            qr_d.append((u >= m) if d == 0 else (u < m))
            kr_d.append((u < m) if d == 0 else (u >= m))
        Mq = np.zeros((C, C), np.float32)
        Mk = np.zeros((C, C), np.float32)
        for i in range(C):
            if d == 0:
                Mq[i, 0:i + 1] = 1.0
                Mk[i, i + 1:C] = 1.0
            else:
                Mq[i, i:C] = 1.0
                Mk[i, 0:i] = 1.0
        rows += [Mq, Mk, np.ones((8, C), np.float32)]
        mats.append(np.concatenate(rows, axis=0))
        qroles.append(np.stack(qr_d))
        kroles.append(np.stack(kr_d))
    same_block = np.stack([(t[:, None] // n == t[None, :] // n) for n in levels]).astype(np.float32)
    return levels, np.stack(mats), np.stack(qroles), np.stack(kroles), same_block


def _hg_kernel(qf_ref, zf_ref, if_ref, qb_ref, zb_ref, ib_ref, lb_ref, m_ref, role_ref, blk_ref,
               ones_ref, mask_ref, of_ref, ob_ref, state, *, n_levels):
    C = HG_CHUNK

    @pl.when(pl.program_id(1) == 0)
    def _():
        state[...] = jnp.zeros_like(state)

    dirs = ((qf_ref, zf_ref, if_ref), (qb_ref, zb_ref, ib_ref))
    st = [state[0], state[1]]
    q, v, vb, kk, r = [], [], [], [], []
    for d, (q_ref, z_ref, i_ref) in enumerate(dirs):
        z = z_ref[0]
        lb = lb_ref[d]
        lbf = jnp.maximum(lb, LB_FLOOR)
        t = jnp.exp(-jnp.abs(z))
        rcp = 1.0 / (1.0 + t)
        pos = z >= 0.0
        sig = jnp.where(pos, rcp, t * rcp)
        nsig = jnp.where(pos, t * rcp, rcp)
        logf = jnp.log(lbf + (1.0 - lb) * sig)
        kk.append((1.0 - lb) * nsig - (lbf - lb))
        hi, lo = _split2(logf)
        r.append(jnp.dot(m_ref[d], hi, preferred_element_type=F32)
                 + jnp.dot(m_ref[d], lo, preferred_element_type=F32))
        q.append(q_ref[0] * (HG_DK ** -0.5))
        v.append(i_ref[0])
        vb.append(v[d].astype(BF16))
    acc = [jnp.zeros((HG_HEADS * C, C), F32), jnp.zeros((HG_HEADS * C, C), F32)]
    for lv in range(n_levels):
        for d in range(2):
            e = jnp.exp(r[d][lv * C:(lv + 1) * C])
            roles = role_ref[d, lv]
            qe = jnp.where(roles > 0.5, q[d] * e, 0.0).astype(BF16)
            ke = jnp.where(roles > 0.5, 0.0, kk[d] * e).astype(BF16)
            a = lax.dot_general(_head_stack(qe, HG_DK, HG_HEADS), ke, (((1,), (1,)), ((), ())),
                                preferred_element_type=F32)
            acc[d] = acc[d] + a * blk_ref[lv]
    outs = []
    for d in range(2):
        o = _head_unstack(jnp.dot(acc[d].astype(BF16), vb[d], preferred_element_type=F32), C, HG_DV, HG_HEADS)
        o = o + jnp.dot((q[d] * kk[d]).astype(BF16), ones_ref[...], preferred_element_type=F32) * v[d]
        e_q = jnp.exp(r[d][n_levels * C:(n_levels + 1) * C])
        o = o + lax.dot_general((q[d] * e_q).astype(BF16), st[d].astype(BF16), (((1,), (1,)), ((), ())),
                                preferred_element_type=F32)
        outs.append(o)
    new_state = []
    for d in range(2):
        e_k = jnp.exp(r[d][(n_levels + 1) * C:(n_levels + 2) * C])
        ku = (kk[d] * e_k).astype(BF16)
        u = lax.dot_general(vb[d], ku, (((0,), (0,)), ((), ())), preferred_element_type=F32)
        dtot = jnp.exp(r[d][(n_levels + 2) * C:(n_levels + 2) * C + 1])
        new_state.append(st[d] * dtot + u * mask_ref[...])
    of_ref[0] = outs[0]
    ob_ref[0] = outs[1]
    state[0] = new_state[0]
    state[1] = new_state[1]


def _hg_mixer(p_hg, lb, n_ctx):
    B, S, _ = p_hg.shape
    C = HG_CHUNK
    levels, mats, qroles, kroles, same_block = _hg_level_tables(C)
    nl = len(levels)
    m_all = jnp.asarray(mats, BF16)
    roles = jnp.asarray(np.broadcast_to(qroles[..., None], (2, nl, C, GROUP_W)).astype(np.float32))
    blk = jnp.asarray(np.tile(same_block, (1, HG_HEADS, 1)))
    hd = np.arange(GROUP_W) // HG_DK
    bd = (hd[:, None] == hd[None, :]).astype(np.float32)
    ones_bd = jnp.asarray(bd, BF16)
    mask = jnp.asarray(bd)
    nc, ncc = S // C, n_ctx // C
    col = lambda k, rev: (lambda b, j: (b, _bwd_chunk(j, ncc, nc) if rev else j, k))
    blkspec = lambda k, rev: pl.BlockSpec((1, C, GROUP_W), col(k, rev))
    full = lambda a: pl.BlockSpec(a.shape, lambda b, j: (0,) * a.ndim)
    lb3 = lb.reshape(2, 1, GROUP_W).astype(F32)
    return pl.pallas_call(
        functools.partial(_hg_kernel, n_levels=nl),
        grid=(B, nc),
        in_specs=[blkspec(0, False), blkspec(1, False), blkspec(3, False),
                  blkspec(0, True), blkspec(2, True), blkspec(3, True),
                  full(lb3), full(m_all), full(roles), full(blk), full(ones_bd), full(mask)],
        out_specs=[blkspec(0, False), blkspec(0, True)],
        out_shape=[jax.ShapeDtypeStruct((B, S, GROUP_W), F32)] * 2,
        scratch_shapes=[pltpu.VMEM((2, GROUP_W, GROUP_W), F32)],
        compiler_params=_cparams(("parallel", "arbitrary")),
        name="hgrn2_scan",
    )(p_hg, p_hg, p_hg, p_hg, p_hg, p_hg, lb3, m_all, roles, blk, ones_bd, mask)


_SWA_OUT_PERM = np.concatenate([np.arange(64 * h, 64 * h + 64) for h in (0, 2, 1, 3)])


def _wout_kernel(h_ref, m_ref, a_ref, hf_ref, hb_ref, hgate_ref, c_ref, rf_ref, rb_ref, rgate_ref,
                 hgn_ref, rgg_ref, rgb_ref, ones_ref, w_ref, o_ref, *, tm, n_ctx, row_off):
    row0 = row_off + pl.program_id(1) * tm
    ones = ones_ref[...]
    o = hf_ref[0] + hb_ref[0]
    yb = o * lax.rsqrt(_seg_mean(o * o, ones) + NORM_EPS) * hgn_ref[...] * _silu(hgate_ref[0])
    o = rf_ref[0] + rb_ref[0]
    xc = o - _seg_mean(o, ones)
    yd = (xc * lax.rsqrt(_seg_mean(xc * xc, ones) + NORM_EPS) * rgg_ref[...] + rgb_ref[...]) * _silu(rgate_ref[0])
    acc = jnp.dot(a_ref[0].astype(BF16), w_ref[0:256, :], preferred_element_type=F32)
    acc += jnp.dot(yb.astype(BF16), w_ref[256:512, :], preferred_element_type=F32)
    acc += jnp.dot(c_ref[0].astype(BF16), w_ref[512:768, :], preferred_element_type=F32)
    acc += jnp.dot(yd.astype(BF16), w_ref[768:1024, :], preferred_element_type=F32)
    o_ref[0] = h_ref[0] + _row_mod(m_ref, 2, row0, tm, n_ctx) * acc


def _out_proj(h, mod, o_mla, hg_f, hg_b, p_hg, o_swa, ret_f, ret_b, p_ret, hg_norm_g, ret_gn_g, ret_gn_b,
              w_out_p, n_ctx, tm=384, row_off=0):
    B, S, D = h.shape
    assert row_off % tm == 0 and (S - row_off) % tm == 0
    off = row_off // tm
    hd = np.arange(GROUP_W) // HG_DV
    ones_bd = jnp.asarray((hd[:, None] == hd[None, :]).astype(np.float32) / HG_DV, BF16)
    row = lambda w, k=0: pl.BlockSpec((1, tm, w), lambda b, i: (b, i + off, k))
    vec = pl.BlockSpec((1, GROUP_W), lambda b, i: (0, 0))
    return pl.pallas_call(
        functools.partial(_wout_kernel, tm=tm, n_ctx=n_ctx, row_off=row_off),
        grid=(B, (S - row_off) // tm),
        in_specs=[row(D), pl.BlockSpec((1, 2, 8, D), lambda b, i: (b, 0, 0, 0)),
                  row(GROUP_W), row(GROUP_W), row(GROUP_W), row(GROUP_W, 4),
                  row(GROUP_W), row(GROUP_W), row(GROUP_W), row(GROUP_W, 3),
                  vec, vec, vec, pl.BlockSpec((GROUP_W, GROUP_W), lambda b, i: (0, 0)),
                  pl.BlockSpec((D, D), lambda b, i: (0, 0))],
        out_specs=pl.BlockSpec((1, tm, D), lambda b, i: (b, i, 0)),
        out_shape=jax.ShapeDtypeStruct((B, S - row_off, D), F32),
        compiler_params=_cparams(("parallel", "parallel")),
        name="out_proj",
    )(h, mod, o_mla, hg_f, hg_b, p_hg, o_swa, ret_f, ret_b, p_ret,
      hg_norm_g.reshape(1, GROUP_W), ret_gn_g.reshape(1, GROUP_W), ret_gn_b.reshape(1, GROUP_W), ones_bd, w_out_p)


def _ffn_kernel(h_ref, m_ref, g_ref, wg_ref, wu_ref, wd_ref, o_ref, xn, acc, *, tm, n_ctx):
    j = pl.program_id(2)
    row0 = pl.program_id(1) * tm

    @pl.when(j == 0)
    def _():
        xn[...] = _norm_mod(h_ref[0], g_ref[...], _row_mod(m_ref, 3, row0, tm, n_ctx),
                            _row_mod(m_ref, 4, row0, tm, n_ctx)).astype(BF16)
        acc[...] = jnp.zeros_like(acc)

    wg, wu, wd = wg_ref[0].astype(BF16), wu_ref[0].astype(BF16), wd_ref[0].astype(BF16)
    half = tm // 2
    for r0 in (0, half):
        x = xn[r0:r0 + half]
        hg = jnp.dot(x, wg, preferred_element_type=F32)
        hu = jnp.dot(x, wu, preferred_element_type=F32)
        acc[r0:r0 + half] += jnp.dot((_silu(hg) * hu).astype(BF16), wd, preferred_element_type=F32)

    @pl.when(j == pl.num_programs(2) - 1)
    def _():
        o_ref[0] = h_ref[0] + _row_mod(m_ref, 5, row0, tm, n_ctx) * acc[...]


def _dense_ffn(h, mod, g, w_gate, w_up, w_down, li, n_ctx, tm=1152, tf=256):
    B, S, D = h.shape
    dff = w_gate.shape[-1]
    return pl.pallas_call(
        functools.partial(_ffn_kernel, tm=tm, n_ctx=n_ctx),
        grid=(B, S // tm, dff // tf),
        in_specs=[pl.BlockSpec((1, tm, D), lambda b, i, j: (b, i, 0)),
                  pl.BlockSpec((1, 2, 8, D), lambda b, i, j: (b, 0, 0, 0)),
                  pl.BlockSpec((1, D), lambda b, i, j: (0, 0)),
                  pl.BlockSpec((1, D, tf), lambda b, i, j: (li, 0, j)),
                  pl.BlockSpec((1, D, tf), lambda b, i, j: (li, 0, j)),
                  pl.BlockSpec((1, tf, D), lambda b, i, j: (li, j, 0))],
        out_specs=pl.BlockSpec((1, tm, D), lambda b, i, j: (b, i, 0)),
        out_shape=jax.ShapeDtypeStruct((B, S, D), F32),
        scratch_shapes=[pltpu.VMEM((tm, D), BF16), pltpu.VMEM((tm, D), F32)],
        compiler_params=_cparams(("parallel", "parallel", "arbitrary")),
        name="dense_ffn",
    )(h, mod, g.reshape(1, D), w_gate, w_up, w_down)


MOE_TM = 1024
MOE_TF = 512
MOE_ROWS = (384, 256)
DMA_UNROLL = 8
SUBLANES = 8
ZERO_ROWS = 256


def _router_kernel(h_ref, m_ref, g_ref, wr_ref, tri_ref, xn_ref, route_ref, cnt_ref, counts, *, tm, n_ctx):
    b, i = pl.program_id(0), pl.program_id(1)
    row0 = i * tm

    @pl.when((b == 0) & (i == 0))
    def _():
        counts[...] = jnp.zeros_like(counts)

    a = _norm_mod(h_ref[0], g_ref[...], _row_mod(m_ref, 3, row0, tm, n_ctx), _row_mod(m_ref, 4, row0, tm, n_ctx))
    xn_ref[0] = a
    logits = jnp.dot(a, wr_ref[...], preferred_element_type=F32, precision=lax.Precision.HIGHEST)
    lane = lax.broadcasted_iota(jnp.int32, logits.shape, 1)
    neg = jnp.float32(-jnp.inf)
    lg = jnp.where(lane < N_EXPERTS, logits, neg)
    v1 = jnp.max(lg, axis=-1, keepdims=True)
    i1 = jnp.min(jnp.where(lg == v1, lane, LANE), axis=-1, keepdims=True)
    lg2 = jnp.where(lane == i1, neg, lg)
    v2 = jnp.max(lg2, axis=-1, keepdims=True)
    i2 = jnp.min(jnp.where(lg2 == v2, lane, LANE), axis=-1, keepdims=True)
    e = jnp.exp(v2 - v1)
    w1 = 1.0 / (1.0 + e)
    w2 = e / (1.0 + e)
    oh1, oh2 = lane == i1, lane == i2
    onehot = jnp.where(oh1 | oh2, 1.0, 0.0)
    earlier = jnp.dot(tri_ref[...], onehot.astype(BF16), preferred_element_type=F32)
    pos = earlier + counts[...]
    r1 = jnp.sum(jnp.where(oh1, pos, 0.0), axis=-1, keepdims=True)
    r2 = jnp.sum(jnp.where(oh2, pos, 0.0), axis=-1, keepdims=True)
    counts[...] = counts[...] + jnp.sum(onehot, axis=0, keepdims=True)
    cnt_ref[...] = jnp.broadcast_to(counts[...], cnt_ref.shape)
    route_ref[0] = jnp.where(lane == 0, i1.astype(F32),
                             jnp.where(lane == 1, i2.astype(F32),
                                       jnp.where(lane == 2, w1,
                                                 jnp.where(lane == 3, w2,
                                                           jnp.where(lane == 4, r1, jnp.where(lane == 5, r2, 0.0))))))


def _scatter_kernel(meta_ref, slot_ref, x_ref, xs_hbm, zeros_v, sem, *, tm, tile_rows, total_rows):
    b, i = pl.program_id(0), pl.program_id(1)

    def start(r, carry):
        for k in range(TOP_K):
            pltpu.make_async_copy(x_ref.at[0, pl.ds(r, 1)], xs_hbm.at[pl.ds(slot_ref[0, 0, 0, k * tm + r], 1)],
                                  sem).start()
        return carry

    lax.fori_loop(0, tm, start, 0, unroll=DMA_UNROLL)
    for k in range(TOP_K):
        pltpu.make_async_copy(x_ref.at[0], xs_hbm.at[pl.ds(0, tm)], sem).wait()

    @pl.when((b == pl.num_programs(0) - 1) & (i == pl.num_programs(1) - 1))
    def _():
        zeros_v[...] = jnp.zeros_like(zeros_v)
        z1 = zeros_v.at[pl.ds(0, 1)]
        z8 = zeros_v.at[pl.ds(0, SUBLANES)]

        def fill(lo, end):
            mid = jnp.minimum(((lo + SUBLANES - 1) // SUBLANES) * SUBLANES, end)
            big = mid + ((end - mid) // ZERO_ROWS) * ZERO_ROWS
            n_row, n_big, n_small = mid - lo, (big - mid) // ZERO_ROWS, (end - big) // SUBLANES

            def start_row(r, c):
                pltpu.make_async_copy(z1, xs_hbm.at[pl.ds(lo + r, 1)], sem).start()
                return c

            def wait_row(r, c):
                pltpu.make_async_copy(z1, xs_hbm.at[pl.ds(0, 1)], sem).wait()
                return c

            def start_big(q, c):
                dst = xs_hbm.at[pl.ds(pl.multiple_of(mid + q * ZERO_ROWS, SUBLANES), ZERO_ROWS)]
                pltpu.make_async_copy(zeros_v, dst, sem).start()
                return c

            def wait_big(q, c):
                pltpu.make_async_copy(zeros_v, xs_hbm.at[pl.ds(0, ZERO_ROWS)], sem).wait()
                return c

            def start_small(q, c):
                dst = xs_hbm.at[pl.ds(pl.multiple_of(big + q * SUBLANES, SUBLANES), SUBLANES)]
                pltpu.make_async_copy(z8, dst, sem).start()
                return c

            def wait_small(q, c):
                pltpu.make_async_copy(z8, xs_hbm.at[pl.ds(0, SUBLANES)], sem).wait()
                return c

            lax.fori_loop(0, n_row, start_row, 0)
            lax.fori_loop(0, n_big, start_big, 0)
            lax.fori_loop(0, n_small, start_small, 0)
            lax.fori_loop(0, n_row, wait_row, 0)
            lax.fori_loop(0, n_big, wait_big, 0)
            lax.fori_loop(0, n_small, wait_small, 0)

        for e in range(N_EXPERTS):
            c = meta_ref[e]
            g0 = meta_ref[N_EXPERTS + e]
            fill(g0 + c, g0 + ((c + tile_rows - 1) // tile_rows) * tile_rows)
        fill(meta_ref[2 * N_EXPERTS] * tile_rows, jnp.int32(total_rows))


def _expert_kernel(te_ref, nu_ref, x_ref, wg_ref, wu_ref, wd_ref, o_ref, xb):
    i, j = pl.program_id(0), pl.program_id(1)
    half = x_ref.shape[0] // 2

    @pl.when(j == 0)
    def _():
        xb[...] = x_ref[...].astype(BF16)
        o_ref[...] = jnp.zeros_like(o_ref)

    @pl.when(i < nu_ref[0])
    def _():
        wg, wu, wd = wg_ref[0, 0].astype(BF16), wu_ref[0, 0].astype(BF16), wd_ref[0, 0].astype(BF16)
        for r0 in (0, half):
            x = xb[r0:r0 + half]
            hg = jnp.dot(x, wg, preferred_element_type=F32)
            hu = jnp.dot(x, wu, preferred_element_type=F32)
            o_ref[r0:r0 + half] += jnp.dot((_silu(hg) * hu).astype(BF16), wd, preferred_element_type=F32)


def _combine_kernel(slot_ref, h_ref, m_ref, route_ref, fg_ref, y_hbm, o_ref, buf, sem, *, tm, n_ctx, final_norm):
    row0 = pl.program_id(1) * tm

    def start(r, carry):
        for k in range(TOP_K):
            pltpu.make_async_copy(y_hbm.at[pl.ds(slot_ref[0, 0, 0, k * tm + r], 1)], buf.at[k, pl.ds(r, 1)], sem).start()
        return carry

    lax.fori_loop(0, tm, start, 0, unroll=DMA_UNROLL)
    for k in range(TOP_K):
        pltpu.make_async_copy(y_hbm.at[pl.ds(0, tm)], buf.at[k], sem).wait()
    w1 = route_ref[0, :, 2:3]
    w2 = route_ref[0, :, 3:4]
    y = h_ref[0] + _row_mod(m_ref, 5, row0, tm, n_ctx) * (w1 * buf[0] + w2 * buf[1])
    if final_norm:
        y = y * lax.rsqrt(jnp.mean(y * y, axis=-1, keepdims=True) + NORM_EPS) * fg_ref[...]
    o_ref[0] = y


def _moe_ffn(h, mod, g, w_router, w_gate, w_up, w_down, li, n_ctx, final_g=None):
    B, S, D = h.shape
    T = B * S
    cr = next(r for r in MOE_ROWS if S % r == 0)
    tm = cr
    dffe = w_gate.shape[-1]
    te_rows = MOE_TM if T % MOE_TM == 0 else 512
    n_tiles = TOP_K * T // te_rows + N_EXPERTS
    P = n_tiles * te_rows
    wr = jnp.pad(w_router, ((0, 0), (0, LANE - N_EXPERTS)))
    tri = jnp.asarray(np.tril(np.ones((tm, tm), np.float32), -1), BF16)
    xn, route, cnt = pl.pallas_call(
        functools.partial(_router_kernel, tm=tm, n_ctx=n_ctx),
        grid=(B, S // tm),
        in_specs=[pl.BlockSpec((1, tm, D), lambda b, i: (b, i, 0)),
                  pl.BlockSpec((1, 2, 8, D), lambda b, i: (b, 0, 0, 0)),
                  pl.BlockSpec((1, D), lambda b, i: (0, 0)),
                  pl.BlockSpec((D, LANE), lambda b, i: (0, 0)),
                  pl.BlockSpec((tm, tm), lambda b, i: (0, 0))],
        out_specs=[pl.BlockSpec((1, tm, D), lambda b, i: (b, i, 0)),
                   pl.BlockSpec((1, tm, LANE), lambda b, i: (b, i, 0)),
                   pl.BlockSpec((SUBLANES, LANE), lambda b, i: (0, 0))],
        out_shape=[jax.ShapeDtypeStruct((B, S, D), F32), jax.ShapeDtypeStruct((B, S, LANE), F32),
                   jax.ShapeDtypeStruct((SUBLANES, LANE), F32)],
        scratch_shapes=[pltpu.VMEM((1, LANE), F32)],
        compiler_params=_cparams(("arbitrary", "arbitrary")),
        name="moe_router",
    )(h, mod, g.reshape(1, D), wr, tri)

    counts = cnt[0, :N_EXPERTS].astype(jnp.int32)
    ptiles = (counts + te_rows - 1) // te_rows
    tile_end = jnp.cumsum(ptiles)
    gstart = (tile_end - ptiles) * te_rows
    n_used = tile_end[-1:].astype(jnp.int32)
    tile_ids = jnp.arange(n_tiles, dtype=jnp.int32)
    tile_e = jnp.minimum(jnp.sum((tile_ids[:, None] >= tile_end[None, :]).astype(jnp.int32), axis=1),
                         N_EXPERTS - 1)
    meta = jnp.concatenate([counts, gstart, n_used])
    choice = route[..., 0:TOP_K].astype(jnp.int32)
    is_e = choice[..., None] == jnp.arange(N_EXPERTS, dtype=jnp.int32)
    slots = jnp.sum(jnp.where(is_e, gstart, 0), axis=-1) + route[..., 4:4 + TOP_K].astype(jnp.int32)
    slots = slots.reshape(B, S // cr, cr, TOP_K).transpose(0, 1, 3, 2)
    slots = slots.reshape(B, S // cr, 1, TOP_K * cr)

    xs = pl.pallas_call(
        functools.partial(_scatter_kernel, tm=cr, tile_rows=te_rows, total_rows=P),
        grid_spec=pltpu.PrefetchScalarGridSpec(
            num_scalar_prefetch=1,
            grid=(B, S // cr),
            in_specs=[pl.BlockSpec((1, 1, 1, TOP_K * cr), lambda b, i, mt: (b, i, 0, 0),
                                   memory_space=pltpu.SMEM),
                      pl.BlockSpec((1, cr, D), lambda b, i, mt: (b, i, 0))],
            out_specs=pl.BlockSpec(memory_space=pl.ANY),
            scratch_shapes=[pltpu.VMEM((ZERO_ROWS, D), F32), pltpu.SemaphoreType.DMA(())]),
        out_shape=jax.ShapeDtypeStruct((P, D), F32),
        compiler_params=_cparams(("arbitrary", "arbitrary")),
        name="moe_scatter",
    )(meta, slots, xn)

    nf = dffe // MOE_TF
    last = lambda i, nu: jnp.minimum(i, nu[0] - 1)
    jj = lambda i, j, nu: jnp.where(i < nu[0], j, nf - 1)
    ys = pl.pallas_call(
        _expert_kernel,
        grid_spec=pltpu.PrefetchScalarGridSpec(
            num_scalar_prefetch=2,
            grid=(n_tiles, nf),
            in_specs=[pl.BlockSpec((te_rows, D), lambda i, j, te, nu: (last(i, nu), 0)),
                      pl.BlockSpec((1, 1, D, MOE_TF), lambda i, j, te, nu: (li, te[last(i, nu)], 0, jj(i, j, nu))),
                      pl.BlockSpec((1, 1, D, MOE_TF), lambda i, j, te, nu: (li, te[last(i, nu)], 0, jj(i, j, nu))),
                      pl.BlockSpec((1, 1, MOE_TF, D), lambda i, j, te, nu: (li, te[last(i, nu)], jj(i, j, nu), 0))],
            out_specs=pl.BlockSpec((te_rows, D), lambda i, j, te, nu: (i, 0)),
            scratch_shapes=[pltpu.VMEM((te_rows, D), BF16)]),
        out_shape=jax.ShapeDtypeStruct((P, D), F32),
        compiler_params=_cparams(("arbitrary", "arbitrary")),
        name="moe_experts",
    )(tile_e, n_used, xs, w_gate, w_up, w_down)
    fg = jnp.ones((1, D), F32) if final_g is None else final_g.reshape(1, D).astype(F32)
    return pl.pallas_call(
        functools.partial(_combine_kernel, tm=cr, n_ctx=n_ctx, final_norm=final_g is not None),
        grid=(B, S // cr),
        in_specs=[pl.BlockSpec((1, 1, 1, TOP_K * cr), lambda b, i: (b, i, 0, 0), memory_space=pltpu.SMEM),
                  pl.BlockSpec((1, cr, D), lambda b, i: (b, i, 0)),
                  pl.BlockSpec((1, 2, 8, D), lambda b, i: (b, 0, 0, 0)),
                  pl.BlockSpec((1, cr, LANE), lambda b, i: (b, i, 0)),
                  pl.BlockSpec((1, D), lambda b, i: (0, 0)),
                  pl.BlockSpec(memory_space=pl.ANY)],
        out_specs=pl.BlockSpec((1, cr, D), lambda b, i: (b, i, 0)),
        out_shape=jax.ShapeDtypeStruct((B, S, D), F32),
        scratch_shapes=[pltpu.VMEM((TOP_K, cr, D), F32), pltpu.SemaphoreType.DMA(())],
        compiler_params=_cparams(("arbitrary", "arbitrary")),
        name="moe_combine",
    )(slots, h, mod, route, fg, ys)


def _final_norm_kernel(h_ref, g_ref, o_ref):
    x = h_ref[0]
    o_ref[0] = x * lax.rsqrt(jnp.mean(x * x, axis=-1, keepdims=True) + NORM_EPS) * g_ref[...]


def _final_norm(h, g, n_ctx, tm=256):
    B, S, D = h.shape
    N = S - n_ctx
    return pl.pallas_call(
        _final_norm_kernel,
        grid=(B, N // tm),
        in_specs=[pl.BlockSpec((1, tm, D), lambda b, i: (b, n_ctx // tm + i, 0)),
                  pl.BlockSpec((1, D), lambda b, i: (0, 0))],
        out_specs=pl.BlockSpec((1, tm, D), lambda b, i: (b, i, 0)),
        out_shape=jax.ShapeDtypeStruct((B, N, D), F32),
        compiler_params=_cparams(("parallel", "parallel")),
        name="final_norm",
    )(h, g.reshape(1, D))


MOD_ROWS = 16


def kernel(x, c, ctx, c_ctx, w_ada, b_ada, norm_mix_g, norm_ffn_g, w_in, w_out, mla_q_norm_g, mla_w_uq, mla_kv_norm_g, mla_w_ukv, hg_lb_logits, hg_norm_g, swa_sink, ret_decay_logit, ret_gn_g, ret_gn_b, ffn_w_gate, ffn_w_up, ffn_w_down, moe_w_router, moe_w_gate, moe_w_up, moe_w_down, final_norm_g):
    B, N, D = x.shape
    L = ctx.shape[1]
    S = L + N
    assert B + 1 <= MOD_ROWS
    h = jnp.concatenate([ctx, x], axis=1)
    cvec = jnp.zeros((MOD_ROWS, D), F32).at[0].set(c_ctx).at[1:1 + B].set(c)
    mod_all = _modulation(cvec, w_ada, b_ada).reshape(DEPTH, MOD_ROWS, 6, D)

    lb_p = jax.nn.softmax(hg_lb_logits.astype(F32), axis=0)
    hg_lb = jnp.cumsum(lb_p, axis=0) - lb_p[0:1]
    ret_log_gamma = jax.nn.log_sigmoid(ret_decay_logit.astype(F32))

    mla_tab = _mla_tables(N, L)
    swa_tab = _swa_tables(N, L)
    w_out_perm = np.concatenate([np.arange(512), 512 + _SWA_OUT_PERM, np.arange(768, 1024)])

    for l in range(DEPTH):
        m = mod_all[l]
        mod = jnp.stack([jnp.broadcast_to(m[0], (B, 6, D)), m[1:1 + B]], axis=1)
        mod = jnp.pad(mod, ((0, 0), (0, 0), (0, 2), (0, 0)))
        w_ext = _take_cols(w_in[l], _EXT_COLS).astype(BF16)
        p_mla, p_hg, p_swa, p_ret = _in_proj(h, mod, norm_mix_g[l], w_ext, L)
        o_mla = _mla_mixer(p_mla, _mla_weights(mla_q_norm_g[l], mla_w_uq[l], mla_kv_norm_g[l], mla_w_ukv[l]),
                           mla_tab, L)
        hg_f, hg_b = _hg_mixer(p_hg, hg_lb[l], L)
        o_swa = _swa_mixer(p_swa, swa_sink[l], swa_tab, L)
        ret_f, ret_b = _ret_mixer(p_ret, _ret_tables(ret_log_gamma[l], L, S), L)
        w_out_p = jnp.take(w_out[l], jnp.asarray(w_out_perm), axis=0).astype(BF16)
        last = l == DEPTH - 1
        if last and l % 2 == 1:
            h = _out_proj(h, mod, o_mla, hg_f, hg_b, p_hg, o_swa, ret_f, ret_b, p_ret,
                          hg_norm_g[l], ret_gn_g[l], ret_gn_b[l], w_out_p, L, tm=math.gcd(L, N, 256), row_off=L)
            return _moe_ffn(h, mod, norm_ffn_g[l], moe_w_router[l // 2], moe_w_gate, moe_w_up, moe_w_down, l // 2,
                            0, final_g=final_norm_g)
        h = _out_proj(h, mod, o_mla, hg_f, hg_b, p_hg, o_swa, ret_f, ret_b, p_ret,
                      hg_norm_g[l], ret_gn_g[l], ret_gn_b[l], w_out_p, L)
        if l % 2 == 0:
            h = _dense_ffn(h, mod, norm_ffn_g[l], ffn_w_gate, ffn_w_up, ffn_w_down, l // 2, L,
                           tm=1152 if S % 1152 == 0 else 384)
        else:
            h = _moe_ffn(h, mod, norm_ffn_g[l], moe_w_router[l // 2], moe_w_gate, moe_w_up, moe_w_down, l // 2, L)
    return _final_norm(h, final_norm_g, L)
```

```python
import functools
import math

import numpy as np
import jax
import jax.numpy as jnp
from jax import lax
from jax.experimental import pallas as pl
from jax.experimental.pallas import tpu as pltpu

F32 = jnp.float32
BF16 = jnp.bfloat16

D_MODEL = 1024
DEPTH = 4
GRID_W = 64
ROPE_THETA = 10000.0
NORM_EPS = 1e-6
LB_FLOOR = 1e-30
GROUP_W = 256

MLA_HEADS, MLA_NOPE, MLA_ROPE, MLA_V = 4, 64, 32, 64
MLA_Q_RANK, MLA_KV_RANK = 192, 128
HG_HEADS, HG_DK, HG_DV = 4, 64, 64
SWA_HEADS, SWA_KV_HEADS, SWA_HD, SWA_WINDOW = 4, 2, 64, 128
RET_HEADS, RET_DK, RET_DV = 4, 32, 64
N_EXPERTS, TOP_K = 8, 2

_IN_SIZES = (192, 128, 32, 256, 256, 256, 256, 256, 256, 128, 128, 128, 128, 256, 256)
_OFF = np.concatenate([[0], np.cumsum(_IN_SIZES)]).astype(np.int64)
D_IN = int(_OFF[-1])

W_MLA, W_HG, W_SWA, W_RET = 512, 1280, 896, 1024
W_EXT = W_MLA + W_HG + W_SWA + W_RET

LANE = 128
VMEM_LIMIT = 56 * 1024 * 1024


def _cparams(sem):
    return pltpu.CompilerParams(dimension_semantics=sem, vmem_limit_bytes=VMEM_LIMIT)


def _swap_halves(idx, width):
    idx = np.asarray(idx)
    base = (idx // width) * width
    j = idx % width
    return base + (j + width // 2) % width


def _ext_columns():
    z = lambda n: -np.ones((n,), np.int64)
    rng = lambda a, n: np.arange(a, a + n)
    o = _OFF
    k_pe = rng(o[2], 32)
    mla = np.concatenate([rng(o[0], 192), z(64), rng(o[1], 128), k_pe,
                          o[2] + _swap_halves(np.arange(32), 32), z(64)])
    hg = rng(o[3], 1280)
    q_heads = [rng(o[8] + 64 * h, 64) for h in range(4)]
    q_perm = np.concatenate([q_heads[0], q_heads[2], q_heads[1], q_heads[3]])
    k = rng(o[9], 128)
    v = rng(o[10], 128)
    sw = lambda cols, base: base + _swap_halves(cols - base, 64)
    swa = np.concatenate([q_perm, k, v, sw(q_perm, o[8]), sw(k, o[9])])
    rq, rk = rng(o[11], 128), rng(o[12], 128)
    ret = np.concatenate([rq, rk, o[11] + _swap_halves(rq - o[11], 32),
                          o[12] + _swap_halves(rk - o[12], 32), rng(o[13], 256), rng(o[14], 256)])
    cols = np.concatenate([mla, hg, swa, ret])
    assert cols.shape[0] == W_EXT
    return cols


_EXT_COLS = _ext_columns()


def _take_cols(w, cols):
    cols = np.asarray(cols)
    g = jnp.take(w, jnp.asarray(np.maximum(cols, 0)), axis=-1)
    return jnp.where(jnp.asarray(cols >= 0), g, jnp.zeros((), w.dtype))


def _axial_angles(n_tok, rot_dim):
    rows = n_tok // GRID_W
    row = jnp.broadcast_to(jnp.arange(rows)[:, None], (rows, GRID_W)).reshape(-1)
    col = jnp.broadcast_to(jnp.arange(GRID_W)[None, :], (rows, GRID_W)).reshape(-1)
    n_freq = rot_dim // 4
    inv = ROPE_THETA ** (-jnp.arange(n_freq, dtype=F32) / n_freq)
    return jnp.concatenate([row.astype(F32)[:, None] * inv, col.astype(F32)[:, None] * inv], axis=-1)


def _rope_tables(ang, n_ctx):
    cos, sin = jnp.cos(ang), jnp.sin(ang)
    c = jnp.concatenate([cos, cos], axis=-1)
    s = jnp.concatenate([-sin, sin], axis=-1)
    r = c.shape[-1]
    c = jnp.concatenate([jnp.ones((n_ctx, r), F32), c], axis=0)
    s = jnp.concatenate([jnp.zeros((n_ctx, r), F32), s], axis=0)
    return c, s


def _split2(x):
    hi = x.astype(BF16)
    lo = (x - hi.astype(F32)).astype(BF16)
    return hi, lo


def _seg_mean(x, ones_bd):
    hi, lo = _split2(x)
    return (jnp.dot(hi, ones_bd, preferred_element_type=F32)
            + jnp.dot(lo, ones_bd, preferred_element_type=F32))


def _silu(x):
    return x / (1.0 + jnp.exp(-x))


def _row_mod(m_ref, idx, row0, tm, n_ctx):
    rows = row0 + lax.broadcasted_iota(jnp.int32, (tm, 1), 0)
    mc = m_ref[0, 0, idx:idx + 1, :]
    ml = m_ref[0, 1, idx:idx + 1, :]
    return jnp.where(rows < n_ctx, mc, ml)


def _norm_mod(x, g, shift, scale):
    ms = jnp.mean(x * x, axis=-1, keepdims=True)
    y = x * lax.rsqrt(ms + NORM_EPS) * g
    return y * (1.0 + scale) + shift


def _mod_kernel(c_ref, w_ref, b_ref, o_ref):
    s = _silu(c_ref[...])
    o_ref[0] = jnp.dot(s.astype(BF16), w_ref[0].astype(BF16), preferred_element_type=F32) + b_ref[0]


def _modulation(cvec, w_ada, b_ada):
    R, D = cvec.shape
    depth, _, n6 = w_ada.shape
    tn = 1536
    return pl.pallas_call(
        _mod_kernel,
        grid=(depth, n6 // tn),
        in_specs=[pl.BlockSpec((R, D), lambda l, j: (0, 0)),
                  pl.BlockSpec((1, D, tn), lambda l, j: (l, 0, j)),
                  pl.BlockSpec((1, 1, tn), lambda l, j: (l, 0, j))],
        out_specs=pl.BlockSpec((1, R, tn), lambda l, j: (l, 0, j)),
        out_shape=jax.ShapeDtypeStruct((depth, R, n6), F32),
        compiler_params=_cparams(("parallel", "parallel")),
        name="adaln_mod",
    )(cvec, w_ada, b_ada.reshape(depth, 1, n6))


def _win_kernel(h_ref, m_ref, g_ref, w_ref, o_mla, o_hg, o_swa, o_ret, *, tm, n_ctx):
    row0 = pl.program_id(1) * tm
    a = _norm_mod(h_ref[0], g_ref[...], _row_mod(m_ref, 0, row0, tm, n_ctx),
                  _row_mod(m_ref, 1, row0, tm, n_ctx)).astype(BF16)
    c0 = 0
    for o_ref, w in ((o_mla, W_MLA), (o_hg, W_HG), (o_swa, W_SWA), (o_ret, W_RET)):
        o_ref[0] = jnp.dot(a, w_ref[:, c0:c0 + w], preferred_element_type=F32)
        c0 += w


def _in_proj(h, mod, g, w_ext, n_ctx, tm=384):
    B, S, D = h.shape
    widths = (W_MLA, W_HG, W_SWA, W_RET)
    return pl.pallas_call(
        functools.partial(_win_kernel, tm=tm, n_ctx=n_ctx),
        grid=(B, S // tm),
        in_specs=[pl.BlockSpec((1, tm, D), lambda b, i: (b, i, 0)),
                  pl.BlockSpec((1, 2, 8, D), lambda b, i: (b, 0, 0, 0)),
                  pl.BlockSpec((1, D), lambda b, i: (0, 0)),
                  pl.BlockSpec((D, W_EXT), lambda b, i: (0, 0))],
        out_specs=[pl.BlockSpec((1, tm, w), lambda b, i: (b, i, 0)) for w in widths],
        out_shape=[jax.ShapeDtypeStruct((B, S, w), F32) for w in widths],
        compiler_params=_cparams(("parallel", "parallel")),
        name="in_proj",
    )(h, mod, g.reshape(1, D), w_ext)


def _mla_weights(q_norm_g, w_uq, kv_norm_g, w_ukv):
    qcols, qsw, kcols = [], [], []
    for h in range(MLA_HEADS):
        b = 96 * h
        nope = np.arange(b, b + 64)
        pe = np.arange(b + 64, b + 96)
        pad = -np.ones((32,), np.int64)
        qcols.append(np.concatenate([nope, pe, pad]))
        qsw.append(np.concatenate([-np.ones((64,), np.int64), (b + 64) + _swap_halves(np.arange(32), 32), pad]))
        kcols.append(np.concatenate([np.arange(128 * h, 128 * h + 64), -np.ones((64,), np.int64)]))
    vcols = np.concatenate([np.concatenate([np.arange(128 * h + 64, 128 * h + 128), -np.ones((64,), np.int64)])
                            for h in range(MLA_HEADS)])
    pad_rows = ((0, 256 - MLA_Q_RANK), (0, 0))
    wq = jnp.pad(_take_cols(w_uq, np.concatenate(qcols)), pad_rows).astype(BF16)
    wq_sw = jnp.pad(_take_cols(w_uq, np.concatenate(qsw)), pad_rows).astype(BF16)
    wk = _take_cols(w_ukv, np.concatenate(kcols)).astype(BF16)
    wv = _take_cols(w_ukv, vcols).astype(BF16)
    e = np.zeros((128, 512), np.float32)
    for h in range(MLA_HEADS):
        for j in range(32):
            e[j, 128 * h + 64 + j] = 1.0
            e[32 + j, 128 * h + 64 + j] = 1.0
    gq = jnp.pad(q_norm_g, (0, 256 - MLA_Q_RANK)).reshape(1, 256)
    return gq, wq, wq_sw, kv_norm_g.reshape(1, 128), wk, wv, jnp.asarray(e, BF16)


def _mla_tables(n_lat, n_ctx):
    ang = _axial_angles(n_lat, MLA_ROPE)
    c32, s32 = _rope_tables(ang, n_ctx)
    S = n_lat + n_ctx
    one, zero = jnp.ones((S, 64), F32), jnp.zeros((S, 64), F32)
    cq = jnp.concatenate([one, c32, jnp.zeros((S, 32), F32)], axis=-1)
    sq = jnp.concatenate([zero, s32, jnp.zeros((S, 32), F32)], axis=-1)
    tk = jnp.concatenate([c32, s32, zero], axis=-1)
    return cq, sq, tk


def _mla_prep_kernel(p_ref, cq_ref, sq_ref, tk_ref, gq_ref, wq_ref, wqs_ref, gk_ref, wk_ref, wv_ref, e_ref,
                     q_out, k_out, v_out):
    scale = (MLA_NOPE + MLA_ROPE) ** -0.5 * float(np.log2(np.e))
    ql = p_ref[0, :, 0:256]
    msq = jnp.sum(ql * ql, axis=-1, keepdims=True) * (1.0 / MLA_Q_RANK)
    qn = (ql * lax.rsqrt(msq + NORM_EPS) * gq_ref[...]).astype(BF16)
    kvl = p_ref[0, :, 256:384]
    msk = jnp.mean(kvl * kvl, axis=-1, keepdims=True)
    kvn = (kvl * lax.rsqrt(msk + NORM_EPS) * gk_ref[...]).astype(BF16)
    q = jnp.dot(qn, wq_ref[...], preferred_element_type=F32)
    qs = jnp.dot(qn, wqs_ref[...], preferred_element_type=F32)
    cq = jnp.concatenate([cq_ref[...]] * MLA_HEADS, axis=-1)
    sq = jnp.concatenate([sq_ref[...]] * MLA_HEADS, axis=-1)
    q_out[0] = ((q * cq + qs * sq) * scale).astype(BF16)
    pe = (p_ref[0, :, 384:512] * tk_ref[...]).astype(BF16)
    k = (jnp.dot(kvn, wk_ref[...], preferred_element_type=F32)
         + jnp.dot(pe, e_ref[...], preferred_element_type=F32))
    k_out[0] = k.astype(BF16)
    lane = lax.broadcasted_iota(jnp.int32, (1, MLA_HEADS * LANE), 1) % LANE
    ones = jnp.where(lane >= MLA_V, 1.0, 0.0)
    v_out[0] = (jnp.dot(kvn, wv_ref[...], preferred_element_type=F32) + ones).astype(BF16)


def _mla_attn_kernel(q_ref, k_ref, v_ref, o_ref, *, n_ctx, tq):
    S = k_ref.shape[1]

    lo = lax.broadcasted_iota(jnp.int32, (tq, LANE), 1) < MLA_V

    def attend(nk):
        pv = []
        for h in range(MLA_HEADS):
            q = q_ref[0, :, LANE * h:LANE * (h + 1)]
            k = k_ref[0, 0:nk, LANE * h:LANE * (h + 1)]
            s = lax.dot_general(q, k, (((1,), (1,)), ((), ())), preferred_element_type=F32)
            p = jnp.exp2(s - jnp.max(s, axis=-1, keepdims=True))
            pv.append(jnp.dot(p.astype(BF16), v_ref[0, 0:nk, LANE * h:LANE * (h + 1)], preferred_element_type=F32))
        for pair in range(MLA_HEADS // 2):
            a, b = pv[2 * pair], pv[2 * pair + 1]
            o_ref[0, :, LANE * pair:LANE * (pair + 1)] = jnp.where(
                lo, a / pltpu.roll(a, MLA_V, axis=1), pltpu.roll(b, MLA_V, axis=1) / b).astype(o_ref.dtype)

    is_ctx = (pl.program_id(1) + 1) * tq <= n_ctx
    pl.when(is_ctx)(lambda: attend(n_ctx))
    pl.when(jnp.logical_not(is_ctx))(lambda: attend(S))


def _mla_mixer(p_mla, weights, tables, n_ctx, tm=384, tq=256):
    B, S, _ = p_mla.shape
    gq, wq, wq_sw, gk, wk, wv, e = weights
    cq, sq, tk = tables
    full = lambda a: pl.BlockSpec(a.shape, lambda b, i: (0,) * a.ndim)
    rows = lambda w: pl.BlockSpec((tm, w), lambda b, i: (i, 0))
    q, k, v = pl.pallas_call(
        _mla_prep_kernel,
        grid=(B, S // tm),
        in_specs=[pl.BlockSpec((1, tm, W_MLA), lambda b, i: (b, i, 0)), rows(128), rows(128), rows(128),
                  full(gq), full(wq), full(wq_sw), full(gk), full(wk), full(wv), full(e)],
        out_specs=[pl.BlockSpec((1, tm, 512), lambda b, i: (b, i, 0)),
                   pl.BlockSpec((1, tm, 512), lambda b, i: (b, i, 0)),
                   pl.BlockSpec((1, tm, 512), lambda b, i: (b, i, 0))],
        out_shape=[jax.ShapeDtypeStruct((B, S, 512), BF16)] * 3,
        compiler_params=_cparams(("parallel", "parallel")),
        name="mla_prep",
    )(p_mla, cq, sq, tk, gq, wq, wq_sw, gk, wk, wv, e)
    assert n_ctx % tq == 0 and S % tq == 0
    return pl.pallas_call(
        functools.partial(_mla_attn_kernel, n_ctx=n_ctx, tq=tq),
        grid=(B, S // tq),
        in_specs=[pl.BlockSpec((1, tq, 512), lambda b, i: (b, i, 0)),
                  pl.BlockSpec((1, S, 512), lambda b, i: (b, 0, 0)),
                  pl.BlockSpec((1, S, 512), lambda b, i: (b, 0, 0))],
        out_specs=pl.BlockSpec((1, tq, GROUP_W), lambda b, i: (b, i, 0)),
        out_shape=jax.ShapeDtypeStruct((B, S, GROUP_W), BF16),
        compiler_params=_cparams(("parallel", "arbitrary")),
        name="mla_attn",
    )(q, k, v)


def _swa_tables(n_lat, n_ctx):
    ang = _axial_angles(n_lat, SWA_HD)
    c64, s64 = _rope_tables(ang, n_ctx)
    return jnp.concatenate([c64, c64], axis=-1), jnp.concatenate([s64, s64], axis=-1)


def _swa_prep_kernel(p_ref, c_ref, s_ref, q_out, k_out, v_out):
    scale = SWA_HD ** -0.5
    c, s = c_ref[...], s_ref[...]
    c2 = jnp.concatenate([c, c], axis=-1)
    s2 = jnp.concatenate([s, s], axis=-1)
    q_out[0] = ((p_ref[0, :, 0:256] * c2 + p_ref[0, :, 512:768] * s2) * scale).astype(BF16)
    k_out[0] = (p_ref[0, :, 256:384] * c + p_ref[0, :, 768:896] * s).astype(BF16)
    v_out[0] = p_ref[0, :, 384:512].astype(BF16)


def _swa_attn_kernel(sink_ref, q_ref, kc_ref, vc_ref, kp_ref, kn_ref, kx_ref, vp_ref, vn_ref, vx_ref, bias_ref,
                     o_ref, *, n_ctx, blk):
    j = pl.program_id(1)
    ncb = n_ctx // blk
    lane = lax.broadcasted_iota(jnp.int32, (blk, LANE), 1)
    lo = lane < SWA_HD
    qa, qb = q_ref[0, :, 0:LANE], q_ref[0, :, LANE:2 * LANE]
    zero = jnp.zeros((), BF16)
    qs = jnp.concatenate([jnp.where(lo, qa, zero), jnp.where(lo, qb, zero),
                          jnp.where(lo, zero, qa), jnp.where(lo, zero, qb)], axis=0)
    rb = lax.broadcasted_iota(jnp.int32, (4 * blk, 1), 0) // blk
    sink = jnp.where(rb == 0, sink_ref[0], jnp.where(rb == 1, sink_ref[1],
                                                     jnp.where(rb == 2, sink_ref[2], sink_ref[3])))

    def finish(s, vcat):
        m = jnp.maximum(jnp.max(s, axis=-1, keepdims=True), sink)
        p = jnp.exp(s - m)
        l = jnp.sum(p, axis=-1, keepdims=True) + jnp.exp(sink - m)
        r = jnp.dot(p.astype(BF16), vcat, preferred_element_type=F32) / l
        o_ref[0, :, 0:LANE] = jnp.where(lo, r[0:blk], r[2 * blk:3 * blk]).astype(o_ref.dtype)
        o_ref[0, :, LANE:2 * LANE] = jnp.where(lo, r[blk:2 * blk], r[3 * blk:4 * blk]).astype(o_ref.dtype)

    def ctx_block():
        s = lax.dot_general(qs, kc_ref[0], (((1,), (1,)), ((), ())), preferred_element_type=F32)
        finish(s, vc_ref[0])

    def lat_block():
        kcat = jnp.concatenate([kc_ref[0], kp_ref[0], kn_ref[0], kx_ref[0]], axis=0)
        vcat = jnp.concatenate([vc_ref[0], vp_ref[0], vn_ref[0], vx_ref[0]], axis=0)
        s = lax.dot_general(qs, kcat, (((1,), (1,)), ((), ())), preferred_element_type=F32)
        finish(s + jnp.concatenate([bias_ref[0]] * SWA_HEADS, axis=0), vcat)

    pl.when(j < ncb)(ctx_block)
    pl.when(j >= ncb)(lat_block)


def _swa_mixer(p_swa, sink, tables, n_ctx, tm=384, blk=128):
    B, S, _ = p_swa.shape
    c, s = tables
    q, k, v = pl.pallas_call(
        _swa_prep_kernel,
        grid=(B, S // tm),
        in_specs=[pl.BlockSpec((1, tm, W_SWA), lambda b, i: (b, i, 0)),
                  pl.BlockSpec((tm, LANE), lambda b, i: (i, 0)), pl.BlockSpec((tm, LANE), lambda b, i: (i, 0))],
        out_specs=[pl.BlockSpec((1, tm, 256), lambda b, i: (b, i, 0)),
                   pl.BlockSpec((1, tm, LANE), lambda b, i: (b, i, 0)),
                   pl.BlockSpec((1, tm, LANE), lambda b, i: (b, i, 0))],
        out_shape=[jax.ShapeDtypeStruct((B, S, 256), BF16), jax.ShapeDtypeStruct((B, S, LANE), BF16),
                   jax.ShapeDtypeStruct((B, S, LANE), BF16)],
        compiler_params=_cparams(("parallel", "parallel")),
        name="swa_prep",
    )(p_swa, c, s)
    nb = S // blk
    ncb = n_ctx // blk
    cur = lambda b, j: (b, j, 0)
    prev = lambda b, j: (b, jnp.maximum(j - 1, ncb), 0)
    nxt = lambda b, j: (b, jnp.minimum(j + 1, nb - 1), 0)
    kv = lambda f: pl.BlockSpec((1, blk, LANE), f)
    ctx = pl.BlockSpec((1, n_ctx, LANE), lambda b, j: (b, 0, 0))
    t = np.arange(blk)[:, None]
    c = np.arange(n_ctx + 3 * blk)[None, :] - n_ctx
    bias = np.stack([np.where((c < 0) | ((c >= 0) & (c < blk) & (c >= t) & bool(hp)) | ((c >= blk) & (c < 2 * blk))
                              | ((c >= 2 * blk) & (c - 2 * blk <= t) & bool(hn)), 0.0, -1e30)
                     for hp in (0, 1) for hn in (0, 1)]).astype(np.float32)
    kind = lambda b, j: (2 * (j > ncb).astype(jnp.int32) + (j < nb - 1).astype(jnp.int32), 0, 0)
    return pl.pallas_call(
        functools.partial(_swa_attn_kernel, n_ctx=n_ctx, blk=blk),
        grid=(B, nb),
        in_specs=[pl.BlockSpec(memory_space=pltpu.SMEM),
                  pl.BlockSpec((1, blk, 256), cur), ctx, ctx,
                  kv(prev), kv(cur), kv(nxt), kv(prev), kv(cur), kv(nxt),
                  pl.BlockSpec((1, blk, n_ctx + 3 * blk), kind)],
        out_specs=pl.BlockSpec((1, blk, GROUP_W), cur),
        out_shape=jax.ShapeDtypeStruct((B, S, GROUP_W), BF16),
        compiler_params=_cparams(("parallel", "arbitrary")),
        name="swa_attn",
    )(sink.astype(F32), q, k, v, k, k, k, v, v, v, jnp.asarray(bias))


def _bwd_chunk(j, ncc, nc):
    return jnp.where(j < ncc, ncc - 1 - j, nc - 1 - (j - ncc))


def _head_stack(x, width, heads):
    lane_h = lax.broadcasted_iota(jnp.int32, x.shape, 1) // width
    zero = jnp.zeros((), x.dtype)
    return jnp.concatenate([jnp.where(lane_h == h, x, zero) for h in range(heads)], axis=0)


def _head_unstack(y, rows, width, heads):
    lane_h = lax.broadcasted_iota(jnp.int32, (rows, heads * width), 1) // width
    out = y[0:rows]
    for h in range(1, heads):
        out = jnp.where(lane_h == h, y[h * rows:(h + 1) * rows], out)
    return out


RET_CHUNK = 256


def _ret_tables(log_gamma, n_ctx, S):
    C = RET_CHUNK
    t = jnp.arange(C, dtype=F32)
    rel = t[:, None] - t[None, :]
    lg = log_gamma.astype(F32)
    lg_l = jnp.repeat(lg, RET_DK, axis=1)
    dm, xi, zeta, dcay = [], [], [], []
    for d in range(2):
        r = rel if d == 0 else -rel
        dh = jnp.where(r >= 0, jnp.exp(jnp.maximum(r, 0.0)[None] * lg[d][:, None, None]), 0.0)
        dm.append(dh.reshape(RET_HEADS * C, C))
        tq = (t + 1.0) if d == 0 else (C - t)
        tk = (C - 1.0 - t) if d == 0 else t
        xi.append(jnp.exp(tq[:, None] * lg_l[d][None, :]))
        zeta.append(jnp.exp(tk[:, None] * lg_l[d][None, :]))
        dcay.append(jnp.broadcast_to(jnp.exp(C * lg_l[d])[:, None], (RET_HEADS * RET_DK, GROUP_W)))
    mask = (np.arange(128)[:, None] // RET_DK == np.arange(256)[None, :] // RET_DV).astype(np.float32)
    pos = jnp.arange(S, dtype=F32)
    inv = ROPE_THETA ** (-jnp.arange(RET_DK // 2, dtype=F32) / (RET_DK // 2))
    ang = pos[:, None] * inv
    cos, sin = jnp.cos(ang), jnp.sin(ang)
    c = jnp.tile(jnp.concatenate([cos, cos], axis=-1), (1, RET_HEADS))
    s = jnp.tile(jnp.concatenate([-sin, sin], axis=-1), (1, RET_HEADS))
    return (jnp.stack(dm), jnp.stack(xi), jnp.stack(zeta), jnp.stack(dcay), jnp.asarray(mask), c, s)


def _ret_kernel(qkf_ref, vf_ref, cf_ref, sf_ref, qkb_ref, vb_ref, cb_ref, sb_ref,
                dm_ref, xi_ref, zeta_ref, dcay_ref, mask_ref, of_ref, ob_ref, state):
    C = RET_CHUNK

    @pl.when(pl.program_id(1) == 0)
    def _():
        state[...] = jnp.zeros_like(state)

    dirs = ((qkf_ref, vf_ref, cf_ref, sf_ref, of_ref), (qkb_ref, vb_ref, cb_ref, sb_ref, ob_ref))
    for d, (qk_ref, v_ref, c_ref, s_ref, o_ref) in enumerate(dirs):
        c, s = c_ref[...], s_ref[...]
        qr = (qk_ref[0, :, 0:128] * c + qk_ref[0, :, 256:384] * s) * (RET_DK ** -0.5)
        kr = qk_ref[0, :, 128:256] * c + qk_ref[0, :, 384:512] * s
        vb = v_ref[0].astype(BF16)
        qs = _head_stack(qr.astype(BF16), RET_DK, RET_HEADS)
        sc = lax.dot_general(qs, kr.astype(BF16), (((1,), (1,)), ((), ())), preferred_element_type=F32)
        a = (sc * dm_ref[d]).astype(BF16)
        o = _head_unstack(jnp.dot(a, vb, preferred_element_type=F32), C, RET_DV, RET_HEADS)
        st = state[d]
        o = o + jnp.dot((qr * xi_ref[d]).astype(BF16), st.astype(BF16), preferred_element_type=F32)
        o_ref[0] = o.astype(o_ref.dtype)
        kz = (kr * zeta_ref[d]).astype(BF16)
        u = lax.dot_general(kz, vb, (((0,), (0,)), ((), ())), preferred_element_type=F32)
        state[d] = st * dcay_ref[d] + u * mask_ref[...]


def _ret_mixer(p_ret, tables, n_ctx):
    B, S, _ = p_ret.shape
    C = RET_CHUNK
    dm, xi, zeta, dcay, mask, c, s = tables
    nc, ncc = S // C, n_ctx // C
    fwd = lambda b, j: (b, j, 0)
    bwd = lambda b, j: (b, _bwd_chunk(j, ncc, nc), 0)
    fwd_v = lambda b, j: (b, j, 2)
    bwd_v = lambda b, j: (b, _bwd_chunk(j, ncc, nc), 2)
    fwd_t = lambda b, j: (j, 0)
    bwd_t = lambda b, j: (_bwd_chunk(j, ncc, nc), 0)
    full = lambda a: pl.BlockSpec(a.shape, lambda b, j: (0,) * a.ndim)
    return pl.pallas_call(
        _ret_kernel,
        grid=(B, nc),
        in_specs=[pl.BlockSpec((1, C, 512), fwd), pl.BlockSpec((1, C, 256), fwd_v),
                  pl.BlockSpec((C, LANE), fwd_t), pl.BlockSpec((C, LANE), fwd_t),
                  pl.BlockSpec((1, C, 512), bwd), pl.BlockSpec((1, C, 256), bwd_v),
                  pl.BlockSpec((C, LANE), bwd_t), pl.BlockSpec((C, LANE), bwd_t),
                  full(dm), full(xi), full(zeta), full(dcay), full(mask)],
        out_specs=[pl.BlockSpec((1, C, GROUP_W), fwd), pl.BlockSpec((1, C, GROUP_W), bwd)],
        out_shape=[jax.ShapeDtypeStruct((B, S, GROUP_W), BF16)] * 2,
        scratch_shapes=[pltpu.VMEM((2, RET_HEADS * RET_DK, GROUP_W), F32)],
        compiler_params=_cparams(("parallel", "arbitrary")),
        name="retention_scan",
    )(p_ret, p_ret, c, s, p_ret, p_ret, c, s, dm, xi, zeta, dcay, mask)


HG_CHUNK = 128


def _hg_level_tables(C):
    levels = []
    n = 2
    while n <= C:
        levels.append(n)
        n *= 2
    t = np.arange(C)
    mats, qroles, kroles = [], [], []
    for d in range(2):
        rows = []
        qr_d, kr_d = [], []
        for n in levels:
            m = n // 2
            start = (t // n) * n
            mid = start + m
            u = t - start
            M = np.zeros((C, C), np.float32)
            for i in range(C):
                if d == 0:
                    if u[i] >= m:
                        M[i, mid[i]:i + 1] = 1.0
                    else:
                        M[i, i + 1:mid[i]] = 1.0
                else:
                    if u[i] < m:
                        M[i, i:mid[i]] = 1.0
                    else:
                        M[i, mid[i]:i] = 1.0
            rows.append(M)
            qr_d.append((u >= m) if d == 0 else (u < m))
            kr_d.append((u < m) if d == 0 else (u >= m))
        Mq = np.zeros((C, C), np.float32)
        Mk = np.zeros((C, C), np.float32)
        for i in range(C):
            if d == 0:
                Mq[i, 0:i + 1] = 1.0
                Mk[i, i + 1:C] = 1.0
            else:
                Mq[i, i:C] = 1.0
                Mk[i, 0:i] = 1.0
        rows += [Mq, Mk, np.ones((8, C), np.float32)]
        mats.append(np.concatenate(rows, axis=0))
        qroles.append(np.stack(qr_d))
        kroles.append(np.stack(kr_d))
    same_block = np.stack([(t[:, None] // n == t[None, :] // n) for n in levels]).astype(np.float32)
    return levels, np.stack(mats), np.stack(qroles), np.stack(kroles), same_block


def _hg_kernel(qf_ref, zf_ref, if_ref, qb_ref, zb_ref, ib_ref, lb_ref, m_ref, role_ref, blk_ref,
               ones_ref, mask_ref, of_ref, ob_ref, state, *, n_levels):
    C = HG_CHUNK

    @pl.when(pl.program_id(1) == 0)
    def _():
        state[...] = jnp.zeros_like(state)

    dirs = ((qf_ref, zf_ref, if_ref), (qb_ref, zb_ref, ib_ref))
    st = [state[0], state[1]]
    q, v, vb, kk, r = [], [], [], [], []
    for d, (q_ref, z_ref, i_ref) in enumerate(dirs):
        z = z_ref[0]
        lb = lb_ref[d]
        lbf = jnp.maximum(lb, LB_FLOOR)
        t = jnp.exp(-jnp.abs(z))
        rcp = 1.0 / (1.0 + t)
        pos = z >= 0.0
        sig = jnp.where(pos, rcp, t * rcp)
        nsig = jnp.where(pos, t * rcp, rcp)
        logf = jnp.log(lbf + (1.0 - lb) * sig)
        kk.append((1.0 - lb) * nsig - (lbf - lb))
        hi, lo = _split2(logf)
        r.append(jnp.dot(m_ref[d], hi, preferred_element_type=F32)
                 + jnp.dot(m_ref[d], lo, preferred_element_type=F32))
        q.append(q_ref[0] * (HG_DK ** -0.5))
        v.append(i_ref[0])
        vb.append(v[d].astype(BF16))
    acc = [jnp.zeros((HG_HEADS * C, C), F32), jnp.zeros((HG_HEADS * C, C), F32)]
    for lv in range(n_levels):
        for d in range(2):
            e = jnp.exp(r[d][lv * C:(lv + 1) * C])
            roles = role_ref[d, lv]
            qe = jnp.where(roles > 0.5, q[d] * e, 0.0).astype(BF16)
            ke = jnp.where(roles > 0.5, 0.0, kk[d] * e).astype(BF16)
            a = lax.dot_general(_head_stack(qe, HG_DK, HG_HEADS), ke, (((1,), (1,)), ((), ())),
                                preferred_element_type=F32)
            acc[d] = acc[d] + a * blk_ref[lv]
    outs = []
    for d in range(2):
        o = _head_unstack(jnp.dot(acc[d].astype(BF16), vb[d], preferred_element_type=F32), C, HG_DV, HG_HEADS)
        o = o + jnp.dot((q[d] * kk[d]).astype(BF16), ones_ref[...], preferred_element_type=F32) * v[d]
        e_q = jnp.exp(r[d][n_levels * C:(n_levels + 1) * C])
        o = o + lax.dot_general((q[d] * e_q).astype(BF16), st[d].astype(BF16), (((1,), (1,)), ((), ())),
                                preferred_element_type=F32)
        outs.append(o)
    new_state = []
    for d in range(2):
        e_k = jnp.exp(r[d][(n_levels + 1) * C:(n_levels + 2) * C])
        ku = (kk[d] * e_k).astype(BF16)
        u = lax.dot_general(vb[d], ku, (((0,), (0,)), ((), ())), preferred_element_type=F32)
        dtot = jnp.exp(r[d][(n_levels + 2) * C:(n_levels + 2) * C + 1])
        new_state.append(st[d] * dtot + u * mask_ref[...])
    of_ref[0] = outs[0].astype(of_ref.dtype)
    ob_ref[0] = outs[1].astype(ob_ref.dtype)
    state[0] = new_state[0]
    state[1] = new_state[1]


def _hg_mixer(p_hg, lb, n_ctx):
    B, S, _ = p_hg.shape
    C = HG_CHUNK
    levels, mats, qroles, kroles, same_block = _hg_level_tables(C)
    nl = len(levels)
    m_all = jnp.asarray(mats, BF16)
    roles = jnp.asarray(np.broadcast_to(qroles[..., None], (2, nl, C, GROUP_W)).astype(np.float32))
    blk = jnp.asarray(np.tile(same_block, (1, HG_HEADS, 1)))
    hd = np.arange(GROUP_W) // HG_DK
    bd = (hd[:, None] == hd[None, :]).astype(np.float32)
    ones_bd = jnp.asarray(bd, BF16)
    mask = jnp.asarray(bd)
    nc, ncc = S // C, n_ctx // C
    col = lambda k, rev: (lambda b, j: (b, _bwd_chunk(j, ncc, nc) if rev else j, k))
    blkspec = lambda k, rev: pl.BlockSpec((1, C, GROUP_W), col(k, rev))
    full = lambda a: pl.BlockSpec(a.shape, lambda b, j: (0,) * a.ndim)
    lb3 = lb.reshape(2, 1, GROUP_W).astype(F32)
    return pl.pallas_call(
        functools.partial(_hg_kernel, n_levels=nl),
        grid=(B, nc),
        in_specs=[blkspec(0, False), blkspec(1, False), blkspec(3, False),
                  blkspec(0, True), blkspec(2, True), blkspec(3, True),
                  full(lb3), full(m_all), full(roles), full(blk), full(ones_bd), full(mask)],
        out_specs=[blkspec(0, False), blkspec(0, True)],
        out_shape=[jax.ShapeDtypeStruct((B, S, GROUP_W), BF16)] * 2,
        scratch_shapes=[pltpu.VMEM((2, GROUP_W, GROUP_W), F32)],
        compiler_params=_cparams(("parallel", "arbitrary")),
        name="hgrn2_scan",
    )(p_hg, p_hg, p_hg, p_hg, p_hg, p_hg, lb3, m_all, roles, blk, ones_bd, mask)


_SWA_OUT_PERM = np.concatenate([np.arange(64 * h, 64 * h + 64) for h in (0, 2, 1, 3)])


def _wout_kernel(h_ref, m_ref, a_ref, hf_ref, hb_ref, hgate_ref, c_ref, rf_ref, rb_ref, rgate_ref,
                 hgn_ref, rgg_ref, rgb_ref, ones_ref, w_ref, o_ref, *, tm, n_ctx, row_off):
    row0 = row_off + pl.program_id(1) * tm
    ones = ones_ref[...]
    o = hf_ref[0].astype(F32) + hb_ref[0].astype(F32)
    yb = o * lax.rsqrt(_seg_mean(o * o, ones) + NORM_EPS) * hgn_ref[...] * _silu(hgate_ref[0])
    o = rf_ref[0].astype(F32) + rb_ref[0].astype(F32)
    xc = o - _seg_mean(o, ones)
    yd = (xc * lax.rsqrt(_seg_mean(xc * xc, ones) + NORM_EPS) * rgg_ref[...] + rgb_ref[...]) * _silu(rgate_ref[0])
    acc = jnp.dot(a_ref[0].astype(BF16), w_ref[0:256, :], preferred_element_type=F32)
    acc += jnp.dot(yb.astype(BF16), w_ref[256:512, :], preferred_element_type=F32)
    acc += jnp.dot(c_ref[0].astype(BF16), w_ref[512:768, :], preferred_element_type=F32)
    acc += jnp.dot(yd.astype(BF16), w_ref[768:1024, :], preferred_element_type=F32)
    o_ref[0] = h_ref[0] + _row_mod(m_ref, 2, row0, tm, n_ctx) * acc


def _out_proj(h, mod, o_mla, hg_f, hg_b, p_hg, o_swa, ret_f, ret_b, p_ret, hg_norm_g, ret_gn_g, ret_gn_b,
              w_out_p, n_ctx, tm=384, row_off=0):
    B, S, D = h.shape
    assert row_off % tm == 0 and (S - row_off) % tm == 0
    off = row_off // tm
    hd = np.arange(GROUP_W) // HG_DV
    ones_bd = jnp.asarray((hd[:, None] == hd[None, :]).astype(np.float32) / HG_DV, BF16)
    row = lambda w, k=0: pl.BlockSpec((1, tm, w), lambda b, i: (b, i + off, k))
    vec = pl.BlockSpec((1, GROUP_W), lambda b, i: (0, 0))
    return pl.pallas_call(
        functools.partial(_wout_kernel, tm=tm, n_ctx=n_ctx, row_off=row_off),
        grid=(B, (S - row_off) // tm),
        in_specs=[row(D), pl.BlockSpec((1, 2, 8, D), lambda b, i: (b, 0, 0, 0)),
                  row(GROUP_W), row(GROUP_W), row(GROUP_W), row(GROUP_W, 4),
                  row(GROUP_W), row(GROUP_W), row(GROUP_W), row(GROUP_W, 3),
                  vec, vec, vec, pl.BlockSpec((GROUP_W, GROUP_W), lambda b, i: (0, 0)),
                  pl.BlockSpec((D, D), lambda b, i: (0, 0))],
        out_specs=pl.BlockSpec((1, tm, D), lambda b, i: (b, i, 0)),
        out_shape=jax.ShapeDtypeStruct((B, S - row_off, D), F32),
        compiler_params=_cparams(("parallel", "parallel")),
        name="out_proj",
    )(h, mod, o_mla, hg_f, hg_b, p_hg, o_swa, ret_f, ret_b, p_ret,
      hg_norm_g.reshape(1, GROUP_W), ret_gn_g.reshape(1, GROUP_W), ret_gn_b.reshape(1, GROUP_W), ones_bd, w_out_p)


def _ffn_kernel(h_ref, m_ref, g_ref, wg_ref, wu_ref, wd_ref, o_ref, xn, *, tm, n_ctx):
    j = pl.program_id(2)
    row0 = pl.program_id(1) * tm

    @pl.when(j == 0)
    def _():
        xn[...] = _norm_mod(h_ref[0], g_ref[...], _row_mod(m_ref, 3, row0, tm, n_ctx),
                            _row_mod(m_ref, 4, row0, tm, n_ctx)).astype(BF16)
        o_ref[...] = jnp.zeros_like(o_ref)

    wg, wu, wd = wg_ref[0], wu_ref[0], wd_ref[0]
    half = tm // 2
    for r0 in (0, half):
        x = xn[r0:r0 + half]
        hg = jnp.dot(x, wg, preferred_element_type=F32)
        hu = jnp.dot(x, wu, preferred_element_type=F32)
        o_ref[0, r0:r0 + half] += jnp.dot((_silu(hg) * hu).astype(BF16), wd, preferred_element_type=F32)

    @pl.when(j == pl.num_programs(2) - 1)
    def _():
        o_ref[0] = h_ref[0] + _row_mod(m_ref, 5, row0, tm, n_ctx) * o_ref[0]


def _dense_ffn(h, mod, g, w_gate, w_up, w_down, li, n_ctx, tm=1152, tf=1408):
    B, S, D = h.shape
    dff = w_gate.shape[-1]
    return pl.pallas_call(
        functools.partial(_ffn_kernel, tm=tm, n_ctx=n_ctx),
        grid=(B, S // tm, dff // tf),
        in_specs=[pl.BlockSpec((1, tm, D), lambda b, i, j: (b, i, 0)),
                  pl.BlockSpec((1, 2, 8, D), lambda b, i, j: (b, 0, 0, 0)),
                  pl.BlockSpec((1, D), lambda b, i, j: (0, 0)),
                  pl.BlockSpec((1, D, tf), lambda b, i, j: (li, 0, j)),
                  pl.BlockSpec((1, D, tf), lambda b, i, j: (li, 0, j)),
                  pl.BlockSpec((1, tf, D), lambda b, i, j: (li, j, 0))],
        out_specs=pl.BlockSpec((1, tm, D), lambda b, i, j: (b, i, 0)),
        out_shape=jax.ShapeDtypeStruct((B, S, D), F32),
        scratch_shapes=[pltpu.VMEM((tm, D), BF16)],
        compiler_params=_cparams(("parallel", "parallel", "arbitrary")),
        name="dense_ffn",
    )(h, mod, g.reshape(1, D), w_gate, w_up, w_down)


MOE_TM = 1024
MOE_TF = 512
MOE_ROW_PARTS = 4
MOE_ROWS = (384, 256)
DMA_UNROLL = 8
SUBLANES = 8
ZERO_ROWS = 256


def _router_kernel(h_ref, m_ref, g_ref, wr_ref, tri_ref, xn_ref, route_ref, cnt_ref, counts, *, tm, n_ctx):
    b, i = pl.program_id(0), pl.program_id(1)
    row0 = i * tm

    @pl.when((b == 0) & (i == 0))
    def _():
        counts[...] = jnp.zeros_like(counts)

    a = _norm_mod(h_ref[0], g_ref[...], _row_mod(m_ref, 3, row0, tm, n_ctx), _row_mod(m_ref, 4, row0, tm, n_ctx))
    xn_ref[0] = a
    logits = jnp.dot(a, wr_ref[...], preferred_element_type=F32, precision=lax.Precision.HIGHEST)
    lane = lax.broadcasted_iota(jnp.int32, logits.shape, 1)
    neg = jnp.float32(-jnp.inf)
    lg = jnp.where(lane < N_EXPERTS, logits, neg)
    v1 = jnp.max(lg, axis=-1, keepdims=True)
    i1 = jnp.min(jnp.where(lg == v1, lane, LANE), axis=-1, keepdims=True)
    lg2 = jnp.where(lane == i1, neg, lg)
    v2 = jnp.max(lg2, axis=-1, keepdims=True)
    i2 = jnp.min(jnp.where(lg2 == v2, lane, LANE), axis=-1, keepdims=True)
    e = jnp.exp(v2 - v1)
    w1 = 1.0 / (1.0 + e)
    w2 = e / (1.0 + e)
    oh1, oh2 = lane == i1, lane == i2
    onehot = jnp.where(oh1 | oh2, 1.0, 0.0)
    earlier = jnp.dot(tri_ref[...], onehot.astype(BF16), preferred_element_type=F32)
    pos = earlier + counts[...]
    r1 = jnp.sum(jnp.where(oh1, pos, 0.0), axis=-1, keepdims=True)
    r2 = jnp.sum(jnp.where(oh2, pos, 0.0), axis=-1, keepdims=True)
    counts[...] = counts[...] + jnp.sum(onehot, axis=0, keepdims=True)
    cnt_ref[...] = jnp.broadcast_to(counts[...], cnt_ref.shape)
    route_ref[0] = jnp.where(lane == 0, i1.astype(F32),
                             jnp.where(lane == 1, i2.astype(F32),
                                       jnp.where(lane == 2, w1,
                                                 jnp.where(lane == 3, w2,
                                                           jnp.where(lane == 4, r1, jnp.where(lane == 5, r2, 0.0))))))


def _scatter_kernel(meta_ref, slot_ref, x_ref, xs_hbm, zeros_v, sem, *, tm, tile_rows, total_rows):
    b, i = pl.program_id(0), pl.program_id(1)

    def start(r, carry):
        for k in range(TOP_K):
            pltpu.make_async_copy(x_ref.at[0, pl.ds(r, 1)], xs_hbm.at[pl.ds(slot_ref[0, 0, 0, k * tm + r], 1)],
                                  sem).start()
        return carry

    lax.fori_loop(0, tm, start, 0, unroll=DMA_UNROLL)
    for k in range(TOP_K):
        pltpu.make_async_copy(x_ref.at[0], xs_hbm.at[pl.ds(0, tm)], sem).wait()

    @pl.when((b == pl.num_programs(0) - 1) & (i == pl.num_programs(1) - 1))
    def _():
        zeros_v[...] = jnp.zeros_like(zeros_v)
        z1 = zeros_v.at[pl.ds(0, 1)]
        z8 = zeros_v.at[pl.ds(0, SUBLANES)]

        def fill(lo, end):
            mid = jnp.minimum(((lo + SUBLANES - 1) // SUBLANES) * SUBLANES, end)
            big = mid + ((end - mid) // ZERO_ROWS) * ZERO_ROWS
            n_row, n_big, n_small = mid - lo, (big - mid) // ZERO_ROWS, (end - big) // SUBLANES

            def start_row(r, c):
                pltpu.make_async_copy(z1, xs_hbm.at[pl.ds(lo + r, 1)], sem).start()
                return c

            def wait_row(r, c):
                pltpu.make_async_copy(z1, xs_hbm.at[pl.ds(0, 1)], sem).wait()
                return c

            def start_big(q, c):
                dst = xs_hbm.at[pl.ds(pl.multiple_of(mid + q * ZERO_ROWS, SUBLANES), ZERO_ROWS)]
                pltpu.make_async_copy(zeros_v, dst, sem).start()
                return c

            def wait_big(q, c):
                pltpu.make_async_copy(zeros_v, xs_hbm.at[pl.ds(0, ZERO_ROWS)], sem).wait()
                return c

            def start_small(q, c):
                dst = xs_hbm.at[pl.ds(pl.multiple_of(big + q * SUBLANES, SUBLANES), SUBLANES)]
                pltpu.make_async_copy(z8, dst, sem).start()
                return c

            def wait_small(q, c):
                pltpu.make_async_copy(z8, xs_hbm.at[pl.ds(0, SUBLANES)], sem).wait()
                return c

            lax.fori_loop(0, n_row, start_row, 0)
            lax.fori_loop(0, n_big, start_big, 0)
            lax.fori_loop(0, n_small, start_small, 0)
            lax.fori_loop(0, n_row, wait_row, 0)
            lax.fori_loop(0, n_big, wait_big, 0)
            lax.fori_loop(0, n_small, wait_small, 0)

        for e in range(N_EXPERTS):
            c = meta_ref[e]
            g0 = meta_ref[N_EXPERTS + e]
            fill(g0 + c, g0 + ((c + tile_rows - 1) // tile_rows) * tile_rows)
        fill(meta_ref[2 * N_EXPERTS] * tile_rows, jnp.int32(total_rows))


def _expert_kernel(te_ref, tv_ref, nu_ref, x_ref, wg_ref, wu_ref, wd_ref, o_ref, xb, wb, wdb):
    i, j = pl.program_id(0), pl.program_id(1)
    valid = tv_ref[i]
    quarter = x_ref.shape[0] // MOE_ROW_PARTS

    @pl.when(j == 0)
    def _():
        xb[...] = x_ref[...].astype(BF16)
        o_ref[...] = jnp.zeros_like(o_ref)

    @pl.when(valid > 0)
    def _():
        wb[0] = wg_ref[0, 0].astype(BF16)
        wb[1] = wu_ref[0, 0].astype(BF16)
        wdb[...] = wd_ref[0, 0].astype(BF16)

    for part in range(MOE_ROW_PARTS):
        r0 = part * quarter

        @pl.when(r0 < valid)
        def _():
            x = xb[r0:r0 + quarter]
            hg = jnp.dot(x, wb[0], preferred_element_type=F32)
            hu = jnp.dot(x, wb[1], preferred_element_type=F32)
            o_ref[r0:r0 + quarter] += jnp.dot((_silu(hg) * hu).astype(BF16), wdb[...], preferred_element_type=F32)


def _combine_kernel(slot_ref, h_ref, m_ref, route_ref, fg_ref, y_hbm, o_ref, buf, sem, *, tm, n_ctx, final_norm):
    row0 = pl.program_id(1) * tm

    def start(r, carry):
        for k in range(TOP_K):
            pltpu.make_async_copy(y_hbm.at[pl.ds(slot_ref[0, 0, 0, k * tm + r], 1)], buf.at[k, pl.ds(r, 1)], sem).start()
        return carry

    lax.fori_loop(0, tm, start, 0, unroll=DMA_UNROLL)
    for k in range(TOP_K):
        pltpu.make_async_copy(y_hbm.at[pl.ds(0, tm)], buf.at[k], sem).wait()
    w1 = route_ref[0, :, 2:3]
    w2 = route_ref[0, :, 3:4]
    y = h_ref[0] + _row_mod(m_ref, 5, row0, tm, n_ctx) * (w1 * buf[0] + w2 * buf[1])
    if final_norm:
        y = y * lax.rsqrt(jnp.mean(y * y, axis=-1, keepdims=True) + NORM_EPS) * fg_ref[...]
    o_ref[0] = y


def _moe_ffn(h, mod, g, w_router, w_gate, w_up, w_down, li, n_ctx, final_g=None):
    B, S, D = h.shape
    T = B * S
    cr = next(r for r in MOE_ROWS if S % r == 0)
    tm = cr
    dffe = w_gate.shape[-1]
    te_rows = MOE_TM if T % MOE_TM == 0 else 512
    n_tiles = TOP_K * T // te_rows + N_EXPERTS
    P = n_tiles * te_rows
    wr = jnp.pad(w_router, ((0, 0), (0, LANE - N_EXPERTS)))
    tri = jnp.asarray(np.tril(np.ones((tm, tm), np.float32), -1), BF16)
    xn, route, cnt = pl.pallas_call(
        functools.partial(_router_kernel, tm=tm, n_ctx=n_ctx),
        grid=(B, S // tm),
        in_specs=[pl.BlockSpec((1, tm, D), lambda b, i: (b, i, 0)),
                  pl.BlockSpec((1, 2, 8, D), lambda b, i: (b, 0, 0, 0)),
                  pl.BlockSpec((1, D), lambda b, i: (0, 0)),
                  pl.BlockSpec((D, LANE), lambda b, i: (0, 0)),
                  pl.BlockSpec((tm, tm), lambda b, i: (0, 0))],
        out_specs=[pl.BlockSpec((1, tm, D), lambda b, i: (b, i, 0)),
                   pl.BlockSpec((1, tm, LANE), lambda b, i: (b, i, 0)),
                   pl.BlockSpec((SUBLANES, LANE), lambda b, i: (0, 0))],
        out_shape=[jax.ShapeDtypeStruct((B, S, D), F32), jax.ShapeDtypeStruct((B, S, LANE), F32),
                   jax.ShapeDtypeStruct((SUBLANES, LANE), F32)],
        scratch_shapes=[pltpu.VMEM((1, LANE), F32)],
        compiler_params=_cparams(("arbitrary", "arbitrary")),
        name="moe_router",
    )(h, mod, g.reshape(1, D), wr, tri)

    counts = cnt[0, :N_EXPERTS].astype(jnp.int32)
    ptiles = (counts + te_rows - 1) // te_rows
    tile_end = jnp.cumsum(ptiles)
    gstart = (tile_end - ptiles) * te_rows
    n_used = tile_end[-1:].astype(jnp.int32)
    tile_ids = jnp.arange(n_tiles, dtype=jnp.int32)
    tile_e = jnp.minimum(jnp.sum((tile_ids[:, None] >= tile_end[None, :]).astype(jnp.int32), axis=1),
                         N_EXPERTS - 1)
    meta = jnp.concatenate([counts, gstart, n_used])
    tile_valid = jnp.clip(jnp.take(counts, tile_e) - (tile_ids - jnp.take(tile_end - ptiles, tile_e)) * te_rows,
                          0, te_rows)
    choice = route[..., 0:TOP_K].astype(jnp.int32)
    is_e = choice[..., None] == jnp.arange(N_EXPERTS, dtype=jnp.int32)
    slots = jnp.sum(jnp.where(is_e, gstart, 0), axis=-1) + route[..., 4:4 + TOP_K].astype(jnp.int32)
    slots = slots.reshape(B, S // cr, cr, TOP_K).transpose(0, 1, 3, 2)
    slots = slots.reshape(B, S // cr, 1, TOP_K * cr)

    xs = pl.pallas_call(
        functools.partial(_scatter_kernel, tm=cr, tile_rows=te_rows, total_rows=P),
        grid_spec=pltpu.PrefetchScalarGridSpec(
            num_scalar_prefetch=1,
            grid=(B, S // cr),
            in_specs=[pl.BlockSpec((1, 1, 1, TOP_K * cr), lambda b, i, mt: (b, i, 0, 0),
                                   memory_space=pltpu.SMEM),
                      pl.BlockSpec((1, cr, D), lambda b, i, mt: (b, i, 0))],
            out_specs=pl.BlockSpec(memory_space=pl.ANY),
            scratch_shapes=[pltpu.VMEM((ZERO_ROWS, D), F32), pltpu.SemaphoreType.DMA(())]),
        out_shape=jax.ShapeDtypeStruct((P, D), F32),
        compiler_params=_cparams(("arbitrary", "arbitrary")),
        name="moe_scatter",
    )(meta, slots, xn)

    nf = dffe // MOE_TF
    last = lambda i, nu: jnp.minimum(i, nu[0] - 1)
    jj = lambda i, j, nu: jnp.where(i < nu[0], j, nf - 1)
    ys = pl.pallas_call(
        _expert_kernel,
        grid_spec=pltpu.PrefetchScalarGridSpec(
            num_scalar_prefetch=3,
            grid=(n_tiles, nf),
            in_specs=[pl.BlockSpec((te_rows, D), lambda i, j, te, tv, nu: (last(i, nu), 0)),
                      pl.BlockSpec((1, 1, D, MOE_TF), lambda i, j, te, tv, nu: (li, te[last(i, nu)], 0, jj(i, j, nu))),
                      pl.BlockSpec((1, 1, D, MOE_TF), lambda i, j, te, tv, nu: (li, te[last(i, nu)], 0, jj(i, j, nu))),
                      pl.BlockSpec((1, 1, MOE_TF, D), lambda i, j, te, tv, nu: (li, te[last(i, nu)], jj(i, j, nu), 0))],
            out_specs=pl.BlockSpec((te_rows, D), lambda i, j, te, tv, nu: (i, 0)),
            scratch_shapes=[pltpu.VMEM((te_rows, D), BF16), pltpu.VMEM((2, D, MOE_TF), BF16),
                            pltpu.VMEM((MOE_TF, D), BF16)]),
        out_shape=jax.ShapeDtypeStruct((P, D), F32),
        compiler_params=_cparams(("arbitrary", "arbitrary")),
        name="moe_experts",
    )(tile_e, tile_valid.astype(jnp.int32), n_used, xs, w_gate, w_up, w_down)
    fg = jnp.ones((1, D), F32) if final_g is None else final_g.reshape(1, D).astype(F32)
    return pl.pallas_call(
        functools.partial(_combine_kernel, tm=cr, n_ctx=n_ctx, final_norm=final_g is not None),
        grid=(B, S // cr),
        in_specs=[pl.BlockSpec((1, 1, 1, TOP_K * cr), lambda b, i: (b, i, 0, 0), memory_space=pltpu.SMEM),
                  pl.BlockSpec((1, cr, D), lambda b, i: (b, i, 0)),
                  pl.BlockSpec((1, 2, 8, D), lambda b, i: (b, 0, 0, 0)),
                  pl.BlockSpec((1, cr, LANE), lambda b, i: (b, i, 0)),
                  pl.BlockSpec((1, D), lambda b, i: (0, 0)),
                  pl.BlockSpec(memory_space=pl.ANY)],
        out_specs=pl.BlockSpec((1, cr, D), lambda b, i: (b, i, 0)),
        out_shape=jax.ShapeDtypeStruct((B, S, D), F32),
        scratch_shapes=[pltpu.VMEM((TOP_K, cr, D), F32), pltpu.SemaphoreType.DMA(())],
        compiler_params=_cparams(("arbitrary", "arbitrary")),
        name="moe_combine",
    )(slots, h, mod, route, fg, ys)


def _final_norm_kernel(h_ref, g_ref, o_ref):
    x = h_ref[0]
    o_ref[0] = x * lax.rsqrt(jnp.mean(x * x, axis=-1, keepdims=True) + NORM_EPS) * g_ref[...]


def _final_norm(h, g, n_ctx, tm=256):
    B, S, D = h.shape
    N = S - n_ctx
    return pl.pallas_call(
        _final_norm_kernel,
        grid=(B, N // tm),
        in_specs=[pl.BlockSpec((1, tm, D), lambda b, i: (b, n_ctx // tm + i, 0)),
                  pl.BlockSpec((1, D), lambda b, i: (0, 0))],
        out_specs=pl.BlockSpec((1, tm, D), lambda b, i: (b, i, 0)),
        out_shape=jax.ShapeDtypeStruct((B, N, D), F32),
        compiler_params=_cparams(("parallel", "parallel")),
        name="final_norm",
    )(h, g.reshape(1, D))


MOD_ROWS = 16


def kernel(x, c, ctx, c_ctx, w_ada, b_ada, norm_mix_g, norm_ffn_g, w_in, w_out, mla_q_norm_g, mla_w_uq, mla_kv_norm_g, mla_w_ukv, hg_lb_logits, hg_norm_g, swa_sink, ret_decay_logit, ret_gn_g, ret_gn_b, ffn_w_gate, ffn_w_up, ffn_w_down, moe_w_router, moe_w_gate, moe_w_up, moe_w_down, final_norm_g):
    B, N, D = x.shape
    L = ctx.shape[1]
    S = L + N
    assert B + 1 <= MOD_ROWS
    h = jnp.concatenate([ctx, x], axis=1)
    cvec = jnp.zeros((MOD_ROWS, D), F32).at[0].set(c_ctx).at[1:1 + B].set(c)
    mod_all = _modulation(cvec, w_ada, b_ada).reshape(DEPTH, MOD_ROWS, 6, D)

    lb_p = jax.nn.softmax(hg_lb_logits.astype(F32), axis=0)
    hg_lb = jnp.cumsum(lb_p, axis=0) - lb_p[0:1]
    ret_log_gamma = jax.nn.log_sigmoid(ret_decay_logit.astype(F32))

    mla_tab = _mla_tables(N, L)
    swa_tab = _swa_tables(N, L)
    w_out_perm = np.concatenate([np.arange(512), 512 + _SWA_OUT_PERM, np.arange(768, 1024)])
    ffn_bf16 = tuple(w.astype(BF16) for w in (ffn_w_gate, ffn_w_up, ffn_w_down))

    for l in range(DEPTH):
        m = mod_all[l]
        mod = jnp.stack([jnp.broadcast_to(m[0], (B, 6, D)), m[1:1 + B]], axis=1)
        mod = jnp.pad(mod, ((0, 0), (0, 0), (0, 2), (0, 0)))
        w_ext = _take_cols(w_in[l], _EXT_COLS).astype(BF16)
        p_mla, p_hg, p_swa, p_ret = _in_proj(h, mod, norm_mix_g[l], w_ext, L)
        o_mla = _mla_mixer(p_mla, _mla_weights(mla_q_norm_g[l], mla_w_uq[l], mla_kv_norm_g[l], mla_w_ukv[l]),
                           mla_tab, L)
        hg_f, hg_b = _hg_mixer(p_hg, hg_lb[l], L)
        o_swa = _swa_mixer(p_swa, swa_sink[l], swa_tab, L)
        ret_f, ret_b = _ret_mixer(p_ret, _ret_tables(ret_log_gamma[l], L, S), L)
        w_out_p = jnp.take(w_out[l], jnp.asarray(w_out_perm), axis=0).astype(BF16)
        last = l == DEPTH - 1
        if last and l % 2 == 1:
            h = _out_proj(h, mod, o_mla, hg_f, hg_b, p_hg, o_swa, ret_f, ret_b, p_ret,
                          hg_norm_g[l], ret_gn_g[l], ret_gn_b[l], w_out_p, L, tm=math.gcd(L, N, 256), row_off=L)
            return _moe_ffn(h, mod, norm_ffn_g[l], moe_w_router[l // 2], moe_w_gate, moe_w_up, moe_w_down, l // 2,
                            0, final_g=final_norm_g)
        h = _out_proj(h, mod, o_mla, hg_f, hg_b, p_hg, o_swa, ret_f, ret_b, p_ret,
                      hg_norm_g[l], ret_gn_g[l], ret_gn_b[l], w_out_p, L)
        if l % 2 == 0:
            h = _dense_ffn(h, mod, norm_ffn_g[l], ffn_bf16[0], ffn_bf16[1], ffn_bf16[2], l // 2, L,
                           tm=1152 if S % 1152 == 0 else 384)
        else:
            h = _moe_ffn(h, mod, norm_ffn_g[l], moe_w_router[l // 2], moe_w_gate, moe_w_up, moe_w_down, l // 2, L)
    return _final_norm(h, final_norm_g, L)
```

```python
import functools
import math

import numpy as np
import jax
import jax.numpy as jnp
from jax import lax
from jax.experimental import pallas as pl
from jax.experimental.pallas import tpu as pltpu

F32 = jnp.float32
BF16 = jnp.bfloat16

D_MODEL = 1024
DEPTH = 4
GRID_W = 64
ROPE_THETA = 10000.0
NORM_EPS = 1e-6
LB_FLOOR = 1e-30
GROUP_W = 256

MLA_HEADS, MLA_NOPE, MLA_ROPE, MLA_V = 4, 64, 32, 64
MLA_Q_RANK, MLA_KV_RANK = 192, 128
HG_HEADS, HG_DK, HG_DV = 4, 64, 64
SWA_HEADS, SWA_KV_HEADS, SWA_HD, SWA_WINDOW = 4, 2, 64, 128
RET_HEADS, RET_DK, RET_DV = 4, 32, 64
N_EXPERTS, TOP_K = 8, 2

_IN_SIZES = (192, 128, 32, 256, 256, 256, 256, 256, 256, 128, 128, 128, 128, 256, 256)
_OFF = np.concatenate([[0], np.cumsum(_IN_SIZES)]).astype(np.int64)
D_IN = int(_OFF[-1])

W_MLA, W_HG, W_SWA, W_RET = 512, 1280, 896, 1024
W_EXT = W_MLA + W_HG + W_SWA + W_RET

LANE = 128
VMEM_LIMIT = 56 * 1024 * 1024


def _cparams(sem):
    return pltpu.CompilerParams(dimension_semantics=sem, vmem_limit_bytes=VMEM_LIMIT)


def _swap_halves(idx, width):
    idx = np.asarray(idx)
    base = (idx // width) * width
    j = idx % width
    return base + (j + width // 2) % width


def _ext_columns():
    z = lambda n: -np.ones((n,), np.int64)
    rng = lambda a, n: np.arange(a, a + n)
    o = _OFF
    k_pe = rng(o[2], 32)
    mla = np.concatenate([rng(o[0], 192), z(64), rng(o[1], 128), k_pe,
                          o[2] + _swap_halves(np.arange(32), 32), z(64)])
    hg = rng(o[3], 1280)
    q_heads = [rng(o[8] + 64 * h, 64) for h in range(4)]
    q_perm = np.concatenate([q_heads[0], q_heads[2], q_heads[1], q_heads[3]])
    k = rng(o[9], 128)
    v = rng(o[10], 128)
    sw = lambda cols, base: base + _swap_halves(cols - base, 64)
    swa = np.concatenate([q_perm, k, v, sw(q_perm, o[8]), sw(k, o[9])])
    rq, rk = rng(o[11], 128), rng(o[12], 128)
    ret = np.concatenate([rq, rk, o[11] + _swap_halves(rq - o[11], 32),
                          o[12] + _swap_halves(rk - o[12], 32), rng(o[13], 256), rng(o[14], 256)])
    cols = np.concatenate([mla, hg, swa, ret])
    assert cols.shape[0] == W_EXT
    return cols


_EXT_COLS = _ext_columns()


def _take_cols(w, cols):
    cols = np.asarray(cols)
    g = jnp.take(w, jnp.asarray(np.maximum(cols, 0)), axis=-1)
    return jnp.where(jnp.asarray(cols >= 0), g, jnp.zeros((), w.dtype))


def _axial_angles(n_tok, rot_dim):
    rows = n_tok // GRID_W
    row = jnp.broadcast_to(jnp.arange(rows)[:, None], (rows, GRID_W)).reshape(-1)
    col = jnp.broadcast_to(jnp.arange(GRID_W)[None, :], (rows, GRID_W)).reshape(-1)
    n_freq = rot_dim // 4
    inv = ROPE_THETA ** (-jnp.arange(n_freq, dtype=F32) / n_freq)
    return jnp.concatenate([row.astype(F32)[:, None] * inv, col.astype(F32)[:, None] * inv], axis=-1)


def _rope_tables(ang, n_ctx):
    cos, sin = jnp.cos(ang), jnp.sin(ang)
    c = jnp.concatenate([cos, cos], axis=-1)
    s = jnp.concatenate([-sin, sin], axis=-1)
    r = c.shape[-1]
    c = jnp.concatenate([jnp.ones((n_ctx, r), F32), c], axis=0)
    s = jnp.concatenate([jnp.zeros((n_ctx, r), F32), s], axis=0)
    return c, s


def _split2(x):
    hi = x.astype(BF16)
    lo = (x - hi.astype(F32)).astype(BF16)
    return hi, lo


def _seg_mean(x, ones_bd):
    hi, lo = _split2(x)
    return (jnp.dot(hi, ones_bd, preferred_element_type=F32)
            + jnp.dot(lo, ones_bd, preferred_element_type=F32))


def _silu(x):
    return x / (1.0 + jnp.exp(-x))


def _row_mod(m_ref, idx, row0, tm, n_ctx):
    rows = row0 + lax.broadcasted_iota(jnp.int32, (tm, 1), 0)
    mc = m_ref[0, 0, idx:idx + 1, :]
    ml = m_ref[0, 1, idx:idx + 1, :]
    return jnp.where(rows < n_ctx, mc, ml)


def _norm_mod(x, g, shift, scale):
    ms = jnp.mean(x * x, axis=-1, keepdims=True)
    y = x * lax.rsqrt(ms + NORM_EPS) * g
    return y * (1.0 + scale) + shift


def _mod_kernel(c_ref, w_ref, b_ref, o_ref):
    s = _silu(c_ref[...])
    o_ref[0] = jnp.dot(s.astype(BF16), w_ref[0].astype(BF16), preferred_element_type=F32) + b_ref[0]


def _modulation(cvec, w_ada, b_ada):
    R, D = cvec.shape
    depth, _, n6 = w_ada.shape
    tn = 1536
    return pl.pallas_call(
        _mod_kernel,
        grid=(depth, n6 // tn),
        in_specs=[pl.BlockSpec((R, D), lambda l, j: (0, 0)),
                  pl.BlockSpec((1, D, tn), lambda l, j: (l, 0, j)),
                  pl.BlockSpec((1, 1, tn), lambda l, j: (l, 0, j))],
        out_specs=pl.BlockSpec((1, R, tn), lambda l, j: (l, 0, j)),
        out_shape=jax.ShapeDtypeStruct((depth, R, n6), F32),
        compiler_params=_cparams(("parallel", "parallel")),
        name="adaln_mod",
    )(cvec, w_ada, b_ada.reshape(depth, 1, n6))


def _win_kernel(h_ref, m_ref, g_ref, w_ref, cq_ref, sq_ref, tk_ref, gq_ref, wq_ref, wqs_ref, gk_ref, wk_ref,
                wv_ref, e_ref, sc_ref, ss_ref, mq_out, mk_out, mv_out, o_hg, sq_out, sk_out, sv_out, o_ret,
                *, tm, n_ctx):
    row0 = pl.program_id(1) * tm
    a = _norm_mod(h_ref[0], g_ref[...], _row_mod(m_ref, 0, row0, tm, n_ctx),
                  _row_mod(m_ref, 1, row0, tm, n_ctx)).astype(BF16)
    group = lambda c0, w: jnp.dot(a, w_ref[:, c0:c0 + w], preferred_element_type=F32)
    mq_out[0], mk_out[0], mv_out[0] = _mla_prep(group(0, W_MLA), cq_ref, sq_ref, tk_ref, gq_ref, wq_ref, wqs_ref,
                                                gk_ref, wk_ref, wv_ref, e_ref)
    o_hg[0] = group(W_MLA, W_HG)
    sq_out[0], sk_out[0], sv_out[0] = _swa_prep(group(W_MLA + W_HG, W_SWA), sc_ref, ss_ref)
    o_ret[0] = group(W_MLA + W_HG + W_SWA, W_RET)


def _in_proj(h, mod, g, w_ext, mla_weights, mla_tables, swa_tables, n_ctx, tm=384):
    B, S, D = h.shape
    full = lambda a: pl.BlockSpec(a.shape, lambda b, i: (0,) * a.ndim)
    tab = pl.BlockSpec((tm, LANE), lambda b, i: (i, 0))
    out_w = ((512, BF16), (512, BF16), (512, BF16), (W_HG, F32), (256, BF16), (LANE, BF16), (LANE, BF16), (W_RET, F32))
    outs = pl.pallas_call(
        functools.partial(_win_kernel, tm=tm, n_ctx=n_ctx),
        grid=(B, S // tm),
        in_specs=[pl.BlockSpec((1, tm, D), lambda b, i: (b, i, 0)),
                  pl.BlockSpec((1, 2, 8, D), lambda b, i: (b, 0, 0, 0)),
                  pl.BlockSpec((1, D), lambda b, i: (0, 0)),
                  pl.BlockSpec((D, W_EXT), lambda b, i: (0, 0)),
                  tab, tab, tab] + [full(w) for w in mla_weights] + [tab, tab],
        out_specs=[pl.BlockSpec((1, tm, w), lambda b, i: (b, i, 0)) for w, _ in out_w],
        out_shape=[jax.ShapeDtypeStruct((B, S, w), dt) for w, dt in out_w],
        compiler_params=_cparams(("parallel", "parallel")),
        name="in_proj",
    )(h, mod, g.reshape(1, D), w_ext, *mla_tables, *mla_weights, *swa_tables)
    return outs[0:3], outs[3], outs[4:7], outs[7]


def _mla_weights(q_norm_g, w_uq, kv_norm_g, w_ukv):
    qcols, qsw, kcols = [], [], []
    for h in range(MLA_HEADS):
        b = 96 * h
        nope = np.arange(b, b + 64)
        pe = np.arange(b + 64, b + 96)
        pad = -np.ones((32,), np.int64)
        qcols.append(np.concatenate([nope, pe, pad]))
        qsw.append(np.concatenate([-np.ones((64,), np.int64), (b + 64) + _swap_halves(np.arange(32), 32), pad]))
        kcols.append(np.concatenate([np.arange(128 * h, 128 * h + 64), -np.ones((64,), np.int64)]))
    vcols = np.concatenate([np.concatenate([np.arange(128 * h + 64, 128 * h + 128), -np.ones((64,), np.int64)])
                            for h in range(MLA_HEADS)])
    pad_rows = ((0, 256 - MLA_Q_RANK), (0, 0))
    wq = jnp.pad(_take_cols(w_uq, np.concatenate(qcols)), pad_rows).astype(BF16)
    wq_sw = jnp.pad(_take_cols(w_uq, np.concatenate(qsw)), pad_rows).astype(BF16)
    wk = _take_cols(w_ukv, np.concatenate(kcols)).astype(BF16)
    wv = _take_cols(w_ukv, vcols).astype(BF16)
    e = np.zeros((128, 512), np.float32)
    for h in range(MLA_HEADS):
        for j in range(32):
            e[j, 128 * h + 64 + j] = 1.0
            e[32 + j, 128 * h + 64 + j] = 1.0
    gq = jnp.pad(q_norm_g, (0, 256 - MLA_Q_RANK)).reshape(1, 256)
    return gq, wq, wq_sw, kv_norm_g.reshape(1, 128), wk, wv, jnp.asarray(e, BF16)


def _mla_tables(n_lat, n_ctx):
    ang = _axial_angles(n_lat, MLA_ROPE)
    c32, s32 = _rope_tables(ang, n_ctx)
    S = n_lat + n_ctx
    one, zero = jnp.ones((S, 64), F32), jnp.zeros((S, 64), F32)
    cq = jnp.concatenate([one, c32, jnp.zeros((S, 32), F32)], axis=-1)
    sq = jnp.concatenate([zero, s32, jnp.zeros((S, 32), F32)], axis=-1)
    tk = jnp.concatenate([c32, s32, zero], axis=-1)
    return cq, sq, tk


def _mla_prep(p, cq_ref, sq_ref, tk_ref, gq_ref, wq_ref, wqs_ref, gk_ref, wk_ref, wv_ref, e_ref):
    scale = (MLA_NOPE + MLA_ROPE) ** -0.5 * float(np.log2(np.e))
    ql = p[:, 0:256]
    msq = jnp.sum(ql * ql, axis=-1, keepdims=True) * (1.0 / MLA_Q_RANK)
    qn = (ql * lax.rsqrt(msq + NORM_EPS) * gq_ref[...]).astype(BF16)
    kvl = p[:, 256:384]
    msk = jnp.mean(kvl * kvl, axis=-1, keepdims=True)
    kvn = (kvl * lax.rsqrt(msk + NORM_EPS) * gk_ref[...]).astype(BF16)
    q = jnp.dot(qn, wq_ref[...], preferred_element_type=F32)
    qs = jnp.dot(qn, wqs_ref[...], preferred_element_type=F32)
    cq = jnp.concatenate([cq_ref[...]] * MLA_HEADS, axis=-1)
    sq = jnp.concatenate([sq_ref[...]] * MLA_HEADS, axis=-1)
    q_out = ((q * cq + qs * sq) * scale).astype(BF16)
    pe = (p[:, 384:512] * tk_ref[...]).astype(BF16)
    k = (jnp.dot(kvn, wk_ref[...], preferred_element_type=F32)
         + jnp.dot(pe, e_ref[...], preferred_element_type=F32))
    lane = lax.broadcasted_iota(jnp.int32, (1, MLA_HEADS * LANE), 1) % LANE
    ones = jnp.where(lane >= MLA_V, 1.0, 0.0)
    v = jnp.dot(kvn, wv_ref[...], preferred_element_type=F32) + ones
    return q_out, k.astype(BF16), v.astype(BF16)


def _mla_attn_kernel(q_ref, k_ref, v_ref, o_ref, *, n_ctx, tq):
    S = k_ref.shape[1]

    lo = lax.broadcasted_iota(jnp.int32, (tq, LANE), 1) < MLA_V

    def attend(nk):
        pv = []
        for h in range(MLA_HEADS):
            q = q_ref[0, :, LANE * h:LANE * (h + 1)]
            k = k_ref[0, 0:nk, LANE * h:LANE * (h + 1)]
            s = lax.dot_general(q, k, (((1,), (1,)), ((), ())), preferred_element_type=F32)
            p = jnp.exp2(s - jnp.max(s, axis=-1, keepdims=True))
            pv.append(jnp.dot(p.astype(BF16), v_ref[0, 0:nk, LANE * h:LANE * (h + 1)], preferred_element_type=F32))
        for pair in range(MLA_HEADS // 2):
            a, b = pv[2 * pair], pv[2 * pair + 1]
            o_ref[0, :, LANE * pair:LANE * (pair + 1)] = jnp.where(
                lo, a / pltpu.roll(a, MLA_V, axis=1), pltpu.roll(b, MLA_V, axis=1) / b).astype(o_ref.dtype)

    is_ctx = (pl.program_id(1) + 1) * tq <= n_ctx
    pl.when(is_ctx)(lambda: attend(n_ctx))
    pl.when(jnp.logical_not(is_ctx))(lambda: attend(S))


def _mla_mixer(qkv, n_ctx, tq=256):
    q, k, v = qkv
    B, S, _ = q.shape
    assert n_ctx % tq == 0 and S % tq == 0
    return pl.pallas_call(
        functools.partial(_mla_attn_kernel, n_ctx=n_ctx, tq=tq),
        grid=(B, S // tq),
        in_specs=[pl.BlockSpec((1, tq, 512), lambda b, i: (b, i, 0)),
                  pl.BlockSpec((1, S, 512), lambda b, i: (b, 0, 0)),
                  pl.BlockSpec((1, S, 512), lambda b, i: (b, 0, 0))],
        out_specs=pl.BlockSpec((1, tq, GROUP_W), lambda b, i: (b, i, 0)),
        out_shape=jax.ShapeDtypeStruct((B, S, GROUP_W), BF16),
        compiler_params=_cparams(("parallel", "arbitrary")),
        name="mla_attn",
    )(q, k, v)


def _swa_tables(n_lat, n_ctx):
    ang = _axial_angles(n_lat, SWA_HD)
    c64, s64 = _rope_tables(ang, n_ctx)
    return jnp.concatenate([c64, c64], axis=-1), jnp.concatenate([s64, s64], axis=-1)


def _swa_prep(p, c_ref, s_ref):
    scale = SWA_HD ** -0.5
    c, s = c_ref[...], s_ref[...]
    c2 = jnp.concatenate([c, c], axis=-1)
    s2 = jnp.concatenate([s, s], axis=-1)
    q = ((p[:, 0:256] * c2 + p[:, 512:768] * s2) * scale).astype(BF16)
    k = (p[:, 256:384] * c + p[:, 768:896] * s).astype(BF16)
    return q, k, p[:, 384:512].astype(BF16)


def _swa_attn_kernel(sink_ref, q_ref, kc_ref, vc_ref, kp_ref, kn_ref, kx_ref, vp_ref, vn_ref, vx_ref, bias_ref,
                     o_ref, *, n_ctx, blk):
    j = pl.program_id(1)
    ncb = n_ctx // blk
    lane = lax.broadcasted_iota(jnp.int32, (blk, LANE), 1)
    lo = lane < SWA_HD
    qa, qb = q_ref[0, :, 0:LANE], q_ref[0, :, LANE:2 * LANE]
    zero = jnp.zeros((), BF16)
    qs = jnp.concatenate([jnp.where(lo, qa, zero), jnp.where(lo, qb, zero),
                          jnp.where(lo, zero, qa), jnp.where(lo, zero, qb)], axis=0)
    rb = lax.broadcasted_iota(jnp.int32, (4 * blk, 1), 0) // blk
    sink = jnp.where(rb == 0, sink_ref[0], jnp.where(rb == 1, sink_ref[1],
                                                     jnp.where(rb == 2, sink_ref[2], sink_ref[3])))

    def finish(s, vcat):
        m = jnp.maximum(jnp.max(s, axis=-1, keepdims=True), sink)
        p = jnp.exp(s - m)
        l = jnp.sum(p, axis=-1, keepdims=True) + jnp.exp(sink - m)
        r = jnp.dot(p.astype(BF16), vcat, preferred_element_type=F32) / l
        o_ref[0, :, 0:LANE] = jnp.where(lo, r[0:blk], r[2 * blk:3 * blk]).astype(o_ref.dtype)
        o_ref[0, :, LANE:2 * LANE] = jnp.where(lo, r[blk:2 * blk], r[3 * blk:4 * blk]).astype(o_ref.dtype)

    def ctx_block():
        s = lax.dot_general(qs, kc_ref[0], (((1,), (1,)), ((), ())), preferred_element_type=F32)
        finish(s, vc_ref[0])

    def lat_block():
        kcat = jnp.concatenate([kc_ref[0], kp_ref[0], kn_ref[0], kx_ref[0]], axis=0)
        vcat = jnp.concatenate([vc_ref[0], vp_ref[0], vn_ref[0], vx_ref[0]], axis=0)
        s = lax.dot_general(qs, kcat, (((1,), (1,)), ((), ())), preferred_element_type=F32)
        finish(s + jnp.concatenate([bias_ref[0]] * SWA_HEADS, axis=0), vcat)

    pl.when(j < ncb)(ctx_block)
    pl.when(j >= ncb)(lat_block)


def _swa_mixer(qkv, sink, n_ctx, blk=128):
    q, k, v = qkv
    B, S, _ = q.shape
    nb = S // blk
    ncb = n_ctx // blk
    cur = lambda b, j: (b, j, 0)
    prev = lambda b, j: (b, jnp.maximum(j - 1, ncb), 0)
    nxt = lambda b, j: (b, jnp.minimum(j + 1, nb - 1), 0)
    kv = lambda f: pl.BlockSpec((1, blk, LANE), f)
    ctx = pl.BlockSpec((1, n_ctx, LANE), lambda b, j: (b, 0, 0))
    t = np.arange(blk)[:, None]
    c = np.arange(n_ctx + 3 * blk)[None, :] - n_ctx
    bias = np.stack([np.where((c < 0) | ((c >= 0) & (c < blk) & (c >= t) & bool(hp)) | ((c >= blk) & (c < 2 * blk))
                              | ((c >= 2 * blk) & (c - 2 * blk <= t) & bool(hn)), 0.0, -1e30)
                     for hp in (0, 1) for hn in (0, 1)]).astype(np.float32)
    kind = lambda b, j: (2 * (j > ncb).astype(jnp.int32) + (j < nb - 1).astype(jnp.int32), 0, 0)
    return pl.pallas_call(
        functools.partial(_swa_attn_kernel, n_ctx=n_ctx, blk=blk),
        grid=(B, nb),
        in_specs=[pl.BlockSpec(memory_space=pltpu.SMEM),
                  pl.BlockSpec((1, blk, 256), cur), ctx, ctx,
                  kv(prev), kv(cur), kv(nxt), kv(prev), kv(cur), kv(nxt),
                  pl.BlockSpec((1, blk, n_ctx + 3 * blk), kind)],
        out_specs=pl.BlockSpec((1, blk, GROUP_W), cur),
        out_shape=jax.ShapeDtypeStruct((B, S, GROUP_W), BF16),
        compiler_params=_cparams(("parallel", "arbitrary")),
        name="swa_attn",
    )(sink.astype(F32), q, k, v, k, k, k, v, v, v, jnp.asarray(bias))


def _bwd_chunk(j, ncc, nc):
    return jnp.where(j < ncc, ncc - 1 - j, nc - 1 - (j - ncc))


def _head_stack(x, width, heads):
    lane_h = lax.broadcasted_iota(jnp.int32, x.shape, 1) // width
    zero = jnp.zeros((), x.dtype)
    return jnp.concatenate([jnp.where(lane_h == h, x, zero) for h in range(heads)], axis=0)


def _head_unstack(y, rows, width, heads):
    lane_h = lax.broadcasted_iota(jnp.int32, (rows, heads * width), 1) // width
    out = y[0:rows]
    for h in range(1, heads):
        out = jnp.where(lane_h == h, y[h * rows:(h + 1) * rows], out)
    return out


RET_CHUNK = 256


def _ret_tables(log_gamma, n_ctx, S):
    C = RET_CHUNK
    t = jnp.arange(C, dtype=F32)
    rel = t[:, None] - t[None, :]
    lg = log_gamma.astype(F32)
    lg_l = jnp.repeat(lg, RET_DK, axis=1)
    dm, xi, zeta, dcay = [], [], [], []
    for d in range(2):
        r = rel if d == 0 else -rel
        dh = jnp.where(r >= 0, jnp.exp(jnp.maximum(r, 0.0)[None] * lg[d][:, None, None]), 0.0)
        dm.append(dh.reshape(RET_HEADS * C, C))
        tq = (t + 1.0) if d == 0 else (C - t)
        tk = (C - 1.0 - t) if d == 0 else t
        xi.append(jnp.exp(tq[:, None] * lg_l[d][None, :]))
        zeta.append(jnp.exp(tk[:, None] * lg_l[d][None, :]))
        dcay.append(jnp.broadcast_to(jnp.exp(C * lg_l[d])[:, None], (RET_HEADS * RET_DK, GROUP_W)))
    mask = (np.arange(128)[:, None] // RET_DK == np.arange(256)[None, :] // RET_DV).astype(np.float32)
    pos = jnp.arange(S, dtype=F32)
    inv = ROPE_THETA ** (-jnp.arange(RET_DK // 2, dtype=F32) / (RET_DK // 2))
    ang = pos[:, None] * inv
    cos, sin = jnp.cos(ang), jnp.sin(ang)
    c = jnp.tile(jnp.concatenate([cos, cos], axis=-1), (1, RET_HEADS))
    s = jnp.tile(jnp.concatenate([-sin, sin], axis=-1), (1, RET_HEADS))
    return (jnp.stack(dm), jnp.stack(xi), jnp.stack(zeta), jnp.stack(dcay), jnp.asarray(mask), c, s)


def _ret_kernel(qkf_ref, vf_ref, cf_ref, sf_ref, qkb_ref, vb_ref, cb_ref, sb_ref,
                dm_ref, xi_ref, zeta_ref, dcay_ref, mask_ref, of_ref, ob_ref, state):
    C = RET_CHUNK

    @pl.when(pl.program_id(1) == 0)
    def _():
        state[...] = jnp.zeros_like(state)

    dirs = ((qkf_ref, vf_ref, cf_ref, sf_ref, of_ref), (qkb_ref, vb_ref, cb_ref, sb_ref, ob_ref))
    for d, (qk_ref, v_ref, c_ref, s_ref, o_ref) in enumerate(dirs):
        c, s = c_ref[...], s_ref[...]
        qr = (qk_ref[0, :, 0:128] * c + qk_ref[0, :, 256:384] * s) * (RET_DK ** -0.5)
        kr = qk_ref[0, :, 128:256] * c + qk_ref[0, :, 384:512] * s
        vb = v_ref[0].astype(BF16)
        qs = _head_stack(qr.astype(BF16), RET_DK, RET_HEADS)
        sc = lax.dot_general(qs, kr.astype(BF16), (((1,), (1,)), ((), ())), preferred_element_type=F32)
        a = (sc * dm_ref[d]).astype(BF16)
        o = _head_unstack(jnp.dot(a, vb, preferred_element_type=F32), C, RET_DV, RET_HEADS)
        st = state[d]
        o = o + jnp.dot((qr * xi_ref[d]).astype(BF16), st.astype(BF16), preferred_element_type=F32)
        o_ref[0] = o.astype(o_ref.dtype)
        kz = (kr * zeta_ref[d]).astype(BF16)
        u = lax.dot_general(kz, vb, (((0,), (0,)), ((), ())), preferred_element_type=F32)
        state[d] = st * dcay_ref[d] + u * mask_ref[...]


def _ret_mixer(p_ret, tables, n_ctx):
    B, S, _ = p_ret.shape
    C = RET_CHUNK
    dm, xi, zeta, dcay, mask, c, s = tables
    nc, ncc = S // C, n_ctx // C
    fwd = lambda b, j: (b, j, 0)
    bwd = lambda b, j: (b, _bwd_chunk(j, ncc, nc), 0)
    fwd_v = lambda b, j: (b, j, 2)
    bwd_v = lambda b, j: (b, _bwd_chunk(j, ncc, nc), 2)
    fwd_t = lambda b, j: (j, 0)
    bwd_t = lambda b, j: (_bwd_chunk(j, ncc, nc), 0)
    full = lambda a: pl.BlockSpec(a.shape, lambda b, j: (0,) * a.ndim)
    return pl.pallas_call(
        _ret_kernel,
        grid=(B, nc),
        in_specs=[pl.BlockSpec((1, C, 512), fwd), pl.BlockSpec((1, C, 256), fwd_v),
                  pl.BlockSpec((C, LANE), fwd_t), pl.BlockSpec((C, LANE), fwd_t),
                  pl.BlockSpec((1, C, 512), bwd), pl.BlockSpec((1, C, 256), bwd_v),
                  pl.BlockSpec((C, LANE), bwd_t), pl.BlockSpec((C, LANE), bwd_t),
                  full(dm), full(xi), full(zeta), full(dcay), full(mask)],
        out_specs=[pl.BlockSpec((1, C, GROUP_W), fwd), pl.BlockSpec((1, C, GROUP_W), bwd)],
        out_shape=[jax.ShapeDtypeStruct((B, S, GROUP_W), BF16)] * 2,
        scratch_shapes=[pltpu.VMEM((2, RET_HEADS * RET_DK, GROUP_W), F32)],
        compiler_params=_cparams(("parallel", "arbitrary")),
        name="retention_scan",
    )(p_ret, p_ret, c, s, p_ret, p_ret, c, s, dm, xi, zeta, dcay, mask)


HG_CHUNK = 128


def _hg_level_tables(C):
    levels = []
    n = 2
    while n <= C:
        levels.append(n)
        n *= 2
    t = np.arange(C)
    mats, qroles, kroles = [], [], []
    for d in range(2):
        rows = []
        qr_d, kr_d = [], []
        for n in levels:
            m = n // 2
            start = (t // n) * n
            mid = start + m
            u = t - start
            M = np.zeros((C, C), np.float32)
            for i in range(C):
                if d == 0:
                    if u[i] >= m:
                        M[i, mid[i]:i + 1] = 1.0
                    else:
                        M[i, i + 1:mid[i]] = 1.0
                else:
                    if u[i] < m:
                        M[i, i:mid[i]] = 1.0
                    else:
                        M[i, mid[i]:i] = 1.0
            rows.append(M)
            qr_d.append((u >= m) if d == 0 else (u < m))
            kr_d.append((u < m) if d == 0 else (u >= m))
        Mq = np.zeros((C, C), np.float32)
        Mk = np.zeros((C, C), np.float32)
        for i in range(C):
            if d == 0:
                Mq[i, 0:i + 1] = 1.0
                Mk[i, i + 1:C] = 1.0
            else:
                Mq[i, i:C] = 1.0
                Mk[i, 0:i] = 1.0
        rows += [Mq, Mk, np.ones((8, C), np.float32)]
        mats.append(np.concatenate(rows, axis=0))
        qroles.append(np.stack(qr_d))
        kroles.append(np.stack(kr_d))
    same_block = np.stack([(t[:, None] // n == t[None, :] // n) for n in levels]).astype(np.float32)
    return levels, np.stack(mats), np.stack(qroles), np.stack(kroles), same_block


def _hg_kernel(qf_ref, zf_ref, if_ref, qb_ref, zb_ref, ib_ref, lb_ref, m_ref, role_ref, blk_ref,
               ones_ref, mask_ref, of_ref, ob_ref, state, *, n_levels):
    C = HG_CHUNK

    @pl.when(pl.program_id(1) == 0)
    def _():
        state[...] = jnp.zeros_like(state)

    dirs = ((qf_ref, zf_ref, if_ref), (qb_ref, zb_ref, ib_ref))
    st = [state[0], state[1]]
    q, v, vb, kk, r = [], [], [], [], []
    for d, (q_ref, z_ref, i_ref) in enumerate(dirs):
        z = z_ref[0]
        lb = lb_ref[d]
        lbf = jnp.maximum(lb, LB_FLOOR)
        t = jnp.exp(-jnp.abs(z))
        rcp = 1.0 / (1.0 + t)
        pos = z >= 0.0
        sig = jnp.where(pos, rcp, t * rcp)
        nsig = jnp.where(pos, t * rcp, rcp)
        logf = jnp.log(lbf + (1.0 - lb) * sig)
        kk.append((1.0 - lb) * nsig - (lbf - lb))
        hi, lo = _split2(logf)
        r.append(jnp.dot(m_ref[d], hi, preferred_element_type=F32)
                 + jnp.dot(m_ref[d], lo, preferred_element_type=F32))
        q.append(q_ref[0] * (HG_DK ** -0.5))
        v.append(i_ref[0])
        vb.append(v[d].astype(BF16))
    acc = [jnp.zeros((HG_HEADS * C, C), F32), jnp.zeros((HG_HEADS * C, C), F32)]
    for lv in range(n_levels):
        for d in range(2):
            e = jnp.exp(r[d][lv * C:(lv + 1) * C])
            roles = role_ref[d, lv]
            qe = jnp.where(roles > 0.5, q[d] * e, 0.0).astype(BF16)
            ke = jnp.where(roles > 0.5, 0.0, kk[d] * e).astype(BF16)
            a = lax.dot_general(_head_stack(qe, HG_DK, HG_HEADS), ke, (((1,), (1,)), ((), ())),
                                preferred_element_type=F32)
            acc[d] = acc[d] + a * blk_ref[lv]
    outs = []
    for d in range(2):
        o = _head_unstack(jnp.dot(acc[d].astype(BF16), vb[d], preferred_element_type=F32), C, HG_DV, HG_HEADS)
        o = o + jnp.dot((q[d] * kk[d]).astype(BF16), ones_ref[...], preferred_element_type=F32) * v[d]
        e_q = jnp.exp(r[d][n_levels * C:(n_levels + 1) * C])
        o = o + lax.dot_general((q[d] * e_q).astype(BF16), st[d].astype(BF16), (((1,), (1,)), ((), ())),
                                preferred_element_type=F32)
        outs.append(o)
    new_state = []
    for d in range(2):
        e_k = jnp.exp(r[d][(n_levels + 1) * C:(n_levels + 2) * C])
        ku = (kk[d] * e_k).astype(BF16)
        u = lax.dot_general(vb[d], ku, (((0,), (0,)), ((), ())), preferred_element_type=F32)
        dtot = jnp.exp(r[d][(n_levels + 2) * C:(n_levels + 2) * C + 1])
        new_state.append(st[d] * dtot + u * mask_ref[...])
    of_ref[0] = outs[0].astype(of_ref.dtype)
    ob_ref[0] = outs[1].astype(ob_ref.dtype)
    state[0] = new_state[0]
    state[1] = new_state[1]


def _hg_mixer(p_hg, lb, n_ctx):
    B, S, _ = p_hg.shape
    C = HG_CHUNK
    levels, mats, qroles, kroles, same_block = _hg_level_tables(C)
    nl = len(levels)
    m_all = jnp.asarray(mats, BF16)
    roles = jnp.asarray(np.broadcast_to(qroles[..., None], (2, nl, C, GROUP_W)).astype(np.float32))
    blk = jnp.asarray(np.tile(same_block, (1, HG_HEADS, 1)))
    hd = np.arange(GROUP_W) // HG_DK
    bd = (hd[:, None] == hd[None, :]).astype(np.float32)
    ones_bd = jnp.asarray(bd, BF16)
    mask = jnp.asarray(bd)
    nc, ncc = S // C, n_ctx // C
    col = lambda k, rev: (lambda b, j: (b, _bwd_chunk(j, ncc, nc) if rev else j, k))
    blkspec = lambda k, rev: pl.BlockSpec((1, C, GROUP_W), col(k, rev))
    full = lambda a: pl.BlockSpec(a.shape, lambda b, j: (0,) * a.ndim)
    lb3 = lb.reshape(2, 1, GROUP_W).astype(F32)
    return pl.pallas_call(
        functools.partial(_hg_kernel, n_levels=nl),
        grid=(B, nc),
        in_specs=[blkspec(0, False), blkspec(1, False), blkspec(3, False),
                  blkspec(0, True), blkspec(2, True), blkspec(3, True),
                  full(lb3), full(m_all), full(roles), full(blk), full(ones_bd), full(mask)],
        out_specs=[blkspec(0, False), blkspec(0, True)],
        out_shape=[jax.ShapeDtypeStruct((B, S, GROUP_W), BF16)] * 2,
        scratch_shapes=[pltpu.VMEM((2, GROUP_W, GROUP_W), F32)],
        compiler_params=_cparams(("parallel", "arbitrary")),
        name="hgrn2_scan",
    )(p_hg, p_hg, p_hg, p_hg, p_hg, p_hg, lb3, m_all, roles, blk, ones_bd, mask)


_SWA_OUT_PERM = np.concatenate([np.arange(64 * h, 64 * h + 64) for h in (0, 2, 1, 3)])


def _wout_kernel(h_ref, m_ref, a_ref, hf_ref, hb_ref, hgate_ref, c_ref, rf_ref, rb_ref, rgate_ref,
                 hgn_ref, rgg_ref, rgb_ref, ones_ref, w_ref, o_ref, *, tm, n_ctx, row_off):
    row0 = row_off + pl.program_id(1) * tm
    ones = ones_ref[...]
    o = hf_ref[0].astype(F32) + hb_ref[0].astype(F32)
    yb = o * lax.rsqrt(_seg_mean(o * o, ones) + NORM_EPS) * hgn_ref[...] * _silu(hgate_ref[0])
    o = rf_ref[0].astype(F32) + rb_ref[0].astype(F32)
    xc = o - _seg_mean(o, ones)
    yd = (xc * lax.rsqrt(_seg_mean(xc * xc, ones) + NORM_EPS) * rgg_ref[...] + rgb_ref[...]) * _silu(rgate_ref[0])
    acc = jnp.dot(a_ref[0].astype(BF16), w_ref[0:256, :], preferred_element_type=F32)
    acc += jnp.dot(yb.astype(BF16), w_ref[256:512, :], preferred_element_type=F32)
    acc += jnp.dot(c_ref[0].astype(BF16), w_ref[512:768, :], preferred_element_type=F32)
    acc += jnp.dot(yd.astype(BF16), w_ref[768:1024, :], preferred_element_type=F32)
    o_ref[0] = h_ref[0] + _row_mod(m_ref, 2, row0, tm, n_ctx) * acc


def _out_proj(h, mod, o_mla, hg_f, hg_b, p_hg, o_swa, ret_f, ret_b, p_ret, hg_norm_g, ret_gn_g, ret_gn_b,
              w_out_p, n_ctx, tm=384, row_off=0):
    B, S, D = h.shape
    assert row_off % tm == 0 and (S - row_off) % tm == 0
    off = row_off // tm
    hd = np.arange(GROUP_W) // HG_DV
    ones_bd = jnp.asarray((hd[:, None] == hd[None, :]).astype(np.float32) / HG_DV, BF16)
    row = lambda w, k=0: pl.BlockSpec((1, tm, w), lambda b, i: (b, i + off, k))
    vec = pl.BlockSpec((1, GROUP_W), lambda b, i: (0, 0))
    return pl.pallas_call(
        functools.partial(_wout_kernel, tm=tm, n_ctx=n_ctx, row_off=row_off),
        grid=(B, (S - row_off) // tm),
        in_specs=[row(D), pl.BlockSpec((1, 2, 8, D), lambda b, i: (b, 0, 0, 0)),
                  row(GROUP_W), row(GROUP_W), row(GROUP_W), row(GROUP_W, 4),
                  row(GROUP_W), row(GROUP_W), row(GROUP_W), row(GROUP_W, 3),
                  vec, vec, vec, pl.BlockSpec((GROUP_W, GROUP_W), lambda b, i: (0, 0)),
                  pl.BlockSpec((D, D), lambda b, i: (0, 0))],
        out_specs=pl.BlockSpec((1, tm, D), lambda b, i: (b, i, 0)),
        out_shape=jax.ShapeDtypeStruct((B, S - row_off, D), F32),
        compiler_params=_cparams(("parallel", "parallel")),
        name="out_proj",
    )(h, mod, o_mla, hg_f, hg_b, p_hg, o_swa, ret_f, ret_b, p_ret,
      hg_norm_g.reshape(1, GROUP_W), ret_gn_g.reshape(1, GROUP_W), ret_gn_b.reshape(1, GROUP_W), ones_bd, w_out_p)


def _ffn_kernel(h_ref, m_ref, g_ref, wg_ref, wu_ref, wd_ref, o_ref, xn, *, tm, n_ctx):
    j = pl.program_id(2)
    row0 = pl.program_id(1) * tm

    @pl.when(j == 0)
    def _():
        xn[...] = _norm_mod(h_ref[0], g_ref[...], _row_mod(m_ref, 3, row0, tm, n_ctx),
                            _row_mod(m_ref, 4, row0, tm, n_ctx)).astype(BF16)
        o_ref[...] = jnp.zeros_like(o_ref)

    wg, wu, wd = wg_ref[0], wu_ref[0], wd_ref[0]
    half = tm // 2
    for r0 in (0, half):
        x = xn[r0:r0 + half]
        hg = jnp.dot(x, wg, preferred_element_type=F32)
        hu = jnp.dot(x, wu, preferred_element_type=F32)
        o_ref[0, r0:r0 + half] += jnp.dot((_silu(hg) * hu).astype(BF16), wd, preferred_element_type=F32)

    @pl.when(j == pl.num_programs(2) - 1)
    def _():
        o_ref[0] = h_ref[0] + _row_mod(m_ref, 5, row0, tm, n_ctx) * o_ref[0]


def _dense_ffn(h, mod, g, w_gate, w_up, w_down, li, n_ctx, tm=1152, tf=1408):
    B, S, D = h.shape
    dff = w_gate.shape[-1]
    return pl.pallas_call(
        functools.partial(_ffn_kernel, tm=tm, n_ctx=n_ctx),
        grid=(B, S // tm, dff // tf),
        in_specs=[pl.BlockSpec((1, tm, D), lambda b, i, j: (b, i, 0)),
                  pl.BlockSpec((1, 2, 8, D), lambda b, i, j: (b, 0, 0, 0)),
                  pl.BlockSpec((1, D), lambda b, i, j: (0, 0)),
                  pl.BlockSpec((1, D, tf), lambda b, i, j: (li, 0, j)),
                  pl.BlockSpec((1, D, tf), lambda b, i, j: (li, 0, j)),
                  pl.BlockSpec((1, tf, D), lambda b, i, j: (li, j, 0))],
        out_specs=pl.BlockSpec((1, tm, D), lambda b, i, j: (b, i, 0)),
        out_shape=jax.ShapeDtypeStruct((B, S, D), F32),
        scratch_shapes=[pltpu.VMEM((tm, D), BF16)],
        compiler_params=_cparams(("parallel", "parallel", "arbitrary")),
        name="dense_ffn",
    )(h, mod, g.reshape(1, D), w_gate, w_up, w_down)


MOE_TM = 1024
MOE_TF = 512
MOE_ROWS = (384, 256)
DMA_UNROLL = 8
SUBLANES = 8
ZERO_ROWS = 256


def _router_kernel(h_ref, m_ref, g_ref, wr_ref, tri_ref, xn_ref, route_ref, cnt_ref, counts, *, tm, n_ctx):
    b, i = pl.program_id(0), pl.program_id(1)
    row0 = i * tm

    @pl.when((b == 0) & (i == 0))
    def _():
        counts[...] = jnp.zeros_like(counts)

    a = _norm_mod(h_ref[0], g_ref[...], _row_mod(m_ref, 3, row0, tm, n_ctx), _row_mod(m_ref, 4, row0, tm, n_ctx))
    xn_ref[0] = a
    logits = jnp.dot(a, wr_ref[...], preferred_element_type=F32, precision=lax.Precision.HIGHEST)
    lane = lax.broadcasted_iota(jnp.int32, logits.shape, 1)
    neg = jnp.float32(-jnp.inf)
    lg = jnp.where(lane < N_EXPERTS, logits, neg)
    v1 = jnp.max(lg, axis=-1, keepdims=True)
    i1 = jnp.min(jnp.where(lg == v1, lane, LANE), axis=-1, keepdims=True)
    lg2 = jnp.where(lane == i1, neg, lg)
    v2 = jnp.max(lg2, axis=-1, keepdims=True)
    i2 = jnp.min(jnp.where(lg2 == v2, lane, LANE), axis=-1, keepdims=True)
    e = jnp.exp(v2 - v1)
    w1 = 1.0 / (1.0 + e)
    w2 = e / (1.0 + e)
    oh1, oh2 = lane == i1, lane == i2
    onehot = jnp.where(oh1 | oh2, 1.0, 0.0)
    earlier = jnp.dot(tri_ref[...], onehot.astype(BF16), preferred_element_type=F32)
    pos = earlier + counts[...]
    r1 = jnp.sum(jnp.where(oh1, pos, 0.0), axis=-1, keepdims=True)
    r2 = jnp.sum(jnp.where(oh2, pos, 0.0), axis=-1, keepdims=True)
    counts[...] = counts[...] + jnp.sum(onehot, axis=0, keepdims=True)
    cnt_ref[...] = jnp.broadcast_to(counts[...], cnt_ref.shape)
    route_ref[0] = jnp.where(lane == 0, i1.astype(F32),
                             jnp.where(lane == 1, i2.astype(F32),
                                       jnp.where(lane == 2, w1,
                                                 jnp.where(lane == 3, w2,
                                                           jnp.where(lane == 4, r1, jnp.where(lane == 5, r2, 0.0))))))


def _scatter_kernel(meta_ref, slot_ref, x_ref, xs_hbm, zeros_v, sem, *, tm, tile_rows, total_rows):
    b, i = pl.program_id(0), pl.program_id(1)

    def start(r, carry):
        for k in range(TOP_K):
            pltpu.make_async_copy(x_ref.at[0, pl.ds(r, 1)], xs_hbm.at[pl.ds(slot_ref[0, 0, 0, k * tm + r], 1)],
                                  sem).start()
        return carry

    lax.fori_loop(0, tm, start, 0, unroll=DMA_UNROLL)
    for k in range(TOP_K):
        pltpu.make_async_copy(x_ref.at[0], xs_hbm.at[pl.ds(0, tm)], sem).wait()

    @pl.when((b == pl.num_programs(0) - 1) & (i == pl.num_programs(1) - 1))
    def _():
        zeros_v[...] = jnp.zeros_like(zeros_v)
        z1 = zeros_v.at[pl.ds(0, 1)]
        z8 = zeros_v.at[pl.ds(0, SUBLANES)]

        def fill(lo, end):
            mid = jnp.minimum(((lo + SUBLANES - 1) // SUBLANES) * SUBLANES, end)
            big = mid + ((end - mid) // ZERO_ROWS) * ZERO_ROWS
            n_row, n_big, n_small = mid - lo, (big - mid) // ZERO_ROWS, (end - big) // SUBLANES

            def start_row(r, c):
                pltpu.make_async_copy(z1, xs_hbm.at[pl.ds(lo + r, 1)], sem).start()
                return c

            def wait_row(r, c):
                pltpu.make_async_copy(z1, xs_hbm.at[pl.ds(0, 1)], sem).wait()
                return c

            def start_big(q, c):
                dst = xs_hbm.at[pl.ds(pl.multiple_of(mid + q * ZERO_ROWS, SUBLANES), ZERO_ROWS)]
                pltpu.make_async_copy(zeros_v, dst, sem).start()
                return c

            def wait_big(q, c):
                pltpu.make_async_copy(zeros_v, xs_hbm.at[pl.ds(0, ZERO_ROWS)], sem).wait()
                return c

            def start_small(q, c):
                dst = xs_hbm.at[pl.ds(pl.multiple_of(big + q * SUBLANES, SUBLANES), SUBLANES)]
                pltpu.make_async_copy(z8, dst, sem).start()
                return c

            def wait_small(q, c):
                pltpu.make_async_copy(z8, xs_hbm.at[pl.ds(0, SUBLANES)], sem).wait()
                return c

            lax.fori_loop(0, n_row, start_row, 0)
            lax.fori_loop(0, n_big, start_big, 0)
            lax.fori_loop(0, n_small, start_small, 0)
            lax.fori_loop(0, n_row, wait_row, 0)
            lax.fori_loop(0, n_big, wait_big, 0)
            lax.fori_loop(0, n_small, wait_small, 0)

        for e in range(N_EXPERTS):
            c = meta_ref[e]
            g0 = meta_ref[N_EXPERTS + e]
            fill(g0 + c, g0 + ((c + tile_rows - 1) // tile_rows) * tile_rows)
        fill(meta_ref[2 * N_EXPERTS] * tile_rows, jnp.int32(total_rows))


def _expert_kernel(te_ref, nu_ref, x_ref, wg_ref, wu_ref, wd_ref, o_ref, xb):
    i, j = pl.program_id(0), pl.program_id(1)
    half = x_ref.shape[0] // 2

    @pl.when(j == 0)
    def _():
        xb[...] = x_ref[...].astype(BF16)
        o_ref[...] = jnp.zeros_like(o_ref)

    @pl.when(i < nu_ref[0])
    def _():
        wg, wu, wd = wg_ref[0, 0].astype(BF16), wu_ref[0, 0].astype(BF16), wd_ref[0, 0].astype(BF16)
        for r0 in (0, half):
            x = xb[r0:r0 + half]
            hg = jnp.dot(x, wg, preferred_element_type=F32)
            hu = jnp.dot(x, wu, preferred_element_type=F32)
            o_ref[r0:r0 + half] += jnp.dot((_silu(hg) * hu).astype(BF16), wd, preferred_element_type=F32)


def _combine_kernel(slot_ref, h_ref, m_ref, route_ref, fg_ref, y_hbm, o_ref, buf, sem, *, tm, n_ctx, final_norm):
    row0 = pl.program_id(1) * tm

    def start(r, carry):
        for k in range(TOP_K):
            pltpu.make_async_copy(y_hbm.at[pl.ds(slot_ref[0, 0, 0, k * tm + r], 1)], buf.at[k, pl.ds(r, 1)], sem).start()
        return carry

    lax.fori_loop(0, tm, start, 0, unroll=DMA_UNROLL)
    for k in range(TOP_K):
        pltpu.make_async_copy(y_hbm.at[pl.ds(0, tm)], buf.at[k], sem).wait()
    w1 = route_ref[0, :, 2:3]
    w2 = route_ref[0, :, 3:4]
    y = h_ref[0] + _row_mod(m_ref, 5, row0, tm, n_ctx) * (w1 * buf[0] + w2 * buf[1])
    if final_norm:
        y = y * lax.rsqrt(jnp.mean(y * y, axis=-1, keepdims=True) + NORM_EPS) * fg_ref[...]
    o_ref[0] = y


def _moe_ffn(h, mod, g, w_router, w_gate, w_up, w_down, li, n_ctx, final_g=None):
    B, S, D = h.shape
    T = B * S
    cr = next(r for r in MOE_ROWS if S % r == 0)
    tm = cr
    dffe = w_gate.shape[-1]
    te_rows = MOE_TM if T % MOE_TM == 0 else 512
    n_tiles = TOP_K * T // te_rows + N_EXPERTS
    P = n_tiles * te_rows
    wr = jnp.pad(w_router, ((0, 0), (0, LANE - N_EXPERTS)))
    tri = jnp.asarray(np.tril(np.ones((tm, tm), np.float32), -1), BF16)
    xn, route, cnt = pl.pallas_call(
        functools.partial(_router_kernel, tm=tm, n_ctx=n_ctx),
        grid=(B, S // tm),
        in_specs=[pl.BlockSpec((1, tm, D), lambda b, i: (b, i, 0)),
                  pl.BlockSpec((1, 2, 8, D), lambda b, i: (b, 0, 0, 0)),
                  pl.BlockSpec((1, D), lambda b, i: (0, 0)),
                  pl.BlockSpec((D, LANE), lambda b, i: (0, 0)),
                  pl.BlockSpec((tm, tm), lambda b, i: (0, 0))],
        out_specs=[pl.BlockSpec((1, tm, D), lambda b, i: (b, i, 0)),
                   pl.BlockSpec((1, tm, LANE), lambda b, i: (b, i, 0)),
                   pl.BlockSpec((SUBLANES, LANE), lambda b, i: (0, 0))],
        out_shape=[jax.ShapeDtypeStruct((B, S, D), F32), jax.ShapeDtypeStruct((B, S, LANE), F32),
                   jax.ShapeDtypeStruct((SUBLANES, LANE), F32)],
        scratch_shapes=[pltpu.VMEM((1, LANE), F32)],
        compiler_params=_cparams(("arbitrary", "arbitrary")),
        name="moe_router",
    )(h, mod, g.reshape(1, D), wr, tri)

    counts = cnt[0, :N_EXPERTS].astype(jnp.int32)
    ptiles = (counts + te_rows - 1) // te_rows
    tile_end = jnp.cumsum(ptiles)
    gstart = (tile_end - ptiles) * te_rows
    n_used = tile_end[-1:].astype(jnp.int32)
    tile_ids = jnp.arange(n_tiles, dtype=jnp.int32)
    tile_e = jnp.minimum(jnp.sum((tile_ids[:, None] >= tile_end[None, :]).astype(jnp.int32), axis=1),
                         N_EXPERTS - 1)
    meta = jnp.concatenate([counts, gstart, n_used])
    choice = route[..., 0:TOP_K].astype(jnp.int32)
    is_e = choice[..., None] == jnp.arange(N_EXPERTS, dtype=jnp.int32)
    slots = jnp.sum(jnp.where(is_e, gstart, 0), axis=-1) + route[..., 4:4 + TOP_K].astype(jnp.int32)
    slots = slots.reshape(B, S // cr, cr, TOP_K).transpose(0, 1, 3, 2)
    slots = slots.reshape(B, S // cr, 1, TOP_K * cr)

    xs = pl.pallas_call(
        functools.partial(_scatter_kernel, tm=cr, tile_rows=te_rows, total_rows=P),
        grid_spec=pltpu.PrefetchScalarGridSpec(
            num_scalar_prefetch=1,
            grid=(B, S // cr),
            in_specs=[pl.BlockSpec((1, 1, 1, TOP_K * cr), lambda b, i, mt: (b, i, 0, 0),
                                   memory_space=pltpu.SMEM),
                      pl.BlockSpec((1, cr, D), lambda b, i, mt: (b, i, 0))],
            out_specs=pl.BlockSpec(memory_space=pl.ANY),
            scratch_shapes=[pltpu.VMEM((ZERO_ROWS, D), F32), pltpu.SemaphoreType.DMA(())]),
        out_shape=jax.ShapeDtypeStruct((P, D), F32),
        compiler_params=_cparams(("arbitrary", "arbitrary")),
        name="moe_scatter",
    )(meta, slots, xn)

    nf = dffe // MOE_TF
    last = lambda i, nu: jnp.minimum(i, nu[0] - 1)
    jj = lambda i, j, nu: jnp.where(i < nu[0], j, nf - 1)
    ys = pl.pallas_call(
        _expert_kernel,
        grid_spec=pltpu.PrefetchScalarGridSpec(
            num_scalar_prefetch=2,
            grid=(n_tiles, nf),
            in_specs=[pl.BlockSpec((te_rows, D), lambda i, j, te, nu: (last(i, nu), 0)),
                      pl.BlockSpec((1, 1, D, MOE_TF), lambda i, j, te, nu: (li, te[last(i, nu)], 0, jj(i, j, nu))),
                      pl.BlockSpec((1, 1, D, MOE_TF), lambda i, j, te, nu: (li, te[last(i, nu)], 0, jj(i, j, nu))),
                      pl.BlockSpec((1, 1, MOE_TF, D), lambda i, j, te, nu: (li, te[last(i, nu)], jj(i, j, nu), 0))],
            out_specs=pl.BlockSpec((te_rows, D), lambda i, j, te, nu: (i, 0)),
            scratch_shapes=[pltpu.VMEM((te_rows, D), BF16)]),
        out_shape=jax.ShapeDtypeStruct((P, D), F32),
        compiler_params=_cparams(("arbitrary", "arbitrary")),
        name="moe_experts",
    )(tile_e, n_used, xs, w_gate, w_up, w_down)
    fg = jnp.ones((1, D), F32) if final_g is None else final_g.reshape(1, D).astype(F32)
    return pl.pallas_call(
        functools.partial(_combine_kernel, tm=cr, n_ctx=n_ctx, final_norm=final_g is not None),
        grid=(B, S // cr),
        in_specs=[pl.BlockSpec((1, 1, 1, TOP_K * cr), lambda b, i: (b, i, 0, 0), memory_space=pltpu.SMEM),
                  pl.BlockSpec((1, cr, D), lambda b, i: (b, i, 0)),
                  pl.BlockSpec((1, 2, 8, D), lambda b, i: (b, 0, 0, 0)),
                  pl.BlockSpec((1, cr, LANE), lambda b, i: (b, i, 0)),
                  pl.BlockSpec((1, D), lambda b, i: (0, 0)),
                  pl.BlockSpec(memory_space=pl.ANY)],
        out_specs=pl.BlockSpec((1, cr, D), lambda b, i: (b, i, 0)),
        out_shape=jax.ShapeDtypeStruct((B, S, D), F32),
        scratch_shapes=[pltpu.VMEM((TOP_K, cr, D), F32), pltpu.SemaphoreType.DMA(())],
        compiler_params=_cparams(("arbitrary", "arbitrary")),
        name="moe_combine",
    )(slots, h, mod, route, fg, ys)


def _final_norm_kernel(h_ref, g_ref, o_ref):
    x = h_ref[0]
    o_ref[0] = x * lax.rsqrt(jnp.mean(x * x, axis=-1, keepdims=True) + NORM_EPS) * g_ref[...]


def _final_norm(h, g, n_ctx, tm=256):
    B, S, D = h.shape
    N = S - n_ctx
    return pl.pallas_call(
        _final_norm_kernel,
        grid=(B, N // tm),
        in_specs=[pl.BlockSpec((1, tm, D), lambda b, i: (b, n_ctx // tm + i, 0)),
                  pl.BlockSpec((1, D), lambda b, i: (0, 0))],
        out_specs=pl.BlockSpec((1, tm, D), lambda b, i: (b, i, 0)),
        out_shape=jax.ShapeDtypeStruct((B, N, D), F32),
        compiler_params=_cparams(("parallel", "parallel")),
        name="final_norm",
    )(h, g.reshape(1, D))


MOD_ROWS = 16


def kernel(x, c, ctx, c_ctx, w_ada, b_ada, norm_mix_g, norm_ffn_g, w_in, w_out, mla_q_norm_g, mla_w_uq, mla_kv_norm_g, mla_w_ukv, hg_lb_logits, hg_norm_g, swa_sink, ret_decay_logit, ret_gn_g, ret_gn_b, ffn_w_gate, ffn_w_up, ffn_w_down, moe_w_router, moe_w_gate, moe_w_up, moe_w_down, final_norm_g):
    B, N, D = x.shape
    L = ctx.shape[1]
    S = L + N
    assert B + 1 <= MOD_ROWS
    h = jnp.concatenate([ctx, x], axis=1)
    cvec = jnp.zeros((MOD_ROWS, D), F32).at[0].set(c_ctx).at[1:1 + B].set(c)
    mod_all = _modulation(cvec, w_ada, b_ada).reshape(DEPTH, MOD_ROWS, 6, D)

    lb_p = jax.nn.softmax(hg_lb_logits.astype(F32), axis=0)
    hg_lb = jnp.cumsum(lb_p, axis=0) - lb_p[0:1]
    ret_log_gamma = jax.nn.log_sigmoid(ret_decay_logit.astype(F32))

    mla_tab = _mla_tables(N, L)
    swa_tab = _swa_tables(N, L)
    w_out_perm = np.concatenate([np.arange(512), 512 + _SWA_OUT_PERM, np.arange(768, 1024)])
    ffn_bf16 = tuple(w.astype(BF16) for w in (ffn_w_gate, ffn_w_up, ffn_w_down))

    for l in range(DEPTH):
        m = mod_all[l]
        mod = jnp.stack([jnp.broadcast_to(m[0], (B, 6, D)), m[1:1 + B]], axis=1)
        mod = jnp.pad(mod, ((0, 0), (0, 0), (0, 2), (0, 0)))
        w_ext = _take_cols(w_in[l], _EXT_COLS).astype(BF16)
        mla_w = _mla_weights(mla_q_norm_g[l], mla_w_uq[l], mla_kv_norm_g[l], mla_w_ukv[l])
        mla_qkv, p_hg, swa_qkv, p_ret = _in_proj(h, mod, norm_mix_g[l], w_ext, mla_w, mla_tab, swa_tab, L)
        o_mla = _mla_mixer(mla_qkv, L)
        hg_f, hg_b = _hg_mixer(p_hg, hg_lb[l], L)
        o_swa = _swa_mixer(swa_qkv, swa_sink[l], L)
        ret_f, ret_b = _ret_mixer(p_ret, _ret_tables(ret_log_gamma[l], L, S), L)
        w_out_p = jnp.take(w_out[l], jnp.asarray(w_out_perm), axis=0).astype(BF16)
        last = l == DEPTH - 1
        if last and l % 2 == 1:
            h = _out_proj(h, mod, o_mla, hg_f, hg_b, p_hg, o_swa, ret_f, ret_b, p_ret,
                          hg_norm_g[l], ret_gn_g[l], ret_gn_b[l], w_out_p, L, tm=math.gcd(L, N, 256), row_off=L)
            return _moe_ffn(h, mod, norm_ffn_g[l], moe_w_router[l // 2], moe_w_gate, moe_w_up, moe_w_down, l // 2,
                            0, final_g=final_norm_g)
        h = _out_proj(h, mod, o_mla, hg_f, hg_b, p_hg, o_swa, ret_f, ret_b, p_ret,
                      hg_norm_g[l], ret_gn_g[l], ret_gn_b[l], w_out_p, L)
        if l % 2 == 0:
            h = _dense_ffn(h, mod, norm_ffn_g[l], ffn_bf16[0], ffn_bf16[1], ffn_bf16[2], l // 2, L,
                           tm=1152 if S % 1152 == 0 else 384)
        else:
            h = _moe_ffn(h, mod, norm_ffn_g[l], moe_w_router[l // 2], moe_w_gate, moe_w_up, moe_w_down, l // 2, L)
    return _final_norm(h, final_norm_g, L)
```

```python
import functools
import math

import numpy as np
import jax
import jax.numpy as jnp
from jax import lax
from jax.experimental import pallas as pl
from jax.experimental.pallas import tpu as pltpu

F32 = jnp.float32
BF16 = jnp.bfloat16

D_MODEL = 1024
DEPTH = 4
GRID_W = 64
ROPE_THETA = 10000.0
NORM_EPS = 1e-6
LB_FLOOR = 1e-30
GROUP_W = 256

MLA_HEADS, MLA_NOPE, MLA_ROPE, MLA_V = 4, 64, 32, 64
MLA_Q_RANK, MLA_KV_RANK = 192, 128
HG_HEADS, HG_DK, HG_DV = 4, 64, 64
SWA_HEADS, SWA_KV_HEADS, SWA_HD, SWA_WINDOW = 4, 2, 64, 128
RET_HEADS, RET_DK, RET_DV = 4, 32, 64
N_EXPERTS, TOP_K = 8, 2

_IN_SIZES = (192, 128, 32, 256, 256, 256, 256, 256, 256, 128, 128, 128, 128, 256, 256)
_OFF = np.concatenate([[0], np.cumsum(_IN_SIZES)]).astype(np.int64)
D_IN = int(_OFF[-1])

W_MLA, W_HG, W_SWA, W_RET = 512, 1280, 896, 1024
W_EXT = W_MLA + W_HG + W_SWA + W_RET

LANE = 128
VMEM_LIMIT = 56 * 1024 * 1024


def _cparams(sem):
    return pltpu.CompilerParams(dimension_semantics=sem, vmem_limit_bytes=VMEM_LIMIT)


def _swap_halves(idx, width):
    idx = np.asarray(idx)
    base = (idx // width) * width
    j = idx % width
    return base + (j + width // 2) % width


def _ext_columns():
    z = lambda n: -np.ones((n,), np.int64)
    rng = lambda a, n: np.arange(a, a + n)
    o = _OFF
    k_pe = rng(o[2], 32)
    mla = np.concatenate([rng(o[0], 192), z(64), rng(o[1], 128), k_pe,
                          o[2] + _swap_halves(np.arange(32), 32), z(64)])
    hg = rng(o[3], 1280)
    q_heads = [rng(o[8] + 64 * h, 64) for h in range(4)]
    q_perm = np.concatenate([q_heads[0], q_heads[2], q_heads[1], q_heads[3]])
    k = rng(o[9], 128)
    v = rng(o[10], 128)
    sw = lambda cols, base: base + _swap_halves(cols - base, 64)
    swa = np.concatenate([q_perm, k, v, sw(q_perm, o[8]), sw(k, o[9])])
    rq, rk = rng(o[11], 128), rng(o[12], 128)
    ret = np.concatenate([rq, rk, o[11] + _swap_halves(rq - o[11], 32),
                          o[12] + _swap_halves(rk - o[12], 32), rng(o[13], 256), rng(o[14], 256)])
    cols = np.concatenate([mla, hg, swa, ret])
    assert cols.shape[0] == W_EXT
    return cols


_EXT_COLS = _ext_columns()


def _take_static(w, idx, axis):
    idx = np.asarray(idx)
    axis = axis % w.ndim
    pieces, start = [], 0
    for i in range(1, len(idx) + 1):
        run_ends = i == len(idx) or (idx[i] != idx[i - 1] + 1 if idx[i - 1] >= 0 else idx[i] >= 0)
        if run_ends:
            n = i - start
            if idx[start] < 0:
                shape = list(w.shape)
                shape[axis] = n
                pieces.append(jnp.zeros(shape, w.dtype))
            else:
                pieces.append(lax.slice_in_dim(w, int(idx[start]), int(idx[start]) + n, axis=axis))
            start = i
    return jnp.concatenate(pieces, axis=axis)


def _take_cols(w, cols):
    return _take_static(w, cols, -1)


def _axial_angles(n_tok, rot_dim):
    rows = n_tok // GRID_W
    row = jnp.broadcast_to(jnp.arange(rows)[:, None], (rows, GRID_W)).reshape(-1)
    col = jnp.broadcast_to(jnp.arange(GRID_W)[None, :], (rows, GRID_W)).reshape(-1)
    n_freq = rot_dim // 4
    inv = ROPE_THETA ** (-jnp.arange(n_freq, dtype=F32) / n_freq)
    return jnp.concatenate([row.astype(F32)[:, None] * inv, col.astype(F32)[:, None] * inv], axis=-1)


def _rope_tables(ang, n_ctx):
    cos, sin = jnp.cos(ang), jnp.sin(ang)
    c = jnp.concatenate([cos, cos], axis=-1)
    s = jnp.concatenate([-sin, sin], axis=-1)
    r = c.shape[-1]
    c = jnp.concatenate([jnp.ones((n_ctx, r), F32), c], axis=0)
    s = jnp.concatenate([jnp.zeros((n_ctx, r), F32), s], axis=0)
    return c, s


def _split2(x):
    hi = x.astype(BF16)
    lo = (x - hi.astype(F32)).astype(BF16)
    return hi, lo


def _seg_mean(x, ones_bd):
    hi, lo = _split2(x)
    return (jnp.dot(hi, ones_bd, preferred_element_type=F32)
            + jnp.dot(lo, ones_bd, preferred_element_type=F32))


def _silu(x):
    return x / (1.0 + jnp.exp(-x))


def _row_mod(m_ref, idx, row0, tm, n_ctx):
    rows = row0 + lax.broadcasted_iota(jnp.int32, (tm, 1), 0)
    mc = m_ref[0, 0, idx:idx + 1, :]
    ml = m_ref[0, 1, idx:idx + 1, :]
    return jnp.where(rows < n_ctx, mc, ml)


def _norm_mod(x, g, shift, scale):
    ms = jnp.mean(x * x, axis=-1, keepdims=True)
    y = x * lax.rsqrt(ms + NORM_EPS) * g
    return y * (1.0 + scale) + shift


def _mod_kernel(c_ref, w_ref, b_ref, o_ref):
    s = _silu(c_ref[...])
    o_ref[0] = jnp.dot(s.astype(BF16), w_ref[0].astype(BF16), preferred_element_type=F32) + b_ref[0]


def _modulation(cvec, w_ada, b_ada):
    R, D = cvec.shape
    depth, _, n6 = w_ada.shape
    tn = 1536
    return pl.pallas_call(
        _mod_kernel,
        grid=(depth, n6 // tn),
        in_specs=[pl.BlockSpec((R, D), lambda l, j: (0, 0)),
                  pl.BlockSpec((1, D, tn), lambda l, j: (l, 0, j)),
                  pl.BlockSpec((1, 1, tn), lambda l, j: (l, 0, j))],
        out_specs=pl.BlockSpec((1, R, tn), lambda l, j: (l, 0, j)),
        out_shape=jax.ShapeDtypeStruct((depth, R, n6), F32),
        compiler_params=_cparams(("parallel", "parallel")),
        name="adaln_mod",
    )(cvec, w_ada, b_ada.reshape(depth, 1, n6))


def _win_kernel(h_ref, m_ref, g_ref, w_ref, cq_ref, sq_ref, tk_ref, gq_ref, wq_ref, wqs_ref, gk_ref, wk_ref,
                wv_ref, e_ref, sc_ref, ss_ref, mq_out, mk_out, mv_out, o_hg, sq_out, skv_out, o_ret,
                *, tm, n_ctx):
    row0 = pl.program_id(1) * tm
    a = _norm_mod(h_ref[0], g_ref[...], _row_mod(m_ref, 0, row0, tm, n_ctx),
                  _row_mod(m_ref, 1, row0, tm, n_ctx)).astype(BF16)
    group = lambda c0, w: jnp.dot(a, w_ref[:, c0:c0 + w], preferred_element_type=F32)
    mq_out[0], mk_out[0], mv_out[0] = _mla_prep(group(0, W_MLA), cq_ref, sq_ref, tk_ref, gq_ref, wq_ref, wqs_ref,
                                                gk_ref, wk_ref, wv_ref, e_ref)
    o_hg[0] = group(W_MLA, W_HG)
    sq_out[0], skv_out[0] = _swa_prep(group(W_MLA + W_HG, W_SWA), sc_ref, ss_ref)
    o_ret[0] = group(W_MLA + W_HG + W_SWA, W_RET)


def _in_proj(h, mod, g, w_ext, mla_weights, mla_tables, swa_tables, n_ctx, tm=384):
    B, S, D = h.shape
    full = lambda a: pl.BlockSpec(a.shape, lambda b, i: (0,) * a.ndim)
    tab = pl.BlockSpec((tm, LANE), lambda b, i: (i, 0))
    out_w = ((512, BF16), (512, BF16), (512, BF16), (W_HG, F32), (256, BF16), (256, BF16), (W_RET, F32))
    outs = pl.pallas_call(
        functools.partial(_win_kernel, tm=tm, n_ctx=n_ctx),
        grid=(B, S // tm),
        in_specs=[pl.BlockSpec((1, tm, D), lambda b, i: (b, i, 0)),
                  pl.BlockSpec((1, 2, 8, D), lambda b, i: (b, 0, 0, 0)),
                  pl.BlockSpec((1, D), lambda b, i: (0, 0)),
                  pl.BlockSpec((D, W_EXT), lambda b, i: (0, 0)),
                  tab, tab, tab] + [full(w) for w in mla_weights] + [tab, tab],
        out_specs=[pl.BlockSpec((1, tm, w), lambda b, i: (b, i, 0)) for w, _ in out_w],
        out_shape=[jax.ShapeDtypeStruct((B, S, w), dt) for w, dt in out_w],
        compiler_params=_cparams(("parallel", "parallel")),
        name="in_proj",
    )(h, mod, g.reshape(1, D), w_ext, *mla_tables, *mla_weights, *swa_tables)
    return outs[0:3], outs[3], outs[4:6], outs[6]


def _mla_weights(q_norm_g, w_uq, kv_norm_g, w_ukv):
    qcols, qsw, kcols = [], [], []
    for h in range(MLA_HEADS):
        b = 96 * h
        nope = np.arange(b, b + 64)
        pe = np.arange(b + 64, b + 96)
        pad = -np.ones((32,), np.int64)
        qcols.append(np.concatenate([nope, pe, pad]))
        qsw.append(np.concatenate([-np.ones((64,), np.int64), (b + 64) + _swap_halves(np.arange(32), 32), pad]))
        kcols.append(np.concatenate([np.arange(128 * h, 128 * h + 64), -np.ones((64,), np.int64)]))
    vcols = np.concatenate([np.concatenate([np.arange(128 * h + 64, 128 * h + 128), -np.ones((64,), np.int64)])
                            for h in range(MLA_HEADS)])
    pad_rows = ((0, 256 - MLA_Q_RANK), (0, 0))
    wq = jnp.pad(_take_cols(w_uq, np.concatenate(qcols)), pad_rows).astype(BF16)
    wq_sw = jnp.pad(_take_cols(w_uq, np.concatenate(qsw)), pad_rows).astype(BF16)
    wk = _take_cols(w_ukv, np.concatenate(kcols)).astype(BF16)
    wv = _take_cols(w_ukv, vcols).astype(BF16)
    e = np.zeros((128, 512), np.float32)
    for h in range(MLA_HEADS):
        for j in range(32):
            e[j, 128 * h + 64 + j] = 1.0
            e[32 + j, 128 * h + 64 + j] = 1.0
    gq = jnp.pad(q_norm_g, (0, 256 - MLA_Q_RANK)).reshape(1, 256)
    return gq, wq, wq_sw, kv_norm_g.reshape(1, 128), wk, wv, jnp.asarray(e, BF16)


def _mla_tables(n_lat, n_ctx):
    ang = _axial_angles(n_lat, MLA_ROPE)
    c32, s32 = _rope_tables(ang, n_ctx)
    S = n_lat + n_ctx
    one, zero = jnp.ones((S, 64), F32), jnp.zeros((S, 64), F32)
    cq = jnp.concatenate([one, c32, jnp.zeros((S, 32), F32)], axis=-1)
    sq = jnp.concatenate([zero, s32, jnp.zeros((S, 32), F32)], axis=-1)
    tk = jnp.concatenate([c32, s32, zero], axis=-1)
    return cq, sq, tk


def _mla_prep(p, cq_ref, sq_ref, tk_ref, gq_ref, wq_ref, wqs_ref, gk_ref, wk_ref, wv_ref, e_ref):
    scale = (MLA_NOPE + MLA_ROPE) ** -0.5 * float(np.log2(np.e))
    ql = p[:, 0:256]
    msq = jnp.sum(ql * ql, axis=-1, keepdims=True) * (1.0 / MLA_Q_RANK)
    qn = (ql * lax.rsqrt(msq + NORM_EPS) * gq_ref[...]).astype(BF16)
    kvl = p[:, 256:384]
    msk = jnp.mean(kvl * kvl, axis=-1, keepdims=True)
    kvn = (kvl * lax.rsqrt(msk + NORM_EPS) * gk_ref[...]).astype(BF16)
    q = jnp.dot(qn, wq_ref[...], preferred_element_type=F32)
    qs = jnp.dot(qn, wqs_ref[...], preferred_element_type=F32)
    cq = jnp.concatenate([cq_ref[...]] * MLA_HEADS, axis=-1)
    sq = jnp.concatenate([sq_ref[...]] * MLA_HEADS, axis=-1)
    q_out = ((q * cq + qs * sq) * scale).astype(BF16)
    pe = (p[:, 384:512] * tk_ref[...]).astype(BF16)
    k = (jnp.dot(kvn, wk_ref[...], preferred_element_type=F32)
         + jnp.dot(pe, e_ref[...], preferred_element_type=F32))
    lane = lax.broadcasted_iota(jnp.int32, (1, MLA_HEADS * LANE), 1) % LANE
    ones = jnp.where(lane >= MLA_V, 1.0, 0.0)
    v = jnp.dot(kvn, wv_ref[...], preferred_element_type=F32) + ones
    return q_out, k.astype(BF16), v.astype(BF16)


def _mla_attn_kernel(q_ref, k_ref, v_ref, o_ref, *, n_ctx, tq):
    S = k_ref.shape[1]

    lo = lax.broadcasted_iota(jnp.int32, (tq, LANE), 1) < MLA_V

    def attend(nk):
        pv = []
        for h in range(MLA_HEADS):
            q = q_ref[0, :, LANE * h:LANE * (h + 1)]
            k = k_ref[0, 0:nk, LANE * h:LANE * (h + 1)]
            s = lax.dot_general(q, k, (((1,), (1,)), ((), ())), preferred_element_type=F32)
            p = jnp.exp2(s - jnp.max(s, axis=-1, keepdims=True))
            pv.append(jnp.dot(p.astype(BF16), v_ref[0, 0:nk, LANE * h:LANE * (h + 1)], preferred_element_type=F32))
        for pair in range(MLA_HEADS // 2):
            a, b = pv[2 * pair], pv[2 * pair + 1]
            o_ref[0, :, LANE * pair:LANE * (pair + 1)] = jnp.where(
                lo, a / pltpu.roll(a, MLA_V, axis=1), pltpu.roll(b, MLA_V, axis=1) / b).astype(o_ref.dtype)

    is_ctx = (pl.program_id(1) + 1) * tq <= n_ctx
    pl.when(is_ctx)(lambda: attend(n_ctx))
    pl.when(jnp.logical_not(is_ctx))(lambda: attend(S))


def _mla_mixer(qkv, n_ctx, tq=256):
    q, k, v = qkv
    B, S, _ = q.shape
    assert n_ctx % tq == 0 and S % tq == 0
    return pl.pallas_call(
        functools.partial(_mla_attn_kernel, n_ctx=n_ctx, tq=tq),
        grid=(B, S // tq),
        in_specs=[pl.BlockSpec((1, tq, 512), lambda b, i: (b, i, 0)),
                  pl.BlockSpec((1, S, 512), lambda b, i: (b, 0, 0)),
                  pl.BlockSpec((1, S, 512), lambda b, i: (b, 0, 0))],
        out_specs=pl.BlockSpec((1, tq, GROUP_W), lambda b, i: (b, i, 0)),
        out_shape=jax.ShapeDtypeStruct((B, S, GROUP_W), BF16),
        compiler_params=_cparams(("parallel", "arbitrary")),
        name="mla_attn",
    )(q, k, v)


def _swa_tables(n_lat, n_ctx):
    ang = _axial_angles(n_lat, SWA_HD)
    c64, s64 = _rope_tables(ang, n_ctx)
    return jnp.concatenate([c64, c64], axis=-1), jnp.concatenate([s64, s64], axis=-1)


def _swa_prep(p, c_ref, s_ref):
    scale = SWA_HD ** -0.5
    c, s = c_ref[...], s_ref[...]
    c2 = jnp.concatenate([c, c], axis=-1)
    s2 = jnp.concatenate([s, s], axis=-1)
    q = ((p[:, 0:256] * c2 + p[:, 512:768] * s2) * scale).astype(BF16)
    k = p[:, 256:384] * c + p[:, 768:896] * s
    return q, jnp.concatenate([k, p[:, 384:512]], axis=-1).astype(BF16)


def _swa_attn_kernel(sink_ref, q_ref, kvc_ref, kvp_ref, kvn_ref, kvx_ref, bias_ref, o_ref, *, n_ctx, blk):
    j = pl.program_id(1)
    ncb = n_ctx // blk
    lane = lax.broadcasted_iota(jnp.int32, (blk, LANE), 1)
    lo = lane < SWA_HD
    qa, qb = q_ref[0, :, 0:LANE], q_ref[0, :, LANE:2 * LANE]
    zero = jnp.zeros((), BF16)
    qs = jnp.concatenate([jnp.where(lo, qa, zero), jnp.where(lo, qb, zero),
                          jnp.where(lo, zero, qa), jnp.where(lo, zero, qb)], axis=0)
    rb = lax.broadcasted_iota(jnp.int32, (4 * blk, 1), 0) // blk
    sink = jnp.where(rb == 0, sink_ref[0], jnp.where(rb == 1, sink_ref[1],
                                                     jnp.where(rb == 2, sink_ref[2], sink_ref[3])))

    def finish(s, vcat):
        m = jnp.maximum(jnp.max(s, axis=-1, keepdims=True), sink)
        p = jnp.exp(s - m)
        l = jnp.sum(p, axis=-1, keepdims=True) + jnp.exp(sink - m)
        r = jnp.dot(p.astype(BF16), vcat, preferred_element_type=F32) / l
        o_ref[0, :, 0:LANE] = jnp.where(lo, r[0:blk], r[2 * blk:3 * blk]).astype(o_ref.dtype)
        o_ref[0, :, LANE:2 * LANE] = jnp.where(lo, r[blk:2 * blk], r[3 * blk:4 * blk]).astype(o_ref.dtype)

    def ctx_block():
        s = lax.dot_general(qs, kvc_ref[0, :, 0:LANE], (((1,), (1,)), ((), ())), preferred_element_type=F32)
        finish(s, kvc_ref[0, :, LANE:2 * LANE])

    def lat_block():
        refs = (kvc_ref, kvp_ref, kvn_ref, kvx_ref)
        kcat = jnp.concatenate([r[0, :, 0:LANE] for r in refs], axis=0)
        vcat = jnp.concatenate([r[0, :, LANE:2 * LANE] for r in refs], axis=0)
        s = lax.dot_general(qs, kcat, (((1,), (1,)), ((), ())), preferred_element_type=F32)
        finish(s + jnp.concatenate([bias_ref[0]] * SWA_HEADS, axis=0), vcat)

    pl.when(j < ncb)(ctx_block)
    pl.when(j >= ncb)(lat_block)


def _swa_mixer(qkv, sink, n_ctx, blk=128):
    q, kv_all = qkv
    B, S, _ = q.shape
    nb = S // blk
    ncb = n_ctx // blk
    cur = lambda b, j: (b, j, 0)
    prev = lambda b, j: (b, jnp.maximum(j - 1, ncb), 0)
    nxt = lambda b, j: (b, jnp.minimum(j + 1, nb - 1), 0)
    kv = lambda f: pl.BlockSpec((1, blk, 2 * LANE), f)
    ctx = pl.BlockSpec((1, n_ctx, 2 * LANE), lambda b, j: (b, 0, 0))
    t = np.arange(blk)[:, None]
    c = np.arange(n_ctx + 3 * blk)[None, :] - n_ctx
    bias = np.stack([np.where((c < 0) | ((c >= 0) & (c < blk) & (c >= t) & bool(hp)) | ((c >= blk) & (c < 2 * blk))
                              | ((c >= 2 * blk) & (c - 2 * blk <= t) & bool(hn)), 0.0, -1e30)
                     for hp in (0, 1) for hn in (0, 1)]).astype(np.float32)
    kind = lambda b, j: (2 * (j > ncb).astype(jnp.int32) + (j < nb - 1).astype(jnp.int32), 0, 0)
    return pl.pallas_call(
        functools.partial(_swa_attn_kernel, n_ctx=n_ctx, blk=blk),
        grid=(B, nb),
        in_specs=[pl.BlockSpec(memory_space=pltpu.SMEM),
                  pl.BlockSpec((1, blk, 256), cur), ctx, kv(prev), kv(cur), kv(nxt),
                  pl.BlockSpec((1, blk, n_ctx + 3 * blk), kind)],
        out_specs=pl.BlockSpec((1, blk, GROUP_W), cur),
        out_shape=jax.ShapeDtypeStruct((B, S, GROUP_W), BF16),
        compiler_params=_cparams(("parallel", "arbitrary")),
        name="swa_attn",
    )(sink.astype(F32), q, kv_all, kv_all, kv_all, kv_all, jnp.asarray(bias))


def _bwd_chunk(j, ncc, nc):
    return jnp.where(j < ncc, ncc - 1 - j, nc - 1 - (j - ncc))


def _head_stack(x, width, heads):
    lane_h = lax.broadcasted_iota(jnp.int32, x.shape, 1) // width
    zero = jnp.zeros((), x.dtype)
    return jnp.concatenate([jnp.where(lane_h == h, x, zero) for h in range(heads)], axis=0)


def _head_unstack(y, rows, width, heads):
    lane_h = lax.broadcasted_iota(jnp.int32, (rows, heads * width), 1) // width
    out = y[0:rows]
    for h in range(1, heads):
        out = jnp.where(lane_h == h, y[h * rows:(h + 1) * rows], out)
    return out


RET_CHUNK = 256


def _ret_tables(log_gamma, n_ctx, S):
    C = RET_CHUNK
    t = jnp.arange(C, dtype=F32)
    rel = t[:, None] - t[None, :]
    lg = log_gamma.astype(F32)
    lg_l = jnp.repeat(lg, RET_DK, axis=1)
    dm, xi, zeta, dcay = [], [], [], []
    for d in range(2):
        r = rel if d == 0 else -rel
        dh = jnp.where(r >= 0, jnp.exp(jnp.maximum(r, 0.0)[None] * lg[d][:, None, None]), 0.0)
        dm.append(dh.reshape(RET_HEADS * C, C))
        tq = (t + 1.0) if d == 0 else (C - t)
        tk = (C - 1.0 - t) if d == 0 else t
        xi.append(jnp.exp(tq[:, None] * lg_l[d][None, :]))
        zeta.append(jnp.exp(tk[:, None] * lg_l[d][None, :]))
        dcay.append(jnp.broadcast_to(jnp.exp(C * lg_l[d])[:, None], (RET_HEADS * RET_DK, GROUP_W)))
    mask = (np.arange(128)[:, None] // RET_DK == np.arange(256)[None, :] // RET_DV).astype(np.float32)
    pos = jnp.arange(S, dtype=F32)
    inv = ROPE_THETA ** (-jnp.arange(RET_DK // 2, dtype=F32) / (RET_DK // 2))
    ang = pos[:, None] * inv
    cos, sin = jnp.cos(ang), jnp.sin(ang)
    c = jnp.tile(jnp.concatenate([cos, cos], axis=-1), (1, RET_HEADS))
    s = jnp.tile(jnp.concatenate([-sin, sin], axis=-1), (1, RET_HEADS))
    return (jnp.stack(dm), jnp.stack(xi), jnp.stack(zeta), jnp.stack(dcay), jnp.asarray(mask), c, s)


def _ret_kernel(qkf_ref, vf_ref, cf_ref, sf_ref, qkb_ref, vb_ref, cb_ref, sb_ref,
                dm_ref, xi_ref, zeta_ref, dcay_ref, mask_ref, of_ref, ob_ref, state):
    C = RET_CHUNK

    @pl.when(pl.program_id(1) == 0)
    def _():
        state[...] = jnp.zeros_like(state)

    dirs = ((qkf_ref, vf_ref, cf_ref, sf_ref, of_ref), (qkb_ref, vb_ref, cb_ref, sb_ref, ob_ref))
    for d, (qk_ref, v_ref, c_ref, s_ref, o_ref) in enumerate(dirs):
        c, s = c_ref[...], s_ref[...]
        qr = (qk_ref[0, :, 0:128] * c + qk_ref[0, :, 256:384] * s) * (RET_DK ** -0.5)
        kr = qk_ref[0, :, 128:256] * c + qk_ref[0, :, 384:512] * s
        vb = v_ref[0].astype(BF16)
        qs = _head_stack(qr.astype(BF16), RET_DK, RET_HEADS)
        sc = lax.dot_general(qs, kr.astype(BF16), (((1,), (1,)), ((), ())), preferred_element_type=F32)
        a = (sc * dm_ref[d]).astype(BF16)
        o = _head_unstack(jnp.dot(a, vb, preferred_element_type=F32), C, RET_DV, RET_HEADS)
        st = state[d]
        o = o + jnp.dot((qr * xi_ref[d]).astype(BF16), st.astype(BF16), preferred_element_type=F32)
        o_ref[0] = o.astype(o_ref.dtype)
        kz = (kr * zeta_ref[d]).astype(BF16)
        u = lax.dot_general(kz, vb, (((0,), (0,)), ((), ())), preferred_element_type=F32)
        state[d] = st * dcay_ref[d] + u * mask_ref[...]


def _ret_mixer(p_ret, tables, n_ctx):
    B, S, _ = p_ret.shape
    C = RET_CHUNK
    dm, xi, zeta, dcay, mask, c, s = tables
    nc, ncc = S // C, n_ctx // C
    fwd = lambda b, j: (b, j, 0)
    bwd = lambda b, j: (b, _bwd_chunk(j, ncc, nc), 0)
    fwd_v = lambda b, j: (b, j, 2)
    bwd_v = lambda b, j: (b, _bwd_chunk(j, ncc, nc), 2)
    fwd_t = lambda b, j: (j, 0)
    bwd_t = lambda b, j: (_bwd_chunk(j, ncc, nc), 0)
    full = lambda a: pl.BlockSpec(a.shape, lambda b, j: (0,) * a.ndim)
    return pl.pallas_call(
        _ret_kernel,
        grid=(B, nc),
        in_specs=[pl.BlockSpec((1, C, 512), fwd), pl.BlockSpec((1, C, 256), fwd_v),
                  pl.BlockSpec((C, LANE), fwd_t), pl.BlockSpec((C, LANE), fwd_t),
                  pl.BlockSpec((1, C, 512), bwd), pl.BlockSpec((1, C, 256), bwd_v),
                  pl.BlockSpec((C, LANE), bwd_t), pl.BlockSpec((C, LANE), bwd_t),
                  full(dm), full(xi), full(zeta), full(dcay), full(mask)],
        out_specs=[pl.BlockSpec((1, C, GROUP_W), fwd), pl.BlockSpec((1, C, GROUP_W), bwd)],
        out_shape=[jax.ShapeDtypeStruct((B, S, GROUP_W), BF16)] * 2,
        scratch_shapes=[pltpu.VMEM((2, RET_HEADS * RET_DK, GROUP_W), F32)],
        compiler_params=_cparams(("parallel", "arbitrary")),
        name="retention_scan",
    )(p_ret, p_ret, c, s, p_ret, p_ret, c, s, dm, xi, zeta, dcay, mask)


HG_CHUNK = 128


def _hg_level_tables(C):
    levels = []
    n = 2
    while n <= C:
        levels.append(n)
        n *= 2
    t = np.arange(C)
    mats, qroles, kroles = [], [], []
    for d in range(2):
        rows = []
        qr_d, kr_d = [], []
        for n in levels:
            m = n // 2
            start = (t // n) * n
            mid = start + m
            u = t - start
            M = np.zeros((C, C), np.float32)
            for i in range(C):
                if d == 0:
                    if u[i] >= m:
                        M[i, mid[i]:i + 1] = 1.0
                    else:
                        M[i, i + 1:mid[i]] = 1.0
                else:
                    if u[i] < m:
                        M[i, i:mid[i]] = 1.0
                    else:
                        M[i, mid[i]:i] = 1.0
            rows.append(M)
            qr_d.append((u >= m) if d == 0 else (u < m))
            kr_d.append((u < m) if d == 0 else (u >= m))
        Mq = np.zeros((C, C), np.float32)
        Mk = np.zeros((C, C), np.float32)
        for i in range(C):
            if d == 0:
                Mq[i, 0:i + 1] = 1.0
                Mk[i, i + 1:C] = 1.0
            else:
                Mq[i, i:C] = 1.0
                Mk[i, 0:i] = 1.0
        rows += [Mq, Mk, np.ones((8, C), np.float32)]
        mats.append(np.concatenate(rows, axis=0))
        qroles.append(np.stack(qr_d))
        kroles.append(np.stack(kr_d))
    same_block = np.stack([(t[:, None] // n == t[None, :] // n) for n in levels]).astype(np.float32)
    return levels, np.stack(mats), np.stack(qroles), np.stack(kroles), same_block


def _hg_kernel(qf_ref, zf_ref, if_ref, qb_ref, zb_ref, ib_ref, lb_ref, m_ref, role_ref, blk_ref,
               ones_ref, mask_ref, of_ref, ob_ref, state, *, n_levels):
    C = HG_CHUNK

    @pl.when(pl.program_id(1) == 0)
    def _():
        state[...] = jnp.zeros_like(state)

    dirs = ((qf_ref, zf_ref, if_ref), (qb_ref, zb_ref, ib_ref))
    st = [state[0], state[1]]
    q, v, vb, kk, r = [], [], [], [], []
    for d, (q_ref, z_ref, i_ref) in enumerate(dirs):
        z = z_ref[0]
        lb = lb_ref[d]
        lbf = jnp.maximum(lb, LB_FLOOR)
        t = jnp.exp(-jnp.abs(z))
        rcp = 1.0 / (1.0 + t)
        pos = z >= 0.0
        sig = jnp.where(pos, rcp, t * rcp)
        nsig = jnp.where(pos, t * rcp, rcp)
        logf = jnp.log(lbf + (1.0 - lb) * sig)
        kk.append((1.0 - lb) * nsig - (lbf - lb))
        hi, lo = _split2(logf)
        r.append(jnp.dot(m_ref[d], hi, preferred_element_type=F32)
                 + jnp.dot(m_ref[d], lo, preferred_element_type=F32))
        q.append(q_ref[0] * (HG_DK ** -0.5))
        v.append(i_ref[0])
        vb.append(v[d].astype(BF16))
    acc = [jnp.zeros((HG_HEADS * C, C), F32), jnp.zeros((HG_HEADS * C, C), F32)]
    for lv in range(n_levels):
        for d in range(2):
            e = jnp.exp(r[d][lv * C:(lv + 1) * C])
            roles = role_ref[d, lv]
            qe = jnp.where(roles > 0.5, q[d] * e, 0.0).astype(BF16)
            ke = jnp.where(roles > 0.5, 0.0, kk[d] * e).astype(BF16)
            a = lax.dot_general(_head_stack(qe, HG_DK, HG_HEADS), ke, (((1,), (1,)), ((), ())),
                                preferred_element_type=F32)
            acc[d] = acc[d] + a * blk_ref[lv]
    outs = []
    for d in range(2):
        o = _head_unstack(jnp.dot(acc[d].astype(BF16), vb[d], preferred_element_type=F32), C, HG_DV, HG_HEADS)
        o = o + jnp.dot((q[d] * kk[d]).astype(BF16), ones_ref[...], preferred_element_type=F32) * v[d]
        e_q = jnp.exp(r[d][n_levels * C:(n_levels + 1) * C])
        o = o + lax.dot_general((q[d] * e_q).astype(BF16), st[d].astype(BF16), (((1,), (1,)), ((), ())),
                                preferred_element_type=F32)
        outs.append(o)
    new_state = []
    for d in range(2):
        e_k = jnp.exp(r[d][(n_levels + 1) * C:(n_levels + 2) * C])
        ku = (kk[d] * e_k).astype(BF16)
        u = lax.dot_general(vb[d], ku, (((0,), (0,)), ((), ())), preferred_element_type=F32)
        dtot = jnp.exp(r[d][(n_levels + 2) * C:(n_levels + 2) * C + 1])
        new_state.append(st[d] * dtot + u * mask_ref[...])
    of_ref[0] = outs[0].astype(of_ref.dtype)
    ob_ref[0] = outs[1].astype(ob_ref.dtype)
    state[0] = new_state[0]
    state[1] = new_state[1]


def _hg_mixer(p_hg, lb, n_ctx):
    B, S, _ = p_hg.shape
    C = HG_CHUNK
    levels, mats, qroles, kroles, same_block = _hg_level_tables(C)
    nl = len(levels)
    m_all = jnp.asarray(mats, BF16)
    roles = jnp.asarray(np.broadcast_to(qroles[..., None], (2, nl, C, GROUP_W)).astype(np.float32))
    blk = jnp.asarray(np.tile(same_block, (1, HG_HEADS, 1)))
    hd = np.arange(GROUP_W) // HG_DK
    bd = (hd[:, None] == hd[None, :]).astype(np.float32)
    ones_bd = jnp.asarray(bd, BF16)
    mask = jnp.asarray(bd)
    nc, ncc = S // C, n_ctx // C
    col = lambda k, rev: (lambda b, j: (b, _bwd_chunk(j, ncc, nc) if rev else j, k))
    blkspec = lambda k, rev: pl.BlockSpec((1, C, GROUP_W), col(k, rev))
    full = lambda a: pl.BlockSpec(a.shape, lambda b, j: (0,) * a.ndim)
    lb3 = lb.reshape(2, 1, GROUP_W).astype(F32)
    return pl.pallas_call(
        functools.partial(_hg_kernel, n_levels=nl),
        grid=(B, nc),
        in_specs=[blkspec(0, False), blkspec(1, False), blkspec(3, False),
                  blkspec(0, True), blkspec(2, True), blkspec(3, True),
                  full(lb3), full(m_all), full(roles), full(blk), full(ones_bd), full(mask)],
        out_specs=[blkspec(0, False), blkspec(0, True)],
        out_shape=[jax.ShapeDtypeStruct((B, S, GROUP_W), BF16)] * 2,
        scratch_shapes=[pltpu.VMEM((2, GROUP_W, GROUP_W), F32)],
        compiler_params=_cparams(("parallel", "arbitrary")),
        name="hgrn2_scan",
    )(p_hg, p_hg, p_hg, p_hg, p_hg, p_hg, lb3, m_all, roles, blk, ones_bd, mask)


_SWA_OUT_PERM = np.concatenate([np.arange(64 * h, 64 * h + 64) for h in (0, 2, 1, 3)])


def _wout_kernel(h_ref, m_ref, a_ref, hf_ref, hb_ref, hgate_ref, c_ref, rf_ref, rb_ref, rgate_ref,
                 hgn_ref, rgg_ref, rgb_ref, ones_ref, w_ref, o_ref, *, tm, n_ctx, row_off):
    row0 = row_off + pl.program_id(1) * tm
    ones = ones_ref[...]
    o = hf_ref[0].astype(F32) + hb_ref[0].astype(F32)
    yb = o * lax.rsqrt(_seg_mean(o * o, ones) + NORM_EPS) * hgn_ref[...] * _silu(hgate_ref[0])
    o = rf_ref[0].astype(F32) + rb_ref[0].astype(F32)
    xc = o - _seg_mean(o, ones)
    yd = (xc * lax.rsqrt(_seg_mean(xc * xc, ones) + NORM_EPS) * rgg_ref[...] + rgb_ref[...]) * _silu(rgate_ref[0])
    acc = jnp.dot(a_ref[0].astype(BF16), w_ref[0:256, :], preferred_element_type=F32)
    acc += jnp.dot(yb.astype(BF16), w_ref[256:512, :], preferred_element_type=F32)
    acc += jnp.dot(c_ref[0].astype(BF16), w_ref[512:768, :], preferred_element_type=F32)
    acc += jnp.dot(yd.astype(BF16), w_ref[768:1024, :], preferred_element_type=F32)
    o_ref[0] = h_ref[0] + _row_mod(m_ref, 2, row0, tm, n_ctx) * acc


def _out_proj(h, mod, o_mla, hg_f, hg_b, p_hg, o_swa, ret_f, ret_b, p_ret, hg_norm_g, ret_gn_g, ret_gn_b,
              w_out_p, n_ctx, tm=384, row_off=0):
    B, S, D = h.shape
    assert row_off % tm == 0 and (S - row_off) % tm == 0
    off = row_off // tm
    hd = np.arange(GROUP_W) // HG_DV
    ones_bd = jnp.asarray((hd[:, None] == hd[None, :]).astype(np.float32) / HG_DV, BF16)
    row = lambda w, k=0: pl.BlockSpec((1, tm, w), lambda b, i: (b, i + off, k))
    vec = pl.BlockSpec((1, GROUP_W), lambda b, i: (0, 0))
    return pl.pallas_call(
        functools.partial(_wout_kernel, tm=tm, n_ctx=n_ctx, row_off=row_off),
        grid=(B, (S - row_off) // tm),
        in_specs=[row(D), pl.BlockSpec((1, 2, 8, D), lambda b, i: (b, 0, 0, 0)),
                  row(GROUP_W), row(GROUP_W), row(GROUP_W), row(GROUP_W, 4),
                  row(GROUP_W), row(GROUP_W), row(GROUP_W), row(GROUP_W, 3),
                  vec, vec, vec, pl.BlockSpec((GROUP_W, GROUP_W), lambda b, i: (0, 0)),
                  pl.BlockSpec((D, D), lambda b, i: (0, 0))],
        out_specs=pl.BlockSpec((1, tm, D), lambda b, i: (b, i, 0)),
        out_shape=jax.ShapeDtypeStruct((B, S - row_off, D), F32),
        compiler_params=_cparams(("parallel", "parallel")),
        name="out_proj",
    )(h, mod, o_mla, hg_f, hg_b, p_hg, o_swa, ret_f, ret_b, p_ret,
      hg_norm_g.reshape(1, GROUP_W), ret_gn_g.reshape(1, GROUP_W), ret_gn_b.reshape(1, GROUP_W), ones_bd, w_out_p)


def _ffn_kernel(h_ref, m_ref, g_ref, wg_ref, wu_ref, wd_ref, o_ref, xn, *, tm, n_ctx):
    j = pl.program_id(2)
    row0 = pl.program_id(1) * tm

    @pl.when(j == 0)
    def _():
        xn[...] = _norm_mod(h_ref[0], g_ref[...], _row_mod(m_ref, 3, row0, tm, n_ctx),
                            _row_mod(m_ref, 4, row0, tm, n_ctx)).astype(BF16)
        o_ref[...] = jnp.zeros_like(o_ref)

    wg, wu, wd = wg_ref[0], wu_ref[0], wd_ref[0]
    half = tm // 2
    for r0 in (0, half):
        x = xn[r0:r0 + half]
        hg = jnp.dot(x, wg, preferred_element_type=F32)
        hu = jnp.dot(x, wu, preferred_element_type=F32)
        o_ref[0, r0:r0 + half] += jnp.dot((_silu(hg) * hu).astype(BF16), wd, preferred_element_type=F32)

    @pl.when(j == pl.num_programs(2) - 1)
    def _():
        o_ref[0] = h_ref[0] + _row_mod(m_ref, 5, row0, tm, n_ctx) * o_ref[0]


def _dense_ffn(h, mod, g, w_gate, w_up, w_down, li, n_ctx, tm=1152, tf=1408):
    B, S, D = h.shape
    dff = w_gate.shape[-1]
    return pl.pallas_call(
        functools.partial(_ffn_kernel, tm=tm, n_ctx=n_ctx),
        grid=(B, S // tm, dff // tf),
        in_specs=[pl.BlockSpec((1, tm, D), lambda b, i, j: (b, i, 0)),
                  pl.BlockSpec((1, 2, 8, D), lambda b, i, j: (b, 0, 0, 0)),
                  pl.BlockSpec((1, D), lambda b, i, j: (0, 0)),
                  pl.BlockSpec((1, D, tf), lambda b, i, j: (li, 0, j)),
                  pl.BlockSpec((1, D, tf), lambda b, i, j: (li, 0, j)),
                  pl.BlockSpec((1, tf, D), lambda b, i, j: (li, j, 0))],
        out_specs=pl.BlockSpec((1, tm, D), lambda b, i, j: (b, i, 0)),
        out_shape=jax.ShapeDtypeStruct((B, S, D), F32),
        scratch_shapes=[pltpu.VMEM((tm, D), BF16)],
        compiler_params=_cparams(("parallel", "parallel", "arbitrary")),
        name="dense_ffn",
    )(h, mod, g.reshape(1, D), w_gate, w_up, w_down)


MOE_TM = 1024
MOE_TF = 896
MOE_ROWS = (384, 256)
DMA_UNROLL = 8
SUBLANES = 8
ZERO_ROWS = 256


def _router_kernel(h_ref, m_ref, g_ref, wr_ref, tri_ref, xn_ref, route_ref, cnt_ref, counts, *, tm, n_ctx):
    b, i = pl.program_id(0), pl.program_id(1)
    row0 = i * tm

    @pl.when((b == 0) & (i == 0))
    def _():
        counts[...] = jnp.zeros_like(counts)

    a = _norm_mod(h_ref[0], g_ref[...], _row_mod(m_ref, 3, row0, tm, n_ctx), _row_mod(m_ref, 4, row0, tm, n_ctx))
    xn_ref[0] = a
    ah, al = _split2(a)
    wh, wl = _split2(wr_ref[...])
    logits = (jnp.dot(ah, wh, preferred_element_type=F32) + jnp.dot(ah, wl, preferred_element_type=F32)
              + jnp.dot(al, wh, preferred_element_type=F32))
    lane = lax.broadcasted_iota(jnp.int32, logits.shape, 1)
    neg = jnp.float32(-jnp.inf)
    lg = jnp.where(lane < N_EXPERTS, logits, neg)
    v1 = jnp.max(lg, axis=-1, keepdims=True)
    i1 = jnp.min(jnp.where(lg == v1, lane, LANE), axis=-1, keepdims=True)
    lg2 = jnp.where(lane == i1, neg, lg)
    v2 = jnp.max(lg2, axis=-1, keepdims=True)
    i2 = jnp.min(jnp.where(lg2 == v2, lane, LANE), axis=-1, keepdims=True)
    e = jnp.exp(v2 - v1)
    w1 = 1.0 / (1.0 + e)
    w2 = e / (1.0 + e)
    oh1, oh2 = lane == i1, lane == i2
    onehot = jnp.where(oh1 | oh2, 1.0, 0.0)
    earlier = jnp.dot(tri_ref[...], onehot.astype(BF16), preferred_element_type=F32)
    pos = earlier + counts[...]
    r1 = jnp.sum(jnp.where(oh1, pos, 0.0), axis=-1, keepdims=True)
    r2 = jnp.sum(jnp.where(oh2, pos, 0.0), axis=-1, keepdims=True)
    counts[...] = counts[...] + jnp.sum(onehot, axis=0, keepdims=True)
    cnt_ref[...] = jnp.broadcast_to(counts[...], cnt_ref.shape)
    route_ref[0] = jnp.where(lane == 0, i1.astype(F32),
                             jnp.where(lane == 1, i2.astype(F32),
                                       jnp.where(lane == 2, w1,
                                                 jnp.where(lane == 3, w2,
                                                           jnp.where(lane == 4, r1, jnp.where(lane == 5, r2, 0.0))))))


def _scatter_kernel(meta_ref, slot_ref, x_ref, xs_hbm, zeros_v, sem, *, tm, tile_rows, total_rows):
    b, i = pl.program_id(0), pl.program_id(1)

    def start(r, carry):
        for k in range(TOP_K):
            pltpu.make_async_copy(x_ref.at[0, pl.ds(r, 1)], xs_hbm.at[pl.ds(slot_ref[0, 0, 0, k * tm + r], 1)],
                                  sem).start()
        return carry

    lax.fori_loop(0, tm, start, 0, unroll=DMA_UNROLL)
    for k in range(TOP_K):
        pltpu.make_async_copy(x_ref.at[0], xs_hbm.at[pl.ds(0, tm)], sem).wait()

    @pl.when((b == pl.num_programs(0) - 1) & (i == pl.num_programs(1) - 1))
    def _():
        zeros_v[...] = jnp.zeros_like(zeros_v)
        z1 = zeros_v.at[pl.ds(0, 1)]
        z8 = zeros_v.at[pl.ds(0, SUBLANES)]

        def fill(lo, end):
            mid = jnp.minimum(((lo + SUBLANES - 1) // SUBLANES) * SUBLANES, end)
            big = mid + ((end - mid) // ZERO_ROWS) * ZERO_ROWS
            n_row, n_big, n_small = mid - lo, (big - mid) // ZERO_ROWS, (end - big) // SUBLANES

            def start_row(r, c):
                pltpu.make_async_copy(z1, xs_hbm.at[pl.ds(lo + r, 1)], sem).start()
                return c

            def wait_row(r, c):
                pltpu.make_async_copy(z1, xs_hbm.at[pl.ds(0, 1)], sem).wait()
                return c

            def start_big(q, c):
                dst = xs_hbm.at[pl.ds(pl.multiple_of(mid + q * ZERO_ROWS, SUBLANES), ZERO_ROWS)]
                pltpu.make_async_copy(zeros_v, dst, sem).start()
                return c

            def wait_big(q, c):
                pltpu.make_async_copy(zeros_v, xs_hbm.at[pl.ds(0, ZERO_ROWS)], sem).wait()
                return c

            def start_small(q, c):
                dst = xs_hbm.at[pl.ds(pl.multiple_of(big + q * SUBLANES, SUBLANES), SUBLANES)]
                pltpu.make_async_copy(z8, dst, sem).start()
                return c

            def wait_small(q, c):
                pltpu.make_async_copy(z8, xs_hbm.at[pl.ds(0, SUBLANES)], sem).wait()
                return c

            lax.fori_loop(0, n_row, start_row, 0)
            lax.fori_loop(0, n_big, start_big, 0)
            lax.fori_loop(0, n_small, start_small, 0)
            lax.fori_loop(0, n_row, wait_row, 0)
            lax.fori_loop(0, n_big, wait_big, 0)
            lax.fori_loop(0, n_small, wait_small, 0)

        for e in range(N_EXPERTS):
            c = meta_ref[e]
            g0 = meta_ref[N_EXPERTS + e]
            fill(g0 + c, g0 + ((c + tile_rows - 1) // tile_rows) * tile_rows)
        fill(meta_ref[2 * N_EXPERTS] * tile_rows, jnp.int32(total_rows))


def _expert_kernel(te_ref, nu_ref, x_ref, wg_ref, wu_ref, wd_ref, o_ref, xb):
    i, j = pl.program_id(0), pl.program_id(1)
    half = x_ref.shape[0] // 2

    @pl.when(j == 0)
    def _():
        xb[...] = x_ref[...].astype(BF16)
        o_ref[...] = jnp.zeros_like(o_ref)

    @pl.when(i < nu_ref[0])
    def _():
        wg, wu, wd = wg_ref[0, 0].astype(BF16), wu_ref[0, 0].astype(BF16), wd_ref[0, 0].astype(BF16)
        for r0 in (0, half):
            x = xb[r0:r0 + half]
            hg = jnp.dot(x, wg, preferred_element_type=F32)
            hu = jnp.dot(x, wu, preferred_element_type=F32)
            o_ref[r0:r0 + half] += jnp.dot((_silu(hg) * hu).astype(BF16), wd, preferred_element_type=F32)


def _combine_kernel(slot_ref, h_ref, m_ref, route_ref, fg_ref, y_hbm, o_ref, buf, sem, *, tm, n_ctx, final_norm):
    row0 = pl.program_id(1) * tm

    def start(r, carry):
        for k in range(TOP_K):
            pltpu.make_async_copy(y_hbm.at[pl.ds(slot_ref[0, 0, 0, k * tm + r], 1)], buf.at[k, pl.ds(r, 1)], sem).start()
        return carry

    lax.fori_loop(0, tm, start, 0, unroll=DMA_UNROLL)
    for k in range(TOP_K):
        pltpu.make_async_copy(y_hbm.at[pl.ds(0, tm)], buf.at[k], sem).wait()
    w1 = route_ref[0, :, 2:3]
    w2 = route_ref[0, :, 3:4]
    y = h_ref[0] + _row_mod(m_ref, 5, row0, tm, n_ctx) * (w1 * buf[0] + w2 * buf[1])
    if final_norm:
        y = y * lax.rsqrt(jnp.mean(y * y, axis=-1, keepdims=True) + NORM_EPS) * fg_ref[...]
    o_ref[0] = y


def _moe_ffn(h, mod, g, w_router, w_gate, w_up, w_down, li, n_ctx, final_g=None):
    B, S, D = h.shape
    T = B * S
    cr = next(r for r in MOE_ROWS if S % r == 0)
    tm = cr
    dffe = w_gate.shape[-1]
    te_rows = MOE_TM if T % MOE_TM == 0 else 512
    n_tiles = TOP_K * T // te_rows + N_EXPERTS
    P = n_tiles * te_rows
    wr = jnp.pad(w_router, ((0, 0), (0, LANE - N_EXPERTS)))
    tri = jnp.asarray(np.tril(np.ones((tm, tm), np.float32), -1), BF16)
    xn, route, cnt = pl.pallas_call(
        functools.partial(_router_kernel, tm=tm, n_ctx=n_ctx),
        grid=(B, S // tm),
        in_specs=[pl.BlockSpec((1, tm, D), lambda b, i: (b, i, 0)),
                  pl.BlockSpec((1, 2, 8, D), lambda b, i: (b, 0, 0, 0)),
                  pl.BlockSpec((1, D), lambda b, i: (0, 0)),
                  pl.BlockSpec((D, LANE), lambda b, i: (0, 0)),
                  pl.BlockSpec((tm, tm), lambda b, i: (0, 0))],
        out_specs=[pl.BlockSpec((1, tm, D), lambda b, i: (b, i, 0)),
                   pl.BlockSpec((1, tm, LANE), lambda b, i: (b, i, 0)),
                   pl.BlockSpec((SUBLANES, LANE), lambda b, i: (0, 0))],
        out_shape=[jax.ShapeDtypeStruct((B, S, D), F32), jax.ShapeDtypeStruct((B, S, LANE), F32),
                   jax.ShapeDtypeStruct((SUBLANES, LANE), F32)],
        scratch_shapes=[pltpu.VMEM((1, LANE), F32)],
        compiler_params=_cparams(("arbitrary", "arbitrary")),
        name="moe_router",
    )(h, mod, g.reshape(1, D), wr, tri)

    counts = cnt[0, :N_EXPERTS].astype(jnp.int32)
    ptiles = (counts + te_rows - 1) // te_rows
    tile_end = jnp.cumsum(ptiles)
    gstart = (tile_end - ptiles) * te_rows
    n_used = tile_end[-1:].astype(jnp.int32)
    tile_ids = jnp.arange(n_tiles, dtype=jnp.int32)
    tile_e = jnp.minimum(jnp.sum((tile_ids[:, None] >= tile_end[None, :]).astype(jnp.int32), axis=1),
                         N_EXPERTS - 1)
    meta = jnp.concatenate([counts, gstart, n_used])
    choice = route[..., 0:TOP_K].astype(jnp.int32)
    is_e = choice[..., None] == jnp.arange(N_EXPERTS, dtype=jnp.int32)
    slots = jnp.sum(jnp.where(is_e, gstart, 0), axis=-1) + route[..., 4:4 + TOP_K].astype(jnp.int32)
    slots = slots.reshape(B, S // cr, cr, TOP_K).transpose(0, 1, 3, 2)
    slots = slots.reshape(B, S // cr, 1, TOP_K * cr)

    xs = pl.pallas_call(
        functools.partial(_scatter_kernel, tm=cr, tile_rows=te_rows, total_rows=P),
        grid_spec=pltpu.PrefetchScalarGridSpec(
            num_scalar_prefetch=1,
            grid=(B, S // cr),
            in_specs=[pl.BlockSpec((1, 1, 1, TOP_K * cr), lambda b, i, mt: (b, i, 0, 0),
                                   memory_space=pltpu.SMEM),
                      pl.BlockSpec((1, cr, D), lambda b, i, mt: (b, i, 0))],
            out_specs=pl.BlockSpec(memory_space=pl.ANY),
            scratch_shapes=[pltpu.VMEM((ZERO_ROWS, D), F32), pltpu.SemaphoreType.DMA(())]),
        out_shape=jax.ShapeDtypeStruct((P, D), F32),
        compiler_params=_cparams(("arbitrary", "arbitrary")),
        name="moe_scatter",
    )(meta, slots, xn)

    nf = dffe // MOE_TF
    last = lambda i, nu: jnp.minimum(i, nu[0] - 1)
    jj = lambda i, j, nu: jnp.where(i < nu[0], j, nf - 1)
    ys = pl.pallas_call(
        _expert_kernel,
        grid_spec=pltpu.PrefetchScalarGridSpec(
            num_scalar_prefetch=2,
            grid=(n_tiles, nf),
            in_specs=[pl.BlockSpec((te_rows, D), lambda i, j, te, nu: (last(i, nu), 0)),
                      pl.BlockSpec((1, 1, D, MOE_TF), lambda i, j, te, nu: (li, te[last(i, nu)], 0, jj(i, j, nu))),
                      pl.BlockSpec((1, 1, D, MOE_TF), lambda i, j, te, nu: (li, te[last(i, nu)], 0, jj(i, j, nu))),
                      pl.BlockSpec((1, 1, MOE_TF, D), lambda i, j, te, nu: (li, te[last(i, nu)], jj(i, j, nu), 0))],
            out_specs=pl.BlockSpec((te_rows, D), lambda i, j, te, nu: (i, 0)),
            scratch_shapes=[pltpu.VMEM((te_rows, D), BF16)]),
        out_shape=jax.ShapeDtypeStruct((P, D), F32),
        compiler_params=_cparams(("arbitrary", "arbitrary")),
        name="moe_experts",
    )(tile_e, n_used, xs, w_gate, w_up, w_down)
    fg = jnp.ones((1, D), F32) if final_g is None else final_g.reshape(1, D).astype(F32)
    return pl.pallas_call(
        functools.partial(_combine_kernel, tm=cr, n_ctx=n_ctx, final_norm=final_g is not None),
        grid=(B, S // cr),
        in_specs=[pl.BlockSpec((1, 1, 1, TOP_K * cr), lambda b, i: (b, i, 0, 0), memory_space=pltpu.SMEM),
                  pl.BlockSpec((1, cr, D), lambda b, i: (b, i, 0)),
                  pl.BlockSpec((1, 2, 8, D), lambda b, i: (b, 0, 0, 0)),
                  pl.BlockSpec((1, cr, LANE), lambda b, i: (b, i, 0)),
                  pl.BlockSpec((1, D), lambda b, i: (0, 0)),
                  pl.BlockSpec(memory_space=pl.ANY)],
        out_specs=pl.BlockSpec((1, cr, D), lambda b, i: (b, i, 0)),
        out_shape=jax.ShapeDtypeStruct((B, S, D), F32),
        scratch_shapes=[pltpu.VMEM((TOP_K, cr, D), F32), pltpu.SemaphoreType.DMA(())],
        compiler_params=_cparams(("arbitrary", "arbitrary")),
        name="moe_combine",
    )(slots, h, mod, route, fg, ys)


def _final_norm_kernel(h_ref, g_ref, o_ref):
    x = h_ref[0]
    o_ref[0] = x * lax.rsqrt(jnp.mean(x * x, axis=-1, keepdims=True) + NORM_EPS) * g_ref[...]


def _final_norm(h, g, n_ctx, tm=256):
    B, S, D = h.shape
    N = S - n_ctx
    return pl.pallas_call(
        _final_norm_kernel,
        grid=(B, N // tm),
        in_specs=[pl.BlockSpec((1, tm, D), lambda b, i: (b, n_ctx // tm + i, 0)),
                  pl.BlockSpec((1, D), lambda b, i: (0, 0))],
        out_specs=pl.BlockSpec((1, tm, D), lambda b, i: (b, i, 0)),
        out_shape=jax.ShapeDtypeStruct((B, N, D), F32),
        compiler_params=_cparams(("parallel", "parallel")),
        name="final_norm",
    )(h, g.reshape(1, D))


MOD_ROWS = 16


def kernel(x, c, ctx, c_ctx, w_ada, b_ada, norm_mix_g, norm_ffn_g, w_in, w_out, mla_q_norm_g, mla_w_uq, mla_kv_norm_g, mla_w_ukv, hg_lb_logits, hg_norm_g, swa_sink, ret_decay_logit, ret_gn_g, ret_gn_b, ffn_w_gate, ffn_w_up, ffn_w_down, moe_w_router, moe_w_gate, moe_w_up, moe_w_down, final_norm_g):
    B, N, D = x.shape
    L = ctx.shape[1]
    S = L + N
    assert B + 1 <= MOD_ROWS
    h = jnp.concatenate([ctx, x], axis=1)
    cvec = jnp.zeros((MOD_ROWS, D), F32).at[0].set(c_ctx).at[1:1 + B].set(c)
    mod_all = _modulation(cvec, w_ada, b_ada).reshape(DEPTH, MOD_ROWS, 6, D)

    lb_p = jax.nn.softmax(hg_lb_logits.astype(F32), axis=0)
    hg_lb = jnp.cumsum(lb_p, axis=0) - lb_p[0:1]
    ret_log_gamma = jax.nn.log_sigmoid(ret_decay_logit.astype(F32))

    mla_tab = _mla_tables(N, L)
    swa_tab = _swa_tables(N, L)
    w_out_perm = np.concatenate([np.arange(512), 512 + _SWA_OUT_PERM, np.arange(768, 1024)])
    ffn_bf16 = tuple(w.astype(BF16) for w in (ffn_w_gate, ffn_w_up, ffn_w_down))

    for l in range(DEPTH):
        m = mod_all[l]
        mod = jnp.stack([jnp.broadcast_to(m[0], (B, 6, D)), m[1:1 + B]], axis=1)
        mod = jnp.pad(mod, ((0, 0), (0, 0), (0, 2), (0, 0)))
        w_ext = _take_cols(w_in[l], _EXT_COLS).astype(BF16)
        mla_w = _mla_weights(mla_q_norm_g[l], mla_w_uq[l], mla_kv_norm_g[l], mla_w_ukv[l])
        mla_qkv, p_hg, swa_qkv, p_ret = _in_proj(h, mod, norm_mix_g[l], w_ext, mla_w, mla_tab, swa_tab, L)
        o_mla = _mla_mixer(mla_qkv, L)
        hg_f, hg_b = _hg_mixer(p_hg, hg_lb[l], L)
        o_swa = _swa_mixer(swa_qkv, swa_sink[l], L)
        ret_f, ret_b = _ret_mixer(p_ret, _ret_tables(ret_log_gamma[l], L, S), L)
        w_out_p = _take_static(w_out[l], w_out_perm, 0).astype(BF16)
        last = l == DEPTH - 1
        if last and l % 2 == 1:
            h = _out_proj(h, mod, o_mla, hg_f, hg_b, p_hg, o_swa, ret_f, ret_b, p_ret,
                          hg_norm_g[l], ret_gn_g[l], ret_gn_b[l], w_out_p, L, tm=math.gcd(L, N, 256), row_off=L)
            return _moe_ffn(h, mod, norm_ffn_g[l], moe_w_router[l // 2], moe_w_gate, moe_w_up, moe_w_down, l // 2,
                            0, final_g=final_norm_g)
        h = _out_proj(h, mod, o_mla, hg_f, hg_b, p_hg, o_swa, ret_f, ret_b, p_ret,
                      hg_norm_g[l], ret_gn_g[l], ret_gn_b[l], w_out_p, L)
        if l % 2 == 0:
            h = _dense_ffn(h, mod, norm_ffn_g[l], ffn_bf16[0], ffn_bf16[1], ffn_bf16[2], l // 2, L,
                           tm=1152 if S % 1152 == 0 else 384)
        else:
            h = _moe_ffn(h, mod, norm_ffn_g[l], moe_w_router[l // 2], moe_w_gate, moe_w_up, moe_w_down, l // 2, L)
    return _final_norm(h, final_norm_g, L)
```

```python
import functools
import math

import numpy as np
import jax
import jax.numpy as jnp
from jax import lax
from jax.experimental import pallas as pl
from jax.experimental.pallas import tpu as pltpu

F32 = jnp.float32
BF16 = jnp.bfloat16

D_MODEL = 1024
DEPTH = 4
GRID_W = 64
ROPE_THETA = 10000.0
NORM_EPS = 1e-6
LB_FLOOR = 1e-30
GROUP_W = 256

MLA_HEADS, MLA_NOPE, MLA_ROPE, MLA_V = 4, 64, 32, 64
MLA_Q_RANK, MLA_KV_RANK = 192, 128
HG_HEADS, HG_DK, HG_DV = 4, 64, 64
SWA_HEADS, SWA_KV_HEADS, SWA_HD, SWA_WINDOW = 4, 2, 64, 128
RET_HEADS, RET_DK, RET_DV = 4, 32, 64
N_EXPERTS, TOP_K = 8, 2

_IN_SIZES = (192, 128, 32, 256, 256, 256, 256, 256, 256, 128, 128, 128, 128, 256, 256)
_OFF = np.concatenate([[0], np.cumsum(_IN_SIZES)]).astype(np.int64)
D_IN = int(_OFF[-1])

W_MLA, W_HG, W_SWA, W_RET = 512, 1280, 512, 768
W_EXT = W_MLA + W_HG + W_SWA + W_RET

LANE = 128
VMEM_LIMIT = 56 * 1024 * 1024


def _cparams(sem):
    return pltpu.CompilerParams(dimension_semantics=sem, vmem_limit_bytes=VMEM_LIMIT)


def _swap_halves(idx, width):
    idx = np.asarray(idx)
    base = (idx // width) * width
    j = idx % width
    return base + (j + width // 2) % width


def _ext_columns():
    z = lambda n: -np.ones((n,), np.int64)
    rng = lambda a, n: np.arange(a, a + n)
    o = _OFF
    k_pe = rng(o[2], 32)
    mla = np.concatenate([rng(o[0], 192), z(64), rng(o[1], 128), k_pe,
                          o[2] + _swap_halves(np.arange(32), 32), z(64)])
    hg = rng(o[3], 1280)
    q_heads = [rng(o[8] + 64 * h, 64) for h in range(4)]
    q_perm = np.concatenate([q_heads[0], q_heads[2], q_heads[1], q_heads[3]])
    swa = np.concatenate([q_perm, rng(o[9], 128), rng(o[10], 128)])
    ret = rng(o[11], 768)
    cols = np.concatenate([mla, hg, swa, ret])
    assert cols.shape[0] == W_EXT
    return cols


_EXT_COLS = _ext_columns()


def _take_static(w, idx, axis):
    idx = np.asarray(idx)
    axis = axis % w.ndim
    pieces, start = [], 0
    for i in range(1, len(idx) + 1):
        run_ends = i == len(idx) or (idx[i] != idx[i - 1] + 1 if idx[i - 1] >= 0 else idx[i] >= 0)
        if run_ends:
            n = i - start
            if idx[start] < 0:
                shape = list(w.shape)
                shape[axis] = n
                pieces.append(jnp.zeros(shape, w.dtype))
            else:
                pieces.append(lax.slice_in_dim(w, int(idx[start]), int(idx[start]) + n, axis=axis))
            start = i
    return jnp.concatenate(pieces, axis=axis)


def _take_cols(w, cols):
    return _take_static(w, cols, -1)


def _axial_angles(n_tok, rot_dim):
    rows = n_tok // GRID_W
    row = jnp.broadcast_to(jnp.arange(rows)[:, None], (rows, GRID_W)).reshape(-1)
    col = jnp.broadcast_to(jnp.arange(GRID_W)[None, :], (rows, GRID_W)).reshape(-1)
    n_freq = rot_dim // 4
    inv = ROPE_THETA ** (-jnp.arange(n_freq, dtype=F32) / n_freq)
    return jnp.concatenate([row.astype(F32)[:, None] * inv, col.astype(F32)[:, None] * inv], axis=-1)


def _rope_tables(ang, n_ctx):
    cos, sin = jnp.cos(ang), jnp.sin(ang)
    c = jnp.concatenate([cos, cos], axis=-1)
    s = jnp.concatenate([-sin, sin], axis=-1)
    r = c.shape[-1]
    c = jnp.concatenate([jnp.ones((n_ctx, r), F32), c], axis=0)
    s = jnp.concatenate([jnp.zeros((n_ctx, r), F32), s], axis=0)
    return c, s


def _split2(x):
    hi = x.astype(BF16)
    lo = (x - hi.astype(F32)).astype(BF16)
    return hi, lo


def _seg_mean(x, ones_bd):
    hi, lo = _split2(x)
    return (jnp.dot(hi, ones_bd, preferred_element_type=F32)
            + jnp.dot(lo, ones_bd, preferred_element_type=F32))


def _silu(x):
    return x / (1.0 + jnp.exp(-x))


def _rot_half(x, head):
    half = head // 2
    lane = lax.broadcasted_iota(jnp.int32, x.shape, 1) % head
    return jnp.where(lane < half, pltpu.roll(x, LANE - half, axis=1), pltpu.roll(x, half, axis=1))


def _row_mod(m_ref, idx, row0, tm, n_ctx):
    rows = row0 + lax.broadcasted_iota(jnp.int32, (tm, 1), 0)
    mc = m_ref[0, 0, idx:idx + 1, :]
    ml = m_ref[0, 1, idx:idx + 1, :]
    return jnp.where(rows < n_ctx, mc, ml)


def _norm_mod(x, g, shift, scale):
    ms = jnp.mean(x * x, axis=-1, keepdims=True)
    y = x * lax.rsqrt(ms + NORM_EPS) * g
    return y * (1.0 + scale) + shift


def _mod_kernel(c_ref, w_ref, b_ref, o_ref):
    s = _silu(c_ref[...])
    o_ref[0] = jnp.dot(s.astype(BF16), w_ref[0].astype(BF16), preferred_element_type=F32) + b_ref[0]


def _modulation(cvec, w_ada, b_ada):
    R, D = cvec.shape
    depth, _, n6 = w_ada.shape
    tn = 1536
    return pl.pallas_call(
        _mod_kernel,
        grid=(depth, n6 // tn),
        in_specs=[pl.BlockSpec((R, D), lambda l, j: (0, 0)),
                  pl.BlockSpec((1, D, tn), lambda l, j: (l, 0, j)),
                  pl.BlockSpec((1, 1, tn), lambda l, j: (l, 0, j))],
        out_specs=pl.BlockSpec((1, R, tn), lambda l, j: (l, 0, j)),
        out_shape=jax.ShapeDtypeStruct((depth, R, n6), F32),
        compiler_params=_cparams(("parallel", "parallel")),
        name="adaln_mod",
    )(cvec, w_ada, b_ada.reshape(depth, 1, n6))


def _win_kernel(h_ref, m_ref, g_ref, w_ref, cq_ref, sq_ref, tk_ref, gq_ref, wq_ref, wqs_ref, gk_ref, wk_ref,
                wv_ref, e_ref, sc_ref, ss_ref, mq_out, mk_out, mv_out, o_hg, sq_out, skv_out, o_ret,
                *, tm, n_ctx):
    row0 = pl.program_id(1) * tm
    a = _norm_mod(h_ref[0], g_ref[...], _row_mod(m_ref, 0, row0, tm, n_ctx),
                  _row_mod(m_ref, 1, row0, tm, n_ctx)).astype(BF16)
    group = lambda c0, w: jnp.dot(a, w_ref[:, c0:c0 + w], preferred_element_type=F32)
    mq_out[0], mk_out[0], mv_out[0] = _mla_prep(group(0, W_MLA), cq_ref, sq_ref, tk_ref, gq_ref, wq_ref, wqs_ref,
                                                gk_ref, wk_ref, wv_ref, e_ref)
    o_hg[0] = group(W_MLA, W_HG)
    sq_out[0], skv_out[0] = _swa_prep(group(W_MLA + W_HG, W_SWA), sc_ref, ss_ref)
    o_ret[0] = group(W_MLA + W_HG + W_SWA, W_RET)


def _in_proj(h, mod, g, w_ext, mla_weights, mla_tables, swa_tables, n_ctx, tm=768):
    B, S, D = h.shape
    full = lambda a: pl.BlockSpec(a.shape, lambda b, i: (0,) * a.ndim)
    tab = pl.BlockSpec((tm, LANE), lambda b, i: (i, 0))
    out_w = ((512, BF16), (512, BF16), (512, BF16), (W_HG, F32), (256, BF16), (256, BF16), (W_RET, F32))
    outs = pl.pallas_call(
        functools.partial(_win_kernel, tm=tm, n_ctx=n_ctx),
        grid=(B, S // tm),
        in_specs=[pl.BlockSpec((1, tm, D), lambda b, i: (b, i, 0)),
                  pl.BlockSpec((1, 2, 8, D), lambda b, i: (b, 0, 0, 0)),
                  pl.BlockSpec((1, D), lambda b, i: (0, 0)),
                  pl.BlockSpec((D, W_EXT), lambda b, i: (0, 0)),
                  tab, tab, tab] + [full(w) for w in mla_weights] + [tab, tab],
        out_specs=[pl.BlockSpec((1, tm, w), lambda b, i: (b, i, 0)) for w, _ in out_w],
        out_shape=[jax.ShapeDtypeStruct((B, S, w), dt) for w, dt in out_w],
        compiler_params=_cparams(("parallel", "parallel")),
        name="in_proj",
    )(h, mod, g.reshape(1, D), w_ext, *mla_tables, *mla_weights, *swa_tables)
    return outs[0:3], outs[3], outs[4:6], outs[6]


def _mla_weights(q_norm_g, w_uq, kv_norm_g, w_ukv):
    qcols, qsw, kcols = [], [], []
    for h in range(MLA_HEADS):
        b = 96 * h
        nope = np.arange(b, b + 64)
        pe = np.arange(b + 64, b + 96)
        pad = -np.ones((32,), np.int64)
        qcols.append(np.concatenate([nope, pe, pad]))
        qsw.append(np.concatenate([-np.ones((64,), np.int64), (b + 64) + _swap_halves(np.arange(32), 32), pad]))
        kcols.append(np.concatenate([np.arange(128 * h, 128 * h + 64), -np.ones((64,), np.int64)]))
    vcols = np.concatenate([np.concatenate([np.arange(128 * h + 64, 128 * h + 128), -np.ones((64,), np.int64)])
                            for h in range(MLA_HEADS)])
    pad_rows = ((0, 256 - MLA_Q_RANK), (0, 0))
    wq = jnp.pad(_take_cols(w_uq, np.concatenate(qcols)), pad_rows).astype(BF16)
    wq_sw = jnp.pad(_take_cols(w_uq, np.concatenate(qsw)), pad_rows).astype(BF16)
    wk = _take_cols(w_ukv, np.concatenate(kcols)).astype(BF16)
    wv = _take_cols(w_ukv, vcols).astype(BF16)
    e = np.zeros((128, 512), np.float32)
    for h in range(MLA_HEADS):
        for j in range(32):
            e[j, 128 * h + 64 + j] = 1.0
            e[32 + j, 128 * h + 64 + j] = 1.0
    gq = jnp.pad(q_norm_g, (0, 256 - MLA_Q_RANK)).reshape(1, 256)
    return gq, wq, wq_sw, kv_norm_g.reshape(1, 128), wk, wv, jnp.asarray(e, BF16)


def _mla_tables(n_lat, n_ctx):
    ang = _axial_angles(n_lat, MLA_ROPE)
    c32, s32 = _rope_tables(ang, n_ctx)
    S = n_lat + n_ctx
    one, zero = jnp.ones((S, 64), F32), jnp.zeros((S, 64), F32)
    cq = jnp.concatenate([one, c32, jnp.zeros((S, 32), F32)], axis=-1)
    sq = jnp.concatenate([zero, s32, jnp.zeros((S, 32), F32)], axis=-1)
    tk = jnp.concatenate([c32, s32, zero], axis=-1)
    return cq, sq, tk


def _mla_prep(p, cq_ref, sq_ref, tk_ref, gq_ref, wq_ref, wqs_ref, gk_ref, wk_ref, wv_ref, e_ref):
    scale = (MLA_NOPE + MLA_ROPE) ** -0.5 * float(np.log2(np.e))
    ql = p[:, 0:256]
    msq = jnp.sum(ql * ql, axis=-1, keepdims=True) * (1.0 / MLA_Q_RANK)
    qn = (ql * lax.rsqrt(msq + NORM_EPS) * gq_ref[...]).astype(BF16)
    kvl = p[:, 256:384]
    msk = jnp.mean(kvl * kvl, axis=-1, keepdims=True)
    kvn = (kvl * lax.rsqrt(msk + NORM_EPS) * gk_ref[...]).astype(BF16)
    q = jnp.dot(qn, wq_ref[...], preferred_element_type=F32)
    qs = jnp.dot(qn, wqs_ref[...], preferred_element_type=F32)
    cq = jnp.concatenate([cq_ref[...]] * MLA_HEADS, axis=-1)
    sq = jnp.concatenate([sq_ref[...]] * MLA_HEADS, axis=-1)
    q_out = ((q * cq + qs * sq) * scale).astype(BF16)
    pe = (p[:, 384:512] * tk_ref[...]).astype(BF16)
    k = (jnp.dot(kvn, wk_ref[...], preferred_element_type=F32)
         + jnp.dot(pe, e_ref[...], preferred_element_type=F32))
    lane = lax.broadcasted_iota(jnp.int32, (1, MLA_HEADS * LANE), 1) % LANE
    ones = jnp.where(lane >= MLA_V, 1.0, 0.0)
    v = jnp.dot(kvn, wv_ref[...], preferred_element_type=F32) + ones
    return q_out, k.astype(BF16), v.astype(BF16)


def _mla_attn_kernel(q_ref, k_ref, v_ref, o_ref, *, n_ctx, tq):
    S = k_ref.shape[1]

    lo = lax.broadcasted_iota(jnp.int32, (tq, LANE), 1) < MLA_V

    def attend(nk):
        pv = []
        for h in range(MLA_HEADS):
            q = q_ref[0, :, LANE * h:LANE * (h + 1)]
            k = k_ref[0, 0:nk, LANE * h:LANE * (h + 1)]
            s = lax.dot_general(q, k, (((1,), (1,)), ((), ())), preferred_element_type=F32)
            p = jnp.exp2(s - jnp.max(s, axis=-1, keepdims=True))
            pv.append(jnp.dot(p.astype(BF16), v_ref[0, 0:nk, LANE * h:LANE * (h + 1)], preferred_element_type=F32))
        for pair in range(MLA_HEADS // 2):
            a, b = pv[2 * pair], pv[2 * pair + 1]
            o_ref[0, :, LANE * pair:LANE * (pair + 1)] = jnp.where(
                lo, a / pltpu.roll(a, MLA_V, axis=1), pltpu.roll(b, MLA_V, axis=1) / b).astype(o_ref.dtype)

    is_ctx = (pl.program_id(1) + 1) * tq <= n_ctx
    pl.when(is_ctx)(lambda: attend(n_ctx))
    pl.when(jnp.logical_not(is_ctx))(lambda: attend(S))


def _mla_mixer(qkv, n_ctx, tq=256):
    q, k, v = qkv
    B, S, _ = q.shape
    assert n_ctx % tq == 0 and S % tq == 0
    return pl.pallas_call(
        functools.partial(_mla_attn_kernel, n_ctx=n_ctx, tq=tq),
        grid=(B, S // tq),
        in_specs=[pl.BlockSpec((1, tq, 512), lambda b, i: (b, i, 0)),
                  pl.BlockSpec((1, S, 512), lambda b, i: (b, 0, 0)),
                  pl.BlockSpec((1, S, 512), lambda b, i: (b, 0, 0))],
        out_specs=pl.BlockSpec((1, tq, GROUP_W), lambda b, i: (b, i, 0)),
        out_shape=jax.ShapeDtypeStruct((B, S, GROUP_W), BF16),
        compiler_params=_cparams(("parallel", "arbitrary")),
        name="mla_attn",
    )(q, k, v)


def _swa_tables(n_lat, n_ctx):
    ang = _axial_angles(n_lat, SWA_HD)
    c64, s64 = _rope_tables(ang, n_ctx)
    return jnp.concatenate([c64, c64], axis=-1), jnp.concatenate([s64, s64], axis=-1)


def _swa_prep(p, c_ref, s_ref):
    scale = SWA_HD ** -0.5
    c, s = c_ref[...], s_ref[...]
    rope = lambda x: x * c + _rot_half(x, SWA_HD) * s
    q = jnp.concatenate([rope(p[:, 0:LANE]), rope(p[:, LANE:2 * LANE])], axis=-1) * scale
    return q.astype(BF16), jnp.concatenate([rope(p[:, 256:384]), p[:, 384:512]], axis=-1).astype(BF16)


def _swa_attn_kernel(sink_ref, q_ref, kvc_ref, kvp_ref, kvn_ref, kvx_ref, bias_ref, o_ref, *, n_ctx, blk):
    j = pl.program_id(1)
    ncb = n_ctx // blk
    lane = lax.broadcasted_iota(jnp.int32, (blk, LANE), 1)
    lo = lane < SWA_HD
    qa, qb = q_ref[0, :, 0:LANE], q_ref[0, :, LANE:2 * LANE]
    zero = jnp.zeros((), BF16)
    qs = jnp.concatenate([jnp.where(lo, qa, zero), jnp.where(lo, qb, zero),
                          jnp.where(lo, zero, qa), jnp.where(lo, zero, qb)], axis=0)
    rb = lax.broadcasted_iota(jnp.int32, (4 * blk, 1), 0) // blk
    sink = jnp.where(rb == 0, sink_ref[0], jnp.where(rb == 1, sink_ref[1],
                                                     jnp.where(rb == 2, sink_ref[2], sink_ref[3])))

    def finish(s, vcat):
        m = jnp.maximum(jnp.max(s, axis=-1, keepdims=True), sink)
        p = jnp.exp(s - m)
        l = jnp.sum(p, axis=-1, keepdims=True) + jnp.exp(sink - m)
        r = jnp.dot(p.astype(BF16), vcat, preferred_element_type=F32) / l
        o_ref[0, :, 0:LANE] = jnp.where(lo, r[0:blk], r[2 * blk:3 * blk]).astype(o_ref.dtype)
        o_ref[0, :, LANE:2 * LANE] = jnp.where(lo, r[blk:2 * blk], r[3 * blk:4 * blk]).astype(o_ref.dtype)

    def ctx_block():
        s = lax.dot_general(qs, kvc_ref[0, :, 0:LANE], (((1,), (1,)), ((), ())), preferred_element_type=F32)
        finish(s, kvc_ref[0, :, LANE:2 * LANE])

    def lat_block():
        refs = (kvc_ref, kvp_ref, kvn_ref, kvx_ref)
        kcat = jnp.concatenate([r[0, :, 0:LANE] for r in refs], axis=0)
        vcat = jnp.concatenate([r[0, :, LANE:2 * LANE] for r in refs], axis=0)
        s = lax.dot_general(qs, kcat, (((1,), (1,)), ((), ())), preferred_element_type=F32)
        finish(s + jnp.concatenate([bias_ref[0]] * SWA_HEADS, axis=0), vcat)

    pl.when(j < ncb)(ctx_block)
    pl.when(j >= ncb)(lat_block)


def _swa_mixer(qkv, sink, n_ctx, blk=128):
    q, kv_all = qkv
    B, S, _ = q.shape
    nb = S // blk
    ncb = n_ctx // blk
    cur = lambda b, j: (b, j, 0)
    prev = lambda b, j: (b, jnp.maximum(j - 1, ncb), 0)
    nxt = lambda b, j: (b, jnp.minimum(j + 1, nb - 1), 0)
    kv = lambda f: pl.BlockSpec((1, blk, 2 * LANE), f)
    ctx = pl.BlockSpec((1, n_ctx, 2 * LANE), lambda b, j: (b, 0, 0))
    t = np.arange(blk)[:, None]
    c = np.arange(n_ctx + 3 * blk)[None, :] - n_ctx
    bias = np.stack([np.where((c < 0) | ((c >= 0) & (c < blk) & (c >= t) & bool(hp)) | ((c >= blk) & (c < 2 * blk))
                              | ((c >= 2 * blk) & (c - 2 * blk <= t) & bool(hn)), 0.0, -1e30)
                     for hp in (0, 1) for hn in (0, 1)]).astype(np.float32)
    kind = lambda b, j: (2 * (j > ncb).astype(jnp.int32) + (j < nb - 1).astype(jnp.int32), 0, 0)
    return pl.pallas_call(
        functools.partial(_swa_attn_kernel, n_ctx=n_ctx, blk=blk),
        grid=(B, nb),
        in_specs=[pl.BlockSpec(memory_space=pltpu.SMEM),
                  pl.BlockSpec((1, blk, 256), cur), ctx, kv(prev), kv(cur), kv(nxt),
                  pl.BlockSpec((1, blk, n_ctx + 3 * blk), kind)],
        out_specs=pl.BlockSpec((1, blk, GROUP_W), cur),
        out_shape=jax.ShapeDtypeStruct((B, S, GROUP_W), BF16),
        compiler_params=_cparams(("parallel", "arbitrary")),
        name="swa_attn",
    )(sink.astype(F32), q, kv_all, kv_all, kv_all, kv_all, jnp.asarray(bias))


def _bwd_chunk(j, ncc, nc):
    return jnp.where(j < ncc, ncc - 1 - j, nc - 1 - (j - ncc))


def _head_stack(x, width, heads):
    lane_h = lax.broadcasted_iota(jnp.int32, x.shape, 1) // width
    zero = jnp.zeros((), x.dtype)
    return jnp.concatenate([jnp.where(lane_h == h, x, zero) for h in range(heads)], axis=0)


def _head_unstack(y, rows, width, heads):
    lane_h = lax.broadcasted_iota(jnp.int32, (rows, heads * width), 1) // width
    out = y[0:rows]
    for h in range(1, heads):
        out = jnp.where(lane_h == h, y[h * rows:(h + 1) * rows], out)
    return out


RET_CHUNK = 256


def _ret_tables(log_gamma, n_ctx, S):
    C = RET_CHUNK
    t = jnp.arange(C, dtype=F32)
    rel = t[:, None] - t[None, :]
    lg = log_gamma.astype(F32)
    lg_l = jnp.repeat(lg, RET_DK, axis=1)
    dm, xi, zeta, dcay = [], [], [], []
    for d in range(2):
        r = rel if d == 0 else -rel
        dh = jnp.where(r >= 0, jnp.exp(jnp.maximum(r, 0.0)[None] * lg[d][:, None, None]), 0.0)
        dm.append(dh.reshape(RET_HEADS * C, C))
        tq = (t + 1.0) if d == 0 else (C - t)
        tk = (C - 1.0 - t) if d == 0 else t
        xi.append(jnp.exp(tq[:, None] * lg_l[d][None, :]))
        zeta.append(jnp.exp(tk[:, None] * lg_l[d][None, :]))
        dcay.append(jnp.broadcast_to(jnp.exp(C * lg_l[d])[:, None], (RET_HEADS * RET_DK, GROUP_W)))
    mask = (np.arange(128)[:, None] // RET_DK == np.arange(256)[None, :] // RET_DV).astype(np.float32)
    pos = jnp.arange(S, dtype=F32)
    inv = ROPE_THETA ** (-jnp.arange(RET_DK // 2, dtype=F32) / (RET_DK // 2))
    ang = pos[:, None] * inv
    cos, sin = jnp.cos(ang), jnp.sin(ang)
    c = jnp.tile(jnp.concatenate([cos, cos], axis=-1), (1, RET_HEADS))
    s = jnp.tile(jnp.concatenate([-sin, sin], axis=-1), (1, RET_HEADS))
    return (jnp.stack(dm), jnp.stack(xi), jnp.stack(zeta), jnp.stack(dcay), jnp.asarray(mask), c, s)


def _ret_kernel(qkf_ref, vf_ref, cf_ref, sf_ref, qkb_ref, vb_ref, cb_ref, sb_ref,
                dm_ref, xi_ref, zeta_ref, dcay_ref, mask_ref, of_ref, ob_ref, state):
    C = RET_CHUNK

    @pl.when(pl.program_id(1) == 0)
    def _():
        state[...] = jnp.zeros_like(state)

    dirs = ((qkf_ref, vf_ref, cf_ref, sf_ref, of_ref), (qkb_ref, vb_ref, cb_ref, sb_ref, ob_ref))
    for d, (qk_ref, v_ref, c_ref, s_ref, o_ref) in enumerate(dirs):
        c, s = c_ref[...], s_ref[...]
        q, k = qk_ref[0, :, 0:128], qk_ref[0, :, 128:256]
        qr = (q * c + _rot_half(q, RET_DK) * s) * (RET_DK ** -0.5)
        kr = k * c + _rot_half(k, RET_DK) * s
        vb = v_ref[0].astype(BF16)
        qs = _head_stack(qr.astype(BF16), RET_DK, RET_HEADS)
        sc = lax.dot_general(qs, kr.astype(BF16), (((1,), (1,)), ((), ())), preferred_element_type=F32)
        a = (sc * dm_ref[d]).astype(BF16)
        o = _head_unstack(jnp.dot(a, vb, preferred_element_type=F32), C, RET_DV, RET_HEADS)
        st = state[d]
        o = o + jnp.dot((qr * xi_ref[d]).astype(BF16), st.astype(BF16), preferred_element_type=F32)
        o_ref[0] = o.astype(o_ref.dtype)
        kz = (kr * zeta_ref[d]).astype(BF16)
        u = lax.dot_general(kz, vb, (((0,), (0,)), ((), ())), preferred_element_type=F32)
        state[d] = st * dcay_ref[d] + u * mask_ref[...]


def _ret_mixer(p_ret, tables, n_ctx):
    B, S, _ = p_ret.shape
    C = RET_CHUNK
    dm, xi, zeta, dcay, mask, c, s = tables
    nc, ncc = S // C, n_ctx // C
    fwd = lambda b, j: (b, j, 0)
    bwd = lambda b, j: (b, _bwd_chunk(j, ncc, nc), 0)
    fwd_v = lambda b, j: (b, j, 1)
    bwd_v = lambda b, j: (b, _bwd_chunk(j, ncc, nc), 1)
    fwd_t = lambda b, j: (j, 0)
    bwd_t = lambda b, j: (_bwd_chunk(j, ncc, nc), 0)
    full = lambda a: pl.BlockSpec(a.shape, lambda b, j: (0,) * a.ndim)
    return pl.pallas_call(
        _ret_kernel,
        grid=(B, nc),
        in_specs=[pl.BlockSpec((1, C, 256), fwd), pl.BlockSpec((1, C, 256), fwd_v),
                  pl.BlockSpec((C, LANE), fwd_t), pl.BlockSpec((C, LANE), fwd_t),
                  pl.BlockSpec((1, C, 256), bwd), pl.BlockSpec((1, C, 256), bwd_v),
                  pl.BlockSpec((C, LANE), bwd_t), pl.BlockSpec((C, LANE), bwd_t),
                  full(dm), full(xi), full(zeta), full(dcay), full(mask)],
        out_specs=[pl.BlockSpec((1, C, GROUP_W), fwd), pl.BlockSpec((1, C, GROUP_W), bwd)],
        out_shape=[jax.ShapeDtypeStruct((B, S, GROUP_W), BF16)] * 2,
        scratch_shapes=[pltpu.VMEM((2, RET_HEADS * RET_DK, GROUP_W), F32)],
        compiler_params=_cparams(("parallel", "arbitrary")),
        name="retention_scan",
    )(p_ret, p_ret, c, s, p_ret, p_ret, c, s, dm, xi, zeta, dcay, mask)


HG_CHUNK = 128


def _hg_level_tables(C):
    levels = []
    n = 2
    while n <= C:
        levels.append(n)
        n *= 2
    t = np.arange(C)
    mats, qroles, kroles = [], [], []
    for d in range(2):
        rows = []
        qr_d, kr_d = [], []
        for n in levels:
            m = n // 2
            start = (t // n) * n
            mid = start + m
            u = t - start
            M = np.zeros((C, C), np.float32)
            for i in range(C):
                if d == 0:
                    if u[i] >= m:
                        M[i, mid[i]:i + 1] = 1.0
                    else:
                        M[i, i + 1:mid[i]] = 1.0
                else:
                    if u[i] < m:
                        M[i, i:mid[i]] = 1.0
                    else:
                        M[i, mid[i]:i] = 1.0
            rows.append(M)
            qr_d.append((u >= m) if d == 0 else (u < m))
            kr_d.append((u < m) if d == 0 else (u >= m))
        Mq = np.zeros((C, C), np.float32)
        Mk = np.zeros((C, C), np.float32)
        for i in range(C):
            if d == 0:
                Mq[i, 0:i + 1] = 1.0
                Mk[i, i + 1:C] = 1.0
            else:
                Mq[i, i:C] = 1.0
                Mk[i, 0:i] = 1.0
        rows += [Mq, Mk, np.ones((8, C), np.float32)]
        mats.append(np.concatenate(rows, axis=0))
        qroles.append(np.stack(qr_d))
        kroles.append(np.stack(kr_d))
    same_block = np.stack([(t[:, None] // n == t[None, :] // n) for n in levels]).astype(np.float32)
    return levels, np.stack(mats), np.stack(qroles), np.stack(kroles), same_block


def _hg_kernel(qf_ref, zf_ref, if_ref, qb_ref, zb_ref, ib_ref, lb_ref, m_ref, role_ref, blk_ref,
               ones_ref, mask_ref, of_ref, ob_ref, state, *, n_levels):
    C = HG_CHUNK

    @pl.when(pl.program_id(1) == 0)
    def _():
        state[...] = jnp.zeros_like(state)

    dirs = ((qf_ref, zf_ref, if_ref), (qb_ref, zb_ref, ib_ref))
    st = [state[0], state[1]]
    q, v, vb, kk, r = [], [], [], [], []
    for d, (q_ref, z_ref, i_ref) in enumerate(dirs):
        z = z_ref[0]
        lb = lb_ref[d]
        lbf = jnp.maximum(lb, LB_FLOOR)
        t = jnp.exp(-jnp.abs(z))
        rcp = 1.0 / (1.0 + t)
        pos = z >= 0.0
        sig = jnp.where(pos, rcp, t * rcp)
        nsig = jnp.where(pos, t * rcp, rcp)
        logf = jnp.log(lbf + (1.0 - lb) * sig)
        kk.append((1.0 - lb) * nsig - (lbf - lb))
        hi, lo = _split2(logf)
        r.append(jnp.dot(m_ref[d], hi, preferred_element_type=F32)
                 + jnp.dot(m_ref[d], lo, preferred_element_type=F32))
        q.append(q_ref[0] * (HG_DK ** -0.5))
        v.append(i_ref[0])
        vb.append(v[d].astype(BF16))
    acc = [jnp.zeros((HG_HEADS * C, C), F32), jnp.zeros((HG_HEADS * C, C), F32)]
    for lv in range(n_levels):
        for d in range(2):
            e = jnp.exp(r[d][lv * C:(lv + 1) * C])
            roles = role_ref[d, lv]
            qe = jnp.where(roles > 0.5, q[d] * e, 0.0).astype(BF16)
            ke = jnp.where(roles > 0.5, 0.0, kk[d] * e).astype(BF16)
            a = lax.dot_general(_head_stack(qe, HG_DK, HG_HEADS), ke, (((1,), (1,)), ((), ())),
                                preferred_element_type=F32)
            acc[d] = acc[d] + a * blk_ref[lv]
    outs = []
    for d in range(2):
        o = _head_unstack(jnp.dot(acc[d].astype(BF16), vb[d], preferred_element_type=F32), C, HG_DV, HG_HEADS)
        o = o + jnp.dot((q[d] * kk[d]).astype(BF16), ones_ref[...], preferred_element_type=F32) * v[d]
        e_q = jnp.exp(r[d][n_levels * C:(n_levels + 1) * C])
        o = o + lax.dot_general((q[d] * e_q).astype(BF16), st[d].astype(BF16), (((1,), (1,)), ((), ())),
                                preferred_element_type=F32)
        outs.append(o)
    new_state = []
    for d in range(2):
        e_k = jnp.exp(r[d][(n_levels + 1) * C:(n_levels + 2) * C])
        ku = (kk[d] * e_k).astype(BF16)
        u = lax.dot_general(vb[d], ku, (((0,), (0,)), ((), ())), preferred_element_type=F32)
        dtot = jnp.exp(r[d][(n_levels + 2) * C:(n_levels + 2) * C + 1])
        new_state.append(st[d] * dtot + u * mask_ref[...])
    of_ref[0] = outs[0].astype(of_ref.dtype)
    ob_ref[0] = outs[1].astype(ob_ref.dtype)
    state[0] = new_state[0]
    state[1] = new_state[1]


def _hg_mixer(p_hg, lb, n_ctx):
    B, S, _ = p_hg.shape
    C = HG_CHUNK
    levels, mats, qroles, kroles, same_block = _hg_level_tables(C)
    nl = len(levels)
    m_all = jnp.asarray(mats, BF16)
    roles = jnp.asarray(np.broadcast_to(qroles[..., None], (2, nl, C, GROUP_W)).astype(np.float32))
    blk = jnp.asarray(np.tile(same_block, (1, HG_HEADS, 1)))
    hd = np.arange(GROUP_W) // HG_DK
    bd = (hd[:, None] == hd[None, :]).astype(np.float32)
    ones_bd = jnp.asarray(bd, BF16)
    mask = jnp.asarray(bd)
    nc, ncc = S // C, n_ctx // C
    col = lambda k, rev: (lambda b, j: (b, _bwd_chunk(j, ncc, nc) if rev else j, k))
    blkspec = lambda k, rev: pl.BlockSpec((1, C, GROUP_W), col(k, rev))
    full = lambda a: pl.BlockSpec(a.shape, lambda b, j: (0,) * a.ndim)
    lb3 = lb.reshape(2, 1, GROUP_W).astype(F32)
    return pl.pallas_call(
        functools.partial(_hg_kernel, n_levels=nl),
        grid=(B, nc),
        in_specs=[blkspec(0, False), blkspec(1, False), blkspec(3, False),
                  blkspec(0, True), blkspec(2, True), blkspec(3, True),
                  full(lb3), full(m_all), full(roles), full(blk), full(ones_bd), full(mask)],
        out_specs=[blkspec(0, False), blkspec(0, True)],
        out_shape=[jax.ShapeDtypeStruct((B, S, GROUP_W), BF16)] * 2,
        scratch_shapes=[pltpu.VMEM((2, GROUP_W, GROUP_W), F32)],
        compiler_params=_cparams(("parallel", "arbitrary")),
        name="hgrn2_scan",
    )(p_hg, p_hg, p_hg, p_hg, p_hg, p_hg, lb3, m_all, roles, blk, ones_bd, mask)


_SWA_OUT_PERM = np.concatenate([np.arange(64 * h, 64 * h + 64) for h in (0, 2, 1, 3)])


def _wout_kernel(h_ref, m_ref, a_ref, hf_ref, hb_ref, hgate_ref, c_ref, rf_ref, rb_ref, rgate_ref,
                 hgn_ref, rgg_ref, rgb_ref, ones_ref, w_ref, o_ref, *, tm, n_ctx, row_off):
    row0 = row_off + pl.program_id(1) * tm
    ones = ones_ref[...]
    o = hf_ref[0].astype(F32) + hb_ref[0].astype(F32)
    yb = o * lax.rsqrt(_seg_mean(o * o, ones) + NORM_EPS) * hgn_ref[...] * _silu(hgate_ref[0])
    o = rf_ref[0].astype(F32) + rb_ref[0].astype(F32)
    xc = o - _seg_mean(o, ones)
    yd = (xc * lax.rsqrt(_seg_mean(xc * xc, ones) + NORM_EPS) * rgg_ref[...] + rgb_ref[...]) * _silu(rgate_ref[0])
    acc = jnp.dot(a_ref[0].astype(BF16), w_ref[0:256, :], preferred_element_type=F32)
    acc += jnp.dot(yb.astype(BF16), w_ref[256:512, :], preferred_element_type=F32)
    acc += jnp.dot(c_ref[0].astype(BF16), w_ref[512:768, :], preferred_element_type=F32)
    acc += jnp.dot(yd.astype(BF16), w_ref[768:1024, :], preferred_element_type=F32)
    o_ref[0] = h_ref[0] + _row_mod(m_ref, 2, row0, tm, n_ctx) * acc


def _out_proj(h, mod, o_mla, hg_f, hg_b, p_hg, o_swa, ret_f, ret_b, p_ret, hg_norm_g, ret_gn_g, ret_gn_b,
              w_out_p, n_ctx, tm=384, row_off=0):
    B, S, D = h.shape
    assert row_off % tm == 0 and (S - row_off) % tm == 0
    off = row_off // tm
    hd = np.arange(GROUP_W) // HG_DV
    ones_bd = jnp.asarray((hd[:, None] == hd[None, :]).astype(np.float32) / HG_DV, BF16)
    row = lambda w, k=0: pl.BlockSpec((1, tm, w), lambda b, i: (b, i + off, k))
    vec = pl.BlockSpec((1, GROUP_W), lambda b, i: (0, 0))
    return pl.pallas_call(
        functools.partial(_wout_kernel, tm=tm, n_ctx=n_ctx, row_off=row_off),
        grid=(B, (S - row_off) // tm),
        in_specs=[row(D), pl.BlockSpec((1, 2, 8, D), lambda b, i: (b, 0, 0, 0)),
                  row(GROUP_W), row(GROUP_W), row(GROUP_W), row(GROUP_W, 4),
                  row(GROUP_W), row(GROUP_W), row(GROUP_W), row(GROUP_W, 2),
                  vec, vec, vec, pl.BlockSpec((GROUP_W, GROUP_W), lambda b, i: (0, 0)),
                  pl.BlockSpec((D, D), lambda b, i: (0, 0))],
        out_specs=pl.BlockSpec((1, tm, D), lambda b, i: (b, i, 0)),
        out_shape=jax.ShapeDtypeStruct((B, S - row_off, D), F32),
        compiler_params=_cparams(("parallel", "parallel")),
        name="out_proj",
    )(h, mod, o_mla, hg_f, hg_b, p_hg, o_swa, ret_f, ret_b, p_ret,
      hg_norm_g.reshape(1, GROUP_W), ret_gn_g.reshape(1, GROUP_W), ret_gn_b.reshape(1, GROUP_W), ones_bd, w_out_p)


def _ffn_kernel(h_ref, m_ref, g_ref, wg_ref, wu_ref, wd_ref, o_ref, xn, *, tm, n_ctx):
    j = pl.program_id(2)
    row0 = pl.program_id(1) * tm

    @pl.when(j == 0)
    def _():
        xn[...] = _norm_mod(h_ref[0], g_ref[...], _row_mod(m_ref, 3, row0, tm, n_ctx),
                            _row_mod(m_ref, 4, row0, tm, n_ctx)).astype(BF16)
        o_ref[...] = jnp.zeros_like(o_ref)

    wg, wu, wd = wg_ref[0], wu_ref[0], wd_ref[0]
    half = tm // 2
    for r0 in (0, half):
        x = xn[r0:r0 + half]
        hg = jnp.dot(x, wg, preferred_element_type=F32)
        hu = jnp.dot(x, wu, preferred_element_type=F32)
        o_ref[0, r0:r0 + half] += jnp.dot((_silu(hg) * hu).astype(BF16), wd, preferred_element_type=F32)

    @pl.when(j == pl.num_programs(2) - 1)
    def _():
        o_ref[0] = h_ref[0] + _row_mod(m_ref, 5, row0, tm, n_ctx) * o_ref[0]


def _dense_ffn(h, mod, g, w_gate, w_up, w_down, li, n_ctx, tm=1152, tf=1408):
    B, S, D = h.shape
    dff = w_gate.shape[-1]
    return pl.pallas_call(
        functools.partial(_ffn_kernel, tm=tm, n_ctx=n_ctx),
        grid=(B, S // tm, dff // tf),
        in_specs=[pl.BlockSpec((1, tm, D), lambda b, i, j: (b, i, 0)),
                  pl.BlockSpec((1, 2, 8, D), lambda b, i, j: (b, 0, 0, 0)),
                  pl.BlockSpec((1, D), lambda b, i, j: (0, 0)),
                  pl.BlockSpec((1, D, tf), lambda b, i, j: (li, 0, j)),
                  pl.BlockSpec((1, D, tf), lambda b, i, j: (li, 0, j)),
                  pl.BlockSpec((1, tf, D), lambda b, i, j: (li, j, 0))],
        out_specs=pl.BlockSpec((1, tm, D), lambda b, i, j: (b, i, 0)),
        out_shape=jax.ShapeDtypeStruct((B, S, D), F32),
        scratch_shapes=[pltpu.VMEM((tm, D), BF16)],
        compiler_params=_cparams(("parallel", "parallel", "arbitrary")),
        name="dense_ffn",
    )(h, mod, g.reshape(1, D), w_gate, w_up, w_down)


MOE_TM = 1024
MOE_TF = 512
MOE_ROWS = (384, 256)
DMA_UNROLL = 8
SUBLANES = 8
ZERO_ROWS = 256


def _router_kernel(h_ref, m_ref, g_ref, wr_ref, tri_ref, xn_ref, route_ref, cnt_ref, counts, *, tm, n_ctx):
    b, i = pl.program_id(0), pl.program_id(1)
    row0 = i * tm

    @pl.when((b == 0) & (i == 0))
    def _():
        counts[...] = jnp.zeros_like(counts)

    a = _norm_mod(h_ref[0], g_ref[...], _row_mod(m_ref, 3, row0, tm, n_ctx), _row_mod(m_ref, 4, row0, tm, n_ctx))
    xn_ref[0] = a
    ah, al = _split2(a)
    wh, wl = _split2(wr_ref[...])
    logits = (jnp.dot(ah, wh, preferred_element_type=F32) + jnp.dot(ah, wl, preferred_element_type=F32)
              + jnp.dot(al, wh, preferred_element_type=F32))
    lane = lax.broadcasted_iota(jnp.int32, logits.shape, 1)
    neg = jnp.float32(-jnp.inf)
    lg = jnp.where(lane < N_EXPERTS, logits, neg)
    v1 = jnp.max(lg, axis=-1, keepdims=True)
    i1 = jnp.min(jnp.where(lg == v1, lane, LANE), axis=-1, keepdims=True)
    lg2 = jnp.where(lane == i1, neg, lg)
    v2 = jnp.max(lg2, axis=-1, keepdims=True)
    i2 = jnp.min(jnp.where(lg2 == v2, lane, LANE), axis=-1, keepdims=True)
    e = jnp.exp(v2 - v1)
    w1 = 1.0 / (1.0 + e)
    w2 = e / (1.0 + e)
    oh1, oh2 = lane == i1, lane == i2
    onehot = jnp.where(oh1 | oh2, 1.0, 0.0)
    earlier = jnp.dot(tri_ref[...], onehot.astype(BF16), preferred_element_type=F32)
    pos = earlier + counts[...]
    r1 = jnp.sum(jnp.where(oh1, pos, 0.0), axis=-1, keepdims=True)
    r2 = jnp.sum(jnp.where(oh2, pos, 0.0), axis=-1, keepdims=True)
    counts[...] = counts[...] + jnp.sum(onehot, axis=0, keepdims=True)
    cnt_ref[...] = jnp.broadcast_to(counts[...], cnt_ref.shape)
    route_ref[0] = jnp.where(lane == 0, i1.astype(F32),
                             jnp.where(lane == 1, i2.astype(F32),
                                       jnp.where(lane == 2, w1,
                                                 jnp.where(lane == 3, w2,
                                                           jnp.where(lane == 4, r1, jnp.where(lane == 5, r2, 0.0))))))


def _scatter_kernel(meta_ref, slot_ref, x_ref, xs_hbm, zeros_v, sem, *, tm, tile_rows, total_rows):
    b, i = pl.program_id(0), pl.program_id(1)

    def start(r, carry):
        for k in range(TOP_K):
            pltpu.make_async_copy(x_ref.at[0, pl.ds(r, 1)], xs_hbm.at[pl.ds(slot_ref[0, 0, 0, k * tm + r], 1)],
                                  sem).start()
        return carry

    lax.fori_loop(0, tm, start, 0, unroll=DMA_UNROLL)
    for k in range(TOP_K):
        pltpu.make_async_copy(x_ref.at[0], xs_hbm.at[pl.ds(0, tm)], sem).wait()

    @pl.when((b == pl.num_programs(0) - 1) & (i == pl.num_programs(1) - 1))
    def _():
        zeros_v[...] = jnp.zeros_like(zeros_v)
        z1 = zeros_v.at[pl.ds(0, 1)]
        z8 = zeros_v.at[pl.ds(0, SUBLANES)]

        def fill(lo, end):
            mid = jnp.minimum(((lo + SUBLANES - 1) // SUBLANES) * SUBLANES, end)
            big = mid + ((end - mid) // ZERO_ROWS) * ZERO_ROWS
            n_row, n_big, n_small = mid - lo, (big - mid) // ZERO_ROWS, (end - big) // SUBLANES

            def start_row(r, c):
                pltpu.make_async_copy(z1, xs_hbm.at[pl.ds(lo + r, 1)], sem).start()
                return c

            def wait_row(r, c):
                pltpu.make_async_copy(z1, xs_hbm.at[pl.ds(0, 1)], sem).wait()
                return c

            def start_big(q, c):
                dst = xs_hbm.at[pl.ds(pl.multiple_of(mid + q * ZERO_ROWS, SUBLANES), ZERO_ROWS)]
                pltpu.make_async_copy(zeros_v, dst, sem).start()
                return c

            def wait_big(q, c):
                pltpu.make_async_copy(zeros_v, xs_hbm.at[pl.ds(0, ZERO_ROWS)], sem).wait()
                return c

            def start_small(q, c):
                dst = xs_hbm.at[pl.ds(pl.multiple_of(big + q * SUBLANES, SUBLANES), SUBLANES)]
                pltpu.make_async_copy(z8, dst, sem).start()
                return c

            def wait_small(q, c):
                pltpu.make_async_copy(z8, xs_hbm.at[pl.ds(0, SUBLANES)], sem).wait()
                return c

            lax.fori_loop(0, n_row, start_row, 0)
            lax.fori_loop(0, n_big, start_big, 0)
            lax.fori_loop(0, n_small, start_small, 0)
            lax.fori_loop(0, n_row, wait_row, 0)
            lax.fori_loop(0, n_big, wait_big, 0)
            lax.fori_loop(0, n_small, wait_small, 0)

        for e in range(N_EXPERTS):
            c = meta_ref[e]
            g0 = meta_ref[N_EXPERTS + e]
            fill(g0 + c, g0 + ((c + tile_rows - 1) // tile_rows) * tile_rows)
        fill(meta_ref[2 * N_EXPERTS] * tile_rows, jnp.int32(total_rows))


def _expert_kernel(te_ref, nu_ref, x_ref, wg_ref, wu_ref, wd_ref, o_ref, xb):
    i, j = pl.program_id(0), pl.program_id(1)
    half = x_ref.shape[0] // 2

    @pl.when(j == 0)
    def _():
        xb[...] = x_ref[...].astype(BF16)
        o_ref[...] = jnp.zeros_like(o_ref)

    @pl.when(i < nu_ref[0])
    def _():
        wg, wu, wd = wg_ref[0, 0].astype(BF16), wu_ref[0, 0].astype(BF16), wd_ref[0, 0].astype(BF16)
        for r0 in (0, half):
            x = xb[r0:r0 + half]
            hg = jnp.dot(x, wg, preferred_element_type=F32)
            hu = jnp.dot(x, wu, preferred_element_type=F32)
            o_ref[r0:r0 + half] += jnp.dot((_silu(hg) * hu).astype(BF16), wd, preferred_element_type=F32)


def _combine_kernel(slot_ref, h_ref, m_ref, route_ref, fg_ref, y_hbm, o_ref, buf, sem, *, tm, n_ctx, final_norm):
    row0 = pl.program_id(1) * tm

    def start(r, carry):
        for k in range(TOP_K):
            pltpu.make_async_copy(y_hbm.at[pl.ds(slot_ref[0, 0, 0, k * tm + r], 1)], buf.at[k, pl.ds(r, 1)], sem).start()
        return carry

    lax.fori_loop(0, tm, start, 0, unroll=DMA_UNROLL)
    for k in range(TOP_K):
        pltpu.make_async_copy(y_hbm.at[pl.ds(0, tm)], buf.at[k], sem).wait()
    w1 = route_ref[0, :, 2:3]
    w2 = route_ref[0, :, 3:4]
    y = h_ref[0] + _row_mod(m_ref, 5, row0, tm, n_ctx) * (w1 * buf[0] + w2 * buf[1])
    if final_norm:
        y = y * lax.rsqrt(jnp.mean(y * y, axis=-1, keepdims=True) + NORM_EPS) * fg_ref[...]
    o_ref[0] = y


def _moe_ffn(h, mod, g, w_router, w_gate, w_up, w_down, li, n_ctx, final_g=None):
    B, S, D = h.shape
    T = B * S
    cr = next(r for r in MOE_ROWS if S % r == 0)
    tm = cr
    dffe = w_gate.shape[-1]
    te_rows = MOE_TM if T % MOE_TM == 0 else 512
    n_tiles = TOP_K * T // te_rows + N_EXPERTS
    P = n_tiles * te_rows
    wr = jnp.pad(w_router, ((0, 0), (0, LANE - N_EXPERTS)))
    tri = jnp.asarray(np.tril(np.ones((tm, tm), np.float32), -1), BF16)
    xn, route, cnt = pl.pallas_call(
        functools.partial(_router_kernel, tm=tm, n_ctx=n_ctx),
        grid=(B, S // tm),
        in_specs=[pl.BlockSpec((1, tm, D), lambda b, i: (b, i, 0)),
                  pl.BlockSpec((1, 2, 8, D), lambda b, i: (b, 0, 0, 0)),
                  pl.BlockSpec((1, D), lambda b, i: (0, 0)),
                  pl.BlockSpec((D, LANE), lambda b, i: (0, 0)),
                  pl.BlockSpec((tm, tm), lambda b, i: (0, 0))],
        out_specs=[pl.BlockSpec((1, tm, D), lambda b, i: (b, i, 0)),
                   pl.BlockSpec((1, tm, LANE), lambda b, i: (b, i, 0)),
                   pl.BlockSpec((SUBLANES, LANE), lambda b, i: (0, 0))],
        out_shape=[jax.ShapeDtypeStruct((B, S, D), F32), jax.ShapeDtypeStruct((B, S, LANE), F32),
                   jax.ShapeDtypeStruct((SUBLANES, LANE), F32)],
        scratch_shapes=[pltpu.VMEM((1, LANE), F32)],
        compiler_params=_cparams(("arbitrary", "arbitrary")),
        name="moe_router",
    )(h, mod, g.reshape(1, D), wr, tri)

    counts = cnt[0, :N_EXPERTS].astype(jnp.int32)
    ptiles = (counts + te_rows - 1) // te_rows
    tile_end = jnp.cumsum(ptiles)
    gstart = (tile_end - ptiles) * te_rows
    n_used = tile_end[-1:].astype(jnp.int32)
    tile_ids = jnp.arange(n_tiles, dtype=jnp.int32)
    tile_e = jnp.minimum(jnp.sum((tile_ids[:, None] >= tile_end[None, :]).astype(jnp.int32), axis=1),
                         N_EXPERTS - 1)
    meta = jnp.concatenate([counts, gstart, n_used])
    choice = route[..., 0:TOP_K].astype(jnp.int32)
    is_e = choice[..., None] == jnp.arange(N_EXPERTS, dtype=jnp.int32)
    slots = jnp.sum(jnp.where(is_e, gstart, 0), axis=-1) + route[..., 4:4 + TOP_K].astype(jnp.int32)
    slots = slots.reshape(B, S // cr, cr, TOP_K).transpose(0, 1, 3, 2)
    slots = slots.reshape(B, S // cr, 1, TOP_K * cr)

    xs = pl.pallas_call(
        functools.partial(_scatter_kernel, tm=cr, tile_rows=te_rows, total_rows=P),
        grid_spec=pltpu.PrefetchScalarGridSpec(
            num_scalar_prefetch=1,
            grid=(B, S // cr),
            in_specs=[pl.BlockSpec((1, 1, 1, TOP_K * cr), lambda b, i, mt: (b, i, 0, 0),
                                   memory_space=pltpu.SMEM),
                      pl.BlockSpec((1, cr, D), lambda b, i, mt: (b, i, 0))],
            out_specs=pl.BlockSpec(memory_space=pl.ANY),
            scratch_shapes=[pltpu.VMEM((ZERO_ROWS, D), F32), pltpu.SemaphoreType.DMA(())]),
        out_shape=jax.ShapeDtypeStruct((P, D), F32),
        compiler_params=_cparams(("arbitrary", "arbitrary")),
        name="moe_scatter",
    )(meta, slots, xn)

    nf = dffe // MOE_TF
    last = lambda i, nu: jnp.minimum(i, nu[0] - 1)
    jj = lambda i, j, nu: jnp.where(i < nu[0], j, nf - 1)
    ys = pl.pallas_call(
        _expert_kernel,
        grid_spec=pltpu.PrefetchScalarGridSpec(
            num_scalar_prefetch=2,
            grid=(n_tiles, nf),
            in_specs=[pl.BlockSpec((te_rows, D), lambda i, j, te, nu: (last(i, nu), 0)),
                      pl.BlockSpec((1, 1, D, MOE_TF), lambda i, j, te, nu: (li, te[last(i, nu)], 0, jj(i, j, nu))),
                      pl.BlockSpec((1, 1, D, MOE_TF), lambda i, j, te, nu: (li, te[last(i, nu)], 0, jj(i, j, nu))),
                      pl.BlockSpec((1, 1, MOE_TF, D), lambda i, j, te, nu: (li, te[last(i, nu)], jj(i, j, nu), 0))],
            out_specs=pl.BlockSpec((te_rows, D), lambda i, j, te, nu: (i, 0)),
            scratch_shapes=[pltpu.VMEM((te_rows, D), BF16)]),
        out_shape=jax.ShapeDtypeStruct((P, D), F32),
        compiler_params=_cparams(("arbitrary", "arbitrary")),
        name="moe_experts",
    )(tile_e, n_used, xs, w_gate, w_up, w_down)
    fg = jnp.ones((1, D), F32) if final_g is None else final_g.reshape(1, D).astype(F32)
    return pl.pallas_call(
        functools.partial(_combine_kernel, tm=cr, n_ctx=n_ctx, final_norm=final_g is not None),
        grid=(B, S // cr),
        in_specs=[pl.BlockSpec((1, 1, 1, TOP_K * cr), lambda b, i: (b, i, 0, 0), memory_space=pltpu.SMEM),
                  pl.BlockSpec((1, cr, D), lambda b, i: (b, i, 0)),
                  pl.BlockSpec((1, 2, 8, D), lambda b, i: (b, 0, 0, 0)),
                  pl.BlockSpec((1, cr, LANE), lambda b, i: (b, i, 0)),
                  pl.BlockSpec((1, D), lambda b, i: (0, 0)),
                  pl.BlockSpec(memory_space=pl.ANY)],
        out_specs=pl.BlockSpec((1, cr, D), lambda b, i: (b, i, 0)),
        out_shape=jax.ShapeDtypeStruct((B, S, D), F32),
        scratch_shapes=[pltpu.VMEM((TOP_K, cr, D), F32), pltpu.SemaphoreType.DMA(())],
        compiler_params=_cparams(("arbitrary", "arbitrary")),
        name="moe_combine",
    )(slots, h, mod, route, fg, ys)


def _final_norm_kernel(h_ref, g_ref, o_ref):
    x = h_ref[0]
    o_ref[0] = x * lax.rsqrt(jnp.mean(x * x, axis=-1, keepdims=True) + NORM_EPS) * g_ref[...]


def _final_norm(h, g, n_ctx, tm=256):
    B, S, D = h.shape
    N = S - n_ctx
    return pl.pallas_call(
        _final_norm_kernel,
        grid=(B, N // tm),
        in_specs=[pl.BlockSpec((1, tm, D), lambda b, i: (b, n_ctx // tm + i, 0)),
                  pl.BlockSpec((1, D), lambda b, i: (0, 0))],
        out_specs=pl.BlockSpec((1, tm, D), lambda b, i: (b, i, 0)),
        out_shape=jax.ShapeDtypeStruct((B, N, D), F32),
        compiler_params=_cparams(("parallel", "parallel")),
        name="final_norm",
    )(h, g.reshape(1, D))


MOD_ROWS = 16


def kernel(x, c, ctx, c_ctx, w_ada, b_ada, norm_mix_g, norm_ffn_g, w_in, w_out, mla_q_norm_g, mla_w_uq, mla_kv_norm_g, mla_w_ukv, hg_lb_logits, hg_norm_g, swa_sink, ret_decay_logit, ret_gn_g, ret_gn_b, ffn_w_gate, ffn_w_up, ffn_w_down, moe_w_router, moe_w_gate, moe_w_up, moe_w_down, final_norm_g):
    B, N, D = x.shape
    L = ctx.shape[1]
    S = L + N
    assert B + 1 <= MOD_ROWS
    h = jnp.concatenate([ctx, x], axis=1)
    cvec = jnp.zeros((MOD_ROWS, D), F32).at[0].set(c_ctx).at[1:1 + B].set(c)
    mod_all = _modulation(cvec, w_ada, b_ada).reshape(DEPTH, MOD_ROWS, 6, D)

    lb_p = jax.nn.softmax(hg_lb_logits.astype(F32), axis=0)
    hg_lb = jnp.cumsum(lb_p, axis=0) - lb_p[0:1]
    ret_log_gamma = jax.nn.log_sigmoid(ret_decay_logit.astype(F32))

    mla_tab = _mla_tables(N, L)
    swa_tab = _swa_tables(N, L)
    w_out_perm = np.concatenate([np.arange(512), 512 + _SWA_OUT_PERM, np.arange(768, 1024)])
    ffn_bf16 = tuple(w.astype(BF16) for w in (ffn_w_gate, ffn_w_up, ffn_w_down))

    for l in range(DEPTH):
        m = mod_all[l]
        mod = jnp.stack([jnp.broadcast_to(m[0], (B, 6, D)), m[1:1 + B]], axis=1)
        mod = jnp.pad(mod, ((0, 0), (0, 0), (0, 2), (0, 0)))
        w_ext = _take_cols(w_in[l], _EXT_COLS).astype(BF16)
        mla_w = _mla_weights(mla_q_norm_g[l], mla_w_uq[l], mla_kv_norm_g[l], mla_w_ukv[l])
        mla_qkv, p_hg, swa_qkv, p_ret = _in_proj(h, mod, norm_mix_g[l], w_ext, mla_w, mla_tab, swa_tab, L)
        o_mla = _mla_mixer(mla_qkv, L)
        hg_f, hg_b = _hg_mixer(p_hg, hg_lb[l], L)
        o_swa = _swa_mixer(swa_qkv, swa_sink[l], L)
        ret_f, ret_b = _ret_mixer(p_ret, _ret_tables(ret_log_gamma[l], L, S), L)
        w_out_p = _take_static(w_out[l], w_out_perm, 0).astype(BF16)
        last = l == DEPTH - 1
        if last and l % 2 == 1:
            h = _out_proj(h, mod, o_mla, hg_f, hg_b, p_hg, o_swa, ret_f, ret_b, p_ret,
                          hg_norm_g[l], ret_gn_g[l], ret_gn_b[l], w_out_p, L, tm=math.gcd(L, N, 256), row_off=L)
            return _moe_ffn(h, mod, norm_ffn_g[l], moe_w_router[l // 2], moe_w_gate, moe_w_up, moe_w_down, l // 2,
                            0, final_g=final_norm_g)
        h = _out_proj(h, mod, o_mla, hg_f, hg_b, p_hg, o_swa, ret_f, ret_b, p_ret,
                      hg_norm_g[l], ret_gn_g[l], ret_gn_b[l], w_out_p, L)
        if l % 2 == 0:
            h = _dense_ffn(h, mod, norm_ffn_g[l], ffn_bf16[0], ffn_bf16[1], ffn_bf16[2], l // 2, L,
                           tm=1152 if S % 1152 == 0 else 384)
        else:
            h = _moe_ffn(h, mod, norm_ffn_g[l], moe_w_router[l // 2], moe_w_gate, moe_w_up, moe_w_down, l // 2, L)
    return _final_norm(h, final_norm_g, L)
```

```python
import functools
import math

import numpy as np
import jax
import jax.numpy as jnp
from jax import lax
from jax.experimental import pallas as pl
from jax.experimental.pallas import tpu as pltpu

F32 = jnp.float32
BF16 = jnp.bfloat16

D_MODEL = 1024
DEPTH = 4
GRID_W = 64
ROPE_THETA = 10000.0
NORM_EPS = 1e-6
LB_FLOOR = 1e-30
GROUP_W = 256

MLA_HEADS, MLA_NOPE, MLA_ROPE, MLA_V = 4, 64, 32, 64
MLA_Q_RANK, MLA_KV_RANK = 192, 128
HG_HEADS, HG_DK, HG_DV = 4, 64, 64
SWA_HEADS, SWA_KV_HEADS, SWA_HD, SWA_WINDOW = 4, 2, 64, 128
RET_HEADS, RET_DK, RET_DV = 4, 32, 64
N_EXPERTS, TOP_K = 8, 2

_IN_SIZES = (192, 128, 32, 256, 256, 256, 256, 256, 256, 128, 128, 128, 128, 256, 256)
_OFF = np.concatenate([[0], np.cumsum(_IN_SIZES)]).astype(np.int64)
D_IN = int(_OFF[-1])

W_MLA, W_HG, W_SWA, W_RET = 512, 1280, 512, 768
W_EXT = W_MLA + W_HG + W_SWA + W_RET

LANE = 128
VMEM_LIMIT = 56 * 1024 * 1024


def _cparams(sem):
    return pltpu.CompilerParams(dimension_semantics=sem, vmem_limit_bytes=VMEM_LIMIT)


def _swap_halves(idx, width):
    idx = np.asarray(idx)
    base = (idx // width) * width
    j = idx % width
    return base + (j + width // 2) % width


def _ext_columns():
    z = lambda n: -np.ones((n,), np.int64)
    rng = lambda a, n: np.arange(a, a + n)
    o = _OFF
    k_pe = rng(o[2], 32)
    mla = np.concatenate([rng(o[0], 192), z(64), rng(o[1], 128), k_pe,
                          o[2] + _swap_halves(np.arange(32), 32), z(64)])
    hg = rng(o[3], 1280)
    q_heads = [rng(o[8] + 64 * h, 64) for h in range(4)]
    q_perm = np.concatenate([q_heads[0], q_heads[2], q_heads[1], q_heads[3]])
    swa = np.concatenate([q_perm, rng(o[9], 128), rng(o[10], 128)])
    ret = rng(o[11], 768)
    cols = np.concatenate([mla, hg, swa, ret])
    assert cols.shape[0] == W_EXT
    return cols


_EXT_COLS = _ext_columns()


def _take_static(w, idx, axis):
    idx = np.asarray(idx)
    axis = axis % w.ndim
    pieces, start = [], 0
    for i in range(1, len(idx) + 1):
        run_ends = i == len(idx) or (idx[i] != idx[i - 1] + 1 if idx[i - 1] >= 0 else idx[i] >= 0)
        if run_ends:
            n = i - start
            if idx[start] < 0:
                shape = list(w.shape)
                shape[axis] = n
                pieces.append(jnp.zeros(shape, w.dtype))
            else:
                pieces.append(lax.slice_in_dim(w, int(idx[start]), int(idx[start]) + n, axis=axis))
            start = i
    return jnp.concatenate(pieces, axis=axis)


def _take_cols(w, cols):
    return _take_static(w, cols, -1)


def _axial_angles(n_tok, rot_dim):
    rows = n_tok // GRID_W
    row = jnp.broadcast_to(jnp.arange(rows)[:, None], (rows, GRID_W)).reshape(-1)
    col = jnp.broadcast_to(jnp.arange(GRID_W)[None, :], (rows, GRID_W)).reshape(-1)
    n_freq = rot_dim // 4
    inv = ROPE_THETA ** (-jnp.arange(n_freq, dtype=F32) / n_freq)
    return jnp.concatenate([row.astype(F32)[:, None] * inv, col.astype(F32)[:, None] * inv], axis=-1)


def _rope_tables(ang, n_ctx):
    cos, sin = jnp.cos(ang), jnp.sin(ang)
    c = jnp.concatenate([cos, cos], axis=-1)
    s = jnp.concatenate([-sin, sin], axis=-1)
    r = c.shape[-1]
    c = jnp.concatenate([jnp.ones((n_ctx, r), F32), c], axis=0)
    s = jnp.concatenate([jnp.zeros((n_ctx, r), F32), s], axis=0)
    return c, s


def _split2(x):
    hi = x.astype(BF16)
    lo = (x - hi.astype(F32)).astype(BF16)
    return hi, lo


def _seg_mean(x, ones_bd):
    hi, lo = _split2(x)
    return (jnp.dot(hi, ones_bd, preferred_element_type=F32)
            + jnp.dot(lo, ones_bd, preferred_element_type=F32))


def _silu(x):
    return x / (1.0 + jnp.exp(-x))


def _rot_half(x, head):
    half = head // 2
    lane = lax.broadcasted_iota(jnp.int32, x.shape, 1) % head
    return jnp.where(lane < half, pltpu.roll(x, LANE - half, axis=1), pltpu.roll(x, half, axis=1))


def _row_mod(m_ref, idx, row0, tm, n_ctx):
    rows = row0 + lax.broadcasted_iota(jnp.int32, (tm, 1), 0)
    mc = m_ref[0, 0, idx:idx + 1, :]
    ml = m_ref[0, 1, idx:idx + 1, :]
    return jnp.where(rows < n_ctx, mc, ml)


def _norm_mod(x, g, shift, scale):
    ms = jnp.mean(x * x, axis=-1, keepdims=True)
    y = x * lax.rsqrt(ms + NORM_EPS) * g
    return y * (1.0 + scale) + shift


def _mod_kernel(c_ref, w_ref, b_ref, o_ref):
    s = _silu(c_ref[...])
    o_ref[0] = jnp.dot(s.astype(BF16), w_ref[0].astype(BF16), preferred_element_type=F32) + b_ref[0]


def _modulation(cvec, w_ada, b_ada):
    R, D = cvec.shape
    depth, _, n6 = w_ada.shape
    tn = 1536
    return pl.pallas_call(
        _mod_kernel,
        grid=(depth, n6 // tn),
        in_specs=[pl.BlockSpec((R, D), lambda l, j: (0, 0)),
                  pl.BlockSpec((1, D, tn), lambda l, j: (l, 0, j)),
                  pl.BlockSpec((1, 1, tn), lambda l, j: (l, 0, j))],
        out_specs=pl.BlockSpec((1, R, tn), lambda l, j: (l, 0, j)),
        out_shape=jax.ShapeDtypeStruct((depth, R, n6), F32),
        compiler_params=_cparams(("parallel", "parallel")),
        name="adaln_mod",
    )(cvec, w_ada, b_ada.reshape(depth, 1, n6))


def _win_kernel(h_ref, m_ref, g_ref, w_ref, cq_ref, sq_ref, tk_ref, gq_ref, wq_ref, wqs_ref, gk_ref, wk_ref,
                wv_ref, e_ref, sc_ref, ss_ref, mq_out, mk_out, mv_out, o_hg, sq_out, skv_out, o_ret,
                *, tm, n_ctx):
    row0 = pl.program_id(1) * tm
    a = _norm_mod(h_ref[0], g_ref[...], _row_mod(m_ref, 0, row0, tm, n_ctx),
                  _row_mod(m_ref, 1, row0, tm, n_ctx)).astype(BF16)
    group = lambda c0, w: jnp.dot(a, w_ref[:, c0:c0 + w], preferred_element_type=F32)
    mq_out[0], mk_out[0], mv_out[0] = _mla_prep(group(0, W_MLA), cq_ref, sq_ref, tk_ref, gq_ref, wq_ref, wqs_ref,
                                                gk_ref, wk_ref, wv_ref, e_ref)
    o_hg[0] = group(W_MLA, W_HG)
    sq_out[0], skv_out[0] = _swa_prep(group(W_MLA + W_HG, W_SWA), sc_ref, ss_ref)
    o_ret[0] = group(W_MLA + W_HG + W_SWA, W_RET)


def _in_proj(h, mod, g, w_ext, mla_weights, mla_tables, swa_tables, n_ctx, tm=768):
    B, S, D = h.shape
    full = lambda a: pl.BlockSpec(a.shape, lambda b, i: (0,) * a.ndim)
    tab = pl.BlockSpec((tm, LANE), lambda b, i: (i, 0))
    out_w = ((512, BF16), (512, BF16), (512, BF16), (W_HG, F32), (256, BF16), (256, BF16), (W_RET, F32))
    outs = pl.pallas_call(
        functools.partial(_win_kernel, tm=tm, n_ctx=n_ctx),
        grid=(B, S // tm),
        in_specs=[pl.BlockSpec((1, tm, D), lambda b, i: (b, i, 0)),
                  pl.BlockSpec((1, 2, 8, D), lambda b, i: (b, 0, 0, 0)),
                  pl.BlockSpec((1, D), lambda b, i: (0, 0)),
                  pl.BlockSpec((D, W_EXT), lambda b, i: (0, 0)),
                  tab, tab, tab] + [full(w) for w in mla_weights] + [tab, tab],
        out_specs=[pl.BlockSpec((1, tm, w), lambda b, i: (b, i, 0)) for w, _ in out_w],
        out_shape=[jax.ShapeDtypeStruct((B, S, w), dt) for w, dt in out_w],
        compiler_params=_cparams(("parallel", "parallel")),
        name="in_proj",
    )(h, mod, g.reshape(1, D), w_ext, *mla_tables, *mla_weights, *swa_tables)
    return outs[0:3], outs[3], outs[4:6], outs[6]


def _mla_weights(q_norm_g, w_uq, kv_norm_g, w_ukv):
    qcols, qsw, kcols = [], [], []
    for h in range(MLA_HEADS):
        b = 96 * h
        nope = np.arange(b, b + 64)
        pe = np.arange(b + 64, b + 96)
        pad = -np.ones((32,), np.int64)
        qcols.append(np.concatenate([nope, pe, pad]))
        qsw.append(np.concatenate([-np.ones((64,), np.int64), (b + 64) + _swap_halves(np.arange(32), 32), pad]))
        kcols.append(np.concatenate([np.arange(128 * h, 128 * h + 64), -np.ones((64,), np.int64)]))
    vcols = np.concatenate([np.concatenate([np.arange(128 * h + 64, 128 * h + 128), -np.ones((64,), np.int64)])
                            for h in range(MLA_HEADS)])
    pad_rows = ((0, 256 - MLA_Q_RANK), (0, 0))
    wq = jnp.pad(_take_cols(w_uq, np.concatenate(qcols)), pad_rows).astype(BF16)
    wq_sw = jnp.pad(_take_cols(w_uq, np.concatenate(qsw)), pad_rows).astype(BF16)
    wk = _take_cols(w_ukv, np.concatenate(kcols)).astype(BF16)
    wv = _take_cols(w_ukv, vcols).astype(BF16)
    e = np.zeros((128, 512), np.float32)
    for h in range(MLA_HEADS):
        for j in range(32):
            e[j, 128 * h + 64 + j] = 1.0
            e[32 + j, 128 * h + 64 + j] = 1.0
    gq = jnp.pad(q_norm_g, (0, 256 - MLA_Q_RANK)).reshape(1, 256)
    return gq, wq, wq_sw, kv_norm_g.reshape(1, 128), wk, wv, jnp.asarray(e, BF16)


def _mla_tables(n_lat, n_ctx):
    ang = _axial_angles(n_lat, MLA_ROPE)
    c32, s32 = _rope_tables(ang, n_ctx)
    S = n_lat + n_ctx
    one, zero = jnp.ones((S, 64), F32), jnp.zeros((S, 64), F32)
    cq = jnp.concatenate([one, c32, jnp.zeros((S, 32), F32)], axis=-1)
    sq = jnp.concatenate([zero, s32, jnp.zeros((S, 32), F32)], axis=-1)
    tk = jnp.concatenate([c32, s32, zero], axis=-1)
    return cq, sq, tk


def _mla_prep(p, cq_ref, sq_ref, tk_ref, gq_ref, wq_ref, wqs_ref, gk_ref, wk_ref, wv_ref, e_ref):
    scale = (MLA_NOPE + MLA_ROPE) ** -0.5 * float(np.log2(np.e))
    ql = p[:, 0:256]
    msq = jnp.sum(ql * ql, axis=-1, keepdims=True) * (1.0 / MLA_Q_RANK)
    qn = (ql * lax.rsqrt(msq + NORM_EPS) * gq_ref[...]).astype(BF16)
    kvl = p[:, 256:384]
    msk = jnp.mean(kvl * kvl, axis=-1, keepdims=True)
    kvn = (kvl * lax.rsqrt(msk + NORM_EPS) * gk_ref[...]).astype(BF16)
    q = jnp.dot(qn, wq_ref[...], preferred_element_type=F32)
    qs = jnp.dot(qn, wqs_ref[...], preferred_element_type=F32)
    cq = jnp.concatenate([cq_ref[...]] * MLA_HEADS, axis=-1)
    sq = jnp.concatenate([sq_ref[...]] * MLA_HEADS, axis=-1)
    q_out = ((q * cq + qs * sq) * scale).astype(BF16)
    pe = (p[:, 384:512] * tk_ref[...]).astype(BF16)
    k = (jnp.dot(kvn, wk_ref[...], preferred_element_type=F32)
         + jnp.dot(pe, e_ref[...], preferred_element_type=F32))
    lane = lax.broadcasted_iota(jnp.int32, (1, MLA_HEADS * LANE), 1) % LANE
    ones = jnp.where(lane >= MLA_V, 1.0, 0.0)
    v = jnp.dot(kvn, wv_ref[...], preferred_element_type=F32) + ones
    return q_out, k.astype(BF16), v.astype(BF16)


def _mla_attn_kernel(q_ref, k_ref, v_ref, o_ref, *, n_ctx, tq):
    S = k_ref.shape[1]

    lo = lax.broadcasted_iota(jnp.int32, (tq, LANE), 1) < MLA_V

    def attend(nk):
        pv = []
        for h in range(MLA_HEADS):
            q = q_ref[0, :, LANE * h:LANE * (h + 1)]
            k = k_ref[0, 0:nk, LANE * h:LANE * (h + 1)]
            s = lax.dot_general(q, k, (((1,), (1,)), ((), ())), preferred_element_type=F32)
            p = jnp.exp2(s - jnp.max(s, axis=-1, keepdims=True))
            pv.append(jnp.dot(p.astype(BF16), v_ref[0, 0:nk, LANE * h:LANE * (h + 1)], preferred_element_type=F32))
        for pair in range(MLA_HEADS // 2):
            a, b = pv[2 * pair], pv[2 * pair + 1]
            o_ref[0, :, LANE * pair:LANE * (pair + 1)] = jnp.where(
                lo, a / pltpu.roll(a, MLA_V, axis=1), pltpu.roll(b, MLA_V, axis=1) / b).astype(o_ref.dtype)

    is_ctx = (pl.program_id(1) + 1) * tq <= n_ctx
    pl.when(is_ctx)(lambda: attend(n_ctx))
    pl.when(jnp.logical_not(is_ctx))(lambda: attend(S))


def _mla_mixer(qkv, n_ctx, tq=256):
    q, k, v = qkv
    B, S, _ = q.shape
    assert n_ctx % tq == 0 and S % tq == 0
    return pl.pallas_call(
        functools.partial(_mla_attn_kernel, n_ctx=n_ctx, tq=tq),
        grid=(B, S // tq),
        in_specs=[pl.BlockSpec((1, tq, 512), lambda b, i: (b, i, 0)),
                  pl.BlockSpec((1, S, 512), lambda b, i: (b, 0, 0)),
                  pl.BlockSpec((1, S, 512), lambda b, i: (b, 0, 0))],
        out_specs=pl.BlockSpec((1, tq, GROUP_W), lambda b, i: (b, i, 0)),
        out_shape=jax.ShapeDtypeStruct((B, S, GROUP_W), BF16),
        compiler_params=_cparams(("parallel", "arbitrary")),
        name="mla_attn",
    )(q, k, v)


def _swa_tables(n_lat, n_ctx):
    ang = _axial_angles(n_lat, SWA_HD)
    c64, s64 = _rope_tables(ang, n_ctx)
    return jnp.concatenate([c64, c64], axis=-1), jnp.concatenate([s64, s64], axis=-1)


def _swa_prep(p, c_ref, s_ref):
    scale = SWA_HD ** -0.5
    c, s = c_ref[...], s_ref[...]
    rope = lambda x: x * c + _rot_half(x, SWA_HD) * s
    q = jnp.concatenate([rope(p[:, 0:LANE]), rope(p[:, LANE:2 * LANE])], axis=-1) * scale
    return q.astype(BF16), jnp.concatenate([rope(p[:, 256:384]), p[:, 384:512]], axis=-1).astype(BF16)


def _swa_attn_kernel(sink_ref, q_ref, kvc_ref, kvp_ref, kvn_ref, kvx_ref, bias_ref, o_ref, *, n_ctx, blk):
    j = pl.program_id(1)
    ncb = n_ctx // blk
    lane = lax.broadcasted_iota(jnp.int32, (blk, LANE), 1)
    lo = lane < SWA_HD
    qa, qb = q_ref[0, :, 0:LANE], q_ref[0, :, LANE:2 * LANE]
    zero = jnp.zeros((), BF16)
    qs = jnp.concatenate([jnp.where(lo, qa, zero), jnp.where(lo, qb, zero),
                          jnp.where(lo, zero, qa), jnp.where(lo, zero, qb)], axis=0)
    rb = lax.broadcasted_iota(jnp.int32, (4 * blk, 1), 0) // blk
    sink = jnp.where(rb == 0, sink_ref[0], jnp.where(rb == 1, sink_ref[1],
                                                     jnp.where(rb == 2, sink_ref[2], sink_ref[3])))

    def finish(s, vcat):
        m = jnp.maximum(jnp.max(s, axis=-1, keepdims=True), sink)
        p = jnp.exp(s - m)
        l = jnp.sum(p, axis=-1, keepdims=True) + jnp.exp(sink - m)
        r = jnp.dot(p.astype(BF16), vcat, preferred_element_type=F32) / l
        o_ref[0, :, 0:LANE] = jnp.where(lo, r[0:blk], r[2 * blk:3 * blk]).astype(o_ref.dtype)
        o_ref[0, :, LANE:2 * LANE] = jnp.where(lo, r[blk:2 * blk], r[3 * blk:4 * blk]).astype(o_ref.dtype)

    def ctx_block():
        s = lax.dot_general(qs, kvc_ref[0, :, 0:LANE], (((1,), (1,)), ((), ())), preferred_element_type=F32)
        finish(s, kvc_ref[0, :, LANE:2 * LANE])

    def lat_block():
        refs = (kvc_ref, kvp_ref, kvn_ref, kvx_ref)
        kcat = jnp.concatenate([r[0, :, 0:LANE] for r in refs], axis=0)
        vcat = jnp.concatenate([r[0, :, LANE:2 * LANE] for r in refs], axis=0)
        s = lax.dot_general(qs, kcat, (((1,), (1,)), ((), ())), preferred_element_type=F32)
        finish(s + jnp.concatenate([bias_ref[0]] * SWA_HEADS, axis=0), vcat)

    pl.when(j < ncb)(ctx_block)
    pl.when(j >= ncb)(lat_block)


def _swa_mixer(qkv, sink, n_ctx, blk=128):
    q, kv_all = qkv
    B, S, _ = q.shape
    nb = S // blk
    ncb = n_ctx // blk
    cur = lambda b, j: (b, j, 0)
    prev = lambda b, j: (b, jnp.maximum(j - 1, ncb), 0)
    nxt = lambda b, j: (b, jnp.minimum(j + 1, nb - 1), 0)
    kv = lambda f: pl.BlockSpec((1, blk, 2 * LANE), f)
    ctx = pl.BlockSpec((1, n_ctx, 2 * LANE), lambda b, j: (b, 0, 0))
    t = np.arange(blk)[:, None]
    c = np.arange(n_ctx + 3 * blk)[None, :] - n_ctx
    bias = np.stack([np.where((c < 0) | ((c >= 0) & (c < blk) & (c >= t) & bool(hp)) | ((c >= blk) & (c < 2 * blk))
                              | ((c >= 2 * blk) & (c - 2 * blk <= t) & bool(hn)), 0.0, -1e30)
                     for hp in (0, 1) for hn in (0, 1)]).astype(np.float32)
    kind = lambda b, j: (2 * (j > ncb).astype(jnp.int32) + (j < nb - 1).astype(jnp.int32), 0, 0)
    return pl.pallas_call(
        functools.partial(_swa_attn_kernel, n_ctx=n_ctx, blk=blk),
        grid=(B, nb),
        in_specs=[pl.BlockSpec(memory_space=pltpu.SMEM),
                  pl.BlockSpec((1, blk, 256), cur), ctx, kv(prev), kv(cur), kv(nxt),
                  pl.BlockSpec((1, blk, n_ctx + 3 * blk), kind)],
        out_specs=pl.BlockSpec((1, blk, GROUP_W), cur),
        out_shape=jax.ShapeDtypeStruct((B, S, GROUP_W), BF16),
        compiler_params=_cparams(("parallel", "arbitrary")),
        name="swa_attn",
    )(sink.astype(F32), q, kv_all, kv_all, kv_all, kv_all, jnp.asarray(bias))


def _bwd_chunk(j, ncc, nc):
    return jnp.where(j < ncc, ncc - 1 - j, nc - 1 - (j - ncc))


def _head_stack(x, width, heads):
    lane_h = lax.broadcasted_iota(jnp.int32, x.shape, 1) // width
    zero = jnp.zeros((), x.dtype)
    return jnp.concatenate([jnp.where(lane_h == h, x, zero) for h in range(heads)], axis=0)


def _head_unstack(y, rows, width, heads):
    lane_h = lax.broadcasted_iota(jnp.int32, (rows, heads * width), 1) // width
    out = y[0:rows]
    for h in range(1, heads):
        out = jnp.where(lane_h == h, y[h * rows:(h + 1) * rows], out)
    return out


RET_CHUNK = 256


def _ret_tables(log_gamma, n_ctx, S):
    C = RET_CHUNK
    t = jnp.arange(C, dtype=F32)
    rel = t[:, None] - t[None, :]
    lg = log_gamma.astype(F32)
    lg_l = jnp.repeat(lg, RET_DK, axis=1)
    dm, xi, zeta, dcay = [], [], [], []
    for d in range(2):
        r = rel if d == 0 else -rel
        dh = jnp.where(r >= 0, jnp.exp(jnp.maximum(r, 0.0)[None] * lg[d][:, None, None]), 0.0)
        dm.append(dh.reshape(RET_HEADS * C, C))
        tq = (t + 1.0) if d == 0 else (C - t)
        tk = (C - 1.0 - t) if d == 0 else t
        xi.append(jnp.exp(tq[:, None] * lg_l[d][None, :]))
        zeta.append(jnp.exp(tk[:, None] * lg_l[d][None, :]))
        dcay.append(jnp.broadcast_to(jnp.exp(C * lg_l[d])[:, None], (RET_HEADS * RET_DK, GROUP_W)))
    mask = (np.arange(128)[:, None] // RET_DK == np.arange(256)[None, :] // RET_DV).astype(np.float32)
    pos = jnp.arange(S, dtype=F32)
    inv = ROPE_THETA ** (-jnp.arange(RET_DK // 2, dtype=F32) / (RET_DK // 2))
    ang = pos[:, None] * inv
    cos, sin = jnp.cos(ang), jnp.sin(ang)
    c = jnp.tile(jnp.concatenate([cos, cos], axis=-1), (1, RET_HEADS))
    s = jnp.tile(jnp.concatenate([-sin, sin], axis=-1), (1, RET_HEADS))
    return (jnp.stack(dm), jnp.stack(xi), jnp.stack(zeta), jnp.stack(dcay), jnp.asarray(mask), c, s)


def _ret_kernel(qkf_ref, vf_ref, cf_ref, sf_ref, qkb_ref, vb_ref, cb_ref, sb_ref,
                dm_ref, xi_ref, zeta_ref, dcay_ref, mask_ref, of_ref, ob_ref, state):
    C = RET_CHUNK

    @pl.when(pl.program_id(1) == 0)
    def _():
        state[...] = jnp.zeros_like(state)

    dirs = ((qkf_ref, vf_ref, cf_ref, sf_ref, of_ref), (qkb_ref, vb_ref, cb_ref, sb_ref, ob_ref))
    for d, (qk_ref, v_ref, c_ref, s_ref, o_ref) in enumerate(dirs):
        c, s = c_ref[...], s_ref[...]
        q, k = qk_ref[0, :, 0:128], qk_ref[0, :, 128:256]
        qr = (q * c + _rot_half(q, RET_DK) * s) * (RET_DK ** -0.5)
        kr = k * c + _rot_half(k, RET_DK) * s
        vb = v_ref[0].astype(BF16)
        qs = _head_stack(qr.astype(BF16), RET_DK, RET_HEADS)
        sc = lax.dot_general(qs, kr.astype(BF16), (((1,), (1,)), ((), ())), preferred_element_type=F32)
        a = (sc * dm_ref[d]).astype(BF16)
        o = _head_unstack(jnp.dot(a, vb, preferred_element_type=F32), C, RET_DV, RET_HEADS)
        st = state[d]
        o = o + jnp.dot((qr * xi_ref[d]).astype(BF16), st.astype(BF16), preferred_element_type=F32)
        o_ref[0] = o.astype(o_ref.dtype)
        kz = (kr * zeta_ref[d]).astype(BF16)
        u = lax.dot_general(kz, vb, (((0,), (0,)), ((), ())), preferred_element_type=F32)
        state[d] = st * dcay_ref[d] + u * mask_ref[...]


def _ret_mixer(p_ret, tables, n_ctx):
    B, S, _ = p_ret.shape
    C = RET_CHUNK
    dm, xi, zeta, dcay, mask, c, s = tables
    nc, ncc = S // C, n_ctx // C
    fwd = lambda b, j: (b, j, 0)
    bwd = lambda b, j: (b, _bwd_chunk(j, ncc, nc), 0)
    fwd_v = lambda b, j: (b, j, 1)
    bwd_v = lambda b, j: (b, _bwd_chunk(j, ncc, nc), 1)
    fwd_t = lambda b, j: (j, 0)
    bwd_t = lambda b, j: (_bwd_chunk(j, ncc, nc), 0)
    full = lambda a: pl.BlockSpec(a.shape, lambda b, j: (0,) * a.ndim)
    return pl.pallas_call(
        _ret_kernel,
        grid=(B, nc),
        in_specs=[pl.BlockSpec((1, C, 256), fwd), pl.BlockSpec((1, C, 256), fwd_v),
                  pl.BlockSpec((C, LANE), fwd_t), pl.BlockSpec((C, LANE), fwd_t),
                  pl.BlockSpec((1, C, 256), bwd), pl.BlockSpec((1, C, 256), bwd_v),
                  pl.BlockSpec((C, LANE), bwd_t), pl.BlockSpec((C, LANE), bwd_t),
                  full(dm), full(xi), full(zeta), full(dcay), full(mask)],
        out_specs=[pl.BlockSpec((1, C, GROUP_W), fwd), pl.BlockSpec((1, C, GROUP_W), bwd)],
        out_shape=[jax.ShapeDtypeStruct((B, S, GROUP_W), BF16)] * 2,
        scratch_shapes=[pltpu.VMEM((2, RET_HEADS * RET_DK, GROUP_W), F32)],
        compiler_params=_cparams(("parallel", "arbitrary")),
        name="retention_scan",
    )(p_ret, p_ret, c, s, p_ret, p_ret, c, s, dm, xi, zeta, dcay, mask)


HG_CHUNK = 128
HG_SAMPLES = 2


def _hg_level_tables(C):
    levels = []
    n = 2
    while n <= C:
        levels.append(n)
        n *= 2
    t = np.arange(C)
    mats, qroles, kroles = [], [], []
    for d in range(2):
        rows = []
        qr_d, kr_d = [], []
        for n in levels:
            m = n // 2
            start = (t // n) * n
            mid = start + m
            u = t - start
            M = np.zeros((C, C), np.float32)
            for i in range(C):
                if d == 0:
                    if u[i] >= m:
                        M[i, mid[i]:i + 1] = 1.0
                    else:
                        M[i, i + 1:mid[i]] = 1.0
                else:
                    if u[i] < m:
                        M[i, i:mid[i]] = 1.0
                    else:
                        M[i, mid[i]:i] = 1.0
            rows.append(M)
            qr_d.append((u >= m) if d == 0 else (u < m))
            kr_d.append((u < m) if d == 0 else (u >= m))
        Mq = np.zeros((C, C), np.float32)
        Mk = np.zeros((C, C), np.float32)
        for i in range(C):
            if d == 0:
                Mq[i, 0:i + 1] = 1.0
                Mk[i, i + 1:C] = 1.0
            else:
                Mq[i, i:C] = 1.0
                Mk[i, 0:i] = 1.0
        rows += [Mq, Mk, np.ones((8, C), np.float32)]
        mats.append(np.concatenate(rows, axis=0))
        qroles.append(np.stack(qr_d))
        kroles.append(np.stack(kr_d))
    same_block = np.stack([(t[:, None] // n == t[None, :] // n) for n in levels]).astype(np.float32)
    return levels, np.stack(mats), np.stack(qroles), np.stack(kroles), same_block


def _hg_kernel(qf_ref, zf_ref, if_ref, qb_ref, zb_ref, ib_ref, lb_ref, m_ref, role_ref, blk_ref,
               ones_ref, mask_ref, of_ref, ob_ref, state, *, n_levels):
    C = HG_CHUNK

    @pl.when(pl.program_id(1) == 0)
    def _():
        state[...] = jnp.zeros_like(state)

    dirs = ((qf_ref, zf_ref, if_ref), (qb_ref, zb_ref, ib_ref))
    chains = [(s, d) for s in range(HG_SAMPLES) for d in range(2)]
    st = [state[s, d] for s, d in chains]
    q, v, vb, kk, r = [], [], [], [], []
    for c, (s, d) in enumerate(chains):
        q_ref, z_ref, i_ref = dirs[d]
        z = z_ref[s]
        lb = lb_ref[d]
        lbf = jnp.maximum(lb, LB_FLOOR)
        t = jnp.exp(-jnp.abs(z))
        rcp = 1.0 / (1.0 + t)
        pos = z >= 0.0
        sig = jnp.where(pos, rcp, t * rcp)
        nsig = jnp.where(pos, t * rcp, rcp)
        logf = jnp.log(lbf + (1.0 - lb) * sig)
        kk.append((1.0 - lb) * nsig - (lbf - lb))
        hi, lo = _split2(logf)
        r.append(jnp.dot(m_ref[d], hi, preferred_element_type=F32)
                 + jnp.dot(m_ref[d], lo, preferred_element_type=F32))
        q.append(q_ref[s] * (HG_DK ** -0.5))
        v.append(i_ref[s])
        vb.append(v[c].astype(BF16))
    acc = [jnp.zeros((HG_HEADS * C, C), F32) for _ in chains]
    for lv in range(n_levels):
        for c, (s, d) in enumerate(chains):
            e = jnp.exp(r[c][lv * C:(lv + 1) * C])
            roles = role_ref[d, lv]
            qe = jnp.where(roles > 0.5, q[c] * e, 0.0).astype(BF16)
            ke = jnp.where(roles > 0.5, 0.0, kk[c] * e).astype(BF16)
            a = lax.dot_general(_head_stack(qe, HG_DK, HG_HEADS), ke, (((1,), (1,)), ((), ())),
                                preferred_element_type=F32)
            acc[c] = acc[c] + a * blk_ref[lv]
    outs = []
    for c in range(len(chains)):
        o = _head_unstack(jnp.dot(acc[c].astype(BF16), vb[c], preferred_element_type=F32), C, HG_DV, HG_HEADS)
        o = o + jnp.dot((q[c] * kk[c]).astype(BF16), ones_ref[...], preferred_element_type=F32) * v[c]
        e_q = jnp.exp(r[c][n_levels * C:(n_levels + 1) * C])
        o = o + lax.dot_general((q[c] * e_q).astype(BF16), st[c].astype(BF16), (((1,), (1,)), ((), ())),
                                preferred_element_type=F32)
        outs.append(o)
    new_state = []
    for c in range(len(chains)):
        e_k = jnp.exp(r[c][(n_levels + 1) * C:(n_levels + 2) * C])
        ku = (kk[c] * e_k).astype(BF16)
        u = lax.dot_general(vb[c], ku, (((0,), (0,)), ((), ())), preferred_element_type=F32)
        dtot = jnp.exp(r[c][(n_levels + 2) * C:(n_levels + 2) * C + 1])
        new_state.append(st[c] * dtot + u * mask_ref[...])
    for c, (s, d) in enumerate(chains):
        o_ref = (of_ref, ob_ref)[d]
        o_ref[s] = outs[c].astype(o_ref.dtype)
        state[s, d] = new_state[c]


def _hg_mixer(p_hg, lb, n_ctx):
    B, S, _ = p_hg.shape
    C = HG_CHUNK
    levels, mats, qroles, kroles, same_block = _hg_level_tables(C)
    nl = len(levels)
    m_all = jnp.asarray(mats, BF16)
    roles = jnp.asarray(np.broadcast_to(qroles[..., None], (2, nl, C, GROUP_W)).astype(np.float32))
    blk = jnp.asarray(np.tile(same_block, (1, HG_HEADS, 1)))
    hd = np.arange(GROUP_W) // HG_DK
    bd = (hd[:, None] == hd[None, :]).astype(np.float32)
    ones_bd = jnp.asarray(bd, BF16)
    mask = jnp.asarray(bd)
    nc, ncc = S // C, n_ctx // C
    col = lambda k, rev: (lambda b, j: (b, _bwd_chunk(j, ncc, nc) if rev else j, k))
    assert B % HG_SAMPLES == 0
    blkspec = lambda k, rev: pl.BlockSpec((HG_SAMPLES, C, GROUP_W), col(k, rev))
    full = lambda a: pl.BlockSpec(a.shape, lambda b, j: (0,) * a.ndim)
    lb3 = lb.reshape(2, 1, GROUP_W).astype(F32)
    return pl.pallas_call(
        functools.partial(_hg_kernel, n_levels=nl),
        grid=(B // HG_SAMPLES, nc),
        in_specs=[blkspec(0, False), blkspec(1, False), blkspec(3, False),
                  blkspec(0, True), blkspec(2, True), blkspec(3, True),
                  full(lb3), full(m_all), full(roles), full(blk), full(ones_bd), full(mask)],
        out_specs=[blkspec(0, False), blkspec(0, True)],
        out_shape=[jax.ShapeDtypeStruct((B, S, GROUP_W), BF16)] * 2,
        scratch_shapes=[pltpu.VMEM((HG_SAMPLES, 2, GROUP_W, GROUP_W), F32)],
        compiler_params=_cparams(("parallel", "arbitrary")),
        name="hgrn2_scan",
    )(p_hg, p_hg, p_hg, p_hg, p_hg, p_hg, lb3, m_all, roles, blk, ones_bd, mask)


_SWA_OUT_PERM = np.concatenate([np.arange(64 * h, 64 * h + 64) for h in (0, 2, 1, 3)])


def _wout_kernel(h_ref, m_ref, a_ref, hf_ref, hb_ref, hgate_ref, c_ref, rf_ref, rb_ref, rgate_ref,
                 hgn_ref, rgg_ref, rgb_ref, ones_ref, w_ref, o_ref, *, tm, n_ctx, row_off):
    row0 = row_off + pl.program_id(1) * tm
    ones = ones_ref[...]
    o = hf_ref[0].astype(F32) + hb_ref[0].astype(F32)
    yb = o * lax.rsqrt(_seg_mean(o * o, ones) + NORM_EPS) * hgn_ref[...] * _silu(hgate_ref[0])
    o = rf_ref[0].astype(F32) + rb_ref[0].astype(F32)
    xc = o - _seg_mean(o, ones)
    yd = (xc * lax.rsqrt(_seg_mean(xc * xc, ones) + NORM_EPS) * rgg_ref[...] + rgb_ref[...]) * _silu(rgate_ref[0])
    acc = jnp.dot(a_ref[0].astype(BF16), w_ref[0:256, :], preferred_element_type=F32)
    acc += jnp.dot(yb.astype(BF16), w_ref[256:512, :], preferred_element_type=F32)
    acc += jnp.dot(c_ref[0].astype(BF16), w_ref[512:768, :], preferred_element_type=F32)
    acc += jnp.dot(yd.astype(BF16), w_ref[768:1024, :], preferred_element_type=F32)
    o_ref[0] = h_ref[0] + _row_mod(m_ref, 2, row0, tm, n_ctx) * acc


def _out_proj(h, mod, o_mla, hg_f, hg_b, p_hg, o_swa, ret_f, ret_b, p_ret, hg_norm_g, ret_gn_g, ret_gn_b,
              w_out_p, n_ctx, tm=768, row_off=0):
    B, S, D = h.shape
    assert row_off % tm == 0 and (S - row_off) % tm == 0
    off = row_off // tm
    hd = np.arange(GROUP_W) // HG_DV
    ones_bd = jnp.asarray((hd[:, None] == hd[None, :]).astype(np.float32) / HG_DV, BF16)
    row = lambda w, k=0: pl.BlockSpec((1, tm, w), lambda b, i: (b, i + off, k))
    vec = pl.BlockSpec((1, GROUP_W), lambda b, i: (0, 0))
    return pl.pallas_call(
        functools.partial(_wout_kernel, tm=tm, n_ctx=n_ctx, row_off=row_off),
        grid=(B, (S - row_off) // tm),
        in_specs=[row(D), pl.BlockSpec((1, 2, 8, D), lambda b, i: (b, 0, 0, 0)),
                  row(GROUP_W), row(GROUP_W), row(GROUP_W), row(GROUP_W, 4),
                  row(GROUP_W), row(GROUP_W), row(GROUP_W), row(GROUP_W, 2),
                  vec, vec, vec, pl.BlockSpec((GROUP_W, GROUP_W), lambda b, i: (0, 0)),
                  pl.BlockSpec((D, D), lambda b, i: (0, 0))],
        out_specs=pl.BlockSpec((1, tm, D), lambda b, i: (b, i, 0)),
        out_shape=jax.ShapeDtypeStruct((B, S - row_off, D), F32),
        compiler_params=_cparams(("parallel", "parallel")),
        name="out_proj",
    )(h, mod, o_mla, hg_f, hg_b, p_hg, o_swa, ret_f, ret_b, p_ret,
      hg_norm_g.reshape(1, GROUP_W), ret_gn_g.reshape(1, GROUP_W), ret_gn_b.reshape(1, GROUP_W), ones_bd, w_out_p)


def _ffn_kernel(h_ref, m_ref, g_ref, wg_ref, wu_ref, wd_ref, o_ref, xn, *, tm, n_ctx):
    j = pl.program_id(2)
    row0 = pl.program_id(1) * tm

    @pl.when(j == 0)
    def _():
        xn[...] = _norm_mod(h_ref[0], g_ref[...], _row_mod(m_ref, 3, row0, tm, n_ctx),
                            _row_mod(m_ref, 4, row0, tm, n_ctx)).astype(BF16)
        o_ref[...] = jnp.zeros_like(o_ref)

    x = xn[...]
    hg = jnp.dot(x, wg_ref[0], preferred_element_type=F32)
    hu = jnp.dot(x, wu_ref[0], preferred_element_type=F32)
    o_ref[0] += jnp.dot((_silu(hg) * hu).astype(BF16), wd_ref[0], preferred_element_type=F32)

    @pl.when(j == pl.num_programs(2) - 1)
    def _():
        o_ref[0] = h_ref[0] + _row_mod(m_ref, 5, row0, tm, n_ctx) * o_ref[0]


def _dense_ffn(h, mod, g, w_gate, w_up, w_down, li, n_ctx, tm=1152, tf=1408):
    B, S, D = h.shape
    dff = w_gate.shape[-1]
    return pl.pallas_call(
        functools.partial(_ffn_kernel, tm=tm, n_ctx=n_ctx),
        grid=(B, S // tm, dff // tf),
        in_specs=[pl.BlockSpec((1, tm, D), lambda b, i, j: (b, i, 0)),
                  pl.BlockSpec((1, 2, 8, D), lambda b, i, j: (b, 0, 0, 0)),
                  pl.BlockSpec((1, D), lambda b, i, j: (0, 0)),
                  pl.BlockSpec((1, D, tf), lambda b, i, j: (li, 0, j)),
                  pl.BlockSpec((1, D, tf), lambda b, i, j: (li, 0, j)),
                  pl.BlockSpec((1, tf, D), lambda b, i, j: (li, j, 0))],
        out_specs=pl.BlockSpec((1, tm, D), lambda b, i, j: (b, i, 0)),
        out_shape=jax.ShapeDtypeStruct((B, S, D), F32),
        scratch_shapes=[pltpu.VMEM((tm, D), BF16)],
        compiler_params=_cparams(("parallel", "parallel", "arbitrary")),
        name="dense_ffn",
    )(h, mod, g.reshape(1, D), w_gate, w_up, w_down)


MOE_TM = 1024
MOE_TF = 512
MOE_ROWS = (384, 256)
DMA_UNROLL = 8
SUBLANES = 8
ZERO_ROWS = 256


def _router_kernel(h_ref, m_ref, g_ref, wr_ref, tri_ref, xn_ref, route_ref, cnt_ref, counts, *, tm, n_ctx):
    b, i = pl.program_id(0), pl.program_id(1)
    row0 = i * tm

    @pl.when((b == 0) & (i == 0))
    def _():
        counts[...] = jnp.zeros_like(counts)

    a = _norm_mod(h_ref[0], g_ref[...], _row_mod(m_ref, 3, row0, tm, n_ctx), _row_mod(m_ref, 4, row0, tm, n_ctx))
    xn_ref[0] = a
    ah, al = _split2(a)
    wh, wl = _split2(wr_ref[...])
    logits = (jnp.dot(ah, wh, preferred_element_type=F32) + jnp.dot(ah, wl, preferred_element_type=F32)
              + jnp.dot(al, wh, preferred_element_type=F32))
    lane = lax.broadcasted_iota(jnp.int32, logits.shape, 1)
    neg = jnp.float32(-jnp.inf)
    lg = jnp.where(lane < N_EXPERTS, logits, neg)
    v1 = jnp.max(lg, axis=-1, keepdims=True)
    i1 = jnp.min(jnp.where(lg == v1, lane, LANE), axis=-1, keepdims=True)
    lg2 = jnp.where(lane == i1, neg, lg)
    v2 = jnp.max(lg2, axis=-1, keepdims=True)
    i2 = jnp.min(jnp.where(lg2 == v2, lane, LANE), axis=-1, keepdims=True)
    e = jnp.exp(v2 - v1)
    w1 = 1.0 / (1.0 + e)
    w2 = e / (1.0 + e)
    oh1, oh2 = lane == i1, lane == i2
    onehot = jnp.where(oh1 | oh2, 1.0, 0.0)
    earlier = jnp.dot(tri_ref[...], onehot.astype(BF16), preferred_element_type=F32)
    pos = earlier + counts[...]
    r1 = jnp.sum(jnp.where(oh1, pos, 0.0), axis=-1, keepdims=True)
    r2 = jnp.sum(jnp.where(oh2, pos, 0.0), axis=-1, keepdims=True)
    counts[...] = counts[...] + jnp.sum(onehot, axis=0, keepdims=True)
    cnt_ref[...] = jnp.broadcast_to(counts[...], cnt_ref.shape)
    route_ref[0] = jnp.where(lane == 0, i1.astype(F32),
                             jnp.where(lane == 1, i2.astype(F32),
                                       jnp.where(lane == 2, w1,
                                                 jnp.where(lane == 3, w2,
                                                           jnp.where(lane == 4, r1, jnp.where(lane == 5, r2, 0.0))))))


def _scatter_kernel(meta_ref, slot_ref, x_ref, xs_hbm, zeros_v, sem, *, tm, tile_rows, total_rows):
    b, i = pl.program_id(0), pl.program_id(1)

    def start(r, carry):
        for k in range(TOP_K):
            pltpu.make_async_copy(x_ref.at[0, pl.ds(r, 1)], xs_hbm.at[pl.ds(slot_ref[0, 0, 0, k * tm + r], 1)],
                                  sem).start()
        return carry

    lax.fori_loop(0, tm, start, 0, unroll=DMA_UNROLL)
    for k in range(TOP_K):
        pltpu.make_async_copy(x_ref.at[0], xs_hbm.at[pl.ds(0, tm)], sem).wait()

    @pl.when((b == pl.num_programs(0) - 1) & (i == pl.num_programs(1) - 1))
    def _():
        zeros_v[...] = jnp.zeros_like(zeros_v)
        z1 = zeros_v.at[pl.ds(0, 1)]
        z8 = zeros_v.at[pl.ds(0, SUBLANES)]

        def fill(lo, end):
            mid = jnp.minimum(((lo + SUBLANES - 1) // SUBLANES) * SUBLANES, end)
            big = mid + ((end - mid) // ZERO_ROWS) * ZERO_ROWS
            n_row, n_big, n_small = mid - lo, (big - mid) // ZERO_ROWS, (end - big) // SUBLANES

            def start_row(r, c):
                pltpu.make_async_copy(z1, xs_hbm.at[pl.ds(lo + r, 1)], sem).start()
                return c

            def wait_row(r, c):
                pltpu.make_async_copy(z1, xs_hbm.at[pl.ds(0, 1)], sem).wait()
                return c

            def start_big(q, c):
                dst = xs_hbm.at[pl.ds(pl.multiple_of(mid + q * ZERO_ROWS, SUBLANES), ZERO_ROWS)]
                pltpu.make_async_copy(zeros_v, dst, sem).start()
                return c

            def wait_big(q, c):
                pltpu.make_async_copy(zeros_v, xs_hbm.at[pl.ds(0, ZERO_ROWS)], sem).wait()
                return c

            def start_small(q, c):
                dst = xs_hbm.at[pl.ds(pl.multiple_of(big + q * SUBLANES, SUBLANES), SUBLANES)]
                pltpu.make_async_copy(z8, dst, sem).start()
                return c

            def wait_small(q, c):
                pltpu.make_async_copy(z8, xs_hbm.at[pl.ds(0, SUBLANES)], sem).wait()
                return c

            lax.fori_loop(0, n_row, start_row, 0)
            lax.fori_loop(0, n_big, start_big, 0)
            lax.fori_loop(0, n_small, start_small, 0)
            lax.fori_loop(0, n_row, wait_row, 0)
            lax.fori_loop(0, n_big, wait_big, 0)
            lax.fori_loop(0, n_small, wait_small, 0)

        for e in range(N_EXPERTS):
            c = meta_ref[e]
            g0 = meta_ref[N_EXPERTS + e]
            fill(g0 + c, g0 + ((c + tile_rows - 1) // tile_rows) * tile_rows)
        fill(meta_ref[2 * N_EXPERTS] * tile_rows, jnp.int32(total_rows))


def _expert_kernel(te_ref, nu_ref, x_ref, wg_ref, wu_ref, wd_ref, o_ref, xb):
    i, j = pl.program_id(0), pl.program_id(1)

    @pl.when(j == 0)
    def _():
        xb[...] = x_ref[...].astype(BF16)
        o_ref[...] = jnp.zeros_like(o_ref)

    @pl.when(i < nu_ref[0])
    def _():
        wg, wu, wd = wg_ref[0, 0].astype(BF16), wu_ref[0, 0].astype(BF16), wd_ref[0, 0].astype(BF16)
        x = xb[...]
        hg = jnp.dot(x, wg, preferred_element_type=F32)
        hu = jnp.dot(x, wu, preferred_element_type=F32)
        o_ref[...] += jnp.dot((_silu(hg) * hu).astype(BF16), wd, preferred_element_type=F32)


def _combine_kernel(slot_ref, h_ref, m_ref, route_ref, fg_ref, y_hbm, o_ref, buf, sem, *, tm, n_ctx, final_norm):
    row0 = pl.program_id(1) * tm

    def start(r, carry):
        for k in range(TOP_K):
            pltpu.make_async_copy(y_hbm.at[pl.ds(slot_ref[0, 0, 0, k * tm + r], 1)], buf.at[k, pl.ds(r, 1)], sem).start()
        return carry

    lax.fori_loop(0, tm, start, 0, unroll=DMA_UNROLL)
    for k in range(TOP_K):
        pltpu.make_async_copy(y_hbm.at[pl.ds(0, tm)], buf.at[k], sem).wait()
    w1 = route_ref[0, :, 2:3]
    w2 = route_ref[0, :, 3:4]
    y = h_ref[0] + _row_mod(m_ref, 5, row0, tm, n_ctx) * (w1 * buf[0] + w2 * buf[1])
    if final_norm:
        y = y * lax.rsqrt(jnp.mean(y * y, axis=-1, keepdims=True) + NORM_EPS) * fg_ref[...]
    o_ref[0] = y


def _moe_ffn(h, mod, g, w_router, w_gate, w_up, w_down, li, n_ctx, final_g=None):
    B, S, D = h.shape
    T = B * S
    cr = next(r for r in MOE_ROWS if S % r == 0)
    tm = cr
    dffe = w_gate.shape[-1]
    te_rows = MOE_TM if T % MOE_TM == 0 else 512
    n_tiles = TOP_K * T // te_rows + N_EXPERTS
    P = n_tiles * te_rows
    wr = jnp.pad(w_router, ((0, 0), (0, LANE - N_EXPERTS)))
    tri = jnp.asarray(np.tril(np.ones((tm, tm), np.float32), -1), BF16)
    xn, route, cnt = pl.pallas_call(
        functools.partial(_router_kernel, tm=tm, n_ctx=n_ctx),
        grid=(B, S // tm),
        in_specs=[pl.BlockSpec((1, tm, D), lambda b, i: (b, i, 0)),
                  pl.BlockSpec((1, 2, 8, D), lambda b, i: (b, 0, 0, 0)),
                  pl.BlockSpec((1, D), lambda b, i: (0, 0)),
                  pl.BlockSpec((D, LANE), lambda b, i: (0, 0)),
                  pl.BlockSpec((tm, tm), lambda b, i: (0, 0))],
        out_specs=[pl.BlockSpec((1, tm, D), lambda b, i: (b, i, 0)),
                   pl.BlockSpec((1, tm, LANE), lambda b, i: (b, i, 0)),
                   pl.BlockSpec((SUBLANES, LANE), lambda b, i: (0, 0))],
        out_shape=[jax.ShapeDtypeStruct((B, S, D), F32), jax.ShapeDtypeStruct((B, S, LANE), F32),
                   jax.ShapeDtypeStruct((SUBLANES, LANE), F32)],
        scratch_shapes=[pltpu.VMEM((1, LANE), F32)],
        compiler_params=_cparams(("arbitrary", "arbitrary")),
        name="moe_router",
    )(h, mod, g.reshape(1, D), wr, tri)

    counts = cnt[0, :N_EXPERTS].astype(jnp.int32)
    ptiles = (counts + te_rows - 1) // te_rows
    tile_end = jnp.cumsum(ptiles)
    gstart = (tile_end - ptiles) * te_rows
    n_used = tile_end[-1:].astype(jnp.int32)
    tile_ids = jnp.arange(n_tiles, dtype=jnp.int32)
    tile_e = jnp.minimum(jnp.sum((tile_ids[:, None] >= tile_end[None, :]).astype(jnp.int32), axis=1),
                         N_EXPERTS - 1)
    meta = jnp.concatenate([counts, gstart, n_used])
    choice = route[..., 0:TOP_K].astype(jnp.int32)
    is_e = choice[..., None] == jnp.arange(N_EXPERTS, dtype=jnp.int32)
    slots = jnp.sum(jnp.where(is_e, gstart, 0), axis=-1) + route[..., 4:4 + TOP_K].astype(jnp.int32)
    slots = slots.reshape(B, S // cr, cr, TOP_K).transpose(0, 1, 3, 2)
    slots = slots.reshape(B, S // cr, 1, TOP_K * cr)

    xs = pl.pallas_call(
        functools.partial(_scatter_kernel, tm=cr, tile_rows=te_rows, total_rows=P),
        grid_spec=pltpu.PrefetchScalarGridSpec(
            num_scalar_prefetch=1,
            grid=(B, S // cr),
            in_specs=[pl.BlockSpec((1, 1, 1, TOP_K * cr), lambda b, i, mt: (b, i, 0, 0),
                                   memory_space=pltpu.SMEM),
                      pl.BlockSpec((1, cr, D), lambda b, i, mt: (b, i, 0))],
            out_specs=pl.BlockSpec(memory_space=pl.ANY),
            scratch_shapes=[pltpu.VMEM((ZERO_ROWS, D), F32), pltpu.SemaphoreType.DMA(())]),
        out_shape=jax.ShapeDtypeStruct((P, D), F32),
        compiler_params=_cparams(("arbitrary", "arbitrary")),
        name="moe_scatter",
    )(meta, slots, xn)

    nf = dffe // MOE_TF
    last = lambda i, nu: jnp.minimum(i, nu[0] - 1)
    jj = lambda i, j, nu: jnp.where(i < nu[0], j, nf - 1)
    ys = pl.pallas_call(
        _expert_kernel,
        grid_spec=pltpu.PrefetchScalarGridSpec(
            num_scalar_prefetch=2,
            grid=(n_tiles, nf),
            in_specs=[pl.BlockSpec((te_rows, D), lambda i, j, te, nu: (last(i, nu), 0)),
                      pl.BlockSpec((1, 1, D, MOE_TF), lambda i, j, te, nu: (li, te[last(i, nu)], 0, jj(i, j, nu))),
                      pl.BlockSpec((1, 1, D, MOE_TF), lambda i, j, te, nu: (li, te[last(i, nu)], 0, jj(i, j, nu))),
                      pl.BlockSpec((1, 1, MOE_TF, D), lambda i, j, te, nu: (li, te[last(i, nu)], jj(i, j, nu), 0))],
            out_specs=pl.BlockSpec((te_rows, D), lambda i, j, te, nu: (i, 0)),
            scratch_shapes=[pltpu.VMEM((te_rows, D), BF16)]),
        out_shape=jax.ShapeDtypeStruct((P, D), F32),
        compiler_params=_cparams(("arbitrary", "arbitrary")),
        name="moe_experts",
    )(tile_e, n_used, xs, w_gate, w_up, w_down)
    fg = jnp.ones((1, D), F32) if final_g is None else final_g.reshape(1, D).astype(F32)
    return pl.pallas_call(
        functools.partial(_combine_kernel, tm=cr, n_ctx=n_ctx, final_norm=final_g is not None),
        grid=(B, S // cr),
        in_specs=[pl.BlockSpec((1, 1, 1, TOP_K * cr), lambda b, i: (b, i, 0, 0), memory_space=pltpu.SMEM),
                  pl.BlockSpec((1, cr, D), lambda b, i: (b, i, 0)),
                  pl.BlockSpec((1, 2, 8, D), lambda b, i: (b, 0, 0, 0)),
                  pl.BlockSpec((1, cr, LANE), lambda b, i: (b, i, 0)),
                  pl.BlockSpec((1, D), lambda b, i: (0, 0)),
                  pl.BlockSpec(memory_space=pl.ANY)],
        out_specs=pl.BlockSpec((1, cr, D), lambda b, i: (b, i, 0)),
        out_shape=jax.ShapeDtypeStruct((B, S, D), F32),
        scratch_shapes=[pltpu.VMEM((TOP_K, cr, D), F32), pltpu.SemaphoreType.DMA(())],
        compiler_params=_cparams(("arbitrary", "arbitrary")),
        name="moe_combine",
    )(slots, h, mod, route, fg, ys)


def _final_norm_kernel(h_ref, g_ref, o_ref):
    x = h_ref[0]
    o_ref[0] = x * lax.rsqrt(jnp.mean(x * x, axis=-1, keepdims=True) + NORM_EPS) * g_ref[...]


def _final_norm(h, g, n_ctx, tm=256):
    B, S, D = h.shape
    N = S - n_ctx
    return pl.pallas_call(
        _final_norm_kernel,
        grid=(B, N // tm),
        in_specs=[pl.BlockSpec((1, tm, D), lambda b, i: (b, n_ctx // tm + i, 0)),
                  pl.BlockSpec((1, D), lambda b, i: (0, 0))],
        out_specs=pl.BlockSpec((1, tm, D), lambda b, i: (b, i, 0)),
        out_shape=jax.ShapeDtypeStruct((B, N, D), F32),
        compiler_params=_cparams(("parallel", "parallel")),
        name="final_norm",
    )(h, g.reshape(1, D))


MOD_ROWS = 16


def kernel(x, c, ctx, c_ctx, w_ada, b_ada, norm_mix_g, norm_ffn_g, w_in, w_out, mla_q_norm_g, mla_w_uq, mla_kv_norm_g, mla_w_ukv, hg_lb_logits, hg_norm_g, swa_sink, ret_decay_logit, ret_gn_g, ret_gn_b, ffn_w_gate, ffn_w_up, ffn_w_down, moe_w_router, moe_w_gate, moe_w_up, moe_w_down, final_norm_g):
    B, N, D = x.shape
    L = ctx.shape[1]
    S = L + N
    assert B + 1 <= MOD_ROWS
    h = jnp.concatenate([ctx, x], axis=1)
    cvec = jnp.zeros((MOD_ROWS, D), F32).at[0].set(c_ctx).at[1:1 + B].set(c)
    mod_all = _modulation(cvec, w_ada, b_ada).reshape(DEPTH, MOD_ROWS, 6, D)

    lb_p = jax.nn.softmax(hg_lb_logits.astype(F32), axis=0)
    hg_lb = jnp.cumsum(lb_p, axis=0) - lb_p[0:1]
    ret_log_gamma = jax.nn.log_sigmoid(ret_decay_logit.astype(F32))

    mla_tab = _mla_tables(N, L)
    swa_tab = _swa_tables(N, L)
    w_out_perm = np.concatenate([np.arange(512), 512 + _SWA_OUT_PERM, np.arange(768, 1024)])
    ffn_bf16 = tuple(w.astype(BF16) for w in (ffn_w_gate, ffn_w_up, ffn_w_down))

    for l in range(DEPTH):
        m = mod_all[l]
        mod = jnp.stack([jnp.broadcast_to(m[0], (B, 6, D)), m[1:1 + B]], axis=1)
        mod = jnp.pad(mod, ((0, 0), (0, 0), (0, 2), (0, 0)))
        w_ext = _take_cols(w_in[l], _EXT_COLS).astype(BF16)
        mla_w = _mla_weights(mla_q_norm_g[l], mla_w_uq[l], mla_kv_norm_g[l], mla_w_ukv[l])
        mla_qkv, p_hg, swa_qkv, p_ret = _in_proj(h, mod, norm_mix_g[l], w_ext, mla_w, mla_tab, swa_tab, L)
        o_mla = _mla_mixer(mla_qkv, L)
        hg_f, hg_b = _hg_mixer(p_hg, hg_lb[l], L)
        o_swa = _swa_mixer(swa_qkv, swa_sink[l], L)
        ret_f, ret_b = _ret_mixer(p_ret, _ret_tables(ret_log_gamma[l], L, S), L)
        w_out_p = _take_static(w_out[l], w_out_perm, 0).astype(BF16)
        last = l == DEPTH - 1
        if last and l % 2 == 1:
            h = _out_proj(h, mod, o_mla, hg_f, hg_b, p_hg, o_swa, ret_f, ret_b, p_ret,
                          hg_norm_g[l], ret_gn_g[l], ret_gn_b[l], w_out_p, L, tm=math.gcd(L, N, 256), row_off=L)
            return _moe_ffn(h, mod, norm_ffn_g[l], moe_w_router[l // 2], moe_w_gate, moe_w_up, moe_w_down, l // 2,
                            0, final_g=final_norm_g)
        h = _out_proj(h, mod, o_mla, hg_f, hg_b, p_hg, o_swa, ret_f, ret_b, p_ret,
                      hg_norm_g[l], ret_gn_g[l], ret_gn_b[l], w_out_p, L)
        if l % 2 == 0:
            h = _dense_ffn(h, mod, norm_ffn_g[l], ffn_bf16[0], ffn_bf16[1], ffn_bf16[2], l // 2, L,
                           tm=1152 if S % 1152 == 0 else 384)
        else:
            h = _moe_ffn(h, mod, norm_ffn_g[l], moe_w_router[l // 2], moe_w_gate, moe_w_up, moe_w_down, l // 2, L)
    return _final_norm(h, final_norm_g, L)
```

```python
import functools
import math

import numpy as np
import jax
import jax.numpy as jnp
from jax import lax
from jax.experimental import pallas as pl
from jax.experimental.pallas import tpu as pltpu

F32 = jnp.float32
BF16 = jnp.bfloat16

D_MODEL = 1024
DEPTH = 4
GRID_W = 64
ROPE_THETA = 10000.0
NORM_EPS = 1e-6
LB_FLOOR = 1e-30
GROUP_W = 256

MLA_HEADS, MLA_NOPE, MLA_ROPE, MLA_V = 4, 64, 32, 64
MLA_Q_RANK, MLA_KV_RANK = 192, 128
HG_HEADS, HG_DK, HG_DV = 4, 64, 64
SWA_HEADS, SWA_KV_HEADS, SWA_HD, SWA_WINDOW = 4, 2, 64, 128
RET_HEADS, RET_DK, RET_DV = 4, 32, 64
N_EXPERTS, TOP_K = 8, 2

_IN_SIZES = (192, 128, 32, 256, 256, 256, 256, 256, 256, 128, 128, 128, 128, 256, 256)
_OFF = np.concatenate([[0], np.cumsum(_IN_SIZES)]).astype(np.int64)
D_IN = int(_OFF[-1])

W_MLA, W_HG, W_SWA, W_RET = 512, 1280, 512, 768
W_EXT = W_MLA + W_HG + W_SWA + W_RET

LANE = 128
VMEM_LIMIT = 56 * 1024 * 1024


def _cparams(sem):
    return pltpu.CompilerParams(dimension_semantics=sem, vmem_limit_bytes=VMEM_LIMIT)


def _swap_halves(idx, width):
    idx = np.asarray(idx)
    base = (idx // width) * width
    j = idx % width
    return base + (j + width // 2) % width


def _ext_columns():
    z = lambda n: -np.ones((n,), np.int64)
    rng = lambda a, n: np.arange(a, a + n)
    o = _OFF
    k_pe = rng(o[2], 32)
    mla = np.concatenate([rng(o[0], 192), z(64), rng(o[1], 128), k_pe,
                          o[2] + _swap_halves(np.arange(32), 32), z(64)])
    hg = rng(o[3], 1280)
    q_heads = [rng(o[8] + 64 * h, 64) for h in range(4)]
    q_perm = np.concatenate([q_heads[0], q_heads[2], q_heads[1], q_heads[3]])
    swa = np.concatenate([q_perm, rng(o[9], 128), rng(o[10], 128)])
    ret = rng(o[11], 768)
    cols = np.concatenate([mla, hg, swa, ret])
    assert cols.shape[0] == W_EXT
    return cols


_EXT_COLS = _ext_columns()


def _take_static(w, idx, axis):
    idx = np.asarray(idx)
    axis = axis % w.ndim
    pieces, start = [], 0
    for i in range(1, len(idx) + 1):
        run_ends = i == len(idx) or (idx[i] != idx[i - 1] + 1 if idx[i - 1] >= 0 else idx[i] >= 0)
        if run_ends:
            n = i - start
            if idx[start] < 0:
                shape = list(w.shape)
                shape[axis] = n
                pieces.append(jnp.zeros(shape, w.dtype))
            else:
                pieces.append(lax.slice_in_dim(w, int(idx[start]), int(idx[start]) + n, axis=axis))
            start = i
    return jnp.concatenate(pieces, axis=axis)


def _take_cols(w, cols):
    return _take_static(w, cols, -1)


def _axial_angles(n_tok, rot_dim):
    rows = n_tok // GRID_W
    row = jnp.broadcast_to(jnp.arange(rows)[:, None], (rows, GRID_W)).reshape(-1)
    col = jnp.broadcast_to(jnp.arange(GRID_W)[None, :], (rows, GRID_W)).reshape(-1)
    n_freq = rot_dim // 4
    inv = ROPE_THETA ** (-jnp.arange(n_freq, dtype=F32) / n_freq)
    return jnp.concatenate([row.astype(F32)[:, None] * inv, col.astype(F32)[:, None] * inv], axis=-1)


def _rope_tables(ang, n_ctx):
    cos, sin = jnp.cos(ang), jnp.sin(ang)
    c = jnp.concatenate([cos, cos], axis=-1)
    s = jnp.concatenate([-sin, sin], axis=-1)
    r = c.shape[-1]
    c = jnp.concatenate([jnp.ones((n_ctx, r), F32), c], axis=0)
    s = jnp.concatenate([jnp.zeros((n_ctx, r), F32), s], axis=0)
    return c, s


def _split2(x):
    hi = x.astype(BF16)
    lo = (x - hi.astype(F32)).astype(BF16)
    return hi, lo


def _seg_mean(x, ones_bd):
    hi, lo = _split2(x)
    return (jnp.dot(hi, ones_bd, preferred_element_type=F32)
            + jnp.dot(lo, ones_bd, preferred_element_type=F32))


def _silu(x):
    return x / (1.0 + jnp.exp(-x))


def _rot_half(x, head):
    half = head // 2
    lane = lax.broadcasted_iota(jnp.int32, x.shape, 1) % head
    return jnp.where(lane < half, pltpu.roll(x, LANE - half, axis=1), pltpu.roll(x, half, axis=1))


def _row_mod(m_ref, idx, row0, tm, n_ctx):
    rows = row0 + lax.broadcasted_iota(jnp.int32, (tm, 1), 0)
    mc = m_ref[0, 0, idx:idx + 1, :]
    ml = m_ref[0, 1, idx:idx + 1, :]
    return jnp.where(rows < n_ctx, mc, ml)


def _norm_mod(x, g, shift, scale):
    ms = jnp.mean(x * x, axis=-1, keepdims=True)
    y = x * lax.rsqrt(ms + NORM_EPS) * g
    return y * (1.0 + scale) + shift


def _mod_kernel(c_ref, w_ref, b_ref, o_ref):
    s = _silu(c_ref[...])
    o_ref[0] = jnp.dot(s.astype(BF16), w_ref[0].astype(BF16), preferred_element_type=F32) + b_ref[0]


def _modulation(cvec, w_ada, b_ada):
    R, D = cvec.shape
    depth, _, n6 = w_ada.shape
    tn = 1536
    return pl.pallas_call(
        _mod_kernel,
        grid=(depth, n6 // tn),
        in_specs=[pl.BlockSpec((R, D), lambda l, j: (0, 0)),
                  pl.BlockSpec((1, D, tn), lambda l, j: (l, 0, j)),
                  pl.BlockSpec((1, 1, tn), lambda l, j: (l, 0, j))],
        out_specs=pl.BlockSpec((1, R, tn), lambda l, j: (l, 0, j)),
        out_shape=jax.ShapeDtypeStruct((depth, R, n6), F32),
        compiler_params=_cparams(("parallel", "parallel")),
        name="adaln_mod",
    )(cvec, w_ada, b_ada.reshape(depth, 1, n6))


def _win_kernel(h_ref, m_ref, g_ref, w_ref, cq_ref, sq_ref, tk_ref, gq_ref, wq_ref, wqs_ref, gk_ref, wk_ref,
                wv_ref, e_ref, sc_ref, ss_ref, mq_out, mk_out, mv_out, o_hg, sq_out, skv_out, o_ret,
                *, tm, n_ctx):
    row0 = pl.program_id(1) * tm
    a = _norm_mod(h_ref[0], g_ref[...], _row_mod(m_ref, 0, row0, tm, n_ctx),
                  _row_mod(m_ref, 1, row0, tm, n_ctx)).astype(BF16)
    group = lambda c0, w: jnp.dot(a, w_ref[:, c0:c0 + w], preferred_element_type=F32)
    mq_out[0], mk_out[0], mv_out[0] = _mla_prep(group(0, W_MLA), cq_ref, sq_ref, tk_ref, gq_ref, wq_ref, wqs_ref,
                                                gk_ref, wk_ref, wv_ref, e_ref)
    o_hg[0] = group(W_MLA, W_HG)
    sq_out[0], skv_out[0] = _swa_prep(group(W_MLA + W_HG, W_SWA), sc_ref, ss_ref)
    o_ret[0] = group(W_MLA + W_HG + W_SWA, W_RET)


def _in_proj(h, mod, g, w_ext, mla_weights, mla_tables, swa_tables, n_ctx, tm=768):
    B, S, D = h.shape
    full = lambda a: pl.BlockSpec(a.shape, lambda b, i: (0,) * a.ndim)
    tab = pl.BlockSpec((tm, LANE), lambda b, i: (i, 0))
    out_w = ((512, BF16), (512, BF16), (512, BF16), (W_HG, F32), (256, BF16), (256, BF16), (W_RET, F32))
    outs = pl.pallas_call(
        functools.partial(_win_kernel, tm=tm, n_ctx=n_ctx),
        grid=(B, S // tm),
        in_specs=[pl.BlockSpec((1, tm, D), lambda b, i: (b, i, 0)),
                  pl.BlockSpec((1, 2, 8, D), lambda b, i: (b, 0, 0, 0)),
                  pl.BlockSpec((1, D), lambda b, i: (0, 0)),
                  pl.BlockSpec((D, W_EXT), lambda b, i: (0, 0)),
                  tab, tab, tab] + [full(w) for w in mla_weights] + [tab, tab],
        out_specs=[pl.BlockSpec((1, tm, w), lambda b, i: (b, i, 0)) for w, _ in out_w],
        out_shape=[jax.ShapeDtypeStruct((B, S, w), dt) for w, dt in out_w],
        compiler_params=_cparams(("parallel", "parallel")),
        name="in_proj",
    )(h, mod, g.reshape(1, D), w_ext, *mla_tables, *mla_weights, *swa_tables)
    return outs[0:3], outs[3], outs[4:6], outs[6]


def _mla_weights(q_norm_g, w_uq, kv_norm_g, w_ukv):
    qcols, qsw, kcols = [], [], []
    for h in range(MLA_HEADS):
        b = 96 * h
        nope = np.arange(b, b + 64)
        pe = np.arange(b + 64, b + 96)
        pad = -np.ones((32,), np.int64)
        qcols.append(np.concatenate([nope, pe, pad]))
        qsw.append(np.concatenate([-np.ones((64,), np.int64), (b + 64) + _swap_halves(np.arange(32), 32), pad]))
        kcols.append(np.concatenate([np.arange(128 * h, 128 * h + 64), -np.ones((64,), np.int64)]))
    vcols = np.concatenate([np.concatenate([np.arange(128 * h + 64, 128 * h + 128), -np.ones((64,), np.int64)])
                            for h in range(MLA_HEADS)])
    pad_rows = ((0, 256 - MLA_Q_RANK), (0, 0))
    wq = jnp.pad(_take_cols(w_uq, np.concatenate(qcols)), pad_rows).astype(BF16)
    wq_sw = jnp.pad(_take_cols(w_uq, np.concatenate(qsw)), pad_rows).astype(BF16)
    wk = _take_cols(w_ukv, np.concatenate(kcols)).astype(BF16)
    wv = _take_cols(w_ukv, vcols).astype(BF16)
    e = np.zeros((128, 512), np.float32)
    for h in range(MLA_HEADS):
        for j in range(32):
            e[j, 128 * h + 64 + j] = 1.0
            e[32 + j, 128 * h + 64 + j] = 1.0
    gq = jnp.pad(q_norm_g, (0, 256 - MLA_Q_RANK)).reshape(1, 256)
    return gq, wq, wq_sw, kv_norm_g.reshape(1, 128), wk, wv, jnp.asarray(e, BF16)


def _mla_tables(n_lat, n_ctx):
    ang = _axial_angles(n_lat, MLA_ROPE)
    c32, s32 = _rope_tables(ang, n_ctx)
    S = n_lat + n_ctx
    one, zero = jnp.ones((S, 64), F32), jnp.zeros((S, 64), F32)
    cq = jnp.concatenate([one, c32, jnp.zeros((S, 32), F32)], axis=-1)
    sq = jnp.concatenate([zero, s32, jnp.zeros((S, 32), F32)], axis=-1)
    tk = jnp.concatenate([c32, s32, zero], axis=-1)
    return cq, sq, tk


def _mla_prep(p, cq_ref, sq_ref, tk_ref, gq_ref, wq_ref, wqs_ref, gk_ref, wk_ref, wv_ref, e_ref):
    scale = (MLA_NOPE + MLA_ROPE) ** -0.5 * float(np.log2(np.e))
    ql = p[:, 0:256]
    msq = jnp.sum(ql * ql, axis=-1, keepdims=True) * (1.0 / MLA_Q_RANK)
    qn = (ql * lax.rsqrt(msq + NORM_EPS) * gq_ref[...]).astype(BF16)
    kvl = p[:, 256:384]
    msk = jnp.mean(kvl * kvl, axis=-1, keepdims=True)
    kvn = (kvl * lax.rsqrt(msk + NORM_EPS) * gk_ref[...]).astype(BF16)
    q = jnp.dot(qn, wq_ref[...], preferred_element_type=F32)
    qs = jnp.dot(qn, wqs_ref[...], preferred_element_type=F32)
    cq = jnp.concatenate([cq_ref[...]] * MLA_HEADS, axis=-1)
    sq = jnp.concatenate([sq_ref[...]] * MLA_HEADS, axis=-1)
    q_out = ((q * cq + qs * sq) * scale).astype(BF16)
    pe = (p[:, 384:512] * tk_ref[...]).astype(BF16)
    k = (jnp.dot(kvn, wk_ref[...], preferred_element_type=F32)
         + jnp.dot(pe, e_ref[...], preferred_element_type=F32))
    lane = lax.broadcasted_iota(jnp.int32, (1, MLA_HEADS * LANE), 1) % LANE
    ones = jnp.where(lane >= MLA_V, 1.0, 0.0)
    v = jnp.dot(kvn, wv_ref[...], preferred_element_type=F32) + ones
    return q_out, k.astype(BF16), v.astype(BF16)


def _mla_attn_kernel(q_ref, k_ref, v_ref, o_ref, *, n_ctx, tq):
    S = k_ref.shape[1]

    lo = lax.broadcasted_iota(jnp.int32, (tq, LANE), 1) < MLA_V

    def attend(nk):
        pv = []
        for h in range(MLA_HEADS):
            q = q_ref[0, :, LANE * h:LANE * (h + 1)]
            k = k_ref[0, 0:nk, LANE * h:LANE * (h + 1)]
            s = lax.dot_general(q, k, (((1,), (1,)), ((), ())), preferred_element_type=F32)
            p = jnp.exp2(s - jnp.max(s, axis=-1, keepdims=True))
            pv.append(jnp.dot(p.astype(BF16), v_ref[0, 0:nk, LANE * h:LANE * (h + 1)], preferred_element_type=F32))
        for pair in range(MLA_HEADS // 2):
            a, b = pv[2 * pair], pv[2 * pair + 1]
            o_ref[0, :, LANE * pair:LANE * (pair + 1)] = jnp.where(
                lo, a / pltpu.roll(a, MLA_V, axis=1), pltpu.roll(b, MLA_V, axis=1) / b).astype(o_ref.dtype)

    is_ctx = (pl.program_id(1) + 1) * tq <= n_ctx
    pl.when(is_ctx)(lambda: attend(n_ctx))
    pl.when(jnp.logical_not(is_ctx))(lambda: attend(S))


def _mla_mixer(qkv, n_ctx, tq=256):
    q, k, v = qkv
    B, S, _ = q.shape
    assert n_ctx % tq == 0 and S % tq == 0
    return pl.pallas_call(
        functools.partial(_mla_attn_kernel, n_ctx=n_ctx, tq=tq),
        grid=(B, S // tq),
        in_specs=[pl.BlockSpec((1, tq, 512), lambda b, i: (b, i, 0)),
                  pl.BlockSpec((1, S, 512), lambda b, i: (b, 0, 0)),
                  pl.BlockSpec((1, S, 512), lambda b, i: (b, 0, 0))],
        out_specs=pl.BlockSpec((1, tq, GROUP_W), lambda b, i: (b, i, 0)),
        out_shape=jax.ShapeDtypeStruct((B, S, GROUP_W), BF16),
        compiler_params=_cparams(("parallel", "arbitrary")),
        name="mla_attn",
    )(q, k, v)


SWA_SAMPLES = 8


def _swa_tables(n_lat, n_ctx):
    ang = _axial_angles(n_lat, SWA_HD)
    c64, s64 = _rope_tables(ang, n_ctx)
    return jnp.concatenate([c64, c64], axis=-1), jnp.concatenate([s64, s64], axis=-1)


def _swa_prep(p, c_ref, s_ref):
    scale = SWA_HD ** -0.5
    c, s = c_ref[...], s_ref[...]
    rope = lambda x: x * c + _rot_half(x, SWA_HD) * s
    q = jnp.concatenate([rope(p[:, 0:LANE]), rope(p[:, LANE:2 * LANE])], axis=-1) * scale
    return q.astype(BF16), jnp.concatenate([rope(p[:, 256:384]), p[:, 384:512]], axis=-1).astype(BF16)


def _swa_attn_kernel(sink_ref, q_ref, kvc_ref, kvp_ref, kvn_ref, kvx_ref, bias_ref, o_ref, *, n_ctx, blk):
    j = pl.program_id(1)
    ncb = n_ctx // blk
    samples = range(q_ref.shape[0])
    lane = lax.broadcasted_iota(jnp.int32, (blk, LANE), 1)
    lo = lane < SWA_HD
    zero = jnp.zeros((), BF16)
    rb = lax.broadcasted_iota(jnp.int32, (4 * blk, 1), 0) // blk
    sink = jnp.where(rb == 0, sink_ref[0], jnp.where(rb == 1, sink_ref[1],
                                                     jnp.where(rb == 2, sink_ref[2], sink_ref[3])))

    def stacked_q(n):
        qa, qb = q_ref[n, :, 0:LANE], q_ref[n, :, LANE:2 * LANE]
        return jnp.concatenate([jnp.where(lo, qa, zero), jnp.where(lo, qb, zero),
                                jnp.where(lo, zero, qa), jnp.where(lo, zero, qb)], axis=0)

    def attend(key_refs, bias):
        scores, values = [], []
        for n in samples:
            kcat = jnp.concatenate([r[n, :, 0:LANE] for r in key_refs], axis=0)
            values.append(jnp.concatenate([r[n, :, LANE:2 * LANE] for r in key_refs], axis=0))
            s = lax.dot_general(stacked_q(n), kcat, (((1,), (1,)), ((), ())), preferred_element_type=F32)
            scores.append(s if bias is None else s + bias)
        for n in samples:
            s = scores[n]
            m = jnp.maximum(jnp.max(s, axis=-1, keepdims=True), sink)
            p = jnp.exp(s - m)
            l = jnp.sum(p, axis=-1, keepdims=True) + jnp.exp(sink - m)
            r = jnp.dot(p.astype(BF16), values[n], preferred_element_type=F32) / l
            o_ref[n, :, 0:LANE] = jnp.where(lo, r[0:blk], r[2 * blk:3 * blk]).astype(o_ref.dtype)
            o_ref[n, :, LANE:2 * LANE] = jnp.where(lo, r[blk:2 * blk], r[3 * blk:4 * blk]).astype(o_ref.dtype)

    def ctx_block():
        attend((kvc_ref,), None)

    def lat_block():
        attend((kvc_ref, kvp_ref, kvn_ref, kvx_ref), jnp.concatenate([bias_ref[0]] * SWA_HEADS, axis=0))

    pl.when(j < ncb)(ctx_block)
    pl.when(j >= ncb)(lat_block)


def _swa_mixer(qkv, sink, n_ctx, blk=128):
    q, kv_all = qkv
    B, S, _ = q.shape
    nb = S // blk
    ncb = n_ctx // blk
    cur = lambda b, j: (b, j, 0)
    prev = lambda b, j: (b, jnp.maximum(j - 1, ncb), 0)
    nxt = lambda b, j: (b, jnp.minimum(j + 1, nb - 1), 0)
    ns = math.gcd(B, SWA_SAMPLES)
    kv = lambda f: pl.BlockSpec((ns, blk, 2 * LANE), f)
    ctx = pl.BlockSpec((ns, n_ctx, 2 * LANE), lambda b, j: (b, 0, 0))
    t = np.arange(blk)[:, None]
    c = np.arange(n_ctx + 3 * blk)[None, :] - n_ctx
    bias = np.stack([np.where((c < 0) | ((c >= 0) & (c < blk) & (c >= t) & bool(hp)) | ((c >= blk) & (c < 2 * blk))
                              | ((c >= 2 * blk) & (c - 2 * blk <= t) & bool(hn)), 0.0, -1e30)
                     for hp in (0, 1) for hn in (0, 1)]).astype(np.float32)
    kind = lambda b, j: (2 * (j > ncb).astype(jnp.int32) + (j < nb - 1).astype(jnp.int32), 0, 0)
    return pl.pallas_call(
        functools.partial(_swa_attn_kernel, n_ctx=n_ctx, blk=blk),
        grid=(B // ns, nb),
        in_specs=[pl.BlockSpec(memory_space=pltpu.SMEM),
                  pl.BlockSpec((ns, blk, 256), cur), ctx, kv(prev), kv(cur), kv(nxt),
                  pl.BlockSpec((1, blk, n_ctx + 3 * blk), kind)],
        out_specs=pl.BlockSpec((ns, blk, GROUP_W), cur),
        out_shape=jax.ShapeDtypeStruct((B, S, GROUP_W), BF16),
        compiler_params=_cparams(("parallel", "arbitrary")),
        name="swa_attn",
    )(sink.astype(F32), q, kv_all, kv_all, kv_all, kv_all, jnp.asarray(bias))


def _bwd_chunk(j, ncc, nc):
    return jnp.where(j < ncc, ncc - 1 - j, nc - 1 - (j - ncc))


def _head_stack(x, width, heads):
    lane_h = lax.broadcasted_iota(jnp.int32, x.shape, 1) // width
    zero = jnp.zeros((), x.dtype)
    return jnp.concatenate([jnp.where(lane_h == h, x, zero) for h in range(heads)], axis=0)


def _head_unstack(y, rows, width, heads):
    lane_h = lax.broadcasted_iota(jnp.int32, (rows, heads * width), 1) // width
    out = y[0:rows]
    for h in range(1, heads):
        out = jnp.where(lane_h == h, y[h * rows:(h + 1) * rows], out)
    return out


RET_CHUNK = 256
RET_SAMPLES = 4


def _ret_tables(log_gamma, n_ctx, S):
    C = RET_CHUNK
    t = jnp.arange(C, dtype=F32)
    rel = t[:, None] - t[None, :]
    lg = log_gamma.astype(F32)
    lg_l = jnp.repeat(lg, RET_DK, axis=1)
    dm, xi, zeta, dcay = [], [], [], []
    for d in range(2):
        r = rel if d == 0 else -rel
        dh = jnp.where(r >= 0, jnp.exp(jnp.maximum(r, 0.0)[None] * lg[d][:, None, None]), 0.0)
        dm.append(dh.reshape(RET_HEADS * C, C))
        tq = (t + 1.0) if d == 0 else (C - t)
        tk = (C - 1.0 - t) if d == 0 else t
        xi.append(jnp.exp(tq[:, None] * lg_l[d][None, :]))
        zeta.append(jnp.exp(tk[:, None] * lg_l[d][None, :]))
        dcay.append(jnp.broadcast_to(jnp.exp(C * lg_l[d])[:, None], (RET_HEADS * RET_DK, GROUP_W)))
    mask = (np.arange(128)[:, None] // RET_DK == np.arange(256)[None, :] // RET_DV).astype(np.float32)
    pos = jnp.arange(S, dtype=F32)
    inv = ROPE_THETA ** (-jnp.arange(RET_DK // 2, dtype=F32) / (RET_DK // 2))
    ang = pos[:, None] * inv
    cos, sin = jnp.cos(ang), jnp.sin(ang)
    c = jnp.tile(jnp.concatenate([cos, cos], axis=-1), (1, RET_HEADS))
    s = jnp.tile(jnp.concatenate([-sin, sin], axis=-1), (1, RET_HEADS))
    return (jnp.stack(dm), jnp.stack(xi), jnp.stack(zeta), jnp.stack(dcay), jnp.asarray(mask), c, s)


def _ret_kernel(qkf_ref, vf_ref, cf_ref, sf_ref, qkb_ref, vb_ref, cb_ref, sb_ref,
                dm_ref, xi_ref, zeta_ref, dcay_ref, mask_ref, of_ref, ob_ref, state):
    C = RET_CHUNK

    @pl.when(pl.program_id(1) == 0)
    def _():
        state[...] = jnp.zeros_like(state)

    dirs = ((qkf_ref, vf_ref, cf_ref, sf_ref, of_ref), (qkb_ref, vb_ref, cb_ref, sb_ref, ob_ref))
    for n in range(qkf_ref.shape[0]):
        for d, (qk_ref, v_ref, c_ref, s_ref, o_ref) in enumerate(dirs):
            c, s = c_ref[...], s_ref[...]
            q, k = qk_ref[n, :, 0:128], qk_ref[n, :, 128:256]
            qr = (q * c + _rot_half(q, RET_DK) * s) * (RET_DK ** -0.5)
            kr = k * c + _rot_half(k, RET_DK) * s
            vb = v_ref[n].astype(BF16)
            qs = _head_stack(qr.astype(BF16), RET_DK, RET_HEADS)
            sc = lax.dot_general(qs, kr.astype(BF16), (((1,), (1,)), ((), ())), preferred_element_type=F32)
            a = (sc * dm_ref[d]).astype(BF16)
            o = _head_unstack(jnp.dot(a, vb, preferred_element_type=F32), C, RET_DV, RET_HEADS)
            st = state[n, d]
            o = o + jnp.dot((qr * xi_ref[d]).astype(BF16), st.astype(BF16), preferred_element_type=F32)
            o_ref[n] = o.astype(o_ref.dtype)
            kz = (kr * zeta_ref[d]).astype(BF16)
            u = lax.dot_general(kz, vb, (((0,), (0,)), ((), ())), preferred_element_type=F32)
            state[n, d] = st * dcay_ref[d] + u * mask_ref[...]


def _ret_mixer(p_ret, tables, n_ctx):
    B, S, _ = p_ret.shape
    C = RET_CHUNK
    dm, xi, zeta, dcay, mask, c, s = tables
    nc, ncc = S // C, n_ctx // C
    fwd = lambda b, j: (b, j, 0)
    bwd = lambda b, j: (b, _bwd_chunk(j, ncc, nc), 0)
    fwd_v = lambda b, j: (b, j, 1)
    bwd_v = lambda b, j: (b, _bwd_chunk(j, ncc, nc), 1)
    fwd_t = lambda b, j: (j, 0)
    bwd_t = lambda b, j: (_bwd_chunk(j, ncc, nc), 0)
    full = lambda a: pl.BlockSpec(a.shape, lambda b, j: (0,) * a.ndim)
    ns = math.gcd(B, RET_SAMPLES)
    return pl.pallas_call(
        _ret_kernel,
        grid=(B // ns, nc),
        in_specs=[pl.BlockSpec((ns, C, 256), fwd), pl.BlockSpec((ns, C, 256), fwd_v),
                  pl.BlockSpec((C, LANE), fwd_t), pl.BlockSpec((C, LANE), fwd_t),
                  pl.BlockSpec((ns, C, 256), bwd), pl.BlockSpec((ns, C, 256), bwd_v),
                  pl.BlockSpec((C, LANE), bwd_t), pl.BlockSpec((C, LANE), bwd_t),
                  full(dm), full(xi), full(zeta), full(dcay), full(mask)],
        out_specs=[pl.BlockSpec((ns, C, GROUP_W), fwd), pl.BlockSpec((ns, C, GROUP_W), bwd)],
        out_shape=[jax.ShapeDtypeStruct((B, S, GROUP_W), BF16)] * 2,
        scratch_shapes=[pltpu.VMEM((ns, 2, RET_HEADS * RET_DK, GROUP_W), F32)],
        compiler_params=_cparams(("parallel", "arbitrary")),
        name="retention_scan",
    )(p_ret, p_ret, c, s, p_ret, p_ret, c, s, dm, xi, zeta, dcay, mask)


HG_CHUNK = 128
HG_SAMPLES = 4


def _hg_level_tables(C):
    levels = []
    n = 2
    while n <= C:
        levels.append(n)
        n *= 2
    t = np.arange(C)
    mats, qroles, kroles = [], [], []
    for d in range(2):
        rows = []
        qr_d, kr_d = [], []
        for n in levels:
            m = n // 2
            start = (t // n) * n
            mid = start + m
            u = t - start
            M = np.zeros((C, C), np.float32)
            for i in range(C):
                if d == 0:
                    if u[i] >= m:
                        M[i, mid[i]:i + 1] = 1.0
                    else:
                        M[i, i + 1:mid[i]] = 1.0
                else:
                    if u[i] < m:
                        M[i, i:mid[i]] = 1.0
                    else:
                        M[i, mid[i]:i] = 1.0
            rows.append(M)
            qr_d.append((u >= m) if d == 0 else (u < m))
            kr_d.append((u < m) if d == 0 else (u >= m))
        Mq = np.zeros((C, C), np.float32)
        Mk = np.zeros((C, C), np.float32)
        for i in range(C):
            if d == 0:
                Mq[i, 0:i + 1] = 1.0
                Mk[i, i + 1:C] = 1.0
            else:
                Mq[i, i:C] = 1.0
                Mk[i, 0:i] = 1.0
        rows += [Mq, Mk, np.ones((8, C), np.float32)]
        mats.append(np.concatenate(rows, axis=0))
        qroles.append(np.stack(qr_d))
        kroles.append(np.stack(kr_d))
    same_block = np.stack([(t[:, None] // n == t[None, :] // n) for n in levels]).astype(np.float32)
    return levels, np.stack(mats), np.stack(qroles), np.stack(kroles), same_block


def _hg_kernel(qf_ref, zf_ref, if_ref, qb_ref, zb_ref, ib_ref, lb_ref, m_ref, role_ref, blk_ref,
               ones_ref, mask_ref, of_ref, ob_ref, state, *, n_levels):
    C = HG_CHUNK

    @pl.when(pl.program_id(1) == 0)
    def _():
        state[...] = jnp.zeros_like(state)

    dirs = ((qf_ref, zf_ref, if_ref), (qb_ref, zb_ref, ib_ref))
    chains = [(s, d) for s in range(qf_ref.shape[0]) for d in range(2)]
    st = [state[s, d] for s, d in chains]
    q, v, vb, kk, r = [], [], [], [], []
    for c, (s, d) in enumerate(chains):
        q_ref, z_ref, i_ref = dirs[d]
        z = z_ref[s]
        lb = lb_ref[d]
        lbf = jnp.maximum(lb, LB_FLOOR)
        t = jnp.exp(-jnp.abs(z))
        rcp = 1.0 / (1.0 + t)
        pos = z >= 0.0
        sig = jnp.where(pos, rcp, t * rcp)
        nsig = jnp.where(pos, t * rcp, rcp)
        logf = jnp.log(lbf + (1.0 - lb) * sig)
        kk.append((1.0 - lb) * nsig - (lbf - lb))
        hi, lo = _split2(logf)
        r.append(jnp.dot(m_ref[d], hi, preferred_element_type=F32)
                 + jnp.dot(m_ref[d], lo, preferred_element_type=F32))
        q.append(q_ref[s] * (HG_DK ** -0.5))
        v.append(i_ref[s])
        vb.append(v[c].astype(BF16))
    acc = [jnp.zeros((HG_HEADS * C, C), F32) for _ in chains]
    for lv in range(n_levels):
        for c, (s, d) in enumerate(chains):
            e = jnp.exp(r[c][lv * C:(lv + 1) * C])
            roles = role_ref[d, lv]
            qe = jnp.where(roles > 0.5, q[c] * e, 0.0).astype(BF16)
            ke = jnp.where(roles > 0.5, 0.0, kk[c] * e).astype(BF16)
            a = lax.dot_general(_head_stack(qe, HG_DK, HG_HEADS), ke, (((1,), (1,)), ((), ())),
                                preferred_element_type=F32)
            acc[c] = acc[c] + a * blk_ref[lv]
    outs = []
    for c in range(len(chains)):
        o = _head_unstack(jnp.dot(acc[c].astype(BF16), vb[c], preferred_element_type=F32), C, HG_DV, HG_HEADS)
        o = o + jnp.dot((q[c] * kk[c]).astype(BF16), ones_ref[...], preferred_element_type=F32) * v[c]
        e_q = jnp.exp(r[c][n_levels * C:(n_levels + 1) * C])
        o = o + lax.dot_general((q[c] * e_q).astype(BF16), st[c].astype(BF16), (((1,), (1,)), ((), ())),
                                preferred_element_type=F32)
        outs.append(o)
    new_state = []
    for c in range(len(chains)):
        e_k = jnp.exp(r[c][(n_levels + 1) * C:(n_levels + 2) * C])
        ku = (kk[c] * e_k).astype(BF16)
        u = lax.dot_general(vb[c], ku, (((0,), (0,)), ((), ())), preferred_element_type=F32)
        dtot = jnp.exp(r[c][(n_levels + 2) * C:(n_levels + 2) * C + 1])
        new_state.append(st[c] * dtot + u * mask_ref[...])
    for c, (s, d) in enumerate(chains):
        o_ref = (of_ref, ob_ref)[d]
        o_ref[s] = outs[c].astype(o_ref.dtype)
        state[s, d] = new_state[c]


def _hg_mixer(p_hg, lb, n_ctx):
    B, S, _ = p_hg.shape
    C = HG_CHUNK
    levels, mats, qroles, kroles, same_block = _hg_level_tables(C)
    nl = len(levels)
    m_all = jnp.asarray(mats, BF16)
    roles = jnp.asarray(np.broadcast_to(qroles[..., None], (2, nl, C, GROUP_W)).astype(np.float32))
    blk = jnp.asarray(np.tile(same_block, (1, HG_HEADS, 1)))
    hd = np.arange(GROUP_W) // HG_DK
    bd = (hd[:, None] == hd[None, :]).astype(np.float32)
    ones_bd = jnp.asarray(bd, BF16)
    mask = jnp.asarray(bd)
    nc, ncc = S // C, n_ctx // C
    col = lambda k, rev: (lambda b, j: (b, _bwd_chunk(j, ncc, nc) if rev else j, k))
    ns = math.gcd(B, HG_SAMPLES)
    blkspec = lambda k, rev: pl.BlockSpec((ns, C, GROUP_W), col(k, rev))
    full = lambda a: pl.BlockSpec(a.shape, lambda b, j: (0,) * a.ndim)
    lb3 = lb.reshape(2, 1, GROUP_W).astype(F32)
    return pl.pallas_call(
        functools.partial(_hg_kernel, n_levels=nl),
        grid=(B // ns, nc),
        in_specs=[blkspec(0, False), blkspec(1, False), blkspec(3, False),
                  blkspec(0, True), blkspec(2, True), blkspec(3, True),
                  full(lb3), full(m_all), full(roles), full(blk), full(ones_bd), full(mask)],
        out_specs=[blkspec(0, False), blkspec(0, True)],
        out_shape=[jax.ShapeDtypeStruct((B, S, GROUP_W), BF16)] * 2,
        scratch_shapes=[pltpu.VMEM((ns, 2, GROUP_W, GROUP_W), F32)],
        compiler_params=_cparams(("parallel", "arbitrary")),
        name="hgrn2_scan",
    )(p_hg, p_hg, p_hg, p_hg, p_hg, p_hg, lb3, m_all, roles, blk, ones_bd, mask)


_SWA_OUT_PERM = np.concatenate([np.arange(64 * h, 64 * h + 64) for h in (0, 2, 1, 3)])


def _wout_kernel(h_ref, m_ref, a_ref, hf_ref, hb_ref, hgate_ref, c_ref, rf_ref, rb_ref, rgate_ref,
                 hgn_ref, rgg_ref, rgb_ref, ones_ref, w_ref, o_ref, *, tm, n_ctx, row_off):
    row0 = row_off + pl.program_id(1) * tm
    ones = ones_ref[...]
    o = hf_ref[0].astype(F32) + hb_ref[0].astype(F32)
    yb = o * lax.rsqrt(_seg_mean(o * o, ones) + NORM_EPS) * hgn_ref[...] * _silu(hgate_ref[0])
    o = rf_ref[0].astype(F32) + rb_ref[0].astype(F32)
    xc = o - _seg_mean(o, ones)
    yd = (xc * lax.rsqrt(_seg_mean(xc * xc, ones) + NORM_EPS) * rgg_ref[...] + rgb_ref[...]) * _silu(rgate_ref[0])
    acc = jnp.dot(a_ref[0].astype(BF16), w_ref[0:256, :], preferred_element_type=F32)
    acc += jnp.dot(yb.astype(BF16), w_ref[256:512, :], preferred_element_type=F32)
    acc += jnp.dot(c_ref[0].astype(BF16), w_ref[512:768, :], preferred_element_type=F32)
    acc += jnp.dot(yd.astype(BF16), w_ref[768:1024, :], preferred_element_type=F32)
    o_ref[0] = h_ref[0] + _row_mod(m_ref, 2, row0, tm, n_ctx) * acc


def _out_proj(h, mod, o_mla, hg_f, hg_b, p_hg, o_swa, ret_f, ret_b, p_ret, hg_norm_g, ret_gn_g, ret_gn_b,
              w_out_p, n_ctx, tm=768, row_off=0):
    B, S, D = h.shape
    assert row_off % tm == 0 and (S - row_off) % tm == 0
    off = row_off // tm
    hd = np.arange(GROUP_W) // HG_DV
    ones_bd = jnp.asarray((hd[:, None] == hd[None, :]).astype(np.float32) / HG_DV, BF16)
    row = lambda w, k=0: pl.BlockSpec((1, tm, w), lambda b, i: (b, i + off, k))
    vec = pl.BlockSpec((1, GROUP_W), lambda b, i: (0, 0))
    return pl.pallas_call(
        functools.partial(_wout_kernel, tm=tm, n_ctx=n_ctx, row_off=row_off),
        grid=(B, (S - row_off) // tm),
        in_specs=[row(D), pl.BlockSpec((1, 2, 8, D), lambda b, i: (b, 0, 0, 0)),
                  row(GROUP_W), row(GROUP_W), row(GROUP_W), row(GROUP_W, 4),
                  row(GROUP_W), row(GROUP_W), row(GROUP_W), row(GROUP_W, 2),
                  vec, vec, vec, pl.BlockSpec((GROUP_W, GROUP_W), lambda b, i: (0, 0)),
                  pl.BlockSpec((D, D), lambda b, i: (0, 0))],
        out_specs=pl.BlockSpec((1, tm, D), lambda b, i: (b, i, 0)),
        out_shape=jax.ShapeDtypeStruct((B, S - row_off, D), F32),
        compiler_params=_cparams(("parallel", "parallel")),
        name="out_proj",
    )(h, mod, o_mla, hg_f, hg_b, p_hg, o_swa, ret_f, ret_b, p_ret,
      hg_norm_g.reshape(1, GROUP_W), ret_gn_g.reshape(1, GROUP_W), ret_gn_b.reshape(1, GROUP_W), ones_bd, w_out_p)


def _ffn_kernel(h_ref, m_ref, g_ref, wg_ref, wu_ref, wd_ref, o_ref, xn, *, tm, n_ctx):
    j = pl.program_id(2)
    row0 = pl.program_id(1) * tm

    @pl.when(j == 0)
    def _():
        xn[...] = _norm_mod(h_ref[0], g_ref[...], _row_mod(m_ref, 3, row0, tm, n_ctx),
                            _row_mod(m_ref, 4, row0, tm, n_ctx)).astype(BF16)
        o_ref[...] = jnp.zeros_like(o_ref)

    x = xn[...]
    hg = jnp.dot(x, wg_ref[0], preferred_element_type=F32)
    hu = jnp.dot(x, wu_ref[0], preferred_element_type=F32)
    o_ref[0] += jnp.dot((_silu(hg) * hu).astype(BF16), wd_ref[0], preferred_element_type=F32)

    @pl.when(j == pl.num_programs(2) - 1)
    def _():
        o_ref[0] = h_ref[0] + _row_mod(m_ref, 5, row0, tm, n_ctx) * o_ref[0]


def _dense_ffn(h, mod, g, w_gate, w_up, w_down, li, n_ctx, tm=1152, tf=1408):
    B, S, D = h.shape
    dff = w_gate.shape[-1]
    return pl.pallas_call(
        functools.partial(_ffn_kernel, tm=tm, n_ctx=n_ctx),
        grid=(B, S // tm, dff // tf),
        in_specs=[pl.BlockSpec((1, tm, D), lambda b, i, j: (b, i, 0)),
                  pl.BlockSpec((1, 2, 8, D), lambda b, i, j: (b, 0, 0, 0)),
                  pl.BlockSpec((1, D), lambda b, i, j: (0, 0)),
                  pl.BlockSpec((1, D, tf), lambda b, i, j: (li, 0, j)),
                  pl.BlockSpec((1, D, tf), lambda b, i, j: (li, 0, j)),
                  pl.BlockSpec((1, tf, D), lambda b, i, j: (li, j, 0))],
        out_specs=pl.BlockSpec((1, tm, D), lambda b, i, j: (b, i, 0)),
        out_shape=jax.ShapeDtypeStruct((B, S, D), F32),
        scratch_shapes=[pltpu.VMEM((tm, D), BF16)],
        compiler_params=_cparams(("parallel", "parallel", "arbitrary")),
        name="dense_ffn",
    )(h, mod, g.reshape(1, D), w_gate, w_up, w_down)


MOE_TM = 1024
MOE_TF = 512
MOE_ROWS = (384, 256)
DMA_UNROLL = 8
SUBLANES = 8
ZERO_ROWS = 256


def _router_kernel(h_ref, m_ref, g_ref, wr_ref, tri_ref, xn_ref, route_ref, cnt_ref, counts, *, tm, n_ctx):
    b, i = pl.program_id(0), pl.program_id(1)
    row0 = i * tm

    @pl.when((b == 0) & (i == 0))
    def _():
        counts[...] = jnp.zeros_like(counts)

    a = _norm_mod(h_ref[0], g_ref[...], _row_mod(m_ref, 3, row0, tm, n_ctx), _row_mod(m_ref, 4, row0, tm, n_ctx))
    xn_ref[0] = a
    ah, al = _split2(a)
    wh, wl = _split2(wr_ref[...])
    logits = (jnp.dot(ah, wh, preferred_element_type=F32) + jnp.dot(ah, wl, preferred_element_type=F32)
              + jnp.dot(al, wh, preferred_element_type=F32))
    lane = lax.broadcasted_iota(jnp.int32, logits.shape, 1)
    neg = jnp.float32(-jnp.inf)
    lg = jnp.where(lane < N_EXPERTS, logits, neg)
    v1 = jnp.max(lg, axis=-1, keepdims=True)
    i1 = jnp.min(jnp.where(lg == v1, lane, LANE), axis=-1, keepdims=True)
    lg2 = jnp.where(lane == i1, neg, lg)
    v2 = jnp.max(lg2, axis=-1, keepdims=True)
    i2 = jnp.min(jnp.where(lg2 == v2, lane, LANE), axis=-1, keepdims=True)
    e = jnp.exp(v2 - v1)
    w1 = 1.0 / (1.0 + e)
    w2 = e / (1.0 + e)
    oh1, oh2 = lane == i1, lane == i2
    onehot = jnp.where(oh1 | oh2, 1.0, 0.0)
    earlier = jnp.dot(tri_ref[...], onehot.astype(BF16), preferred_element_type=F32)
    pos = earlier + counts[...]
    r1 = jnp.sum(jnp.where(oh1, pos, 0.0), axis=-1, keepdims=True)
    r2 = jnp.sum(jnp.where(oh2, pos, 0.0), axis=-1, keepdims=True)
    counts[...] = counts[...] + jnp.sum(onehot, axis=0, keepdims=True)
    cnt_ref[...] = jnp.broadcast_to(counts[...], cnt_ref.shape)
    route_ref[0] = jnp.where(lane == 0, i1.astype(F32),
                             jnp.where(lane == 1, i2.astype(F32),
                                       jnp.where(lane == 2, w1,
                                                 jnp.where(lane == 3, w2,
                                                           jnp.where(lane == 4, r1, jnp.where(lane == 5, r2, 0.0))))))


def _scatter_kernel(meta_ref, slot_ref, x_ref, xs_hbm, zeros_v, sem, *, tm, tile_rows, total_rows):
    b, i = pl.program_id(0), pl.program_id(1)

    def start(r, carry):
        for k in range(TOP_K):
            pltpu.make_async_copy(x_ref.at[0, pl.ds(r, 1)], xs_hbm.at[pl.ds(slot_ref[0, 0, 0, k * tm + r], 1)],
                                  sem).start()
        return carry

    lax.fori_loop(0, tm, start, 0, unroll=DMA_UNROLL)
    for k in range(TOP_K):
        pltpu.make_async_copy(x_ref.at[0], xs_hbm.at[pl.ds(0, tm)], sem).wait()

    @pl.when((b == pl.num_programs(0) - 1) & (i == pl.num_programs(1) - 1))
    def _():
        zeros_v[...] = jnp.zeros_like(zeros_v)
        z1 = zeros_v.at[pl.ds(0, 1)]
        z8 = zeros_v.at[pl.ds(0, SUBLANES)]

        def fill(lo, end):
            mid = jnp.minimum(((lo + SUBLANES - 1) // SUBLANES) * SUBLANES, end)
            big = mid + ((end - mid) // ZERO_ROWS) * ZERO_ROWS
            n_row, n_big, n_small = mid - lo, (big - mid) // ZERO_ROWS, (end - big) // SUBLANES

            def start_row(r, c):
                pltpu.make_async_copy(z1, xs_hbm.at[pl.ds(lo + r, 1)], sem).start()
                return c

            def wait_row(r, c):
                pltpu.make_async_copy(z1, xs_hbm.at[pl.ds(0, 1)], sem).wait()
                return c

            def start_big(q, c):
                dst = xs_hbm.at[pl.ds(pl.multiple_of(mid + q * ZERO_ROWS, SUBLANES), ZERO_ROWS)]
                pltpu.make_async_copy(zeros_v, dst, sem).start()
                return c

            def wait_big(q, c):
                pltpu.make_async_copy(zeros_v, xs_hbm.at[pl.ds(0, ZERO_ROWS)], sem).wait()
                return c

            def start_small(q, c):
                dst = xs_hbm.at[pl.ds(pl.multiple_of(big + q * SUBLANES, SUBLANES), SUBLANES)]
                pltpu.make_async_copy(z8, dst, sem).start()
                return c

            def wait_small(q, c):
                pltpu.make_async_copy(z8, xs_hbm.at[pl.ds(0, SUBLANES)], sem).wait()
                return c

            lax.fori_loop(0, n_row, start_row, 0)
            lax.fori_loop(0, n_big, start_big, 0)
            lax.fori_loop(0, n_small, start_small, 0)
            lax.fori_loop(0, n_row, wait_row, 0)
            lax.fori_loop(0, n_big, wait_big, 0)
            lax.fori_loop(0, n_small, wait_small, 0)

        for e in range(N_EXPERTS):
            c = meta_ref[e]
            g0 = meta_ref[N_EXPERTS + e]
            fill(g0 + c, g0 + ((c + tile_rows - 1) // tile_rows) * tile_rows)
        fill(meta_ref[2 * N_EXPERTS] * tile_rows, jnp.int32(total_rows))


def _expert_kernel(te_ref, nu_ref, x_ref, wg_ref, wu_ref, wd_ref, o_ref, xb):
    i, j = pl.program_id(0), pl.program_id(1)

    @pl.when(j == 0)
    def _():
        xb[...] = x_ref[...].astype(BF16)
        o_ref[...] = jnp.zeros_like(o_ref)

    @pl.when(i < nu_ref[0])
    def _():
        wg, wu, wd = wg_ref[0, 0].astype(BF16), wu_ref[0, 0].astype(BF16), wd_ref[0, 0].astype(BF16)
        x = xb[...]
        hg = jnp.dot(x, wg, preferred_element_type=F32)
        hu = jnp.dot(x, wu, preferred_element_type=F32)
        o_ref[...] += jnp.dot((_silu(hg) * hu).astype(BF16), wd, preferred_element_type=F32)


def _combine_kernel(slot_ref, h_ref, m_ref, route_ref, fg_ref, y_hbm, o_ref, buf, sem, *, tm, n_ctx, final_norm):
    row0 = pl.program_id(1) * tm

    def start(r, carry):
        for k in range(TOP_K):
            pltpu.make_async_copy(y_hbm.at[pl.ds(slot_ref[0, 0, 0, k * tm + r], 1)], buf.at[k, pl.ds(r, 1)], sem).start()
        return carry

    lax.fori_loop(0, tm, start, 0, unroll=DMA_UNROLL)
    for k in range(TOP_K):
        pltpu.make_async_copy(y_hbm.at[pl.ds(0, tm)], buf.at[k], sem).wait()
    w1 = route_ref[0, :, 2:3]
    w2 = route_ref[0, :, 3:4]
    y = h_ref[0] + _row_mod(m_ref, 5, row0, tm, n_ctx) * (w1 * buf[0] + w2 * buf[1])
    if final_norm:
        y = y * lax.rsqrt(jnp.mean(y * y, axis=-1, keepdims=True) + NORM_EPS) * fg_ref[...]
    o_ref[0] = y


def _moe_ffn(h, mod, g, w_router, w_gate, w_up, w_down, li, n_ctx, final_g=None):
    B, S, D = h.shape
    T = B * S
    cr = next(r for r in MOE_ROWS if S % r == 0)
    tm = cr
    dffe = w_gate.shape[-1]
    te_rows = MOE_TM if T % MOE_TM == 0 else 512
    n_tiles = TOP_K * T // te_rows + N_EXPERTS
    P = n_tiles * te_rows
    wr = jnp.pad(w_router, ((0, 0), (0, LANE - N_EXPERTS)))
    tri = jnp.asarray(np.tril(np.ones((tm, tm), np.float32), -1), BF16)
    xn, route, cnt = pl.pallas_call(
        functools.partial(_router_kernel, tm=tm, n_ctx=n_ctx),
        grid=(B, S // tm),
        in_specs=[pl.BlockSpec((1, tm, D), lambda b, i: (b, i, 0)),
                  pl.BlockSpec((1, 2, 8, D), lambda b, i: (b, 0, 0, 0)),
                  pl.BlockSpec((1, D), lambda b, i: (0, 0)),
                  pl.BlockSpec((D, LANE), lambda b, i: (0, 0)),
                  pl.BlockSpec((tm, tm), lambda b, i: (0, 0))],
        out_specs=[pl.BlockSpec((1, tm, D), lambda b, i: (b, i, 0)),
                   pl.BlockSpec((1, tm, LANE), lambda b, i: (b, i, 0)),
                   pl.BlockSpec((SUBLANES, LANE), lambda b, i: (0, 0))],
        out_shape=[jax.ShapeDtypeStruct((B, S, D), F32), jax.ShapeDtypeStruct((B, S, LANE), F32),
                   jax.ShapeDtypeStruct((SUBLANES, LANE), F32)],
        scratch_shapes=[pltpu.VMEM((1, LANE), F32)],
        compiler_params=_cparams(("arbitrary", "arbitrary")),
        name="moe_router",
    )(h, mod, g.reshape(1, D), wr, tri)

    counts = cnt[0, :N_EXPERTS].astype(jnp.int32)
    ptiles = (counts + te_rows - 1) // te_rows
    tile_end = jnp.cumsum(ptiles)
    gstart = (tile_end - ptiles) * te_rows
    n_used = tile_end[-1:].astype(jnp.int32)
    tile_ids = jnp.arange(n_tiles, dtype=jnp.int32)
    tile_e = jnp.minimum(jnp.sum((tile_ids[:, None] >= tile_end[None, :]).astype(jnp.int32), axis=1),
                         N_EXPERTS - 1)
    meta = jnp.concatenate([counts, gstart, n_used])
    choice = route[..., 0:TOP_K].astype(jnp.int32)
    is_e = choice[..., None] == jnp.arange(N_EXPERTS, dtype=jnp.int32)
    slots = jnp.sum(jnp.where(is_e, gstart, 0), axis=-1) + route[..., 4:4 + TOP_K].astype(jnp.int32)
    slots = slots.reshape(B, S // cr, cr, TOP_K).transpose(0, 1, 3, 2)
    slots = slots.reshape(B, S // cr, 1, TOP_K * cr)

    xs = pl.pallas_call(
        functools.partial(_scatter_kernel, tm=cr, tile_rows=te_rows, total_rows=P),
        grid_spec=pltpu.PrefetchScalarGridSpec(
            num_scalar_prefetch=1,
            grid=(B, S // cr),
            in_specs=[pl.BlockSpec((1, 1, 1, TOP_K * cr), lambda b, i, mt: (b, i, 0, 0),
                                   memory_space=pltpu.SMEM),
                      pl.BlockSpec((1, cr, D), lambda b, i, mt: (b, i, 0))],
            out_specs=pl.BlockSpec(memory_space=pl.ANY),
            scratch_shapes=[pltpu.VMEM((ZERO_ROWS, D), F32), pltpu.SemaphoreType.DMA(())]),
        out_shape=jax.ShapeDtypeStruct((P, D), F32),
        compiler_params=_cparams(("arbitrary", "arbitrary")),
        name="moe_scatter",
    )(meta, slots, xn)

    nf = dffe // MOE_TF
    last = lambda i, nu: jnp.minimum(i, nu[0] - 1)
    jj = lambda i, j, nu: jnp.where(i < nu[0], j, nf - 1)
    ys = pl.pallas_call(
        _expert_kernel,
        grid_spec=pltpu.PrefetchScalarGridSpec(
            num_scalar_prefetch=2,
            grid=(n_tiles, nf),
            in_specs=[pl.BlockSpec((te_rows, D), lambda i, j, te, nu: (last(i, nu), 0)),
                      pl.BlockSpec((1, 1, D, MOE_TF), lambda i, j, te, nu: (li, te[last(i, nu)], 0, jj(i, j, nu))),
                      pl.BlockSpec((1, 1, D, MOE_TF), lambda i, j, te, nu: (li, te[last(i, nu)], 0, jj(i, j, nu))),
                      pl.BlockSpec((1, 1, MOE_TF, D), lambda i, j, te, nu: (li, te[last(i, nu)], jj(i, j, nu), 0))],
            out_specs=pl.BlockSpec((te_rows, D), lambda i, j, te, nu: (i, 0)),
            scratch_shapes=[pltpu.VMEM((te_rows, D), BF16)]),
        out_shape=jax.ShapeDtypeStruct((P, D), F32),
        compiler_params=_cparams(("arbitrary", "arbitrary")),
        name="moe_experts",
    )(tile_e, n_used, xs, w_gate, w_up, w_down)
    fg = jnp.ones((1, D), F32) if final_g is None else final_g.reshape(1, D).astype(F32)
    return pl.pallas_call(
        functools.partial(_combine_kernel, tm=cr, n_ctx=n_ctx, final_norm=final_g is not None),
        grid=(B, S // cr),
        in_specs=[pl.BlockSpec((1, 1, 1, TOP_K * cr), lambda b, i: (b, i, 0, 0), memory_space=pltpu.SMEM),
                  pl.BlockSpec((1, cr, D), lambda b, i: (b, i, 0)),
                  pl.BlockSpec((1, 2, 8, D), lambda b, i: (b, 0, 0, 0)),
                  pl.BlockSpec((1, cr, LANE), lambda b, i: (b, i, 0)),
                  pl.BlockSpec((1, D), lambda b, i: (0, 0)),
                  pl.BlockSpec(memory_space=pl.ANY)],
        out_specs=pl.BlockSpec((1, cr, D), lambda b, i: (b, i, 0)),
        out_shape=jax.ShapeDtypeStruct((B, S, D), F32),
        scratch_shapes=[pltpu.VMEM((TOP_K, cr, D), F32), pltpu.SemaphoreType.DMA(())],
        compiler_params=_cparams(("arbitrary", "arbitrary")),
        name="moe_combine",
    )(slots, h, mod, route, fg, ys)


def _final_norm_kernel(h_ref, g_ref, o_ref):
    x = h_ref[0]
    o_ref[0] = x * lax.rsqrt(jnp.mean(x * x, axis=-1, keepdims=True) + NORM_EPS) * g_ref[...]


def _final_norm(h, g, n_ctx, tm=256):
    B, S, D = h.shape
    N = S - n_ctx
    return pl.pallas_call(
        _final_norm_kernel,
        grid=(B, N // tm),
        in_specs=[pl.BlockSpec((1, tm, D), lambda b, i: (b, n_ctx // tm + i, 0)),
                  pl.BlockSpec((1, D), lambda b, i: (0, 0))],
        out_specs=pl.BlockSpec((1, tm, D), lambda b, i: (b, i, 0)),
        out_shape=jax.ShapeDtypeStruct((B, N, D), F32),
        compiler_params=_cparams(("parallel", "parallel")),
        name="final_norm",
    )(h, g.reshape(1, D))


MOD_ROWS = 16


def kernel(x, c, ctx, c_ctx, w_ada, b_ada, norm_mix_g, norm_ffn_g, w_in, w_out, mla_q_norm_g, mla_w_uq, mla_kv_norm_g, mla_w_ukv, hg_lb_logits, hg_norm_g, swa_sink, ret_decay_logit, ret_gn_g, ret_gn_b, ffn_w_gate, ffn_w_up, ffn_w_down, moe_w_router, moe_w_gate, moe_w_up, moe_w_down, final_norm_g):
    B, N, D = x.shape
    L = ctx.shape[1]
    S = L + N
    assert B + 1 <= MOD_ROWS
    h = jnp.concatenate([ctx, x], axis=1)
    cvec = jnp.zeros((MOD_ROWS, D), F32).at[0].set(c_ctx).at[1:1 + B].set(c)
    mod_all = _modulation(cvec, w_ada, b_ada).reshape(DEPTH, MOD_ROWS, 6, D)

    lb_p = jax.nn.softmax(hg_lb_logits.astype(F32), axis=0)
    hg_lb = jnp.cumsum(lb_p, axis=0) - lb_p[0:1]
    ret_log_gamma = jax.nn.log_sigmoid(ret_decay_logit.astype(F32))

    mla_tab = _mla_tables(N, L)
    swa_tab = _swa_tables(N, L)
    w_out_perm = np.concatenate([np.arange(512), 512 + _SWA_OUT_PERM, np.arange(768, 1024)])
    ffn_bf16 = tuple(w.astype(BF16) for w in (ffn_w_gate, ffn_w_up, ffn_w_down))

    for l in range(DEPTH):
        m = mod_all[l]
        mod = jnp.stack([jnp.broadcast_to(m[0], (B, 6, D)), m[1:1 + B]], axis=1)
        mod = jnp.pad(mod, ((0, 0), (0, 0), (0, 2), (0, 0)))
        w_ext = _take_cols(w_in[l], _EXT_COLS).astype(BF16)
        mla_w = _mla_weights(mla_q_norm_g[l], mla_w_uq[l], mla_kv_norm_g[l], mla_w_ukv[l])
        mla_qkv, p_hg, swa_qkv, p_ret = _in_proj(h, mod, norm_mix_g[l], w_ext, mla_w, mla_tab, swa_tab, L)
        o_mla = _mla_mixer(mla_qkv, L)
        hg_f, hg_b = _hg_mixer(p_hg, hg_lb[l], L)
        o_swa = _swa_mixer(swa_qkv, swa_sink[l], L)
        ret_f, ret_b = _ret_mixer(p_ret, _ret_tables(ret_log_gamma[l], L, S), L)
        w_out_p = _take_static(w_out[l], w_out_perm, 0).astype(BF16)
        last = l == DEPTH - 1
        if last and l % 2 == 1:
            h = _out_proj(h, mod, o_mla, hg_f, hg_b, p_hg, o_swa, ret_f, ret_b, p_ret,
                          hg_norm_g[l], ret_gn_g[l], ret_gn_b[l], w_out_p, L, tm=math.gcd(L, N, 256), row_off=L)
            return _moe_ffn(h, mod, norm_ffn_g[l], moe_w_router[l // 2], moe_w_gate, moe_w_up, moe_w_down, l // 2,
                            0, final_g=final_norm_g)
        h = _out_proj(h, mod, o_mla, hg_f, hg_b, p_hg, o_swa, ret_f, ret_b, p_ret,
                      hg_norm_g[l], ret_gn_g[l], ret_gn_b[l], w_out_p, L)
        if l % 2 == 0:
            h = _dense_ffn(h, mod, norm_ffn_g[l], ffn_bf16[0], ffn_bf16[1], ffn_bf16[2], l // 2, L,
                           tm=1152 if S % 1152 == 0 else 384)
        else:
            h = _moe_ffn(h, mod, norm_ffn_g[l], moe_w_router[l // 2], moe_w_gate, moe_w_up, moe_w_down, l // 2, L)
    return _final_norm(h, final_norm_g, L)
```

```python
import functools
import math

import numpy as np
import jax
import jax.numpy as jnp
from jax import lax
from jax.experimental import pallas as pl
from jax.experimental.pallas import tpu as pltpu

F32 = jnp.float32
BF16 = jnp.bfloat16

D_MODEL = 1024
DEPTH = 4
GRID_W = 64
ROPE_THETA = 10000.0
NORM_EPS = 1e-6
LB_FLOOR = 1e-30
GROUP_W = 256

MLA_HEADS, MLA_NOPE, MLA_ROPE, MLA_V = 4, 64, 32, 64
MLA_Q_RANK, MLA_KV_RANK = 192, 128
HG_HEADS, HG_DK, HG_DV = 4, 64, 64
SWA_HEADS, SWA_KV_HEADS, SWA_HD, SWA_WINDOW = 4, 2, 64, 128
RET_HEADS, RET_DK, RET_DV = 4, 32, 64
N_EXPERTS, TOP_K = 8, 2

_IN_SIZES = (192, 128, 32, 256, 256, 256, 256, 256, 256, 128, 128, 128, 128, 256, 256)
_OFF = np.concatenate([[0], np.cumsum(_IN_SIZES)]).astype(np.int64)
D_IN = int(_OFF[-1])

W_MLA, W_HG, W_SWA, W_RET = 512, 1280, 512, 768
W_EXT = W_MLA + W_HG + W_SWA + W_RET

LANE = 128
VMEM_LIMIT = 56 * 1024 * 1024


def _cparams(sem):
    return pltpu.CompilerParams(dimension_semantics=sem, vmem_limit_bytes=VMEM_LIMIT)


def _swap_halves(idx, width):
    idx = np.asarray(idx)
    base = (idx // width) * width
    j = idx % width
    return base + (j + width // 2) % width


def _ext_columns():
    z = lambda n: -np.ones((n,), np.int64)
    rng = lambda a, n: np.arange(a, a + n)
    o = _OFF
    k_pe = rng(o[2], 32)
    mla = np.concatenate([rng(o[0], 192), z(64), rng(o[1], 128), k_pe,
                          o[2] + _swap_halves(np.arange(32), 32), z(64)])
    hg = rng(o[3], 1280)
    q_heads = [rng(o[8] + 64 * h, 64) for h in range(4)]
    q_perm = np.concatenate([q_heads[0], q_heads[2], q_heads[1], q_heads[3]])
    swa = np.concatenate([q_perm, rng(o[9], 128), rng(o[10], 128)])
    ret = rng(o[11], 768)
    cols = np.concatenate([mla, hg, swa, ret])
    assert cols.shape[0] == W_EXT
    return cols


_EXT_COLS = _ext_columns()


def _take_static(w, idx, axis):
    idx = np.asarray(idx)
    axis = axis % w.ndim
    pieces, start = [], 0
    for i in range(1, len(idx) + 1):
        run_ends = i == len(idx) or (idx[i] != idx[i - 1] + 1 if idx[i - 1] >= 0 else idx[i] >= 0)
        if run_ends:
            n = i - start
            if idx[start] < 0:
                shape = list(w.shape)
                shape[axis] = n
                pieces.append(jnp.zeros(shape, w.dtype))
            else:
                pieces.append(lax.slice_in_dim(w, int(idx[start]), int(idx[start]) + n, axis=axis))
            start = i
    return jnp.concatenate(pieces, axis=axis)


def _take_cols(w, cols):
    return _take_static(w, cols, -1)


def _axial_angles(n_tok, rot_dim):
    rows = n_tok // GRID_W
    row = jnp.broadcast_to(jnp.arange(rows)[:, None], (rows, GRID_W)).reshape(-1)
    col = jnp.broadcast_to(jnp.arange(GRID_W)[None, :], (rows, GRID_W)).reshape(-1)
    n_freq = rot_dim // 4
    inv = ROPE_THETA ** (-jnp.arange(n_freq, dtype=F32) / n_freq)
    return jnp.concatenate([row.astype(F32)[:, None] * inv, col.astype(F32)[:, None] * inv], axis=-1)


def _rope_tables(ang, n_ctx):
    cos, sin = jnp.cos(ang), jnp.sin(ang)
    c = jnp.concatenate([cos, cos], axis=-1)
    s = jnp.concatenate([-sin, sin], axis=-1)
    r = c.shape[-1]
    c = jnp.concatenate([jnp.ones((n_ctx, r), F32), c], axis=0)
    s = jnp.concatenate([jnp.zeros((n_ctx, r), F32), s], axis=0)
    return c, s


def _split2(x):
    hi = x.astype(BF16)
    lo = (x - hi.astype(F32)).astype(BF16)
    return hi, lo


def _seg_mean(x, ones_bd):
    hi, lo = _split2(x)
    return (jnp.dot(hi, ones_bd, preferred_element_type=F32)
            + jnp.dot(lo, ones_bd, preferred_element_type=F32))


def _silu(x):
    return x / (1.0 + jnp.exp(-x))


def _rot_half(x, head):
    half = head // 2
    lane = lax.broadcasted_iota(jnp.int32, x.shape, 1) % head
    return jnp.where(lane < half, pltpu.roll(x, LANE - half, axis=1), pltpu.roll(x, half, axis=1))


def _row_mod(m_ref, idx, row0, tm, n_ctx):
    rows = row0 + lax.broadcasted_iota(jnp.int32, (tm, 1), 0)
    mc = m_ref[0, 0, idx:idx + 1, :]
    ml = m_ref[0, 1, idx:idx + 1, :]
    return jnp.where(rows < n_ctx, mc, ml)


def _norm_mod(x, g, shift, scale):
    ms = jnp.mean(x * x, axis=-1, keepdims=True)
    y = x * lax.rsqrt(ms + NORM_EPS) * g
    return y * (1.0 + scale) + shift


def _mod_kernel(c_ref, w_ref, b_ref, o_ref):
    s = _silu(c_ref[...])
    o_ref[0] = jnp.dot(s.astype(BF16), w_ref[0].astype(BF16), preferred_element_type=F32) + b_ref[0]


def _modulation(cvec, w_ada, b_ada):
    R, D = cvec.shape
    depth, _, n6 = w_ada.shape
    tn = 1536
    return pl.pallas_call(
        _mod_kernel,
        grid=(depth, n6 // tn),
        in_specs=[pl.BlockSpec((R, D), lambda l, j: (0, 0)),
                  pl.BlockSpec((1, D, tn), lambda l, j: (l, 0, j)),
                  pl.BlockSpec((1, 1, tn), lambda l, j: (l, 0, j))],
        out_specs=pl.BlockSpec((1, R, tn), lambda l, j: (l, 0, j)),
        out_shape=jax.ShapeDtypeStruct((depth, R, n6), F32),
        compiler_params=_cparams(("parallel", "parallel")),
        name="adaln_mod",
    )(cvec, w_ada, b_ada.reshape(depth, 1, n6))


def _win_kernel(h_ref, m_ref, g_ref, w_ref, cq_ref, sq_ref, tk_ref, gq_ref, wq_ref, wqs_ref, gk_ref, wk_ref,
                wv_ref, e_ref, sc_ref, ss_ref, mq_out, mk_out, mv_out, o_hg, sq_out, skv_out, o_ret,
                *, tm, n_ctx):
    row0 = pl.program_id(1) * tm
    a = _norm_mod(h_ref[0], g_ref[...], _row_mod(m_ref, 0, row0, tm, n_ctx),
                  _row_mod(m_ref, 1, row0, tm, n_ctx)).astype(BF16)
    group = lambda c0, w: jnp.dot(a, w_ref[:, c0:c0 + w], preferred_element_type=F32)
    mq_out[0], mk_out[0], mv_out[0] = _mla_prep(group(0, W_MLA), cq_ref, sq_ref, tk_ref, gq_ref, wq_ref, wqs_ref,
                                                gk_ref, wk_ref, wv_ref, e_ref)
    o_hg[0] = group(W_MLA, W_HG)
    sq_out[0], skv_out[0] = _swa_prep(group(W_MLA + W_HG, W_SWA), sc_ref, ss_ref)
    o_ret[0] = group(W_MLA + W_HG + W_SWA, W_RET)


def _in_proj(h, mod, g, w_ext, mla_weights, mla_tables, swa_tables, n_ctx, tm=768):
    B, S, D = h.shape
    full = lambda a: pl.BlockSpec(a.shape, lambda b, i: (0,) * a.ndim)
    tab = pl.BlockSpec((tm, LANE), lambda b, i: (i, 0))
    out_w = ((512, BF16), (512, BF16), (512, BF16), (W_HG, F32), (256, BF16), (256, BF16), (W_RET, F32))
    outs = pl.pallas_call(
        functools.partial(_win_kernel, tm=tm, n_ctx=n_ctx),
        grid=(B, S // tm),
        in_specs=[pl.BlockSpec((1, tm, D), lambda b, i: (b, i, 0)),
                  pl.BlockSpec((1, 2, 8, D), lambda b, i: (b, 0, 0, 0)),
                  pl.BlockSpec((1, D), lambda b, i: (0, 0)),
                  pl.BlockSpec((D, W_EXT), lambda b, i: (0, 0)),
                  tab, tab, tab] + [full(w) for w in mla_weights] + [tab, tab],
        out_specs=[pl.BlockSpec((1, tm, w), lambda b, i: (b, i, 0)) for w, _ in out_w],
        out_shape=[jax.ShapeDtypeStruct((B, S, w), dt) for w, dt in out_w],
        compiler_params=_cparams(("parallel", "parallel")),
        name="in_proj",
    )(h, mod, g.reshape(1, D), w_ext, *mla_tables, *mla_weights, *swa_tables)
    return outs[0:3], outs[3], outs[4:6], outs[6]


def _mla_weights(q_norm_g, w_uq, kv_norm_g, w_ukv):
    qcols, qsw, kcols = [], [], []
    for h in range(MLA_HEADS):
        b = 96 * h
        nope = np.arange(b, b + 64)
        pe = np.arange(b + 64, b + 96)
        pad = -np.ones((32,), np.int64)
        qcols.append(np.concatenate([nope, pe, pad]))
        qsw.append(np.concatenate([-np.ones((64,), np.int64), (b + 64) + _swap_halves(np.arange(32), 32), pad]))
        kcols.append(np.concatenate([np.arange(128 * h, 128 * h + 64), -np.ones((64,), np.int64)]))
    vcols = np.concatenate([np.concatenate([np.arange(128 * h + 64, 128 * h + 128), -np.ones((64,), np.int64)])
                            for h in range(MLA_HEADS)])
    pad_rows = ((0, 256 - MLA_Q_RANK), (0, 0))
    wq = jnp.pad(_take_cols(w_uq, np.concatenate(qcols)), pad_rows).astype(BF16)
    wq_sw = jnp.pad(_take_cols(w_uq, np.concatenate(qsw)), pad_rows).astype(BF16)
    wk = _take_cols(w_ukv, np.concatenate(kcols)).astype(BF16)
    wv = _take_cols(w_ukv, vcols).astype(BF16)
    e = np.zeros((128, 512), np.float32)
    for h in range(MLA_HEADS):
        for j in range(32):
            e[j, 128 * h + 64 + j] = 1.0
            e[32 + j, 128 * h + 64 + j] = 1.0
    gq = jnp.pad(q_norm_g, (0, 256 - MLA_Q_RANK)).reshape(1, 256)
    return gq, wq, wq_sw, kv_norm_g.reshape(1, 128), wk, wv, jnp.asarray(e, BF16)


def _mla_tables(n_lat, n_ctx):
    ang = _axial_angles(n_lat, MLA_ROPE)
    c32, s32 = _rope_tables(ang, n_ctx)
    S = n_lat + n_ctx
    one, zero = jnp.ones((S, 64), F32), jnp.zeros((S, 64), F32)
    cq = jnp.concatenate([one, c32, jnp.zeros((S, 32), F32)], axis=-1)
    sq = jnp.concatenate([zero, s32, jnp.zeros((S, 32), F32)], axis=-1)
    tk = jnp.concatenate([c32, s32, zero], axis=-1)
    return cq, sq, tk


def _mla_prep(p, cq_ref, sq_ref, tk_ref, gq_ref, wq_ref, wqs_ref, gk_ref, wk_ref, wv_ref, e_ref):
    scale = (MLA_NOPE + MLA_ROPE) ** -0.5 * float(np.log2(np.e))
    ql = p[:, 0:256]
    msq = jnp.sum(ql * ql, axis=-1, keepdims=True) * (1.0 / MLA_Q_RANK)
    qn = (ql * lax.rsqrt(msq + NORM_EPS) * gq_ref[...]).astype(BF16)
    kvl = p[:, 256:384]
    msk = jnp.mean(kvl * kvl, axis=-1, keepdims=True)
    kvn = (kvl * lax.rsqrt(msk + NORM_EPS) * gk_ref[...]).astype(BF16)
    q = jnp.dot(qn, wq_ref[...], preferred_element_type=F32)
    qs = jnp.dot(qn, wqs_ref[...], preferred_element_type=F32)
    cq = jnp.concatenate([cq_ref[...]] * MLA_HEADS, axis=-1)
    sq = jnp.concatenate([sq_ref[...]] * MLA_HEADS, axis=-1)
    q_out = ((q * cq + qs * sq) * scale).astype(BF16)
    pe = (p[:, 384:512] * tk_ref[...]).astype(BF16)
    k = (jnp.dot(kvn, wk_ref[...], preferred_element_type=F32)
         + jnp.dot(pe, e_ref[...], preferred_element_type=F32))
    lane = lax.broadcasted_iota(jnp.int32, (1, MLA_HEADS * LANE), 1) % LANE
    ones = jnp.where(lane >= MLA_V, 1.0, 0.0)
    v = jnp.dot(kvn, wv_ref[...], preferred_element_type=F32) + ones
    return q_out, k.astype(BF16), v.astype(BF16)


MLA_SAMPLES = 2


def _mla_attn_kernel(q_ref, k_ref, v_ref, o_ref, *, n_ctx, tq):
    S = k_ref.shape[1]

    lo = lax.broadcasted_iota(jnp.int32, (tq, LANE), 1) < MLA_V

    def attend(nk):
        for n in range(q_ref.shape[0]):
            pv = []
            for h in range(MLA_HEADS):
                q = q_ref[n, :, LANE * h:LANE * (h + 1)]
                k = k_ref[n, 0:nk, LANE * h:LANE * (h + 1)]
                s = lax.dot_general(q, k, (((1,), (1,)), ((), ())), preferred_element_type=F32)
                p = jnp.exp2(s - jnp.max(s, axis=-1, keepdims=True))
                pv.append(jnp.dot(p.astype(BF16), v_ref[n, 0:nk, LANE * h:LANE * (h + 1)],
                                  preferred_element_type=F32))
            for pair in range(MLA_HEADS // 2):
                a, b = pv[2 * pair], pv[2 * pair + 1]
                o_ref[n, :, LANE * pair:LANE * (pair + 1)] = jnp.where(
                    lo, a / pltpu.roll(a, MLA_V, axis=1), pltpu.roll(b, MLA_V, axis=1) / b).astype(o_ref.dtype)

    is_ctx = (pl.program_id(1) + 1) * tq <= n_ctx
    pl.when(is_ctx)(lambda: attend(n_ctx))
    pl.when(jnp.logical_not(is_ctx))(lambda: attend(S))


def _mla_mixer(qkv, n_ctx, tq=256):
    q, k, v = qkv
    B, S, _ = q.shape
    assert n_ctx % tq == 0 and S % tq == 0
    ns = math.gcd(B, MLA_SAMPLES)
    return pl.pallas_call(
        functools.partial(_mla_attn_kernel, n_ctx=n_ctx, tq=tq),
        grid=(B // ns, S // tq),
        in_specs=[pl.BlockSpec((ns, tq, 512), lambda b, i: (b, i, 0)),
                  pl.BlockSpec((ns, S, 512), lambda b, i: (b, 0, 0)),
                  pl.BlockSpec((ns, S, 512), lambda b, i: (b, 0, 0))],
        out_specs=pl.BlockSpec((ns, tq, GROUP_W), lambda b, i: (b, i, 0)),
        out_shape=jax.ShapeDtypeStruct((B, S, GROUP_W), BF16),
        compiler_params=_cparams(("parallel", "arbitrary")),
        name="mla_attn",
    )(q, k, v)


SWA_SAMPLES = 8


def _swa_tables(n_lat, n_ctx):
    ang = _axial_angles(n_lat, SWA_HD)
    c64, s64 = _rope_tables(ang, n_ctx)
    return jnp.concatenate([c64, c64], axis=-1), jnp.concatenate([s64, s64], axis=-1)


def _swa_prep(p, c_ref, s_ref):
    scale = SWA_HD ** -0.5
    c, s = c_ref[...], s_ref[...]
    rope = lambda x: x * c + _rot_half(x, SWA_HD) * s
    q = jnp.concatenate([rope(p[:, 0:LANE]), rope(p[:, LANE:2 * LANE])], axis=-1) * scale
    return q.astype(BF16), jnp.concatenate([rope(p[:, 256:384]), p[:, 384:512]], axis=-1).astype(BF16)


def _swa_attn_kernel(sink_ref, q_ref, kvc_ref, kvp_ref, kvn_ref, kvx_ref, bias_ref, o_ref, *, n_ctx, blk):
    j = pl.program_id(1)
    ncb = n_ctx // blk
    samples = range(q_ref.shape[0])
    lane = lax.broadcasted_iota(jnp.int32, (blk, LANE), 1)
    lo = lane < SWA_HD
    zero = jnp.zeros((), BF16)
    rb = lax.broadcasted_iota(jnp.int32, (4 * blk, 1), 0) // blk
    sink = jnp.where(rb == 0, sink_ref[0], jnp.where(rb == 1, sink_ref[1],
                                                     jnp.where(rb == 2, sink_ref[2], sink_ref[3])))

    def stacked_q(n):
        qa, qb = q_ref[n, :, 0:LANE], q_ref[n, :, LANE:2 * LANE]
        return jnp.concatenate([jnp.where(lo, qa, zero), jnp.where(lo, qb, zero),
                                jnp.where(lo, zero, qa), jnp.where(lo, zero, qb)], axis=0)

    def attend(key_refs, bias):
        scores, values = [], []
        for n in samples:
            kcat = jnp.concatenate([r[n, :, 0:LANE] for r in key_refs], axis=0)
            values.append(jnp.concatenate([r[n, :, LANE:2 * LANE] for r in key_refs], axis=0))
            s = lax.dot_general(stacked_q(n), kcat, (((1,), (1,)), ((), ())), preferred_element_type=F32)
            scores.append(s if bias is None else s + bias)
        for n in samples:
            s = scores[n]
            m = jnp.maximum(jnp.max(s, axis=-1, keepdims=True), sink)
            p = jnp.exp(s - m)
            l = jnp.sum(p, axis=-1, keepdims=True) + jnp.exp(sink - m)
            r = jnp.dot(p.astype(BF16), values[n], preferred_element_type=F32) / l
            o_ref[n, :, 0:LANE] = jnp.where(lo, r[0:blk], r[2 * blk:3 * blk]).astype(o_ref.dtype)
            o_ref[n, :, LANE:2 * LANE] = jnp.where(lo, r[blk:2 * blk], r[3 * blk:4 * blk]).astype(o_ref.dtype)

    def ctx_block():
        attend((kvc_ref,), None)

    def lat_block():
        attend((kvc_ref, kvp_ref, kvn_ref, kvx_ref), jnp.concatenate([bias_ref[0]] * SWA_HEADS, axis=0))

    pl.when(j < ncb)(ctx_block)
    pl.when(j >= ncb)(lat_block)


def _swa_mixer(qkv, sink, n_ctx, blk=128):
    q, kv_all = qkv
    B, S, _ = q.shape
    nb = S // blk
    ncb = n_ctx // blk
    cur = lambda b, j: (b, j, 0)
    prev = lambda b, j: (b, jnp.maximum(j - 1, ncb), 0)
    nxt = lambda b, j: (b, jnp.minimum(j + 1, nb - 1), 0)
    ns = math.gcd(B, SWA_SAMPLES)
    kv = lambda f: pl.BlockSpec((ns, blk, 2 * LANE), f)
    ctx = pl.BlockSpec((ns, n_ctx, 2 * LANE), lambda b, j: (b, 0, 0))
    t = np.arange(blk)[:, None]
    c = np.arange(n_ctx + 3 * blk)[None, :] - n_ctx
    bias = np.stack([np.where((c < 0) | ((c >= 0) & (c < blk) & (c >= t) & bool(hp)) | ((c >= blk) & (c < 2 * blk))
                              | ((c >= 2 * blk) & (c - 2 * blk <= t) & bool(hn)), 0.0, -1e30)
                     for hp in (0, 1) for hn in (0, 1)]).astype(np.float32)
    kind = lambda b, j: (2 * (j > ncb).astype(jnp.int32) + (j < nb - 1).astype(jnp.int32), 0, 0)
    return pl.pallas_call(
        functools.partial(_swa_attn_kernel, n_ctx=n_ctx, blk=blk),
        grid=(B // ns, nb),
        in_specs=[pl.BlockSpec(memory_space=pltpu.SMEM),
                  pl.BlockSpec((ns, blk, 256), cur), ctx, kv(prev), kv(cur), kv(nxt),
                  pl.BlockSpec((1, blk, n_ctx + 3 * blk), kind)],
        out_specs=pl.BlockSpec((ns, blk, GROUP_W), cur),
        out_shape=jax.ShapeDtypeStruct((B, S, GROUP_W), BF16),
        compiler_params=_cparams(("parallel", "arbitrary")),
        name="swa_attn",
    )(sink.astype(F32), q, kv_all, kv_all, kv_all, kv_all, jnp.asarray(bias))


def _bwd_chunk(j, ncc, nc):
    return jnp.where(j < ncc, ncc - 1 - j, nc - 1 - (j - ncc))


def _head_stack(x, width, heads):
    lane_h = lax.broadcasted_iota(jnp.int32, x.shape, 1) // width
    zero = jnp.zeros((), x.dtype)
    return jnp.concatenate([jnp.where(lane_h == h, x, zero) for h in range(heads)], axis=0)


def _head_unstack(y, rows, width, heads):
    lane_h = lax.broadcasted_iota(jnp.int32, (rows, heads * width), 1) // width
    out = y[0:rows]
    for h in range(1, heads):
        out = jnp.where(lane_h == h, y[h * rows:(h + 1) * rows], out)
    return out


RET_CHUNK = 256
RET_SAMPLES = 8


def _ret_tables(log_gamma, n_ctx, S):
    C = RET_CHUNK
    t = jnp.arange(C, dtype=F32)
    rel = t[:, None] - t[None, :]
    lg = log_gamma.astype(F32)
    lg_l = jnp.repeat(lg, RET_DK, axis=1)
    dm, xi, zeta, dcay = [], [], [], []
    for d in range(2):
        r = rel if d == 0 else -rel
        dh = jnp.where(r >= 0, jnp.exp(jnp.maximum(r, 0.0)[None] * lg[d][:, None, None]), 0.0)
        dm.append(dh.reshape(RET_HEADS * C, C))
        tq = (t + 1.0) if d == 0 else (C - t)
        tk = (C - 1.0 - t) if d == 0 else t
        xi.append(jnp.exp(tq[:, None] * lg_l[d][None, :]))
        zeta.append(jnp.exp(tk[:, None] * lg_l[d][None, :]))
        dcay.append(jnp.broadcast_to(jnp.exp(C * lg_l[d])[:, None], (RET_HEADS * RET_DK, GROUP_W)))
    mask = (np.arange(128)[:, None] // RET_DK == np.arange(256)[None, :] // RET_DV).astype(np.float32)
    pos = jnp.arange(S, dtype=F32)
    inv = ROPE_THETA ** (-jnp.arange(RET_DK // 2, dtype=F32) / (RET_DK // 2))
    ang = pos[:, None] * inv
    cos, sin = jnp.cos(ang), jnp.sin(ang)
    c = jnp.tile(jnp.concatenate([cos, cos], axis=-1), (1, RET_HEADS))
    s = jnp.tile(jnp.concatenate([-sin, sin], axis=-1), (1, RET_HEADS))
    return (jnp.stack(dm), jnp.stack(xi), jnp.stack(zeta), jnp.stack(dcay), jnp.asarray(mask), c, s)


def _ret_kernel(qkf_ref, vf_ref, cf_ref, sf_ref, qkb_ref, vb_ref, cb_ref, sb_ref,
                dm_ref, xi_ref, zeta_ref, dcay_ref, mask_ref, of_ref, ob_ref, state):
    C = RET_CHUNK

    @pl.when(pl.program_id(1) == 0)
    def _():
        state[...] = jnp.zeros_like(state)

    dirs = ((qkf_ref, vf_ref, cf_ref, sf_ref, of_ref), (qkb_ref, vb_ref, cb_ref, sb_ref, ob_ref))
    for n in range(qkf_ref.shape[0]):
        for d, (qk_ref, v_ref, c_ref, s_ref, o_ref) in enumerate(dirs):
            c, s = c_ref[...], s_ref[...]
            q, k = qk_ref[n, :, 0:128], qk_ref[n, :, 128:256]
            qr = (q * c + _rot_half(q, RET_DK) * s) * (RET_DK ** -0.5)
            kr = k * c + _rot_half(k, RET_DK) * s
            vb = v_ref[n].astype(BF16)
            qs = _head_stack(qr.astype(BF16), RET_DK, RET_HEADS)
            sc = lax.dot_general(qs, kr.astype(BF16), (((1,), (1,)), ((), ())), preferred_element_type=F32)
            a = (sc * dm_ref[d]).astype(BF16)
            o = _head_unstack(jnp.dot(a, vb, preferred_element_type=F32), C, RET_DV, RET_HEADS)
            st = state[n, d]
            o = o + jnp.dot((qr * xi_ref[d]).astype(BF16), st.astype(BF16), preferred_element_type=F32)
            o_ref[n] = o.astype(o_ref.dtype)
            kz = (kr * zeta_ref[d]).astype(BF16)
            u = lax.dot_general(kz, vb, (((0,), (0,)), ((), ())), preferred_element_type=F32)
            state[n, d] = st * dcay_ref[d] + u * mask_ref[...]


def _ret_mixer(p_ret, tables, n_ctx):
    B, S, _ = p_ret.shape
    C = RET_CHUNK
    dm, xi, zeta, dcay, mask, c, s = tables
    nc, ncc = S // C, n_ctx // C
    fwd = lambda b, j: (b, j, 0)
    bwd = lambda b, j: (b, _bwd_chunk(j, ncc, nc), 0)
    fwd_v = lambda b, j: (b, j, 1)
    bwd_v = lambda b, j: (b, _bwd_chunk(j, ncc, nc), 1)
    fwd_t = lambda b, j: (j, 0)
    bwd_t = lambda b, j: (_bwd_chunk(j, ncc, nc), 0)
    full = lambda a: pl.BlockSpec(a.shape, lambda b, j: (0,) * a.ndim)
    ns = math.gcd(B, RET_SAMPLES)
    return pl.pallas_call(
        _ret_kernel,
        grid=(B // ns, nc),
        in_specs=[pl.BlockSpec((ns, C, 256), fwd), pl.BlockSpec((ns, C, 256), fwd_v),
                  pl.BlockSpec((C, LANE), fwd_t), pl.BlockSpec((C, LANE), fwd_t),
                  pl.BlockSpec((ns, C, 256), bwd), pl.BlockSpec((ns, C, 256), bwd_v),
                  pl.BlockSpec((C, LANE), bwd_t), pl.BlockSpec((C, LANE), bwd_t),
                  full(dm), full(xi), full(zeta), full(dcay), full(mask)],
        out_specs=[pl.BlockSpec((ns, C, GROUP_W), fwd), pl.BlockSpec((ns, C, GROUP_W), bwd)],
        out_shape=[jax.ShapeDtypeStruct((B, S, GROUP_W), BF16)] * 2,
        scratch_shapes=[pltpu.VMEM((ns, 2, RET_HEADS * RET_DK, GROUP_W), F32)],
        compiler_params=_cparams(("parallel", "arbitrary")),
        name="retention_scan",
    )(p_ret, p_ret, c, s, p_ret, p_ret, c, s, dm, xi, zeta, dcay, mask)


HG_CHUNK = 128
HG_SAMPLES = 4


def _hg_level_tables(C):
    levels = []
    n = 2
    while n <= C:
        levels.append(n)
        n *= 2
    t = np.arange(C)
    mats, qroles, kroles = [], [], []
    for d in range(2):
        rows = []
        qr_d, kr_d = [], []
        for n in levels:
            m = n // 2
            start = (t // n) * n
            mid = start + m
            u = t - start
            M = np.zeros((C, C), np.float32)
            for i in range(C):
                if d == 0:
                    if u[i] >= m:
                        M[i, mid[i]:i + 1] = 1.0
                    else:
                        M[i, i + 1:mid[i]] = 1.0
                else:
                    if u[i] < m:
                        M[i, i:mid[i]] = 1.0
                    else:
                        M[i, mid[i]:i] = 1.0
            rows.append(M)
            qr_d.append((u >= m) if d == 0 else (u < m))
            kr_d.append((u < m) if d == 0 else (u >= m))
        Mq = np.zeros((C, C), np.float32)
        Mk = np.zeros((C, C), np.float32)
        for i in range(C):
            if d == 0:
                Mq[i, 0:i + 1] = 1.0
                Mk[i, i + 1:C] = 1.0
            else:
                Mq[i, i:C] = 1.0
                Mk[i, 0:i] = 1.0
        rows += [Mq, Mk, np.ones((8, C), np.float32)]
        mats.append(np.concatenate(rows, axis=0))
        qroles.append(np.stack(qr_d))
        kroles.append(np.stack(kr_d))
    same_block = np.stack([(t[:, None] // n == t[None, :] // n) for n in levels]).astype(np.float32)
    return levels, np.stack(mats), np.stack(qroles), np.stack(kroles), same_block


def _hg_kernel(qf_ref, zf_ref, if_ref, qb_ref, zb_ref, ib_ref, lb_ref, m_ref, role_ref, blk_ref,
               ones_ref, mask_ref, of_ref, ob_ref, state, *, n_levels):
    C = HG_CHUNK

    @pl.when(pl.program_id(1) == 0)
    def _():
        state[...] = jnp.zeros_like(state)

    dirs = ((qf_ref, zf_ref, if_ref), (qb_ref, zb_ref, ib_ref))
    chains = [(s, d) for s in range(qf_ref.shape[0]) for d in range(2)]
    st = [state[s, d] for s, d in chains]
    q, v, vb, kk, r = [], [], [], [], []
    for c, (s, d) in enumerate(chains):
        q_ref, z_ref, i_ref = dirs[d]
        z = z_ref[s]
        lb = lb_ref[d]
        lbf = jnp.maximum(lb, LB_FLOOR)
        t = jnp.exp(-jnp.abs(z))
        rcp = 1.0 / (1.0 + t)
        pos = z >= 0.0
        sig = jnp.where(pos, rcp, t * rcp)
        nsig = jnp.where(pos, t * rcp, rcp)
        logf = jnp.log(lbf + (1.0 - lb) * sig)
        kk.append((1.0 - lb) * nsig - (lbf - lb))
        hi, lo = _split2(logf)
        r.append(jnp.dot(m_ref[d], hi, preferred_element_type=F32)
                 + jnp.dot(m_ref[d], lo, preferred_element_type=F32))
        q.append(q_ref[s] * (HG_DK ** -0.5))
        v.append(i_ref[s])
        vb.append(v[c].astype(BF16))
    acc = [jnp.zeros((HG_HEADS * C, C), F32) for _ in chains]
    for lv in range(n_levels):
        for c, (s, d) in enumerate(chains):
            e = jnp.exp(r[c][lv * C:(lv + 1) * C])
            roles = role_ref[d, lv]
            qe = jnp.where(roles > 0.5, q[c] * e, 0.0).astype(BF16)
            ke = jnp.where(roles > 0.5, 0.0, kk[c] * e).astype(BF16)
            a = lax.dot_general(_head_stack(qe, HG_DK, HG_HEADS), ke, (((1,), (1,)), ((), ())),
                                preferred_element_type=F32)
            acc[c] = acc[c] + a * blk_ref[lv]
    outs = []
    for c in range(len(chains)):
        o = _head_unstack(jnp.dot(acc[c].astype(BF16), vb[c], preferred_element_type=F32), C, HG_DV, HG_HEADS)
        o = o + jnp.dot((q[c] * kk[c]).astype(BF16), ones_ref[...], preferred_element_type=F32) * v[c]
        e_q = jnp.exp(r[c][n_levels * C:(n_levels + 1) * C])
        o = o + lax.dot_general((q[c] * e_q).astype(BF16), st[c].astype(BF16), (((1,), (1,)), ((), ())),
                                preferred_element_type=F32)
        outs.append(o)
    new_state = []
    for c in range(len(chains)):
        e_k = jnp.exp(r[c][(n_levels + 1) * C:(n_levels + 2) * C])
        ku = (kk[c] * e_k).astype(BF16)
        u = lax.dot_general(vb[c], ku, (((0,), (0,)), ((), ())), preferred_element_type=F32)
        dtot = jnp.exp(r[c][(n_levels + 2) * C:(n_levels + 2) * C + 1])
        new_state.append(st[c] * dtot + u * mask_ref[...])
    for c, (s, d) in enumerate(chains):
        o_ref = (of_ref, ob_ref)[d]
        o_ref[s] = outs[c].astype(o_ref.dtype)
        state[s, d] = new_state[c]


def _hg_mixer(p_hg, lb, n_ctx):
    B, S, _ = p_hg.shape
    C = HG_CHUNK
    levels, mats, qroles, kroles, same_block = _hg_level_tables(C)
    nl = len(levels)
    m_all = jnp.asarray(mats, BF16)
    roles = jnp.asarray(np.broadcast_to(qroles[..., None], (2, nl, C, GROUP_W)).astype(np.float32))
    blk = jnp.asarray(np.tile(same_block, (1, HG_HEADS, 1)))
    hd = np.arange(GROUP_W) // HG_DK
    bd = (hd[:, None] == hd[None, :]).astype(np.float32)
    ones_bd = jnp.asarray(bd, BF16)
    mask = jnp.asarray(bd)
    nc, ncc = S // C, n_ctx // C
    col = lambda k, rev: (lambda b, j: (b, _bwd_chunk(j, ncc, nc) if rev else j, k))
    ns = math.gcd(B, HG_SAMPLES)
    blkspec = lambda k, rev: pl.BlockSpec((ns, C, GROUP_W), col(k, rev))
    full = lambda a: pl.BlockSpec(a.shape, lambda b, j: (0,) * a.ndim)
    lb3 = lb.reshape(2, 1, GROUP_W).astype(F32)
    return pl.pallas_call(
        functools.partial(_hg_kernel, n_levels=nl),
        grid=(B // ns, nc),
        in_specs=[blkspec(0, False), blkspec(1, False), blkspec(3, False),
                  blkspec(0, True), blkspec(2, True), blkspec(3, True),
                  full(lb3), full(m_all), full(roles), full(blk), full(ones_bd), full(mask)],
        out_specs=[blkspec(0, False), blkspec(0, True)],
        out_shape=[jax.ShapeDtypeStruct((B, S, GROUP_W), BF16)] * 2,
        scratch_shapes=[pltpu.VMEM((ns, 2, GROUP_W, GROUP_W), F32)],
        compiler_params=_cparams(("parallel", "arbitrary")),
        name="hgrn2_scan",
    )(p_hg, p_hg, p_hg, p_hg, p_hg, p_hg, lb3, m_all, roles, blk, ones_bd, mask)


_SWA_OUT_PERM = np.concatenate([np.arange(64 * h, 64 * h + 64) for h in (0, 2, 1, 3)])


def _wout_kernel(h_ref, m_ref, a_ref, hf_ref, hb_ref, hgate_ref, c_ref, rf_ref, rb_ref, rgate_ref,
                 hgn_ref, rgg_ref, rgb_ref, ones_ref, w_ref, o_ref, *, tm, n_ctx, row_off):
    row0 = row_off + pl.program_id(1) * tm
    ones = ones_ref[...]
    o = hf_ref[0].astype(F32) + hb_ref[0].astype(F32)
    yb = o * lax.rsqrt(_seg_mean(o * o, ones) + NORM_EPS) * hgn_ref[...] * _silu(hgate_ref[0])
    o = rf_ref[0].astype(F32) + rb_ref[0].astype(F32)
    xc = o - _seg_mean(o, ones)
    yd = (xc * lax.rsqrt(_seg_mean(xc * xc, ones) + NORM_EPS) * rgg_ref[...] + rgb_ref[...]) * _silu(rgate_ref[0])
    acc = jnp.dot(a_ref[0].astype(BF16), w_ref[0:256, :], preferred_element_type=F32)
    acc += jnp.dot(yb.astype(BF16), w_ref[256:512, :], preferred_element_type=F32)
    acc += jnp.dot(c_ref[0].astype(BF16), w_ref[512:768, :], preferred_element_type=F32)
    acc += jnp.dot(yd.astype(BF16), w_ref[768:1024, :], preferred_element_type=F32)
    o_ref[0] = h_ref[0] + _row_mod(m_ref, 2, row0, tm, n_ctx) * acc


def _out_proj(h, mod, o_mla, hg_f, hg_b, p_hg, o_swa, ret_f, ret_b, p_ret, hg_norm_g, ret_gn_g, ret_gn_b,
              w_out_p, n_ctx, tm=768, row_off=0):
    B, S, D = h.shape
    assert row_off % tm == 0 and (S - row_off) % tm == 0
    off = row_off // tm
    hd = np.arange(GROUP_W) // HG_DV
    ones_bd = jnp.asarray((hd[:, None] == hd[None, :]).astype(np.float32) / HG_DV, BF16)
    row = lambda w, k=0: pl.BlockSpec((1, tm, w), lambda b, i: (b, i + off, k))
    vec = pl.BlockSpec((1, GROUP_W), lambda b, i: (0, 0))
    return pl.pallas_call(
        functools.partial(_wout_kernel, tm=tm, n_ctx=n_ctx, row_off=row_off),
        grid=(B, (S - row_off) // tm),
        in_specs=[row(D), pl.BlockSpec((1, 2, 8, D), lambda b, i: (b, 0, 0, 0)),
                  row(GROUP_W), row(GROUP_W), row(GROUP_W), row(GROUP_W, 4),
                  row(GROUP_W), row(GROUP_W), row(GROUP_W), row(GROUP_W, 2),
                  vec, vec, vec, pl.BlockSpec((GROUP_W, GROUP_W), lambda b, i: (0, 0)),
                  pl.BlockSpec((D, D), lambda b, i: (0, 0))],
        out_specs=pl.BlockSpec((1, tm, D), lambda b, i: (b, i, 0)),
        out_shape=jax.ShapeDtypeStruct((B, S - row_off, D), F32),
        compiler_params=_cparams(("parallel", "parallel")),
        name="out_proj",
    )(h, mod, o_mla, hg_f, hg_b, p_hg, o_swa, ret_f, ret_b, p_ret,
      hg_norm_g.reshape(1, GROUP_W), ret_gn_g.reshape(1, GROUP_W), ret_gn_b.reshape(1, GROUP_W), ones_bd, w_out_p)


def _ffn_kernel(h_ref, m_ref, g_ref, wg_ref, wu_ref, wd_ref, o_ref, xn, *, tm, n_ctx):
    j = pl.program_id(2)
    row0 = pl.program_id(1) * tm

    @pl.when(j == 0)
    def _():
        xn[...] = _norm_mod(h_ref[0], g_ref[...], _row_mod(m_ref, 3, row0, tm, n_ctx),
                            _row_mod(m_ref, 4, row0, tm, n_ctx)).astype(BF16)
        o_ref[...] = jnp.zeros_like(o_ref)

    x = xn[...]
    hg = jnp.dot(x, wg_ref[0], preferred_element_type=F32)
    hu = jnp.dot(x, wu_ref[0], preferred_element_type=F32)
    o_ref[0] += jnp.dot((_silu(hg) * hu).astype(BF16), wd_ref[0], preferred_element_type=F32)

    @pl.when(j == pl.num_programs(2) - 1)
    def _():
        o_ref[0] = h_ref[0] + _row_mod(m_ref, 5, row0, tm, n_ctx) * o_ref[0]


def _dense_ffn(h, mod, g, w_gate, w_up, w_down, li, n_ctx, tm=1152, tf=1408):
    B, S, D = h.shape
    dff = w_gate.shape[-1]
    return pl.pallas_call(
        functools.partial(_ffn_kernel, tm=tm, n_ctx=n_ctx),
        grid=(B, S // tm, dff // tf),
        in_specs=[pl.BlockSpec((1, tm, D), lambda b, i, j: (b, i, 0)),
                  pl.BlockSpec((1, 2, 8, D), lambda b, i, j: (b, 0, 0, 0)),
                  pl.BlockSpec((1, D), lambda b, i, j: (0, 0)),
                  pl.BlockSpec((1, D, tf), lambda b, i, j: (li, 0, j)),
                  pl.BlockSpec((1, D, tf), lambda b, i, j: (li, 0, j)),
                  pl.BlockSpec((1, tf, D), lambda b, i, j: (li, j, 0))],
        out_specs=pl.BlockSpec((1, tm, D), lambda b, i, j: (b, i, 0)),
        out_shape=jax.ShapeDtypeStruct((B, S, D), F32),
        scratch_shapes=[pltpu.VMEM((tm, D), BF16)],
        compiler_params=_cparams(("parallel", "parallel", "arbitrary")),
        name="dense_ffn",
    )(h, mod, g.reshape(1, D), w_gate, w_up, w_down)


MOE_TM = 1024
MOE_TF = 512
MOE_ROWS = (384, 256)
DMA_UNROLL = 8
SUBLANES = 8
ZERO_ROWS = 256


def _router_kernel(h_ref, m_ref, g_ref, wr_ref, tri_ref, xn_ref, route_ref, cnt_ref, counts, *, tm, n_ctx):
    b, i = pl.program_id(0), pl.program_id(1)
    row0 = i * tm

    @pl.when((b == 0) & (i == 0))
    def _():
        counts[...] = jnp.zeros_like(counts)

    a = _norm_mod(h_ref[0], g_ref[...], _row_mod(m_ref, 3, row0, tm, n_ctx), _row_mod(m_ref, 4, row0, tm, n_ctx))
    xn_ref[0] = a
    ah, al = _split2(a)
    wh, wl = _split2(wr_ref[...])
    logits = (jnp.dot(ah, wh, preferred_element_type=F32) + jnp.dot(ah, wl, preferred_element_type=F32)
              + jnp.dot(al, wh, preferred_element_type=F32))
    lane = lax.broadcasted_iota(jnp.int32, logits.shape, 1)
    neg = jnp.float32(-jnp.inf)
    lg = jnp.where(lane < N_EXPERTS, logits, neg)
    v1 = jnp.max(lg, axis=-1, keepdims=True)
    i1 = jnp.min(jnp.where(lg == v1, lane, LANE), axis=-1, keepdims=True)
    lg2 = jnp.where(lane == i1, neg, lg)
    v2 = jnp.max(lg2, axis=-1, keepdims=True)
    i2 = jnp.min(jnp.where(lg2 == v2, lane, LANE), axis=-1, keepdims=True)
    e = jnp.exp(v2 - v1)
    w1 = 1.0 / (1.0 + e)
    w2 = e / (1.0 + e)
    oh1, oh2 = lane == i1, lane == i2
    onehot = jnp.where(oh1 | oh2, 1.0, 0.0)
    earlier = jnp.dot(tri_ref[...], onehot.astype(BF16), preferred_element_type=F32)
    pos = earlier + counts[...]
    r1 = jnp.sum(jnp.where(oh1, pos, 0.0), axis=-1, keepdims=True)
    r2 = jnp.sum(jnp.where(oh2, pos, 0.0), axis=-1, keepdims=True)
    counts[...] = counts[...] + jnp.sum(onehot, axis=0, keepdims=True)
    cnt_ref[...] = jnp.broadcast_to(counts[...], cnt_ref.shape)
    route_ref[0] = jnp.where(lane == 0, i1.astype(F32),
                             jnp.where(lane == 1, i2.astype(F32),
                                       jnp.where(lane == 2, w1,
                                                 jnp.where(lane == 3, w2,
                                                           jnp.where(lane == 4, r1, jnp.where(lane == 5, r2, 0.0))))))


def _scatter_kernel(meta_ref, slot_ref, x_ref, xs_hbm, zeros_v, sem, *, tm, tile_rows, total_rows):
    b, i = pl.program_id(0), pl.program_id(1)

    def start(r, carry):
        for k in range(TOP_K):
            pltpu.make_async_copy(x_ref.at[0, pl.ds(r, 1)], xs_hbm.at[pl.ds(slot_ref[0, 0, 0, k * tm + r], 1)],
                                  sem).start()
        return carry

    lax.fori_loop(0, tm, start, 0, unroll=DMA_UNROLL)
    for k in range(TOP_K):
        pltpu.make_async_copy(x_ref.at[0], xs_hbm.at[pl.ds(0, tm)], sem).wait()

    @pl.when((b == pl.num_programs(0) - 1) & (i == pl.num_programs(1) - 1))
    def _():
        zeros_v[...] = jnp.zeros_like(zeros_v)
        z1 = zeros_v.at[pl.ds(0, 1)]
        z8 = zeros_v.at[pl.ds(0, SUBLANES)]

        def fill(lo, end):
            mid = jnp.minimum(((lo + SUBLANES - 1) // SUBLANES) * SUBLANES, end)
            big = mid + ((end - mid) // ZERO_ROWS) * ZERO_ROWS
            n_row, n_big, n_small = mid - lo, (big - mid) // ZERO_ROWS, (end - big) // SUBLANES

            def start_row(r, c):
                pltpu.make_async_copy(z1, xs_hbm.at[pl.ds(lo + r, 1)], sem).start()
                return c

            def wait_row(r, c):
                pltpu.make_async_copy(z1, xs_hbm.at[pl.ds(0, 1)], sem).wait()
                return c

            def start_big(q, c):
                dst = xs_hbm.at[pl.ds(pl.multiple_of(mid + q * ZERO_ROWS, SUBLANES), ZERO_ROWS)]
                pltpu.make_async_copy(zeros_v, dst, sem).start()
                return c

            def wait_big(q, c):
                pltpu.make_async_copy(zeros_v, xs_hbm.at[pl.ds(0, ZERO_ROWS)], sem).wait()
                return c

            def start_small(q, c):
                dst = xs_hbm.at[pl.ds(pl.multiple_of(big + q * SUBLANES, SUBLANES), SUBLANES)]
                pltpu.make_async_copy(z8, dst, sem).start()
                return c

            def wait_small(q, c):
                pltpu.make_async_copy(z8, xs_hbm.at[pl.ds(0, SUBLANES)], sem).wait()
                return c

            lax.fori_loop(0, n_row, start_row, 0)
            lax.fori_loop(0, n_big, start_big, 0)
            lax.fori_loop(0, n_small, start_small, 0)
            lax.fori_loop(0, n_row, wait_row, 0)
            lax.fori_loop(0, n_big, wait_big, 0)
            lax.fori_loop(0, n_small, wait_small, 0)

        for e in range(N_EXPERTS):
            c = meta_ref[e]
            g0 = meta_ref[N_EXPERTS + e]
            fill(g0 + c, g0 + ((c + tile_rows - 1) // tile_rows) * tile_rows)
        fill(meta_ref[2 * N_EXPERTS] * tile_rows, jnp.int32(total_rows))


def _expert_kernel(te_ref, nu_ref, x_ref, wg_ref, wu_ref, wd_ref, o_ref, xb):
    i, j = pl.program_id(0), pl.program_id(1)

    @pl.when(j == 0)
    def _():
        xb[...] = x_ref[...].astype(BF16)
        o_ref[...] = jnp.zeros_like(o_ref)

    @pl.when(i < nu_ref[0])
    def _():
        wg, wu, wd = wg_ref[0, 0].astype(BF16), wu_ref[0, 0].astype(BF16), wd_ref[0, 0].astype(BF16)
        x = xb[...]
        hg = jnp.dot(x, wg, preferred_element_type=F32)
        hu = jnp.dot(x, wu, preferred_element_type=F32)
        o_ref[...] += jnp.dot((_silu(hg) * hu).astype(BF16), wd, preferred_element_type=F32)


def _combine_kernel(slot_ref, h_ref, m_ref, route_ref, fg_ref, y_hbm, o_ref, buf, sem, *, tm, n_ctx, final_norm):
    row0 = pl.program_id(1) * tm

    def start(r, carry):
        for k in range(TOP_K):
            pltpu.make_async_copy(y_hbm.at[pl.ds(slot_ref[0, 0, 0, k * tm + r], 1)], buf.at[k, pl.ds(r, 1)], sem).start()
        return carry

    lax.fori_loop(0, tm, start, 0, unroll=DMA_UNROLL)
    for k in range(TOP_K):
        pltpu.make_async_copy(y_hbm.at[pl.ds(0, tm)], buf.at[k], sem).wait()
    w1 = route_ref[0, :, 2:3]
    w2 = route_ref[0, :, 3:4]
    y = h_ref[0] + _row_mod(m_ref, 5, row0, tm, n_ctx) * (w1 * buf[0] + w2 * buf[1])
    if final_norm:
        y = y * lax.rsqrt(jnp.mean(y * y, axis=-1, keepdims=True) + NORM_EPS) * fg_ref[...]
    o_ref[0] = y


def _moe_ffn(h, mod, g, w_router, w_gate, w_up, w_down, li, n_ctx, final_g=None):
    B, S, D = h.shape
    T = B * S
    cr = next(r for r in MOE_ROWS if S % r == 0)
    tm = cr
    dffe = w_gate.shape[-1]
    te_rows = MOE_TM if T % MOE_TM == 0 else 512
    n_tiles = TOP_K * T // te_rows + N_EXPERTS
    P = n_tiles * te_rows
    wr = jnp.pad(w_router, ((0, 0), (0, LANE - N_EXPERTS)))
    tri = jnp.asarray(np.tril(np.ones((tm, tm), np.float32), -1), BF16)
    xn, route, cnt = pl.pallas_call(
        functools.partial(_router_kernel, tm=tm, n_ctx=n_ctx),
        grid=(B, S // tm),
        in_specs=[pl.BlockSpec((1, tm, D), lambda b, i: (b, i, 0)),
                  pl.BlockSpec((1, 2, 8, D), lambda b, i: (b, 0, 0, 0)),
                  pl.BlockSpec((1, D), lambda b, i: (0, 0)),
                  pl.BlockSpec((D, LANE), lambda b, i: (0, 0)),
                  pl.BlockSpec((tm, tm), lambda b, i: (0, 0))],
        out_specs=[pl.BlockSpec((1, tm, D), lambda b, i: (b, i, 0)),
                   pl.BlockSpec((1, tm, LANE), lambda b, i: (b, i, 0)),
                   pl.BlockSpec((SUBLANES, LANE), lambda b, i: (0, 0))],
        out_shape=[jax.ShapeDtypeStruct((B, S, D), F32), jax.ShapeDtypeStruct((B, S, LANE), F32),
                   jax.ShapeDtypeStruct((SUBLANES, LANE), F32)],
        scratch_shapes=[pltpu.VMEM((1, LANE), F32)],
        compiler_params=_cparams(("arbitrary", "arbitrary")),
        name="moe_router",
    )(h, mod, g.reshape(1, D), wr, tri)

    counts = cnt[0, :N_EXPERTS].astype(jnp.int32)
    ptiles = (counts + te_rows - 1) // te_rows
    tile_end = jnp.cumsum(ptiles)
    gstart = (tile_end - ptiles) * te_rows
    n_used = tile_end[-1:].astype(jnp.int32)
    tile_ids = jnp.arange(n_tiles, dtype=jnp.int32)
    tile_e = jnp.minimum(jnp.sum((tile_ids[:, None] >= tile_end[None, :]).astype(jnp.int32), axis=1),
                         N_EXPERTS - 1)
    meta = jnp.concatenate([counts, gstart, n_used])
    choice = route[..., 0:TOP_K].astype(jnp.int32)
    is_e = choice[..., None] == jnp.arange(N_EXPERTS, dtype=jnp.int32)
    slots = jnp.sum(jnp.where(is_e, gstart, 0), axis=-1) + route[..., 4:4 + TOP_K].astype(jnp.int32)
    slots = slots.reshape(B, S // cr, cr, TOP_K).transpose(0, 1, 3, 2)
    slots = slots.reshape(B, S // cr, 1, TOP_K * cr)

    xs = pl.pallas_call(
        functools.partial(_scatter_kernel, tm=cr, tile_rows=te_rows, total_rows=P),
        grid_spec=pltpu.PrefetchScalarGridSpec(
            num_scalar_prefetch=1,
            grid=(B, S // cr),
            in_specs=[pl.BlockSpec((1, 1, 1, TOP_K * cr), lambda b, i, mt: (b, i, 0, 0),
                                   memory_space=pltpu.SMEM),
                      pl.BlockSpec((1, cr, D), lambda b, i, mt: (b, i, 0))],
            out_specs=pl.BlockSpec(memory_space=pl.ANY),
            scratch_shapes=[pltpu.VMEM((ZERO_ROWS, D), F32), pltpu.SemaphoreType.DMA(())]),
        out_shape=jax.ShapeDtypeStruct((P, D), F32),
        compiler_params=_cparams(("arbitrary", "arbitrary")),
        name="moe_scatter",
    )(meta, slots, xn)

    nf = dffe // MOE_TF
    last = lambda i, nu: jnp.minimum(i, nu[0] - 1)
    jj = lambda i, j, nu: jnp.where(i < nu[0], j, nf - 1)
    ys = pl.pallas_call(
        _expert_kernel,
        grid_spec=pltpu.PrefetchScalarGridSpec(
            num_scalar_prefetch=2,
            grid=(n_tiles, nf),
            in_specs=[pl.BlockSpec((te_rows, D), lambda i, j, te, nu: (last(i, nu), 0)),
                      pl.BlockSpec((1, 1, D, MOE_TF), lambda i, j, te, nu: (li, te[last(i, nu)], 0, jj(i, j, nu))),
                      pl.BlockSpec((1, 1, D, MOE_TF), lambda i, j, te, nu: (li, te[last(i, nu)], 0, jj(i, j, nu))),
                      pl.BlockSpec((1, 1, MOE_TF, D), lambda i, j, te, nu: (li, te[last(i, nu)], jj(i, j, nu), 0))],
            out_specs=pl.BlockSpec((te_rows, D), lambda i, j, te, nu: (i, 0)),
            scratch_shapes=[pltpu.VMEM((te_rows, D), BF16)]),
        out_shape=jax.ShapeDtypeStruct((P, D), F32),
        compiler_params=_cparams(("arbitrary", "arbitrary")),
        name="moe_experts",
    )(tile_e, n_used, xs, w_gate, w_up, w_down)
    fg = jnp.ones((1, D), F32) if final_g is None else final_g.reshape(1, D).astype(F32)
    return pl.pallas_call(
        functools.partial(_combine_kernel, tm=cr, n_ctx=n_ctx, final_norm=final_g is not None),
        grid=(B, S // cr),
        in_specs=[pl.BlockSpec((1, 1, 1, TOP_K * cr), lambda b, i: (b, i, 0, 0), memory_space=pltpu.SMEM),
                  pl.BlockSpec((1, cr, D), lambda b, i: (b, i, 0)),
                  pl.BlockSpec((1, 2, 8, D), lambda b, i: (b, 0, 0, 0)),
                  pl.BlockSpec((1, cr, LANE), lambda b, i: (b, i, 0)),
                  pl.BlockSpec((1, D), lambda b, i: (0, 0)),
                  pl.BlockSpec(memory_space=pl.ANY)],
        out_specs=pl.BlockSpec((1, cr, D), lambda b, i: (b, i, 0)),
        out_shape=jax.ShapeDtypeStruct((B, S, D), F32),
        scratch_shapes=[pltpu.VMEM((TOP_K, cr, D), F32), pltpu.SemaphoreType.DMA(())],
        compiler_params=_cparams(("arbitrary", "arbitrary")),
        name="moe_combine",
    )(slots, h, mod, route, fg, ys)


def _final_norm_kernel(h_ref, g_ref, o_ref):
    x = h_ref[0]
    o_ref[0] = x * lax.rsqrt(jnp.mean(x * x, axis=-1, keepdims=True) + NORM_EPS) * g_ref[...]


def _final_norm(h, g, n_ctx, tm=256):
    B, S, D = h.shape
    N = S - n_ctx
    return pl.pallas_call(
        _final_norm_kernel,
        grid=(B, N // tm),
        in_specs=[pl.BlockSpec((1, tm, D), lambda b, i: (b, n_ctx // tm + i, 0)),
                  pl.BlockSpec((1, D), lambda b, i: (0, 0))],
        out_specs=pl.BlockSpec((1, tm, D), lambda b, i: (b, i, 0)),
        out_shape=jax.ShapeDtypeStruct((B, N, D), F32),
        compiler_params=_cparams(("parallel", "parallel")),
        name="final_norm",
    )(h, g.reshape(1, D))


MOD_ROWS = 16


def kernel(x, c, ctx, c_ctx, w_ada, b_ada, norm_mix_g, norm_ffn_g, w_in, w_out, mla_q_norm_g, mla_w_uq, mla_kv_norm_g, mla_w_ukv, hg_lb_logits, hg_norm_g, swa_sink, ret_decay_logit, ret_gn_g, ret_gn_b, ffn_w_gate, ffn_w_up, ffn_w_down, moe_w_router, moe_w_gate, moe_w_up, moe_w_down, final_norm_g):
    B, N, D = x.shape
    L = ctx.shape[1]
    S = L + N
    assert B + 1 <= MOD_ROWS
    h = jnp.concatenate([ctx, x], axis=1)
    cvec = jnp.zeros((MOD_ROWS, D), F32).at[0].set(c_ctx).at[1:1 + B].set(c)
    mod_all = _modulation(cvec, w_ada, b_ada).reshape(DEPTH, MOD_ROWS, 6, D)

    lb_p = jax.nn.softmax(hg_lb_logits.astype(F32), axis=0)
    hg_lb = jnp.cumsum(lb_p, axis=0) - lb_p[0:1]
    ret_log_gamma = jax.nn.log_sigmoid(ret_decay_logit.astype(F32))

    mla_tab = _mla_tables(N, L)
    swa_tab = _swa_tables(N, L)
    w_out_perm = np.concatenate([np.arange(512), 512 + _SWA_OUT_PERM, np.arange(768, 1024)])
    ffn_bf16 = tuple(w.astype(BF16) for w in (ffn_w_gate, ffn_w_up, ffn_w_down))

    for l in range(DEPTH):
        m = mod_all[l]
        mod = jnp.stack([jnp.broadcast_to(m[0], (B, 6, D)), m[1:1 + B]], axis=1)
        mod = jnp.pad(mod, ((0, 0), (0, 0), (0, 2), (0, 0)))
        w_ext = _take_cols(w_in[l], _EXT_COLS).astype(BF16)
        mla_w = _mla_weights(mla_q_norm_g[l], mla_w_uq[l], mla_kv_norm_g[l], mla_w_ukv[l])
        mla_qkv, p_hg, swa_qkv, p_ret = _in_proj(h, mod, norm_mix_g[l], w_ext, mla_w, mla_tab, swa_tab, L)
        o_mla = _mla_mixer(mla_qkv, L)
        hg_f, hg_b = _hg_mixer(p_hg, hg_lb[l], L)
        o_swa = _swa_mixer(swa_qkv, swa_sink[l], L)
        ret_f, ret_b = _ret_mixer(p_ret, _ret_tables(ret_log_gamma[l], L, S), L)
        w_out_p = _take_static(w_out[l], w_out_perm, 0).astype(BF16)
        last = l == DEPTH - 1
        if last and l % 2 == 1:
            h = _out_proj(h, mod, o_mla, hg_f, hg_b, p_hg, o_swa, ret_f, ret_b, p_ret,
                          hg_norm_g[l], ret_gn_g[l], ret_gn_b[l], w_out_p, L, tm=math.gcd(L, N, 256), row_off=L)
            return _moe_ffn(h, mod, norm_ffn_g[l], moe_w_router[l // 2], moe_w_gate, moe_w_up, moe_w_down, l // 2,
                            0, final_g=final_norm_g)
        h = _out_proj(h, mod, o_mla, hg_f, hg_b, p_hg, o_swa, ret_f, ret_b, p_ret,
                      hg_norm_g[l], ret_gn_g[l], ret_gn_b[l], w_out_p, L)
        if l % 2 == 0:
            h = _dense_ffn(h, mod, norm_ffn_g[l], ffn_bf16[0], ffn_bf16[1], ffn_bf16[2], l // 2, L,
                           tm=1152 if S % 1152 == 0 else 384)
        else:
            h = _moe_ffn(h, mod, norm_ffn_g[l], moe_w_router[l // 2], moe_w_gate, moe_w_up, moe_w_down, l // 2, L)
    return _final_norm(h, final_norm_g, L)
```

```python
import functools
import math

import numpy as np
import jax
import jax.numpy as jnp
from jax import lax
from jax.experimental import pallas as pl
from jax.experimental.pallas import tpu as pltpu

F32 = jnp.float32
BF16 = jnp.bfloat16

D_MODEL = 1024
DEPTH = 4
GRID_W = 64
ROPE_THETA = 10000.0
NORM_EPS = 1e-6
LB_FLOOR = 1e-30
GROUP_W = 256

MLA_HEADS, MLA_NOPE, MLA_ROPE, MLA_V = 4, 64, 32, 64
MLA_Q_RANK, MLA_KV_RANK = 192, 128
HG_HEADS, HG_DK, HG_DV = 4, 64, 64
SWA_HEADS, SWA_KV_HEADS, SWA_HD, SWA_WINDOW = 4, 2, 64, 128
RET_HEADS, RET_DK, RET_DV = 4, 32, 64
N_EXPERTS, TOP_K = 8, 2

_IN_SIZES = (192, 128, 32, 256, 256, 256, 256, 256, 256, 128, 128, 128, 128, 256, 256)
_OFF = np.concatenate([[0], np.cumsum(_IN_SIZES)]).astype(np.int64)
D_IN = int(_OFF[-1])

W_MLA, W_HG, W_SWA, W_RET = 512, 1280, 512, 768
W_EXT = W_MLA + W_HG + W_SWA + W_RET

LANE = 128
VMEM_LIMIT = 56 * 1024 * 1024


def _cparams(sem):
    return pltpu.CompilerParams(dimension_semantics=sem, vmem_limit_bytes=VMEM_LIMIT)


def _swap_halves(idx, width):
    idx = np.asarray(idx)
    base = (idx // width) * width
    j = idx % width
    return base + (j + width // 2) % width


def _ext_columns():
    z = lambda n: -np.ones((n,), np.int64)
    rng = lambda a, n: np.arange(a, a + n)
    o = _OFF
    k_pe = rng(o[2], 32)
    mla = np.concatenate([rng(o[0], 192), z(64), rng(o[1], 128), k_pe,
                          o[2] + _swap_halves(np.arange(32), 32), z(64)])
    hg = rng(o[3], 1280)
    q_heads = [rng(o[8] + 64 * h, 64) for h in range(4)]
    q_perm = np.concatenate([q_heads[0], q_heads[2], q_heads[1], q_heads[3]])
    swa = np.concatenate([q_perm, rng(o[9], 128), rng(o[10], 128)])
    ret = rng(o[11], 768)
    cols = np.concatenate([mla, hg, swa, ret])
    assert cols.shape[0] == W_EXT
    return cols


_EXT_COLS = _ext_columns()


def _take_static(w, idx, axis):
    idx = np.asarray(idx)
    axis = axis % w.ndim
    pieces, start = [], 0
    for i in range(1, len(idx) + 1):
        run_ends = i == len(idx) or (idx[i] != idx[i - 1] + 1 if idx[i - 1] >= 0 else idx[i] >= 0)
        if run_ends:
            n = i - start
            if idx[start] < 0:
                shape = list(w.shape)
                shape[axis] = n
                pieces.append(jnp.zeros(shape, w.dtype))
            else:
                pieces.append(lax.slice_in_dim(w, int(idx[start]), int(idx[start]) + n, axis=axis))
            start = i
    return jnp.concatenate(pieces, axis=axis)


def _take_cols(w, cols):
    return _take_static(w, cols, -1)


def _axial_angles(n_tok, rot_dim):
    rows = n_tok // GRID_W
    row = jnp.broadcast_to(jnp.arange(rows)[:, None], (rows, GRID_W)).reshape(-1)
    col = jnp.broadcast_to(jnp.arange(GRID_W)[None, :], (rows, GRID_W)).reshape(-1)
    n_freq = rot_dim // 4
    inv = ROPE_THETA ** (-jnp.arange(n_freq, dtype=F32) / n_freq)
    return jnp.concatenate([row.astype(F32)[:, None] * inv, col.astype(F32)[:, None] * inv], axis=-1)


def _rope_tables(ang, n_ctx):
    cos, sin = jnp.cos(ang), jnp.sin(ang)
    c = jnp.concatenate([cos, cos], axis=-1)
    s = jnp.concatenate([-sin, sin], axis=-1)
    r = c.shape[-1]
    c = jnp.concatenate([jnp.ones((n_ctx, r), F32), c], axis=0)
    s = jnp.concatenate([jnp.zeros((n_ctx, r), F32), s], axis=0)
    return c, s


def _split2(x):
    hi = x.astype(BF16)
    lo = (x - hi.astype(F32)).astype(BF16)
    return hi, lo


def _seg_mean(x, ones_bd):
    hi, lo = _split2(x)
    return (jnp.dot(hi, ones_bd, preferred_element_type=F32)
            + jnp.dot(lo, ones_bd, preferred_element_type=F32))


def _silu(x):
    return x / (1.0 + jnp.exp(-x))


def _rot_half(x, head):
    half = head // 2
    lane = lax.broadcasted_iota(jnp.int32, x.shape, 1) % head
    return jnp.where(lane < half, pltpu.roll(x, LANE - half, axis=1), pltpu.roll(x, half, axis=1))


def _row_mod(m_ref, idx, row0, tm, n_ctx):
    rows = row0 + lax.broadcasted_iota(jnp.int32, (tm, 1), 0)
    mc = m_ref[0, 0, idx:idx + 1, :]
    ml = m_ref[0, 1, idx:idx + 1, :]
    return jnp.where(rows < n_ctx, mc, ml)


def _norm_mod(x, g, shift, scale):
    ms = jnp.mean(x * x, axis=-1, keepdims=True)
    y = x * lax.rsqrt(ms + NORM_EPS) * g
    return y * (1.0 + scale) + shift


def _mod_kernel(c_ref, w_ref, b_ref, o_ref):
    s = _silu(c_ref[...])
    o_ref[0] = jnp.dot(s.astype(BF16), w_ref[0].astype(BF16), preferred_element_type=F32) + b_ref[0]


def _modulation(cvec, w_ada, b_ada):
    R, D = cvec.shape
    depth, _, n6 = w_ada.shape
    tn = 1536
    return pl.pallas_call(
        _mod_kernel,
        grid=(depth, n6 // tn),
        in_specs=[pl.BlockSpec((R, D), lambda l, j: (0, 0)),
                  pl.BlockSpec((1, D, tn), lambda l, j: (l, 0, j)),
                  pl.BlockSpec((1, 1, tn), lambda l, j: (l, 0, j))],
        out_specs=pl.BlockSpec((1, R, tn), lambda l, j: (l, 0, j)),
        out_shape=jax.ShapeDtypeStruct((depth, R, n6), F32),
        compiler_params=_cparams(("parallel", "parallel")),
        name="adaln_mod",
    )(cvec, w_ada, b_ada.reshape(depth, 1, n6))


def _win_kernel(h_ref, m_ref, g_ref, w_ref, cq_ref, sq_ref, tk_ref, gq_ref, wq_ref, wqs_ref, gk_ref, wk_ref,
                wv_ref, e_ref, sc_ref, ss_ref, mq_out, mk_out, mv_out, o_hg, sq_out, skv_out, o_ret,
                *, tm, n_ctx):
    row0 = pl.program_id(1) * tm
    a = _norm_mod(h_ref[0], g_ref[...], _row_mod(m_ref, 0, row0, tm, n_ctx),
                  _row_mod(m_ref, 1, row0, tm, n_ctx)).astype(BF16)
    group = lambda c0, w: jnp.dot(a, w_ref[:, c0:c0 + w], preferred_element_type=F32)
    mq_out[0], mk_out[0], mv_out[0] = _mla_prep(group(0, W_MLA), cq_ref, sq_ref, tk_ref, gq_ref, wq_ref, wqs_ref,
                                                gk_ref, wk_ref, wv_ref, e_ref)
    o_hg[0] = group(W_MLA, W_HG)
    sq_out[0], skv_out[0] = _swa_prep(group(W_MLA + W_HG, W_SWA), sc_ref, ss_ref)
    o_ret[0] = group(W_MLA + W_HG + W_SWA, W_RET)


def _in_proj(h, mod, g, w_ext, mla_weights, mla_tables, swa_tables, n_ctx, tm=768):
    B, S, D = h.shape
    full = lambda a: pl.BlockSpec(a.shape, lambda b, i: (0,) * a.ndim)
    tab = pl.BlockSpec((tm, LANE), lambda b, i: (i, 0))
    out_w = ((512, BF16), (512, BF16), (512, BF16), (W_HG, F32), (256, BF16), (256, BF16), (W_RET, F32))
    outs = pl.pallas_call(
        functools.partial(_win_kernel, tm=tm, n_ctx=n_ctx),
        grid=(B, S // tm),
        in_specs=[pl.BlockSpec((1, tm, D), lambda b, i: (b, i, 0)),
                  pl.BlockSpec((1, 2, 8, D), lambda b, i: (b, 0, 0, 0)),
                  pl.BlockSpec((1, D), lambda b, i: (0, 0)),
                  pl.BlockSpec((D, W_EXT), lambda b, i: (0, 0)),
                  tab, tab, tab] + [full(w) for w in mla_weights] + [tab, tab],
        out_specs=[pl.BlockSpec((1, tm, w), lambda b, i: (b, i, 0)) for w, _ in out_w],
        out_shape=[jax.ShapeDtypeStruct((B, S, w), dt) for w, dt in out_w],
        compiler_params=_cparams(("parallel", "parallel")),
        name="in_proj",
    )(h, mod, g.reshape(1, D), w_ext, *mla_tables, *mla_weights, *swa_tables)
    return outs[0:3], outs[3], outs[4:6], outs[6]


def _mla_weights(q_norm_g, w_uq, kv_norm_g, w_ukv):
    qcols, qsw, kcols = [], [], []
    for h in range(MLA_HEADS):
        b = 96 * h
        nope = np.arange(b, b + 64)
        pe = np.arange(b + 64, b + 96)
        pad = -np.ones((32,), np.int64)
        qcols.append(np.concatenate([nope, pe, pad]))
        qsw.append(np.concatenate([-np.ones((64,), np.int64), (b + 64) + _swap_halves(np.arange(32), 32), pad]))
        kcols.append(np.concatenate([np.arange(128 * h, 128 * h + 64), -np.ones((64,), np.int64)]))
    vcols = np.concatenate([np.concatenate([np.arange(128 * h + 64, 128 * h + 128), -np.ones((64,), np.int64)])
                            for h in range(MLA_HEADS)])
    pad_rows = ((0, 256 - MLA_Q_RANK), (0, 0))
    wq = jnp.pad(_take_cols(w_uq, np.concatenate(qcols)), pad_rows).astype(BF16)
    wq_sw = jnp.pad(_take_cols(w_uq, np.concatenate(qsw)), pad_rows).astype(BF16)
    wk = _take_cols(w_ukv, np.concatenate(kcols)).astype(BF16)
    wv = _take_cols(w_ukv, vcols).astype(BF16)
    e = np.zeros((128, 512), np.float32)
    for h in range(MLA_HEADS):
        for j in range(32):
            e[j, 128 * h + 64 + j] = 1.0
            e[32 + j, 128 * h + 64 + j] = 1.0
    gq = jnp.pad(q_norm_g, (0, 256 - MLA_Q_RANK)).reshape(1, 256)
    return gq, wq, wq_sw, kv_norm_g.reshape(1, 128), wk, wv, jnp.asarray(e, BF16)


def _mla_tables(n_lat, n_ctx):
    ang = _axial_angles(n_lat, MLA_ROPE)
    c32, s32 = _rope_tables(ang, n_ctx)
    S = n_lat + n_ctx
    one, zero = jnp.ones((S, 64), F32), jnp.zeros((S, 64), F32)
    cq = jnp.concatenate([one, c32, jnp.zeros((S, 32), F32)], axis=-1)
    sq = jnp.concatenate([zero, s32, jnp.zeros((S, 32), F32)], axis=-1)
    tk = jnp.concatenate([c32, s32, zero], axis=-1)
    return cq, sq, tk


def _mla_prep(p, cq_ref, sq_ref, tk_ref, gq_ref, wq_ref, wqs_ref, gk_ref, wk_ref, wv_ref, e_ref):
    scale = (MLA_NOPE + MLA_ROPE) ** -0.5 * float(np.log2(np.e))
    ql = p[:, 0:256]
    msq = jnp.sum(ql * ql, axis=-1, keepdims=True) * (1.0 / MLA_Q_RANK)
    qn = (ql * lax.rsqrt(msq + NORM_EPS) * gq_ref[...]).astype(BF16)
    kvl = p[:, 256:384]
    msk = jnp.mean(kvl * kvl, axis=-1, keepdims=True)
    kvn = (kvl * lax.rsqrt(msk + NORM_EPS) * gk_ref[...]).astype(BF16)
    q = jnp.dot(qn, wq_ref[...], preferred_element_type=F32)
    qs = jnp.dot(qn, wqs_ref[...], preferred_element_type=F32)
    cq = jnp.concatenate([cq_ref[...]] * MLA_HEADS, axis=-1)
    sq = jnp.concatenate([sq_ref[...]] * MLA_HEADS, axis=-1)
    q_out = ((q * cq + qs * sq) * scale).astype(BF16)
    pe = (p[:, 384:512] * tk_ref[...]).astype(BF16)
    k = (jnp.dot(kvn, wk_ref[...], preferred_element_type=F32)
         + jnp.dot(pe, e_ref[...], preferred_element_type=F32))
    lane = lax.broadcasted_iota(jnp.int32, (1, MLA_HEADS * LANE), 1) % LANE
    ones = jnp.where(lane >= MLA_V, 1.0, 0.0)
    v = jnp.dot(kvn, wv_ref[...], preferred_element_type=F32) + ones
    return q_out, k.astype(BF16), v.astype(BF16)


MLA_SAMPLES = 2


def _mla_attn_kernel(q_ref, k_ref, v_ref, o_ref, *, n_ctx, tq):
    S = k_ref.shape[1]

    lo = lax.broadcasted_iota(jnp.int32, (tq, LANE), 1) < MLA_V

    def attend(nk):
        for n in range(q_ref.shape[0]):
            pv = []
            for h in range(MLA_HEADS):
                q = q_ref[n, :, LANE * h:LANE * (h + 1)]
                k = k_ref[n, 0:nk, LANE * h:LANE * (h + 1)]
                s = lax.dot_general(q, k, (((1,), (1,)), ((), ())), preferred_element_type=F32)
                p = jnp.exp2(s - jnp.max(s, axis=-1, keepdims=True))
                pv.append(jnp.dot(p.astype(BF16), v_ref[n, 0:nk, LANE * h:LANE * (h + 1)],
                                  preferred_element_type=F32))
            for pair in range(MLA_HEADS // 2):
                a, b = pv[2 * pair], pv[2 * pair + 1]
                o_ref[n, :, LANE * pair:LANE * (pair + 1)] = jnp.where(
                    lo, a / pltpu.roll(a, MLA_V, axis=1), pltpu.roll(b, MLA_V, axis=1) / b).astype(o_ref.dtype)

    is_ctx = (pl.program_id(1) + 1) * tq <= n_ctx
    pl.when(is_ctx)(lambda: attend(n_ctx))
    pl.when(jnp.logical_not(is_ctx))(lambda: attend(S))


def _mla_mixer(qkv, n_ctx, tq=256):
    q, k, v = qkv
    B, S, _ = q.shape
    assert n_ctx % tq == 0 and S % tq == 0
    ns = math.gcd(B, MLA_SAMPLES)
    return pl.pallas_call(
        functools.partial(_mla_attn_kernel, n_ctx=n_ctx, tq=tq),
        grid=(B // ns, S // tq),
        in_specs=[pl.BlockSpec((ns, tq, 512), lambda b, i: (b, i, 0)),
                  pl.BlockSpec((ns, S, 512), lambda b, i: (b, 0, 0)),
                  pl.BlockSpec((ns, S, 512), lambda b, i: (b, 0, 0))],
        out_specs=pl.BlockSpec((ns, tq, GROUP_W), lambda b, i: (b, i, 0)),
        out_shape=jax.ShapeDtypeStruct((B, S, GROUP_W), BF16),
        compiler_params=_cparams(("parallel", "arbitrary")),
        name="mla_attn",
    )(q, k, v)


SWA_SAMPLES = 8


def _swa_tables(n_lat, n_ctx):
    ang = _axial_angles(n_lat, SWA_HD)
    c64, s64 = _rope_tables(ang, n_ctx)
    return jnp.concatenate([c64, c64], axis=-1), jnp.concatenate([s64, s64], axis=-1)


def _swa_prep(p, c_ref, s_ref):
    scale = SWA_HD ** -0.5
    c, s = c_ref[...], s_ref[...]
    rope = lambda x: x * c + _rot_half(x, SWA_HD) * s
    q = jnp.concatenate([rope(p[:, 0:LANE]), rope(p[:, LANE:2 * LANE])], axis=-1) * scale
    return q.astype(BF16), jnp.concatenate([rope(p[:, 256:384]), p[:, 384:512]], axis=-1).astype(BF16)


def _swa_attn_kernel(sink_ref, q_ref, kvc_ref, kvp_ref, kvn_ref, kvx_ref, bias_ref, o_ref, *, n_ctx, blk):
    j = pl.program_id(1)
    ncb = n_ctx // blk
    samples = range(q_ref.shape[0])
    lane = lax.broadcasted_iota(jnp.int32, (blk, LANE), 1)
    lo = lane < SWA_HD
    zero = jnp.zeros((), BF16)
    rb = lax.broadcasted_iota(jnp.int32, (4 * blk, 1), 0) // blk
    sink = jnp.where(rb == 0, sink_ref[0], jnp.where(rb == 1, sink_ref[1],
                                                     jnp.where(rb == 2, sink_ref[2], sink_ref[3])))

    def stacked_q(n):
        qa, qb = q_ref[n, :, 0:LANE], q_ref[n, :, LANE:2 * LANE]
        return jnp.concatenate([jnp.where(lo, qa, zero), jnp.where(lo, qb, zero),
                                jnp.where(lo, zero, qa), jnp.where(lo, zero, qb)], axis=0)

    def attend(key_refs, bias):
        scores, values = [], []
        for n in samples:
            kcat = jnp.concatenate([r[n, :, 0:LANE] for r in key_refs], axis=0)
            values.append(jnp.concatenate([r[n, :, LANE:2 * LANE] for r in key_refs], axis=0))
            s = lax.dot_general(stacked_q(n), kcat, (((1,), (1,)), ((), ())), preferred_element_type=F32)
            scores.append(s if bias is None else s + bias)
        for n in samples:
            s = scores[n]
            m = jnp.maximum(jnp.max(s, axis=-1, keepdims=True), sink)
            p = jnp.exp(s - m)
            l = jnp.sum(p, axis=-1, keepdims=True) + jnp.exp(sink - m)
            r = jnp.dot(p.astype(BF16), values[n], preferred_element_type=F32) / l
            o_ref[n, :, 0:LANE] = jnp.where(lo, r[0:blk], r[2 * blk:3 * blk]).astype(o_ref.dtype)
            o_ref[n, :, LANE:2 * LANE] = jnp.where(lo, r[blk:2 * blk], r[3 * blk:4 * blk]).astype(o_ref.dtype)

    def ctx_block():
        attend((kvc_ref,), None)

    def lat_block():
        attend((kvc_ref, kvp_ref, kvn_ref, kvx_ref), jnp.concatenate([bias_ref[0]] * SWA_HEADS, axis=0))

    pl.when(j < ncb)(ctx_block)
    pl.when(j >= ncb)(lat_block)


def _swa_mixer(qkv, sink, n_ctx, blk=128):
    q, kv_all = qkv
    B, S, _ = q.shape
    nb = S // blk
    ncb = n_ctx // blk
    cur = lambda b, j: (b, j, 0)
    prev = lambda b, j: (b, jnp.maximum(j - 1, ncb), 0)
    nxt = lambda b, j: (b, jnp.minimum(j + 1, nb - 1), 0)
    ns = math.gcd(B, SWA_SAMPLES)
    kv = lambda f: pl.BlockSpec((ns, blk, 2 * LANE), f)
    ctx = pl.BlockSpec((ns, n_ctx, 2 * LANE), lambda b, j: (b, 0, 0))
    t = np.arange(blk)[:, None]
    c = np.arange(n_ctx + 3 * blk)[None, :] - n_ctx
    bias = np.stack([np.where((c < 0) | ((c >= 0) & (c < blk) & (c >= t) & bool(hp)) | ((c >= blk) & (c < 2 * blk))
                              | ((c >= 2 * blk) & (c - 2 * blk <= t) & bool(hn)), 0.0, -1e30)
                     for hp in (0, 1) for hn in (0, 1)]).astype(np.float32)
    kind = lambda b, j: (2 * (j > ncb).astype(jnp.int32) + (j < nb - 1).astype(jnp.int32), 0, 0)
    return pl.pallas_call(
        functools.partial(_swa_attn_kernel, n_ctx=n_ctx, blk=blk),
        grid=(B // ns, nb),
        in_specs=[pl.BlockSpec(memory_space=pltpu.SMEM),
                  pl.BlockSpec((ns, blk, 256), cur), ctx, kv(prev), kv(cur), kv(nxt),
                  pl.BlockSpec((1, blk, n_ctx + 3 * blk), kind)],
        out_specs=pl.BlockSpec((ns, blk, GROUP_W), cur),
        out_shape=jax.ShapeDtypeStruct((B, S, GROUP_W), BF16),
        compiler_params=_cparams(("parallel", "arbitrary")),
        name="swa_attn",
    )(sink.astype(F32), q, kv_all, kv_all, kv_all, kv_all, jnp.asarray(bias))


def _bwd_chunk(j, ncc, nc):
    return jnp.where(j < ncc, ncc - 1 - j, nc - 1 - (j - ncc))


def _head_stack(x, width, heads):
    lane_h = lax.broadcasted_iota(jnp.int32, x.shape, 1) // width
    zero = jnp.zeros((), x.dtype)
    return jnp.concatenate([jnp.where(lane_h == h, x, zero) for h in range(heads)], axis=0)


def _head_unstack(y, rows, width, heads):
    lane_h = lax.broadcasted_iota(jnp.int32, (rows, heads * width), 1) // width
    out = y[0:rows]
    for h in range(1, heads):
        out = jnp.where(lane_h == h, y[h * rows:(h + 1) * rows], out)
    return out


RET_CHUNK = 256
RET_SAMPLES = 8


def _ret_tables(log_gamma, n_ctx, S):
    C = RET_CHUNK
    t = jnp.arange(C, dtype=F32)
    rel = t[:, None] - t[None, :]
    lg = log_gamma.astype(F32)
    lg_l = jnp.repeat(lg, RET_DK, axis=1)
    dm, xi, zeta, dcay = [], [], [], []
    for d in range(2):
        r = rel if d == 0 else -rel
        dh = jnp.where(r >= 0, jnp.exp(jnp.maximum(r, 0.0)[None] * lg[d][:, None, None]), 0.0)
        dm.append(dh.reshape(RET_HEADS * C, C))
        tq = (t + 1.0) if d == 0 else (C - t)
        tk = (C - 1.0 - t) if d == 0 else t
        xi.append(jnp.exp(tq[:, None] * lg_l[d][None, :]))
        zeta.append(jnp.exp(tk[:, None] * lg_l[d][None, :]))
        dcay.append(jnp.broadcast_to(jnp.exp(C * lg_l[d])[:, None], (RET_HEADS * RET_DK, GROUP_W)))
    mask = (np.arange(128)[:, None] // RET_DK == np.arange(256)[None, :] // RET_DV).astype(np.float32)
    pos = jnp.arange(S, dtype=F32)
    inv = ROPE_THETA ** (-jnp.arange(RET_DK // 2, dtype=F32) / (RET_DK // 2))
    ang = pos[:, None] * inv
    cos, sin = jnp.cos(ang), jnp.sin(ang)
    c = jnp.tile(jnp.concatenate([cos, cos], axis=-1), (1, RET_HEADS))
    s = jnp.tile(jnp.concatenate([-sin, sin], axis=-1), (1, RET_HEADS))
    return (jnp.stack(dm), jnp.stack(xi), jnp.stack(zeta), jnp.stack(dcay), jnp.asarray(mask), c, s)


def _ret_kernel(qkf_ref, vf_ref, cf_ref, sf_ref, qkb_ref, vb_ref, cb_ref, sb_ref,
                dm_ref, xi_ref, zeta_ref, dcay_ref, mask_ref, of_ref, ob_ref, state):
    C = RET_CHUNK

    @pl.when(pl.program_id(1) == 0)
    def _():
        state[...] = jnp.zeros_like(state)

    dirs = ((qkf_ref, vf_ref, cf_ref, sf_ref, of_ref), (qkb_ref, vb_ref, cb_ref, sb_ref, ob_ref))
    for n in range(qkf_ref.shape[0]):
        for d, (qk_ref, v_ref, c_ref, s_ref, o_ref) in enumerate(dirs):
            c, s = c_ref[...], s_ref[...]
            q, k = qk_ref[n, :, 0:128], qk_ref[n, :, 128:256]
            qr = (q * c + _rot_half(q, RET_DK) * s) * (RET_DK ** -0.5)
            kr = k * c + _rot_half(k, RET_DK) * s
            vb = v_ref[n].astype(BF16)
            qs = _head_stack(qr.astype(BF16), RET_DK, RET_HEADS)
            sc = lax.dot_general(qs, kr.astype(BF16), (((1,), (1,)), ((), ())), preferred_element_type=F32)
            a = (sc * dm_ref[d]).astype(BF16)
            o = _head_unstack(jnp.dot(a, vb, preferred_element_type=F32), C, RET_DV, RET_HEADS)
            st = state[n, d]
            o = o + jnp.dot((qr * xi_ref[d]).astype(BF16), st.astype(BF16), preferred_element_type=F32)
            o_ref[n] = o.astype(o_ref.dtype)
            kz = (kr * zeta_ref[d]).astype(BF16)
            u = lax.dot_general(kz, vb, (((0,), (0,)), ((), ())), preferred_element_type=F32)
            state[n, d] = st * dcay_ref[d] + u * mask_ref[...]


def _ret_mixer(p_ret, tables, n_ctx):
    B, S, _ = p_ret.shape
    C = RET_CHUNK
    dm, xi, zeta, dcay, mask, c, s = tables
    nc, ncc = S // C, n_ctx // C
    fwd = lambda b, j: (b, j, 0)
    bwd = lambda b, j: (b, _bwd_chunk(j, ncc, nc), 0)
    fwd_v = lambda b, j: (b, j, 1)
    bwd_v = lambda b, j: (b, _bwd_chunk(j, ncc, nc), 1)
    fwd_t = lambda b, j: (j, 0)
    bwd_t = lambda b, j: (_bwd_chunk(j, ncc, nc), 0)
    full = lambda a: pl.BlockSpec(a.shape, lambda b, j: (0,) * a.ndim)
    ns = math.gcd(B, RET_SAMPLES)
    return pl.pallas_call(
        _ret_kernel,
        grid=(B // ns, nc),
        in_specs=[pl.BlockSpec((ns, C, 256), fwd), pl.BlockSpec((ns, C, 256), fwd_v),
                  pl.BlockSpec((C, LANE), fwd_t), pl.BlockSpec((C, LANE), fwd_t),
                  pl.BlockSpec((ns, C, 256), bwd), pl.BlockSpec((ns, C, 256), bwd_v),
                  pl.BlockSpec((C, LANE), bwd_t), pl.BlockSpec((C, LANE), bwd_t),
                  full(dm), full(xi), full(zeta), full(dcay), full(mask)],
        out_specs=[pl.BlockSpec((ns, C, GROUP_W), fwd), pl.BlockSpec((ns, C, GROUP_W), bwd)],
        out_shape=[jax.ShapeDtypeStruct((B, S, GROUP_W), BF16)] * 2,
        scratch_shapes=[pltpu.VMEM((ns, 2, RET_HEADS * RET_DK, GROUP_W), F32)],
        compiler_params=_cparams(("parallel", "arbitrary")),
        name="retention_scan",
    )(p_ret, p_ret, c, s, p_ret, p_ret, c, s, dm, xi, zeta, dcay, mask)


HG_CHUNK = 128
HG_SAMPLES = 4


def _hg_level_tables(C):
    levels = []
    n = 2
    while n <= C:
        levels.append(n)
        n *= 2
    t = np.arange(C)
    mats, qroles, kroles = [], [], []
    for d in range(2):
        rows = []
        qr_d, kr_d = [], []
        for n in levels:
            m = n // 2
            start = (t // n) * n
            mid = start + m
            u = t - start
            M = np.zeros((C, C), np.float32)
            for i in range(C):
                if d == 0:
                    if u[i] >= m:
                        M[i, mid[i]:i + 1] = 1.0
                    else:
                        M[i, i + 1:mid[i]] = 1.0
                else:
                    if u[i] < m:
                        M[i, i:mid[i]] = 1.0
                    else:
                        M[i, mid[i]:i] = 1.0
            rows.append(M)
            qr_d.append((u >= m) if d == 0 else (u < m))
            kr_d.append((u < m) if d == 0 else (u >= m))
        Mq = np.zeros((C, C), np.float32)
        Mk = np.zeros((C, C), np.float32)
        for i in range(C):
            if d == 0:
                Mq[i, 0:i + 1] = 1.0
                Mk[i, i + 1:C] = 1.0
            else:
                Mq[i, i:C] = 1.0
                Mk[i, 0:i] = 1.0
        rows += [Mq, Mk, np.ones((8, C), np.float32)]
        mats.append(np.concatenate(rows, axis=0))
        qroles.append(np.stack(qr_d))
        kroles.append(np.stack(kr_d))
    same_block = np.stack([(t[:, None] // n == t[None, :] // n) for n in levels]).astype(np.float32)
    return levels, np.stack(mats), np.stack(qroles), np.stack(kroles), same_block


def _hg_kernel(qf_ref, zf_ref, if_ref, qb_ref, zb_ref, ib_ref, lb_ref, m_ref, role_ref, blk_ref,
               ones_ref, mask_ref, of_ref, ob_ref, state, *, n_levels):
    C = HG_CHUNK

    @pl.when(pl.program_id(1) == 0)
    def _():
        state[...] = jnp.zeros_like(state)

    dirs = ((qf_ref, zf_ref, if_ref), (qb_ref, zb_ref, ib_ref))
    chains = [(s, d) for s in range(qf_ref.shape[0]) for d in range(2)]
    st = [state[s, d] for s, d in chains]
    q, v, vb, kk, r = [], [], [], [], []
    for c, (s, d) in enumerate(chains):
        q_ref, z_ref, i_ref = dirs[d]
        z = z_ref[s]
        lb = lb_ref[d]
        lbf = jnp.maximum(lb, LB_FLOOR)
        t = jnp.exp(-jnp.abs(z))
        rcp = 1.0 / (1.0 + t)
        pos = z >= 0.0
        sig = jnp.where(pos, rcp, t * rcp)
        nsig = jnp.where(pos, t * rcp, rcp)
        logf = jnp.log(lbf + (1.0 - lb) * sig)
        kk.append((1.0 - lb) * nsig - (lbf - lb))
        hi, lo = _split2(logf)
        r.append(jnp.dot(m_ref[d], hi, preferred_element_type=F32)
                 + jnp.dot(m_ref[d], lo, preferred_element_type=F32))
        q.append(q_ref[s] * (HG_DK ** -0.5))
        v.append(i_ref[s])
        vb.append(v[c].astype(BF16))
    acc = [jnp.zeros((HG_HEADS * C, C), F32) for _ in chains]
    for lv in range(n_levels):
        for c, (s, d) in enumerate(chains):
            e = jnp.exp(r[c][lv * C:(lv + 1) * C])
            roles = role_ref[d, lv]
            qe = jnp.where(roles > 0.5, q[c] * e, 0.0).astype(BF16)
            ke = jnp.where(roles > 0.5, 0.0, kk[c] * e).astype(BF16)
            a = lax.dot_general(_head_stack(qe, HG_DK, HG_HEADS), ke, (((1,), (1,)), ((), ())),
                                preferred_element_type=F32)
            acc[c] = acc[c] + a * blk_ref[lv]
    outs = []
    for c in range(len(chains)):
        o = _head_unstack(jnp.dot(acc[c].astype(BF16), vb[c], preferred_element_type=F32), C, HG_DV, HG_HEADS)
        o = o + jnp.dot((q[c] * kk[c]).astype(BF16), ones_ref[...], preferred_element_type=F32) * v[c]
        e_q = jnp.exp(r[c][n_levels * C:(n_levels + 1) * C])
        o = o + lax.dot_general((q[c] * e_q).astype(BF16), st[c].astype(BF16), (((1,), (1,)), ((), ())),
                                preferred_element_type=F32)
        outs.append(o)
    new_state = []
    for c in range(len(chains)):
        e_k = jnp.exp(r[c][(n_levels + 1) * C:(n_levels + 2) * C])
        ku = (kk[c] * e_k).astype(BF16)
        u = lax.dot_general(vb[c], ku, (((0,), (0,)), ((), ())), preferred_element_type=F32)
        dtot = jnp.exp(r[c][(n_levels + 2) * C:(n_levels + 2) * C + 1])
        new_state.append(st[c] * dtot + u * mask_ref[...])
    for c, (s, d) in enumerate(chains):
        o_ref = (of_ref, ob_ref)[d]
        o_ref[s] = outs[c].astype(o_ref.dtype)
        state[s, d] = new_state[c]


def _hg_mixer(p_hg, lb, n_ctx):
    B, S, _ = p_hg.shape
    C = HG_CHUNK
    levels, mats, qroles, kroles, same_block = _hg_level_tables(C)
    nl = len(levels)
    m_all = jnp.asarray(mats, BF16)
    roles = jnp.asarray(np.broadcast_to(qroles[..., None], (2, nl, C, GROUP_W)).astype(np.float32))
    blk = jnp.asarray(np.tile(same_block, (1, HG_HEADS, 1)))
    hd = np.arange(GROUP_W) // HG_DK
    bd = (hd[:, None] == hd[None, :]).astype(np.float32)
    ones_bd = jnp.asarray(bd, BF16)
    mask = jnp.asarray(bd)
    nc, ncc = S // C, n_ctx // C
    col = lambda k, rev: (lambda b, j: (b, _bwd_chunk(j, ncc, nc) if rev else j, k))
    ns = math.gcd(B, HG_SAMPLES)
    blkspec = lambda k, rev: pl.BlockSpec((ns, C, GROUP_W), col(k, rev))
    full = lambda a: pl.BlockSpec(a.shape, lambda b, j: (0,) * a.ndim)
    lb3 = lb.reshape(2, 1, GROUP_W).astype(F32)
    return pl.pallas_call(
        functools.partial(_hg_kernel, n_levels=nl),
        grid=(B // ns, nc),
        in_specs=[blkspec(0, False), blkspec(1, False), blkspec(3, False),
                  blkspec(0, True), blkspec(2, True), blkspec(3, True),
                  full(lb3), full(m_all), full(roles), full(blk), full(ones_bd), full(mask)],
        out_specs=[blkspec(0, False), blkspec(0, True)],
        out_shape=[jax.ShapeDtypeStruct((B, S, GROUP_W), BF16)] * 2,
        scratch_shapes=[pltpu.VMEM((ns, 2, GROUP_W, GROUP_W), F32)],
        compiler_params=_cparams(("parallel", "arbitrary")),
        name="hgrn2_scan",
    )(p_hg, p_hg, p_hg, p_hg, p_hg, p_hg, lb3, m_all, roles, blk, ones_bd, mask)


_SWA_OUT_PERM = np.concatenate([np.arange(64 * h, 64 * h + 64) for h in (0, 2, 1, 3)])


def _wout_kernel(h_ref, m_ref, a_ref, hf_ref, hb_ref, hgate_ref, c_ref, rf_ref, rb_ref, rgate_ref,
                 hgn_ref, rgg_ref, rgb_ref, ones_ref, w_ref, o_ref, *, tm, n_ctx, row_off):
    row0 = row_off + pl.program_id(1) * tm
    ones = ones_ref[...]
    o = hf_ref[0].astype(F32) + hb_ref[0].astype(F32)
    yb = o * lax.rsqrt(_seg_mean(o * o, ones) + NORM_EPS) * hgn_ref[...] * _silu(hgate_ref[0])
    o = rf_ref[0].astype(F32) + rb_ref[0].astype(F32)
    xc = o - _seg_mean(o, ones)
    yd = (xc * lax.rsqrt(_seg_mean(xc * xc, ones) + NORM_EPS) * rgg_ref[...] + rgb_ref[...]) * _silu(rgate_ref[0])
    acc = jnp.dot(a_ref[0].astype(BF16), w_ref[0:256, :], preferred_element_type=F32)
    acc += jnp.dot(yb.astype(BF16), w_ref[256:512, :], preferred_element_type=F32)
    acc += jnp.dot(c_ref[0].astype(BF16), w_ref[512:768, :], preferred_element_type=F32)
    acc += jnp.dot(yd.astype(BF16), w_ref[768:1024, :], preferred_element_type=F32)
    o_ref[0] = h_ref[0] + _row_mod(m_ref, 2, row0, tm, n_ctx) * acc


def _out_proj(h, mod, o_mla, hg_f, hg_b, p_hg, o_swa, ret_f, ret_b, p_ret, hg_norm_g, ret_gn_g, ret_gn_b,
              w_out_p, n_ctx, tm=768, row_off=0):
    B, S, D = h.shape
    assert row_off % tm == 0 and (S - row_off) % tm == 0
    off = row_off // tm
    hd = np.arange(GROUP_W) // HG_DV
    ones_bd = jnp.asarray((hd[:, None] == hd[None, :]).astype(np.float32) / HG_DV, BF16)
    row = lambda w, k=0: pl.BlockSpec((1, tm, w), lambda b, i: (b, i + off, k))
    vec = pl.BlockSpec((1, GROUP_W), lambda b, i: (0, 0))
    return pl.pallas_call(
        functools.partial(_wout_kernel, tm=tm, n_ctx=n_ctx, row_off=row_off),
        grid=(B, (S - row_off) // tm),
        in_specs=[row(D), pl.BlockSpec((1, 2, 8, D), lambda b, i: (b, 0, 0, 0)),
                  row(GROUP_W), row(GROUP_W), row(GROUP_W), row(GROUP_W, 4),
                  row(GROUP_W), row(GROUP_W), row(GROUP_W), row(GROUP_W, 2),
                  vec, vec, vec, pl.BlockSpec((GROUP_W, GROUP_W), lambda b, i: (0, 0)),
                  pl.BlockSpec((D, D), lambda b, i: (0, 0))],
        out_specs=pl.BlockSpec((1, tm, D), lambda b, i: (b, i, 0)),
        out_shape=jax.ShapeDtypeStruct((B, S - row_off, D), F32),
        compiler_params=_cparams(("parallel", "parallel")),
        name="out_proj",
    )(h, mod, o_mla, hg_f, hg_b, p_hg, o_swa, ret_f, ret_b, p_ret,
      hg_norm_g.reshape(1, GROUP_W), ret_gn_g.reshape(1, GROUP_W), ret_gn_b.reshape(1, GROUP_W), ones_bd, w_out_p)


def _ffn_kernel(h_ref, m_ref, g_ref, wg_ref, wu_ref, wd_ref, o_ref, xn, *, tm, n_ctx):
    j = pl.program_id(2)
    row0 = pl.program_id(1) * tm

    @pl.when(j == 0)
    def _():
        xn[...] = _norm_mod(h_ref[0], g_ref[...], _row_mod(m_ref, 3, row0, tm, n_ctx),
                            _row_mod(m_ref, 4, row0, tm, n_ctx)).astype(BF16)
        o_ref[...] = jnp.zeros_like(o_ref)

    x = xn[...]
    hg = jnp.dot(x, wg_ref[0], preferred_element_type=F32)
    hu = jnp.dot(x, wu_ref[0], preferred_element_type=F32)
    o_ref[0] += jnp.dot((_silu(hg) * hu).astype(BF16), wd_ref[0], preferred_element_type=F32)

    @pl.when(j == pl.num_programs(2) - 1)
    def _():
        o_ref[0] = h_ref[0] + _row_mod(m_ref, 5, row0, tm, n_ctx) * o_ref[0]


def _dense_ffn(h, mod, g, w_gate, w_up, w_down, li, n_ctx, tm=1152, tf=1408):
    B, S, D = h.shape
    dff = w_gate.shape[-1]
    return pl.pallas_call(
        functools.partial(_ffn_kernel, tm=tm, n_ctx=n_ctx),
        grid=(B, S // tm, dff // tf),
        in_specs=[pl.BlockSpec((1, tm, D), lambda b, i, j: (b, i, 0)),
                  pl.BlockSpec((1, 2, 8, D), lambda b, i, j: (b, 0, 0, 0)),
                  pl.BlockSpec((1, D), lambda b, i, j: (0, 0)),
                  pl.BlockSpec((1, D, tf), lambda b, i, j: (li, 0, j)),
                  pl.BlockSpec((1, D, tf), lambda b, i, j: (li, 0, j)),
                  pl.BlockSpec((1, tf, D), lambda b, i, j: (li, j, 0))],
        out_specs=pl.BlockSpec((1, tm, D), lambda b, i, j: (b, i, 0)),
        out_shape=jax.ShapeDtypeStruct((B, S, D), F32),
        scratch_shapes=[pltpu.VMEM((tm, D), BF16)],
        compiler_params=_cparams(("parallel", "parallel", "arbitrary")),
        name="dense_ffn",
    )(h, mod, g.reshape(1, D), w_gate, w_up, w_down)


MOE_TM = 1024
MOE_TF = 512
MOE_ROWS = (384, 256)
DMA_UNROLL = 8
SUBLANES = 8
ZERO_ROWS = 256


def _router_kernel(h_ref, m_ref, g_ref, wr_ref, tri_ref, xn_ref, route_ref, cnt_ref, counts, *, tm, n_ctx):
    b, i = pl.program_id(0), pl.program_id(1)
    row0 = i * tm

    @pl.when((b == 0) & (i == 0))
    def _():
        counts[...] = jnp.zeros_like(counts)

    a = _norm_mod(h_ref[0], g_ref[...], _row_mod(m_ref, 3, row0, tm, n_ctx), _row_mod(m_ref, 4, row0, tm, n_ctx))
    xn_ref[0] = a
    ah, al = _split2(a)
    wh, wl = _split2(wr_ref[...])
    logits = (jnp.dot(ah, wh, preferred_element_type=F32) + jnp.dot(ah, wl, preferred_element_type=F32)
              + jnp.dot(al, wh, preferred_element_type=F32))
    lane = lax.broadcasted_iota(jnp.int32, logits.shape, 1)
    neg = jnp.float32(-jnp.inf)
    lg = jnp.where(lane < N_EXPERTS, logits, neg)
    v1 = jnp.max(lg, axis=-1, keepdims=True)
    i1 = jnp.min(jnp.where(lg == v1, lane, LANE), axis=-1, keepdims=True)
    lg2 = jnp.where(lane == i1, neg, lg)
    v2 = jnp.max(lg2, axis=-1, keepdims=True)
    i2 = jnp.min(jnp.where(lg2 == v2, lane, LANE), axis=-1, keepdims=True)
    e = jnp.exp(v2 - v1)
    w1 = 1.0 / (1.0 + e)
    w2 = e / (1.0 + e)
    oh1, oh2 = lane == i1, lane == i2
    onehot = jnp.where(oh1 | oh2, 1.0, 0.0)
    earlier = jnp.dot(tri_ref[...], onehot.astype(BF16), preferred_element_type=F32)
    pos = earlier + counts[...]
    r1 = jnp.sum(jnp.where(oh1, pos, 0.0), axis=-1, keepdims=True)
    r2 = jnp.sum(jnp.where(oh2, pos, 0.0), axis=-1, keepdims=True)
    counts[...] = counts[...] + jnp.sum(onehot, axis=0, keepdims=True)
    cnt_ref[...] = jnp.broadcast_to(counts[...], cnt_ref.shape)
    route_ref[0] = jnp.where(lane == 0, i1.astype(F32),
                             jnp.where(lane == 1, i2.astype(F32),
                                       jnp.where(lane == 2, w1,
                                                 jnp.where(lane == 3, w2,
                                                           jnp.where(lane == 4, r1, jnp.where(lane == 5, r2, 0.0))))))


def _scatter_kernel(meta_ref, slot_ref, x_ref, xs_hbm, zeros_v, sem, *, tm, tile_rows, total_rows):
    b, i = pl.program_id(0), pl.program_id(1)

    def start(r, carry):
        for k in range(TOP_K):
            pltpu.make_async_copy(x_ref.at[0, pl.ds(r, 1)], xs_hbm.at[pl.ds(slot_ref[0, 0, 0, k * tm + r], 1)],
                                  sem).start(priority=k)
        return carry

    lax.fori_loop(0, tm, start, 0, unroll=DMA_UNROLL)
    for k in range(TOP_K):
        pltpu.make_async_copy(x_ref.at[0], xs_hbm.at[pl.ds(0, tm)], sem).wait()

    @pl.when((b == pl.num_programs(0) - 1) & (i == pl.num_programs(1) - 1))
    def _():
        zeros_v[...] = jnp.zeros_like(zeros_v)
        z1 = zeros_v.at[pl.ds(0, 1)]
        z8 = zeros_v.at[pl.ds(0, SUBLANES)]

        def fill(lo, end):
            mid = jnp.minimum(((lo + SUBLANES - 1) // SUBLANES) * SUBLANES, end)
            big = mid + ((end - mid) // ZERO_ROWS) * ZERO_ROWS
            n_row, n_big, n_small = mid - lo, (big - mid) // ZERO_ROWS, (end - big) // SUBLANES

            def start_row(r, c):
                pltpu.make_async_copy(z1, xs_hbm.at[pl.ds(lo + r, 1)], sem).start()
                return c

            def wait_row(r, c):
                pltpu.make_async_copy(z1, xs_hbm.at[pl.ds(0, 1)], sem).wait()
                return c

            def start_big(q, c):
                dst = xs_hbm.at[pl.ds(pl.multiple_of(mid + q * ZERO_ROWS, SUBLANES), ZERO_ROWS)]
                pltpu.make_async_copy(zeros_v, dst, sem).start()
                return c

            def wait_big(q, c):
                pltpu.make_async_copy(zeros_v, xs_hbm.at[pl.ds(0, ZERO_ROWS)], sem).wait()
                return c

            def start_small(q, c):
                dst = xs_hbm.at[pl.ds(pl.multiple_of(big + q * SUBLANES, SUBLANES), SUBLANES)]
                pltpu.make_async_copy(z8, dst, sem).start()
                return c

            def wait_small(q, c):
                pltpu.make_async_copy(z8, xs_hbm.at[pl.ds(0, SUBLANES)], sem).wait()
                return c

            lax.fori_loop(0, n_row, start_row, 0)
            lax.fori_loop(0, n_big, start_big, 0)
            lax.fori_loop(0, n_small, start_small, 0)
            lax.fori_loop(0, n_row, wait_row, 0)
            lax.fori_loop(0, n_big, wait_big, 0)
            lax.fori_loop(0, n_small, wait_small, 0)

        for e in range(N_EXPERTS):
            c = meta_ref[e]
            g0 = meta_ref[N_EXPERTS + e]
            fill(g0 + c, g0 + ((c + tile_rows - 1) // tile_rows) * tile_rows)
        fill(meta_ref[2 * N_EXPERTS] * tile_rows, jnp.int32(total_rows))


def _expert_kernel(te_ref, nu_ref, x_ref, wg_ref, wu_ref, wd_ref, o_ref, xb):
    i, j = pl.program_id(0), pl.program_id(1)

    @pl.when(j == 0)
    def _():
        xb[...] = x_ref[...].astype(BF16)
        o_ref[...] = jnp.zeros_like(o_ref)

    @pl.when(i < nu_ref[0])
    def _():
        wg, wu, wd = wg_ref[0, 0].astype(BF16), wu_ref[0, 0].astype(BF16), wd_ref[0, 0].astype(BF16)
        x = xb[...]
        hg = jnp.dot(x, wg, preferred_element_type=F32)
        hu = jnp.dot(x, wu, preferred_element_type=F32)
        o_ref[...] += jnp.dot((_silu(hg) * hu).astype(BF16), wd, preferred_element_type=F32)


def _combine_kernel(slot_ref, h_ref, m_ref, route_ref, fg_ref, y_hbm, o_ref, buf, sem, *, tm, n_ctx, final_norm):
    row0 = pl.program_id(1) * tm

    def start(r, carry):
        for k in range(TOP_K):
            pltpu.make_async_copy(y_hbm.at[pl.ds(slot_ref[0, 0, 0, k * tm + r], 1)], buf.at[k, pl.ds(r, 1)],
                                  sem).start(priority=k)
        return carry

    lax.fori_loop(0, tm, start, 0, unroll=DMA_UNROLL)
    for k in range(TOP_K):
        pltpu.make_async_copy(y_hbm.at[pl.ds(0, tm)], buf.at[k], sem).wait()
    w1 = route_ref[0, :, 2:3]
    w2 = route_ref[0, :, 3:4]
    y = h_ref[0] + _row_mod(m_ref, 5, row0, tm, n_ctx) * (w1 * buf[0] + w2 * buf[1])
    if final_norm:
        y = y * lax.rsqrt(jnp.mean(y * y, axis=-1, keepdims=True) + NORM_EPS) * fg_ref[...]
    o_ref[0] = y


def _moe_ffn(h, mod, g, w_router, w_gate, w_up, w_down, li, n_ctx, final_g=None):
    B, S, D = h.shape
    T = B * S
    cr = next(r for r in MOE_ROWS if S % r == 0)
    tm = cr
    dffe = w_gate.shape[-1]
    te_rows = MOE_TM if T % MOE_TM == 0 else 512
    n_tiles = TOP_K * T // te_rows + N_EXPERTS
    P = n_tiles * te_rows
    wr = jnp.pad(w_router, ((0, 0), (0, LANE - N_EXPERTS)))
    tri = jnp.asarray(np.tril(np.ones((tm, tm), np.float32), -1), BF16)
    xn, route, cnt = pl.pallas_call(
        functools.partial(_router_kernel, tm=tm, n_ctx=n_ctx),
        grid=(B, S // tm),
        in_specs=[pl.BlockSpec((1, tm, D), lambda b, i: (b, i, 0)),
                  pl.BlockSpec((1, 2, 8, D), lambda b, i: (b, 0, 0, 0)),
                  pl.BlockSpec((1, D), lambda b, i: (0, 0)),
                  pl.BlockSpec((D, LANE), lambda b, i: (0, 0)),
                  pl.BlockSpec((tm, tm), lambda b, i: (0, 0))],
        out_specs=[pl.BlockSpec((1, tm, D), lambda b, i: (b, i, 0)),
                   pl.BlockSpec((1, tm, LANE), lambda b, i: (b, i, 0)),
                   pl.BlockSpec((SUBLANES, LANE), lambda b, i: (0, 0))],
        out_shape=[jax.ShapeDtypeStruct((B, S, D), F32), jax.ShapeDtypeStruct((B, S, LANE), F32),
                   jax.ShapeDtypeStruct((SUBLANES, LANE), F32)],
        scratch_shapes=[pltpu.VMEM((1, LANE), F32)],
        compiler_params=_cparams(("arbitrary", "arbitrary")),
        name="moe_router",
    )(h, mod, g.reshape(1, D), wr, tri)

    counts = cnt[0, :N_EXPERTS].astype(jnp.int32)
    ptiles = (counts + te_rows - 1) // te_rows
    tile_end = jnp.cumsum(ptiles)
    gstart = (tile_end - ptiles) * te_rows
    n_used = tile_end[-1:].astype(jnp.int32)
    tile_ids = jnp.arange(n_tiles, dtype=jnp.int32)
    tile_e = jnp.minimum(jnp.sum((tile_ids[:, None] >= tile_end[None, :]).astype(jnp.int32), axis=1),
                         N_EXPERTS - 1)
    meta = jnp.concatenate([counts, gstart, n_used])
    choice = route[..., 0:TOP_K].astype(jnp.int32)
    is_e = choice[..., None] == jnp.arange(N_EXPERTS, dtype=jnp.int32)
    slots = jnp.sum(jnp.where(is_e, gstart, 0), axis=-1) + route[..., 4:4 + TOP_K].astype(jnp.int32)
    slots = slots.reshape(B, S // cr, cr, TOP_K).transpose(0, 1, 3, 2)
    slots = slots.reshape(B, S // cr, 1, TOP_K * cr)

    xs = pl.pallas_call(
        functools.partial(_scatter_kernel, tm=cr, tile_rows=te_rows, total_rows=P),
        grid_spec=pltpu.PrefetchScalarGridSpec(
            num_scalar_prefetch=1,
            grid=(B, S // cr),
            in_specs=[pl.BlockSpec((1, 1, 1, TOP_K * cr), lambda b, i, mt: (b, i, 0, 0),
                                   memory_space=pltpu.SMEM),
                      pl.BlockSpec((1, cr, D), lambda b, i, mt: (b, i, 0))],
            out_specs=pl.BlockSpec(memory_space=pl.ANY),
            scratch_shapes=[pltpu.VMEM((ZERO_ROWS, D), F32), pltpu.SemaphoreType.DMA(())]),
        out_shape=jax.ShapeDtypeStruct((P, D), F32),
        compiler_params=_cparams(("arbitrary", "arbitrary")),
        name="moe_scatter",
    )(meta, slots, xn)

    nf = dffe // MOE_TF
    last = lambda i, nu: jnp.minimum(i, nu[0] - 1)
    jj = lambda i, j, nu: jnp.where(i < nu[0], j, nf - 1)
    ys = pl.pallas_call(
        _expert_kernel,
        grid_spec=pltpu.PrefetchScalarGridSpec(
            num_scalar_prefetch=2,
            grid=(n_tiles, nf),
            in_specs=[pl.BlockSpec((te_rows, D), lambda i, j, te, nu: (last(i, nu), 0)),
                      pl.BlockSpec((1, 1, D, MOE_TF), lambda i, j, te, nu: (li, te[last(i, nu)], 0, jj(i, j, nu))),
                      pl.BlockSpec((1, 1, D, MOE_TF), lambda i, j, te, nu: (li, te[last(i, nu)], 0, jj(i, j, nu))),
                      pl.BlockSpec((1, 1, MOE_TF, D), lambda i, j, te, nu: (li, te[last(i, nu)], jj(i, j, nu), 0))],
            out_specs=pl.BlockSpec((te_rows, D), lambda i, j, te, nu: (i, 0)),
            scratch_shapes=[pltpu.VMEM((te_rows, D), BF16)]),
        out_shape=jax.ShapeDtypeStruct((P, D), F32),
        compiler_params=_cparams(("arbitrary", "arbitrary")),
        name="moe_experts",
    )(tile_e, n_used, xs, w_gate, w_up, w_down)
    fg = jnp.ones((1, D), F32) if final_g is None else final_g.reshape(1, D).astype(F32)
    return pl.pallas_call(
        functools.partial(_combine_kernel, tm=cr, n_ctx=n_ctx, final_norm=final_g is not None),
        grid=(B, S // cr),
        in_specs=[pl.BlockSpec((1, 1, 1, TOP_K * cr), lambda b, i: (b, i, 0, 0), memory_space=pltpu.SMEM),
                  pl.BlockSpec((1, cr, D), lambda b, i: (b, i, 0)),
                  pl.BlockSpec((1, 2, 8, D), lambda b, i: (b, 0, 0, 0)),
                  pl.BlockSpec((1, cr, LANE), lambda b, i: (b, i, 0)),
                  pl.BlockSpec((1, D), lambda b, i: (0, 0)),
                  pl.BlockSpec(memory_space=pl.ANY)],
        out_specs=pl.BlockSpec((1, cr, D), lambda b, i: (b, i, 0)),
        out_shape=jax.ShapeDtypeStruct((B, S, D), F32),
        scratch_shapes=[pltpu.VMEM((TOP_K, cr, D), F32), pltpu.SemaphoreType.DMA(())],
        compiler_params=_cparams(("arbitrary", "arbitrary")),
        name="moe_combine",
    )(slots, h, mod, route, fg, ys)


def _final_norm_kernel(h_ref, g_ref, o_ref):
    x = h_ref[0]
    o_ref[0] = x * lax.rsqrt(jnp.mean(x * x, axis=-1, keepdims=True) + NORM_EPS) * g_ref[...]


def _final_norm(h, g, n_ctx, tm=256):
    B, S, D = h.shape
    N = S - n_ctx
    return pl.pallas_call(
        _final_norm_kernel,
        grid=(B, N // tm),
        in_specs=[pl.BlockSpec((1, tm, D), lambda b, i: (b, n_ctx // tm + i, 0)),
                  pl.BlockSpec((1, D), lambda b, i: (0, 0))],
        out_specs=pl.BlockSpec((1, tm, D), lambda b, i: (b, i, 0)),
        out_shape=jax.ShapeDtypeStruct((B, N, D), F32),
        compiler_params=_cparams(("parallel", "parallel")),
        name="final_norm",
    )(h, g.reshape(1, D))


MOD_ROWS = 16


def kernel(x, c, ctx, c_ctx, w_ada, b_ada, norm_mix_g, norm_ffn_g, w_in, w_out, mla_q_norm_g, mla_w_uq, mla_kv_norm_g, mla_w_ukv, hg_lb_logits, hg_norm_g, swa_sink, ret_decay_logit, ret_gn_g, ret_gn_b, ffn_w_gate, ffn_w_up, ffn_w_down, moe_w_router, moe_w_gate, moe_w_up, moe_w_down, final_norm_g):
    B, N, D = x.shape
    L = ctx.shape[1]
    S = L + N
    assert B + 1 <= MOD_ROWS
    h = jnp.concatenate([ctx, x], axis=1)
    cvec = jnp.zeros((MOD_ROWS, D), F32).at[0].set(c_ctx).at[1:1 + B].set(c)
    mod_all = _modulation(cvec, w_ada, b_ada).reshape(DEPTH, MOD_ROWS, 6, D)

    lb_p = jax.nn.softmax(hg_lb_logits.astype(F32), axis=0)
    hg_lb = jnp.cumsum(lb_p, axis=0) - lb_p[0:1]
    ret_log_gamma = jax.nn.log_sigmoid(ret_decay_logit.astype(F32))

    mla_tab = _mla_tables(N, L)
    swa_tab = _swa_tables(N, L)
    w_out_perm = np.concatenate([np.arange(512), 512 + _SWA_OUT_PERM, np.arange(768, 1024)])
    ffn_bf16 = tuple(w.astype(BF16) for w in (ffn_w_gate, ffn_w_up, ffn_w_down))

    for l in range(DEPTH):
        m = mod_all[l]
        mod = jnp.stack([jnp.broadcast_to(m[0], (B, 6, D)), m[1:1 + B]], axis=1)
        mod = jnp.pad(mod, ((0, 0), (0, 0), (0, 2), (0, 0)))
        w_ext = _take_cols(w_in[l], _EXT_COLS).astype(BF16)
        mla_w = _mla_weights(mla_q_norm_g[l], mla_w_uq[l], mla_kv_norm_g[l], mla_w_ukv[l])
        mla_qkv, p_hg, swa_qkv, p_ret = _in_proj(h, mod, norm_mix_g[l], w_ext, mla_w, mla_tab, swa_tab, L)
        o_mla = _mla_mixer(mla_qkv, L)
        hg_f, hg_b = _hg_mixer(p_hg, hg_lb[l], L)
        o_swa = _swa_mixer(swa_qkv, swa_sink[l], L)
        ret_f, ret_b = _ret_mixer(p_ret, _ret_tables(ret_log_gamma[l], L, S), L)
        w_out_p = _take_static(w_out[l], w_out_perm, 0).astype(BF16)
        last = l == DEPTH - 1
        if last and l % 2 == 1:
            h = _out_proj(h, mod, o_mla, hg_f, hg_b, p_hg, o_swa, ret_f, ret_b, p_ret,
                          hg_norm_g[l], ret_gn_g[l], ret_gn_b[l], w_out_p, L, tm=math.gcd(L, N, 256), row_off=L)
            return _moe_ffn(h, mod, norm_ffn_g[l], moe_w_router[l // 2], moe_w_gate, moe_w_up, moe_w_down, l // 2,
                            0, final_g=final_norm_g)
        h = _out_proj(h, mod, o_mla, hg_f, hg_b, p_hg, o_swa, ret_f, ret_b, p_ret,
                      hg_norm_g[l], ret_gn_g[l], ret_gn_b[l], w_out_p, L)
        if l % 2 == 0:
            h = _dense_ffn(h, mod, norm_ffn_g[l], ffn_bf16[0], ffn_bf16[1], ffn_bf16[2], l // 2, L,
                           tm=1152 if S % 1152 == 0 else 384)
        else:
            h = _moe_ffn(h, mod, norm_ffn_g[l], moe_w_router[l // 2], moe_w_gate, moe_w_up, moe_w_down, l // 2, L)
    return _final_norm(h, final_norm_g, L)
```
